```python
import math
import jax
import jax.numpy as jnp
from jax import lax
import numpy as np

D_MODEL = 1024
BATCH = 8
SEQ = 8192
DEPTH = 2

F32 = jnp.float32
N_EVEN = (DEPTH + 1) // 2
N_ODD = DEPTH // 2

RK_HEADS = 8
RK_HEAD = 64
RK_W = RK_HEADS * RK_HEAD
RK_DECAY_LORA = 64
RK_AAA_LORA = 64
RK_GATE_LORA = 128
RK_DECAY_SCALE = 0.606531
RK_LN_EPS = 64e-5

NS_HEADS = 8
NS_KV = 2
NS_HPG = NS_HEADS // NS_KV
NS_HEAD = 64
NS_W = NS_HEADS * NS_HEAD
CMP_LEN = 32
CMP_STRIDE = 16
CMP_HIDDEN = 128
SEL_BLOCK = 64
SEL_TOPK = 16
WINDOW = 512
Q_BLOCK = 128
ROPE_THETA = 500000.0
ROPE_DIM = NS_HEAD // 4
AB_COLS = 4 * RK_W + NS_W + 6 * NS_KV * NS_HEAD + 3 * NS_HEADS

RT_HEADS = 8
RT_QK = 128
RT_V = 256
RT_QKW = RT_HEADS * RT_QK
RT_VW = RT_HEADS * RT_V
RT_COLS = 2 * RT_QKW + 2 * RT_VW
RT_CHUNK = 128
RT_THETA = 10000.0
RT_GN_EPS = 1e-5

N_EXPERTS = 16
N_GROUPS = 4
EXP_PER_GROUP = N_EXPERTS // N_GROUPS
TOP_K = 2
D_EXPERT = 1024
MOE_BLOCK = 512

ALPHA = (2.0 * DEPTH) ** 0.25
BETA = (8.0 * DEPTH) ** -0.25
LN_EPS = 1e-5
NEG = -1e30

kernel_name = 'hybrid_rwkv7_nsa_retnet_grouped_moe_deepnorm'


def _layer_norm(x, g, b):
    xf = x.astype(F32)
    mu = xf.mean(-1, keepdims=True)
    var = jnp.square(xf - mu).mean(-1, keepdims=True)
    return ((xf - mu) * lax.rsqrt(var + LN_EPS) * g + b).astype(x.dtype)


def _head_norm(o, g, b, eps):
    h, n = o.shape[-2:]
    of = o.astype(F32)
    mu = of.mean(-1, keepdims=True)
    var = jnp.square(of - mu).mean(-1, keepdims=True)
    return (of - mu) * lax.rsqrt(var + eps) * g.reshape(h, n) + b.reshape(h, n)


def _rotate_half(x, cos, sin):
    x1, x2 = jnp.split(x, 2, axis=-1)
    return jnp.concatenate([x1 * cos - x2 * sin, x2 * cos + x1 * sin], axis=-1)


def _partial_rope(x):
    s = x.shape[1]
    half = ROPE_DIM // 2
    inv = ROPE_THETA ** (-jnp.arange(half, dtype=F32) / half)
    ang = jnp.arange(s, dtype=F32)[:, None] * inv[None, :]
    cos = jnp.cos(ang)[:, None, :]
    sin = jnp.sin(ang)[:, None, :]
    rot = _rotate_half(x[..., :ROPE_DIM].astype(F32), cos, sin).astype(x.dtype)
    return jnp.concatenate([rot, x[..., ROPE_DIM:]], axis=-1)


def _masked_softmax(s, mask, axis=-1):
    p = jax.nn.softmax(jnp.where(mask, s, NEG), axis=axis)
    return jnp.where(mask, p, 0.0)


def _token_shift(p):
    return jnp.pad(p[:, :-1], ((0, 0), (1, 0), (0, 0)))


def _rwkv7(p, mu, w0, w1, w2, a0, a1, a2, g1, g2, k_k, k_a, r_k, ln_gb):
    B, S, _ = p.shape
    dp = _token_shift(p) - p
    pr, pk, pv, pz = jnp.split(p, 4, axis=-1)
    dr, dk, dv, dz = jnp.split(dp, 4, axis=-1)
    r = pr + dr * mu[0]
    k = pk + dk * mu[1]
    v = pv + dv * mu[2]
    xw = pz + dz * mu[3]
    xa = pz + dz * mu[4]
    xg = pz + dz * mu[5]
    w = jnp.exp(-RK_DECAY_SCALE * jax.nn.sigmoid((w0 + jnp.tanh(xw @ w1) @ w2).astype(F32)))
    a = jax.nn.sigmoid((a0 + (xa @ a1) @ a2).astype(F32))
    g = jax.nn.sigmoid(xg @ g1) @ g2
    hd = lambda t: t.astype(F32).reshape(B, S, RK_HEADS, RK_HEAD)
    kk = hd(k * k_k)
    kk = kk / jnp.maximum(jnp.sqrt(jnp.sum(kk * kk, axis=-1, keepdims=True)), 1e-12)
    k = hd(k.astype(F32) * (1.0 + (a - 1.0) * k_a))
    r, v, w, a = hd(r), hd(v), hd(w), hd(a)

    def step(state, inp):
        r_t, w_t, k_t, v_t, kk_t, b_t = inp
        sa = jnp.einsum('bhvk,bhk->bhv', state, -kk_t)
        state = (state * w_t[:, :, None, :] + sa[..., None] * b_t[:, :, None, :]
                 + v_t[..., None] * k_t[:, :, None, :])
        return state, jnp.einsum('bhvk,bhk->bhv', state, r_t)

    tm = lambda t: jnp.moveaxis(t, 1, 0)
    s0 = jnp.zeros((B, RK_HEADS, RK_HEAD, RK_HEAD), F32)
    _, o = lax.scan(step, s0, (tm(r), tm(w), tm(k), tm(v), tm(kk), tm(kk * a)))
    o = jnp.moveaxis(o, 0, 1)
    o = _head_norm(o, ln_gb[0], ln_gb[1], RK_LN_EPS)
    bonus = jnp.sum(r * k * r_k.astype(F32).reshape(RK_HEADS, RK_HEAD), axis=-1, keepdims=True) * v
    o = (o + bonus).reshape(B, S, RK_W) * g
    return o.astype(p.dtype)


def _nsa(p, pe, c_w1, c_w2):
    B, S, _ = p.shape
    q = _partial_rope(p[..., :NS_W].reshape(B, S, NS_HEADS, NS_HEAD))
    kv = p[..., NS_W:NS_W + 6 * NS_KV * NS_HEAD].reshape(B, S, 6, NS_KV, NS_HEAD)
    kc = _partial_rope(kv[:, :, 0])
    vc = kv[:, :, 1]
    ks = _partial_rope(kv[:, :, 2])
    vs = kv[:, :, 3]
    kw = _partial_rope(kv[:, :, 4])
    vw = kv[:, :, 5]
    gates = jax.nn.sigmoid(p[..., -3 * NS_HEADS:].astype(F32)).reshape(B, S, 3, NS_KV, NS_HPG)

    n_cmp = (S - CMP_LEN) // CMP_STRIDE + 1
    cidx = jnp.arange(n_cmp)[:, None] * CMP_STRIDE + jnp.arange(CMP_LEN)[None, :]

    def compress(t, pe_i, w1, w2):
        blk = t[:, cidx] + pe_i[None, None, :, None, :]
        blk = jnp.moveaxis(blk, 3, 2).reshape(B, n_cmp, NS_KV, CMP_LEN * NS_HEAD)
        return jax.nn.gelu(blk @ w1) @ w2

    k_cmp = compress(kc, pe[0], c_w1[0], c_w2[0])
    v_cmp = compress(vc, pe[1], c_w1[1], c_w2[1])
    cmp_end = jnp.arange(n_cmp) * CMP_STRIDE + CMP_LEN - 1

    n_sel = S // SEL_BLOCK
    cs = jnp.arange(n_cmp) * CMP_STRIDE
    ss = jnp.arange(n_sel) * SEL_BLOCK
    ov = jnp.clip(jnp.minimum(cs[:, None] + CMP_LEN, ss[None, :] + SEL_BLOCK)
                  - jnp.maximum(cs[:, None], ss[None, :]), 0, None).astype(F32) / CMP_LEN
    top = min(SEL_TOPK, n_sel)
    ks_blk = ks.reshape(B, n_sel, SEL_BLOCK, NS_KV, NS_HEAD).transpose(0, 3, 1, 2, 4)
    vs_blk = vs.reshape(B, n_sel, SEL_BLOCK, NS_KV, NS_HEAD).transpose(0, 3, 1, 2, 4)
    kw_pad = jnp.pad(kw, ((0, 0), (WINDOW, 0), (0, 0), (0, 0)))
    vw_pad = jnp.pad(vw, ((0, 0), (WINDOW, 0), (0, 0), (0, 0)))
    scale = NS_HEAD ** -0.5
    b_ix = jnp.arange(B)[:, None, None, None]
    g_ix = jnp.arange(NS_KV)[None, None, :, None]
    blk_id = jnp.arange(n_sel)

    def block(s0):
        t = s0 + jnp.arange(Q_BLOCK)
        qb = lax.dynamic_slice_in_dim(q, s0, Q_BLOCK, 1).reshape(B, Q_BLOCK, NS_KV, NS_HPG, NS_HEAD)
        gb = lax.dynamic_slice_in_dim(gates, s0, Q_BLOCK, 1)
        s = jnp.einsum('bqghd,bngd->bqghn', qb, k_cmp).astype(F32) * scale
        pc = _masked_softmax(s, (cmp_end[None, :] <= t[:, None])[None, :, None, None, :])
        o_c = jnp.einsum('bqghn,bngd->bqghd', pc, v_cmp.astype(F32))
        imp = jnp.einsum('bqgn,nj->bqgj', pc.sum(3), ov)
        cur = t // SEL_BLOCK
        valid = (blk_id[None, :] <= cur[:, None])[None, :, None, :]
        forced = ((blk_id[None, :] == 0) | (blk_id[None, :] == cur[:, None])
                  | (blk_id[None, :] == cur[:, None] - 1))[None, :, None, :]
        pri = jnp.where(valid, jnp.where(forced, jnp.inf, imp), -jnp.inf)
        _, sel = lax.top_k(pri, top)
        sel_ok = sel <= cur[None, :, None, None]
        k_sel = ks_blk[b_ix, g_ix, sel]
        v_sel = vs_blk[b_ix, g_ix, sel]
        s = jnp.einsum('bqghd,bqgnld->bqghnl', qb, k_sel).astype(F32) * scale
        kpos = sel[..., None] * SEL_BLOCK + jnp.arange(SEL_BLOCK)
        m = (sel_ok[..., None] & (kpos <= t[None, :, None, None, None]))[:, :, :, None]
        ps = _masked_softmax(s, m, axis=(-2, -1))
        o_s = jnp.einsum('bqghnl,bqgnld->bqghd', ps, v_sel.astype(F32))
        kwb = lax.dynamic_slice_in_dim(kw_pad, s0, WINDOW + Q_BLOCK, 1)
        vwb = lax.dynamic_slice_in_dim(vw_pad, s0, WINDOW + Q_BLOCK, 1)
        wpos = s0 - WINDOW + jnp.arange(WINDOW + Q_BLOCK)
        wm = ((wpos[None, :] <= t[:, None]) & (wpos[None, :] > t[:, None] - WINDOW)
              & (wpos[None, :] >= 0))[None, :, None, None, :]
        s = jnp.einsum('bqghd,bkgd->bqghk', qb, kwb).astype(F32) * scale
        pw = _masked_softmax(s, wm)
        o_w = jnp.einsum('bqghk,bkgd->bqghd', pw, vwb.astype(F32))
        o = (gb[:, :, 0][..., None] * o_c + gb[:, :, 1][..., None] * o_s
             + gb[:, :, 2][..., None] * o_w)
        return o.reshape(B, Q_BLOCK, NS_W).astype(p.dtype)

    out = lax.map(block, jnp.arange(S // Q_BLOCK) * Q_BLOCK)
    return jnp.moveaxis(out, 0, 1).reshape(B, S, NS_W)


def _retention(p, gn_gb):
    B, S, _ = p.shape
    q, k, v, g = jnp.split(p, [RT_QKW, 2 * RT_QKW, 2 * RT_QKW + RT_VW], axis=-1)
    inv = RT_THETA ** (-jnp.linspace(0.0, 1.0, RT_QK // 2, dtype=F32))
    ang = jnp.arange(S, dtype=F32)[:, None] * inv[None, :]
    cos = jnp.cos(ang)[:, None, :]
    sin = jnp.sin(ang)[:, None, :]
    q = _rotate_half(q.astype(F32).reshape(B, S, RT_HEADS, RT_QK), cos, sin)
    k = _rotate_half(k.astype(F32).reshape(B, S, RT_HEADS, RT_QK), cos, sin) * RT_QK ** -0.5
    v = v.astype(F32).reshape(B, S, RT_HEADS, RT_V)
    log_g = jnp.log(1.0 - 2.0 ** (-5.0 - jnp.arange(RT_HEADS, dtype=F32)))
    idx = jnp.arange(RT_CHUNK, dtype=F32)
    diff = idx[:, None] - idx[None, :]
    inner_decay = jnp.where(diff >= 0, jnp.exp(jnp.maximum(diff, 0.0) * log_g[:, None, None]), 0.0)
    q_decay = jnp.exp((idx + 1.0) * log_g[:, None])[..., None]
    k_decay = jnp.exp((RT_CHUNK - 1.0 - idx) * log_g[:, None])[..., None]
    chunk_decay = jnp.exp(RT_CHUNK * log_g)[:, None, None]
    n_ch = S // RT_CHUNK
    chunks = lambda t: t.reshape(B, n_ch, RT_CHUNK, RT_HEADS, t.shape[-1]).transpose(1, 0, 3, 2, 4)

    def step(R, inp):
        qc, kc, vc = inp
        att = jnp.einsum('bhid,bhjd->bhij', qc, kc) * inner_decay
        inner = jnp.einsum('bhij,bhjv->bhiv', att, vc)
        cross = jnp.einsum('bhid,bhdv->bhiv', qc, R) * q_decay
        R = R * chunk_decay + jnp.einsum('bhjd,bhjv->bhdv', kc * k_decay, vc)
        return R, inner + cross

    R0 = jnp.zeros((B, RT_HEADS, RT_QK, RT_V), F32)
    _, o = lax.scan(step, R0, (chunks(q), chunks(k), chunks(v)))
    o = o.transpose(1, 0, 3, 2, 4).reshape(B, S, RT_HEADS, RT_V)
    o = _head_norm(o, gn_gb[0], gn_gb[1], RT_GN_EPS).reshape(B, S, RT_VW)
    return (jax.nn.silu(g.astype(F32)) * o).astype(p.dtype)


def _moe(h, router_w, router_b, w_gate, w_up, w_down):
    B, S, D = h.shape
    T = B * S
    xt = h.reshape(T, D)
    aff = jax.nn.sigmoid((xt @ router_w).astype(F32))
    grouped = (aff + router_b.astype(F32)).reshape(T, N_GROUPS, EXP_PER_GROUP)
    grp = jnp.argmax(lax.top_k(grouped, TOP_K)[0].sum(-1), axis=-1)
    in_grp = jnp.take_along_axis(grouped, grp[:, None, None], axis=1)[:, 0]
    _, loc = lax.top_k(in_grp, TOP_K)
    eidx = grp[:, None] * EXP_PER_GROUP + loc
    wts = jnp.take_along_axis(aff, eidx, axis=1)
    wts = wts / wts.sum(-1, keepdims=True)
    A = T * TOP_K
    fe = eidx.reshape(A)
    ftok = jnp.repeat(jnp.arange(T, dtype=jnp.int32), TOP_K)
    fw = wts.reshape(A)
    order = jnp.argsort(fe)
    se, stok, sw = fe[order], ftok[order], fw[order]
    counts = jnp.bincount(fe, length=N_EXPERTS)
    padded = (counts + MOE_BLOCK - 1) // MOE_BLOCK * MOE_BLOCK
    pad_end = jnp.cumsum(padded)
    pad_start = pad_end - padded
    start = jnp.cumsum(counts) - counts
    dest = pad_start[se] + jnp.arange(A) - start[se]
    n_blocks = -(-A // MOE_BLOCK) + N_EXPERTS
    rows = n_blocks * MOE_BLOCK
    row_tok = jnp.full((rows,), T, jnp.int32).at[dest].set(stok)
    row_w = jnp.zeros((rows,), F32).at[dest].set(sw)
    blk_exp = jnp.minimum(jnp.searchsorted(pad_end, jnp.arange(n_blocks) * MOE_BLOCK, side='right'),
                          N_EXPERTS - 1)
    x_pad = jnp.concatenate([xt, jnp.zeros((1, D), xt.dtype)], axis=0)

    def run(args):
        tok, wr, e = args
        xb = x_pad[tok]
        hb = jax.nn.silu(xb @ w_gate[e]) * (xb @ w_up[e])
        return (hb @ w_down[e]) * wr[:, None]

    y = lax.map(run, (row_tok.reshape(n_blocks, MOE_BLOCK), row_w.reshape(n_blocks, MOE_BLOCK), blk_exp))
    out = jax.ops.segment_sum(y.reshape(rows, D), row_tok, num_segments=T + 1)[:T]
    return out.reshape(B, S, D).astype(h.dtype)


def _gain_bias(key, lead, width):
    kg, kb = jax.random.split(key)
    g = 1.0 + 0.05 * jax.random.normal(kg, lead + (1, width), F32)
    b = 0.02 * jax.random.normal(kb, lead + (1, width), F32)
    return jnp.concatenate([g, b], axis=-2)


def setup_inputs(seed: int = 0) -> dict:
    key = jax.random.key(seed)
    ks = jax.random.split(key, 32)
    nrm = lambda k, shape, s: jax.random.normal(k, shape, F32) * s
    NE, NO = N_EVEN, N_ODD
    mix_w = RK_W + NS_W
    return {
        'x': nrm(ks[0], (BATCH, SEQ, D_MODEL), 1.0),
        'ab_w_in': nrm(ks[1], (NE, D_MODEL, AB_COLS), D_MODEL ** -0.5),
        'ab_w_out': nrm(ks[2], (NE, mix_w, D_MODEL), BETA * mix_w ** -0.5),
        'rk_mu': jax.random.uniform(ks[3], (NE, 6, RK_W), F32),
        'rk_w0': -2.0 + nrm(ks[4], (NE, RK_W), 1.5),
        'rk_w1': nrm(ks[5], (NE, RK_W, RK_DECAY_LORA), RK_W ** -0.5),
        'rk_w2': nrm(ks[6], (NE, RK_DECAY_LORA, RK_W), 0.5 * RK_DECAY_LORA ** -0.5),
        'rk_a0': nrm(ks[7], (NE, RK_W), 0.5),
        'rk_a1': nrm(ks[8], (NE, RK_W, RK_AAA_LORA), RK_W ** -0.5),
        'rk_a2': nrm(ks[9], (NE, RK_AAA_LORA, RK_W), 0.5 * RK_AAA_LORA ** -0.5),
        'rk_g1': nrm(ks[10], (NE, RK_W, RK_GATE_LORA), RK_W ** -0.5),
        'rk_g2': nrm(ks[11], (NE, RK_GATE_LORA, RK_W), RK_GATE_LORA ** -0.5),
        'rk_kk': 0.85 + nrm(ks[12], (NE, RK_W), 0.05),
        'rk_ka': 1.0 + nrm(ks[13], (NE, RK_W), 0.05),
        'rk_rk': nrm(ks[14], (NE, RK_W), 0.1),
        'rk_ln': _gain_bias(ks[15], (NE,), RK_W),
        'ns_pe': nrm(ks[16], (NE, 2, CMP_LEN, NS_HEAD), 0.1),
        'ns_c_w1': nrm(ks[17], (NE, 2, CMP_LEN * NS_HEAD, CMP_HIDDEN), (CMP_LEN * NS_HEAD) ** -0.5),
        'ns_c_w2': nrm(ks[18], (NE, 2, CMP_HIDDEN, NS_HEAD), CMP_HIDDEN ** -0.5),
        'rt_w_in': nrm(ks[19], (NO, D_MODEL, RT_COLS), D_MODEL ** -0.5),
        'rt_w_out': nrm(ks[20], (NO, RT_VW, D_MODEL), BETA * RT_VW ** -0.5),
        'rt_gn': _gain_bias(ks[21], (NO,), RT_VW),
        'router_w': nrm(ks[22], (D_MODEL, N_EXPERTS), D_MODEL ** -0.5),
        'router_b': nrm(ks[23], (N_EXPERTS,), 0.01),
        'moe_w_gate': nrm(ks[24], (DEPTH, N_EXPERTS, D_MODEL, D_EXPERT), D_MODEL ** -0.5),
        'moe_w_up': nrm(ks[25], (DEPTH, N_EXPERTS, D_MODEL, D_EXPERT), D_MODEL ** -0.5),
        'moe_w_down': nrm(ks[26], (DEPTH, N_EXPERTS, D_EXPERT, D_MODEL), BETA * D_EXPERT ** -0.5),
        'ln': _gain_bias(ks[27], (DEPTH, 2), D_MODEL),
    }


def reference(x, ab_w_in, ab_w_out, rk_mu, rk_w0, rk_w1, rk_w2, rk_a0, rk_a1, rk_a2, rk_g1, rk_g2,
              rk_kk, rk_ka, rk_rk, rk_ln, ns_pe, ns_c_w1, ns_c_w2, rt_w_in, rt_w_out, rt_gn,
              router_w, router_b, moe_w_gate, moe_w_up, moe_w_down, ln):
    for layer in range(DEPTH):
        i = layer // 2
        if layer % 2 == 0:
            p = x @ ab_w_in[i]
            o_a = _rwkv7(p[..., :4 * RK_W], rk_mu[i], rk_w0[i], rk_w1[i], rk_w2[i], rk_a0[i],
                         rk_a1[i], rk_a2[i], rk_g1[i], rk_g2[i], rk_kk[i], rk_ka[i], rk_rk[i], rk_ln[i])
            o_b = _nsa(p[..., 4 * RK_W:], ns_pe[i], ns_c_w1[i], ns_c_w2[i])
            mix = jnp.concatenate([o_a, o_b], axis=-1) @ ab_w_out[i]
        else:
            p = x @ rt_w_in[i]
            mix = _retention(p, rt_gn[i]) @ rt_w_out[i]
        x = _layer_norm(ALPHA * x + mix, ln[layer, 0, 0], ln[layer, 0, 1])
        ffn = _moe(x, router_w, router_b, moe_w_gate[layer], moe_w_up[layer], moe_w_down[layer])
        x = _layer_norm(ALPHA * x + ffn, ln[layer, 1, 0], ln[layer, 1, 1])
    return x
```

```python
import functools
import math

import jax
import jax.numpy as jnp
from jax import lax
from jax.experimental import pallas as pl
from jax.experimental.pallas import tpu as pltpu

F32 = jnp.float32
BF16 = jnp.bfloat16
I32 = jnp.int32
HI = lax.Precision.HIGHEST

LANES = 128
VMEM_LIMIT = 56 * 1024 * 1024

D_MODEL = 1024
RK_HEADS, RK_HEAD = 8, 64
RK_W = RK_HEADS * RK_HEAD
RK_DECAY_SCALE = 0.606531
RK_LN_EPS = 64e-5
NS_HEADS, NS_KV, NS_HPG, NS_HEAD = 8, 2, 4, 64
NS_W = NS_HEADS * NS_HEAD
CMP_LEN, CMP_STRIDE, SEL_BLOCK, SEL_TOPK, WINDOW = 32, 16, 64, 16, 512
ROPE_THETA = 500000.0
ROPE_DIM = NS_HEAD // 4
QT = 128
RT_HEADS, RT_QK, RT_V = 8, 128, 256
RT_QKW, RT_VW = RT_HEADS * RT_QK, RT_HEADS * RT_V
RT_CHUNK = 128
RT_THETA = 10000.0
RT_GN_EPS = 1e-5
N_EXPERTS, N_GROUPS, EXP_PER_GROUP, TOP_K = 16, 4, 4, 2
D_EXPERT = 1024
MOE_BLOCK = 512
DEPTH = 2
ALPHA = (2.0 * DEPTH) ** 0.25
LN_EPS = 1e-5
NEG = -1e30


def _cparams(sem):
    return pltpu.CompilerParams(dimension_semantics=sem, vmem_limit_bytes=VMEM_LIMIT)


def _bdot(a, w):
    return jnp.dot(a.astype(BF16), w, preferred_element_type=F32)


def _dot_nt(a, b):
    return lax.dot_general(a, b, (((1,), (1,)), ((), ())), preferred_element_type=F32)


def _layer_norm_rows(z, g, b):
    mu = jnp.mean(z, axis=1, keepdims=True)
    zc = z - mu
    var = jnp.mean(zc * zc, axis=1, keepdims=True)
    return zc * lax.rsqrt(var + LN_EPS) * g + b


def _proj_ab_kernel(x_ref, w_ref, cn_ref, s1_ref, s2_ref, prk_ref, q_ref, kv_ref, gt_ref):
    xb = x_ref[...].astype(BF16)
    for c in range(4):
        prk_ref[:, c * RK_W:(c + 1) * RK_W] = jnp.dot(xb, w_ref[:, c * RK_W:(c + 1) * RK_W],
                                                      preferred_element_type=F32)
    cn, s1, s2 = cn_ref[...], s1_ref[...], s2_ref[...]

    def rope(y):
        return y * cn + pltpu.roll(y, LANES - ROPE_DIM // 2, 1) * s1 + pltpu.roll(y, ROPE_DIM // 2, 1) * s2

    base = 4 * RK_W
    yq = jnp.dot(xb, w_ref[:, base:base + NS_W], preferred_element_type=F32)
    for c in range(NS_W // LANES):
        q_ref[:, c * LANES:(c + 1) * LANES] = (rope(yq[:, c * LANES:(c + 1) * LANES]) * NS_HEAD ** -0.5).astype(BF16)
    base += NS_W
    ykv = jnp.dot(xb, w_ref[:, base:base + 6 * LANES], preferred_element_type=F32)
    for c in range(6):
        y = ykv[:, c * LANES:(c + 1) * LANES]
        if c % 2 == 0:
            y = rope(y)
        kv_ref[:, c * LANES:(c + 1) * LANES] = y.astype(BF16)
    base += 6 * LANES
    gt_ref[...] = jax.nn.sigmoid(jnp.dot(xb, w_ref[:, base:base + LANES], preferred_element_type=F32))


def _proj_ab(xt, w, cn, s1, s2, S, tm=256):
    T = xt.shape[0]
    ncols = w.shape[1]
    nseq = S // tm
    row = lambda i: (i, 0)
    tab = lambda i: (i % nseq, 0)
    return pl.pallas_call(
        _proj_ab_kernel,
        grid=(T // tm,),
        in_specs=[pl.BlockSpec((tm, D_MODEL), row),
                  pl.BlockSpec((D_MODEL, ncols), lambda i: (0, 0)),
                  pl.BlockSpec((tm, LANES), tab), pl.BlockSpec((tm, LANES), tab), pl.BlockSpec((tm, LANES), tab)],
        out_specs=[pl.BlockSpec((tm, 4 * RK_W), row), pl.BlockSpec((tm, NS_W), row),
                   pl.BlockSpec((tm, 6 * LANES), row), pl.BlockSpec((tm, LANES), row)],
        out_shape=[jax.ShapeDtypeStruct((T, 4 * RK_W), F32), jax.ShapeDtypeStruct((T, NS_W), BF16),
                   jax.ShapeDtypeStruct((T, 6 * LANES), BF16), jax.ShapeDtypeStruct((T, LANES), F32)],
        compiler_params=_cparams(("parallel",)),
        name="proj_ab",
    )(xt, w, cn, s1, s2)


def _rwkv_prep_kernel(p_ref, mu_ref, w0_ref, w1_ref, w2_ref, a0_ref, a1_ref, a2_ref, g1_ref, g2_ref,
                      kk_ref, ka_ref, rk_ref, ones_ref,
                      r_o, w_o, k_o, v_o, nkk_o, b_o, g_o, bon_o, carry):
    j = pl.program_id(1)
    p = p_ref[...]
    tm = p.shape[0]

    @pl.when(j == 0)
    def _():
        carry[...] = jnp.zeros_like(carry)

    rowi = lax.broadcasted_iota(I32, p.shape, 0)
    prev = jnp.where(rowi == 0, carry[...], pltpu.roll(p, 1, 0))
    carry[...] = p[tm - 1:tm, :]
    dp = prev - p
    sl = lambda a, c: a[:, c * RK_W:(c + 1) * RK_W]
    mu = mu_ref[...]
    r = sl(p, 0) + sl(dp, 0) * mu[0:1]
    k = sl(p, 1) + sl(dp, 1) * mu[1:2]
    v = sl(p, 2) + sl(dp, 2) * mu[2:3]
    xw = sl(p, 3) + sl(dp, 3) * mu[3:4]
    xa = sl(p, 3) + sl(dp, 3) * mu[4:5]
    xg = sl(p, 3) + sl(dp, 3) * mu[5:6]
    w = jnp.exp(-RK_DECAY_SCALE * jax.nn.sigmoid(w0_ref[...] + _bdot(jnp.tanh(_bdot(xw, w1_ref[...])), w2_ref[...])))
    a = jax.nn.sigmoid(a0_ref[...] + _bdot(_bdot(xa, a1_ref[...]), a2_ref[...]))
    g = _bdot(jax.nn.sigmoid(_bdot(xg, g1_ref[...])), g2_ref[...])
    ones = ones_ref[...]
    kk = k * kk_ref[...]
    ss = jnp.dot(kk * kk, ones, precision=HI, preferred_element_type=F32)
    kk = kk / jnp.maximum(jnp.sqrt(ss), 1e-12)
    km = k * (1.0 + (a - 1.0) * ka_ref[...])
    bon = jnp.dot(r * km * rk_ref[...], ones, precision=HI, preferred_element_type=F32) * v
    r_o[...] = r
    w_o[...] = w
    k_o[...] = km
    v_o[...] = v
    nkk_o[...] = -kk
    b_o[...] = kk * a
    g_o[...] = g
    bon_o[...] = bon


def _rwkv_prep(prk3, mu, w0, w1, w2, a0, a1, a2, g1, g2, k_k, k_a, r_k, ones, tm=256):
    B, S, _ = prk3.shape
    full = lambda a: pl.BlockSpec(a.shape, lambda b, j: (0,) * a.ndim)
    params = [mu, w0, w1, w2, a0, a1, a2, g1, g2, k_k, k_a, r_k, ones]
    ospec = pl.BlockSpec((None, tm, RK_W), lambda b, j: (b, j, 0))
    return pl.pallas_call(
        _rwkv_prep_kernel,
        grid=(B, S // tm),
        in_specs=[pl.BlockSpec((None, tm, 4 * RK_W), lambda b, j: (b, j, 0))] + [full(a) for a in params],
        out_specs=[ospec] * 8,
        out_shape=[jax.ShapeDtypeStruct((B, S, RK_W), F32)] * 8,
        scratch_shapes=[pltpu.VMEM((1, 4 * RK_W), F32)],
        compiler_params=_cparams(("parallel", "arbitrary")),
        name="rwkv_prep",
    )(prk3, *params)


def _rwkv_scan_kernel(r_ref, w_ref, k_ref, v_ref, nkk_ref, b_ref, g_ref, bon_ref, lng_ref, lnb_ref, o_ref,
                      st, vt, oacc, *, B, TC):
    NP = RK_HEADS // 2
    N = RK_HEAD
    i = pl.program_id(0)

    @pl.when(i == 0)
    def _():
        st[...] = jnp.zeros_like(st)

    for b in range(B):
        for p in range(NP):
            vt[b * NP + p] = v_ref[b, :, p * LANES:(p + 1) * LANES].T

    lane = lax.broadcasted_iota(I32, (N, LANES), 1)
    in_a = lane < N

    def seg_sums(x):
        sa = jnp.sum(jnp.where(in_a, x, 0.0), axis=1, keepdims=True)
        sb = jnp.sum(jnp.where(in_a, 0.0, x), axis=1, keepdims=True)
        return sa, sb

    oacc[...] = jnp.zeros_like(oacc)
    SUB = 8

    def batch(b, c):
        def group(tg, c2):
            t0 = pl.multiple_of(tg * SUB, SUB)
            rows8 = [ref[b, pl.ds(t0, SUB), :] for ref in (nkk_ref, w_ref, b_ref, k_ref, r_ref)]
            for p in range(NP):
                idx = b * NP + p
                cols = slice(p * LANES, (p + 1) * LANES)
                s = st[idx]
                vta = vt[idx, 0:N, :]
                vtb = vt[idx, N:2 * N, :]
                o_a = jnp.zeros((N, LANES), F32)
                o_b = jnp.zeros((N, LANES), F32)
                for j in range(SUB):
                    at_t = lane == t0 + j
                    nkk_t, w_t, b_t, k_t, r_t = [x[j:j + 1, cols] for x in rows8]
                    sa, sb = seg_sums(s * nkk_t)
                    va = jnp.sum(jnp.where(at_t, vta, 0.0), axis=1, keepdims=True)
                    vb = jnp.sum(jnp.where(at_t, vtb, 0.0), axis=1, keepdims=True)
                    s = s * w_t + jnp.where(in_a, sa, sb) * b_t + jnp.where(in_a, va, vb) * k_t
                    oa, ob = seg_sums(s * r_t)
                    o_a = jnp.where(at_t, oa, o_a)
                    o_b = jnp.where(at_t, ob, o_b)
                st[idx] = s
                oacc[idx, 0:N, :] = oacc[idx, 0:N, :] + o_a
                oacc[idx, N:2 * N, :] = oacc[idx, N:2 * N, :] + o_b
            return c2

        return lax.fori_loop(0, TC // SUB, group, c)

    lax.fori_loop(0, B, batch, 0)

    for b in range(B):
        for p in range(NP):
            idx = b * NP + p
            cols = slice(p * LANES, (p + 1) * LANES)
            halves = []
            for h in range(2):
                oh = oacc[idx, h * N:(h + 1) * N, :]
                mu = jnp.mean(oh, axis=0, keepdims=True)
                oc = oh - mu
                var = jnp.mean(oc * oc, axis=0, keepdims=True)
                halves.append(oc * lax.rsqrt(var + RK_LN_EPS))
            on = jnp.concatenate(halves, axis=0).T
            o_ref[b, :, cols] = (on * lng_ref[:, cols] + lnb_ref[:, cols] + bon_ref[b, :, cols]) * g_ref[b, :, cols]


def _rwkv_scan(r, w, k, v, nkk, bb, g, bon, lng, lnb, tc=128):
    B, S, _ = r.shape
    NP = RK_HEADS // 2
    blk = pl.BlockSpec((B, tc, RK_W), lambda i: (0, i, 0))
    vec = pl.BlockSpec((1, RK_W), lambda i: (0, 0))
    return pl.pallas_call(
        functools.partial(_rwkv_scan_kernel, B=B, TC=tc),
        grid=(S // tc,),
        in_specs=[blk] * 8 + [vec, vec],
        out_specs=blk,
        out_shape=jax.ShapeDtypeStruct((B, S, RK_W), F32),
        scratch_shapes=[pltpu.VMEM((B * NP, RK_HEAD, LANES), F32), pltpu.VMEM((B * NP, LANES, tc), F32),
                        pltpu.VMEM((B * NP, LANES, tc), F32)],
        compiler_params=_cparams(("arbitrary",)),
        name="rwkv_scan",
    )(r, w, k, v, nkk, bb, g, bon, lng, lnb)


def _nsa_compress_kernel(x_ref, w1b_ref, w1f_ref, w2_ref, pe_ref, o_ref):
    half = CMP_STRIDE * NS_HEAD
    w1b = w1b_ref[...]
    bias = jnp.dot(jnp.broadcast_to(pe_ref[...], (8, CMP_LEN * NS_HEAD)), w1f_ref[...], precision=HI,
                   preferred_element_type=F32)[0:1]
    outs = []
    for g in range(NS_KV):
        x = x_ref[g]
        ya = jnp.dot(x, w1b[:half], preferred_element_type=F32)
        yb = jnp.dot(x, w1b[half:], preferred_element_type=F32)
        n = x.shape[0]
        h = ya + pltpu.roll(yb, n - 1, 0) + bias
        outs.append(_bdot(jax.nn.gelu(h), w2_ref[...]))
    o_ref[...] = jnp.concatenate(outs, axis=1).astype(BF16)


def _nsa_compress(x16, w1b, w1f, w2b, pe):
    B, _, _, ncp, width = x16.shape
    hid = w1b.shape[-1]
    return pl.pallas_call(
        _nsa_compress_kernel,
        grid=(B, 2),
        in_specs=[pl.BlockSpec((None, None, NS_KV, ncp, width), lambda b, c: (b, c, 0, 0, 0)),
                  pl.BlockSpec((None, 2 * width, hid), lambda b, c: (c, 0, 0)),
                  pl.BlockSpec((None, 2 * width, hid), lambda b, c: (c, 0, 0)),
                  pl.BlockSpec((None, hid, NS_HEAD), lambda b, c: (c, 0, 0)),
                  pl.BlockSpec((None, 1, 2 * width), lambda b, c: (c, 0, 0))],
        out_specs=pl.BlockSpec((None, None, ncp, LANES), lambda b, c: (b, c, 0, 0)),
        out_shape=jax.ShapeDtypeStruct((B, 2, ncp, LANES), BF16),
        compiler_params=_cparams(("parallel", "parallel")),
        name="nsa_compress",
    )(x16, w1b, w1f, w2b, pe)


def _nsa_attn_kernel(q_ref, kc_ref, vc_ref, ks_ref, vs_ref, kw_ref, vw_ref, gt_ref, ov_ref, o_ref,
                     m_s, l_s, acc_s, m_w, l_w, acc_w, *, ncp):
    i = pl.program_id(1)
    g = pl.program_id(2)
    s0 = i * QT
    R = NS_HPG * QT

    q4 = q_ref[...]
    lane_q = lax.broadcasted_iota(I32, (QT, LANES), 1)
    mine = (lane_q // NS_HEAD) == g
    rows = []
    for h in range(NS_HPG):
        qh = q4[:, h * NS_HEAD:(h + 1) * NS_HEAD]
        rows.append(jnp.where(mine, jnp.concatenate([qh, qh], axis=1), jnp.zeros((), BF16)))
    q = jnp.concatenate(rows, axis=0)

    tq = s0 + (lax.broadcasted_iota(I32, (R, LANES), 0) & (QT - 1))
    col = lax.broadcasted_iota(I32, (R, LANES), 1)

    sc = _dot_nt(q, kc_ref[...])
    tqc = s0 + (lax.broadcasted_iota(I32, (R, ncp), 0) & (QT - 1))
    ncol = lax.broadcasted_iota(I32, (R, ncp), 1)
    cmask = (ncol * CMP_STRIDE + (CMP_LEN - 1) <= tqc) & (ncol < ncp - 1)
    sc = jnp.where(cmask, sc, NEG)
    mc = jnp.max(sc, axis=1, keepdims=True)
    pc = jnp.where(cmask, jnp.exp(sc - mc), 0.0)
    lc = jnp.sum(pc, axis=1, keepdims=True)
    pc = pc / jnp.where(lc > 0.0, lc, 1.0)
    o_c = jnp.dot(pc.astype(BF16), vc_ref[...], preferred_element_type=F32)

    pcs = pc[0:QT] + pc[QT:2 * QT] + pc[2 * QT:3 * QT] + pc[3 * QT:4 * QT]
    imp = jnp.dot(pcs, ov_ref[...], precision=HI, preferred_element_type=F32)
    blk = lax.broadcasted_iota(I32, (QT, LANES), 1)
    cur = (s0 + lax.broadcasted_iota(I32, (QT, LANES), 0)) // SEL_BLOCK
    valid = blk <= cur
    forced = (blk == 0) | (blk == cur) | (blk == cur - 1)
    pri = jnp.where(valid, jnp.where(forced, jnp.inf, imp), -jnp.inf)
    blkf = blk.astype(F32)
    sel = jnp.zeros((QT, LANES), F32)
    for _ in range(SEL_TOPK):
        mx = jnp.max(pri, axis=1, keepdims=True)
        first = jnp.min(jnp.where(pri == mx, blkf, float(LANES)), axis=1, keepdims=True)
        hit = blkf == first
        sel = jnp.where(hit, 1.0, sel)
        pri = jnp.where(hit, -jnp.inf, pri)
    sel = jnp.where(valid, sel, 0.0).astype(BF16)
    sel4 = jnp.concatenate([sel] * NS_HPG, axis=0)

    def online_update(s, vb, m_ref, l_ref, acc_ref):
        m_old = m_ref[...]
        m_new = jnp.maximum(m_old, jnp.max(s, axis=1, keepdims=True))
        alpha = jnp.exp(m_old - m_new)
        p = jnp.exp(s - m_new)
        l_ref[...] = alpha * l_ref[...] + jnp.sum(p, axis=1, keepdims=True)
        acc_ref[...] = alpha * acc_ref[...] + jnp.dot(p.astype(BF16), vb, preferred_element_type=F32)
        m_ref[...] = m_new

    def reset(m_ref, l_ref, acc_ref):
        m_ref[...] = jnp.full(m_ref.shape, NEG, F32)
        l_ref[...] = jnp.zeros_like(l_ref)
        acc_ref[...] = jnp.zeros_like(acc_ref)

    reset(m_s, l_s, acc_s)
    erow = lax.broadcasted_iota(I32, (LANES, LANES), 0)
    ecol_hi = lax.broadcasted_iota(I32, (LANES, LANES), 1) // SEL_BLOCK

    def sel_body(kt, c):
        k0 = pl.multiple_of(kt * QT, QT)
        s = _dot_nt(q, ks_ref[pl.ds(k0, QT), :])
        expand = (erow == (QT // SEL_BLOCK) * kt + ecol_hi).astype(BF16)
        chosen = jnp.dot(sel4, expand, preferred_element_type=F32)
        ok = (chosen > 0.5) & (k0 + col <= tq)
        online_update(jnp.where(ok, s, NEG), vs_ref[pl.ds(k0, QT), :], m_s, l_s, acc_s)
        return c

    lax.fori_loop(0, i + 1, sel_body, 0)

    reset(m_w, l_w, acc_w)
    for d in range(WINDOW // QT + 1):
        kt = i - d

        @pl.when(kt >= 0)
        def _():
            k0 = pl.multiple_of(kt * QT, QT)
            s = _dot_nt(q, kw_ref[pl.ds(k0, QT), :])
            kpos = k0 + col
            ok = (kpos <= tq) & (kpos > tq - WINDOW)
            online_update(jnp.where(ok, s, NEG), vw_ref[pl.ds(k0, QT), :], m_w, l_w, acc_w)

    gt = gt_ref[...]
    gcol = lax.broadcasted_iota(I32, (QT, LANES), 1)
    gsel = (g == 0).astype(F32)
    heads = []
    for h in range(NS_HPG):
        rs = slice(h * QT, (h + 1) * QT)
        gate = lambda br: jnp.sum(jnp.where(gcol == br * NS_HEADS + g * NS_HPG + h, gt, 0.0), axis=1, keepdims=True)
        o = (gate(0) * o_c[rs] + gate(1) * (acc_s[rs, :] / l_s[rs, :]) + gate(2) * (acc_w[rs, :] / l_w[rs, :]))
        heads.append(o[:, :NS_HEAD] * gsel + o[:, NS_HEAD:] * (1.0 - gsel))
    o_ref[...] = jnp.concatenate(heads, axis=1).astype(BF16)


def _nsa_attn(q3, kcv, kv3, gt3, ov):
    B, S, _ = q3.shape
    ncp = kcv.shape[2]
    R = NS_HPG * QT
    gw = NS_HPG * NS_HEAD
    seq = lambda c: pl.BlockSpec((None, S, LANES), lambda b, i, g, c=c: (b, 0, c))
    return pl.pallas_call(
        functools.partial(_nsa_attn_kernel, ncp=ncp),
        grid=(B, S // QT, NS_KV),
        in_specs=[pl.BlockSpec((None, QT, gw), lambda b, i, g: (b, i, g)),
                  pl.BlockSpec((None, None, ncp, LANES), lambda b, i, g: (b, 0, 0, 0)),
                  pl.BlockSpec((None, None, ncp, LANES), lambda b, i, g: (b, 1, 0, 0)),
                  seq(2), seq(3), seq(4), seq(5),
                  pl.BlockSpec((None, QT, LANES), lambda b, i, g: (b, i, 0)),
                  pl.BlockSpec(ov.shape, lambda b, i, g: (0, 0))],
        out_specs=pl.BlockSpec((None, QT, gw), lambda b, i, g: (b, i, g)),
        out_shape=jax.ShapeDtypeStruct((B, S, NS_W), BF16),
        scratch_shapes=[pltpu.VMEM((R, 1), F32), pltpu.VMEM((R, 1), F32), pltpu.VMEM((R, LANES), F32),
                        pltpu.VMEM((R, 1), F32), pltpu.VMEM((R, 1), F32), pltpu.VMEM((R, LANES), F32)],
        compiler_params=_cparams(("parallel", "parallel", "arbitrary")),
        name="nsa_attn",
    )(q3, kcv, kcv, kv3, kv3, kv3, kv3, gt3, ov)


def _first_argmax(vals):
    m = vals[0]
    for v in vals[1:]:
        m = jnp.maximum(m, v)
    idx = jnp.full(m.shape, len(vals) - 1, I32)
    for j in range(len(vals) - 2, -1, -1):
        idx = jnp.where(vals[j] == m, j, idx)
    return m, idx


def _outproj_router_kernel(*refs, n_in):
    acts, ws = refs[:n_in], refs[n_in:2 * n_in]
    x_ref, lng_ref, lnb_ref, rwt_ref, rb_ref, tri_ref = refs[2 * n_in:2 * n_in + 6]
    y_ref, e_ref, wt_ref, pos_ref, cnt_ref, cnt = refs[2 * n_in + 6:]
    i = pl.program_id(0)

    @pl.when(i == 0)
    def _():
        cnt[...] = jnp.zeros_like(cnt)

    mix = _bdot(acts[0][...], ws[0][...])
    for a, w in zip(acts[1:], ws[1:]):
        mix = mix + _bdot(a[...], w[...])
    y = _layer_norm_rows(ALPHA * x_ref[...] + mix, lng_ref[...], lnb_ref[...])
    y_ref[...] = y

    logit = lax.dot_general(rwt_ref[...], y, (((1,), (1,)), ((), ())), precision=HI, preferred_element_type=F32)
    aff = jax.nn.sigmoid(logit)
    biased = aff + rb_ref[...]
    neg_inf = -jnp.inf
    g_score, g_i1, g_i2 = [], [], []
    for gi in range(N_GROUPS):
        vals = [biased[gi * EXP_PER_GROUP + j:gi * EXP_PER_GROUP + j + 1, :] for j in range(EXP_PER_GROUP)]
        m1, i1 = _first_argmax(vals)
        m2, i2 = _first_argmax([jnp.where(i1 == j, neg_inf, vals[j]) for j in range(EXP_PER_GROUP)])
        g_score.append(m1 + m2)
        g_i1.append(i1)
        g_i2.append(i2)
    _, grp = _first_argmax(g_score)
    loc1, loc2 = g_i1[-1], g_i2[-1]
    for gi in range(N_GROUPS - 2, -1, -1):
        loc1 = jnp.where(grp == gi, g_i1[gi], loc1)
        loc2 = jnp.where(grp == gi, g_i2[gi], loc2)
    e1 = grp * EXP_PER_GROUP + loc1
    e2 = grp * EXP_PER_GROUP + loc2
    eio = lax.broadcasted_iota(I32, aff.shape, 0)
    oh1 = eio == e1
    oh2 = eio == e2
    a1 = jnp.sum(jnp.where(oh1, aff, 0.0), axis=0, keepdims=True)
    a2 = jnp.sum(jnp.where(oh2, aff, 0.0), axis=0, keepdims=True)
    tot = a1 + a2
    e_ref[...] = jnp.concatenate([e1, e2], axis=0)
    wt_ref[...] = jnp.concatenate([a1 / tot, a2 / tot], axis=0)

    ohs = oh1.astype(F32) + oh2.astype(F32)
    before = jnp.dot(ohs.astype(BF16), tri_ref[...], preferred_element_type=F32) + cnt[...]
    p1 = jnp.sum(jnp.where(oh1, before, 0.0), axis=0, keepdims=True)
    p2 = jnp.sum(jnp.where(oh2, before, 0.0), axis=0, keepdims=True)
    pos_ref[...] = jnp.concatenate([p1, p2], axis=0).astype(I32)
    cnt[...] = cnt[...] + jnp.sum(ohs, axis=1, keepdims=True)
    cnt_ref[...] = jnp.broadcast_to(cnt[...], cnt_ref.shape)


def _outproj_router(acts, ws, xres, lng, lnb, rwt, rb, tm=256):
    T = xres.shape[0]
    n_in = len(acts)
    tri = (lax.broadcasted_iota(I32, (tm, tm), 0) < lax.broadcasted_iota(I32, (tm, tm), 1)).astype(BF16)
    row = lambda i: (i, 0)
    const = lambda a: pl.BlockSpec(a.shape, lambda i: (0,) * a.ndim)
    lane_blk = pl.BlockSpec((TOP_K, tm), lambda i: (0, i))
    return pl.pallas_call(
        functools.partial(_outproj_router_kernel, n_in=n_in),
        grid=(T // tm,),
        in_specs=([pl.BlockSpec((tm, a.shape[1]), row) for a in acts] + [const(w) for w in ws]
                  + [pl.BlockSpec((tm, D_MODEL), row), const(lng), const(lnb), const(rwt), const(rb), const(tri)]),
        out_specs=[pl.BlockSpec((tm, D_MODEL), row), lane_blk, lane_blk, lane_blk,
                   pl.BlockSpec((N_EXPERTS, LANES), lambda i: (0, 0))],
        out_shape=[jax.ShapeDtypeStruct((T, D_MODEL), F32), jax.ShapeDtypeStruct((TOP_K, T), I32),
                   jax.ShapeDtypeStruct((TOP_K, T), F32), jax.ShapeDtypeStruct((TOP_K, T), I32),
                   jax.ShapeDtypeStruct((N_EXPERTS, LANES), F32)],
        scratch_shapes=[pltpu.VMEM((N_EXPERTS, 1), F32)],
        compiler_params=_cparams(("arbitrary",)),
        name="outproj_router",
    )(*acts, *ws, xres, lng, lnb, rwt, rb, tri)


def _dispatch_kernel(dest_hbm, x_ref, zero_hbm, xs_hbm, dsm, sem_idx, sem):
    del zero_hbm
    i = pl.program_id(0)
    tm = x_ref.shape[0]
    idx_copy = pltpu.make_async_copy(dest_hbm.at[i], dsm, sem_idx)
    idx_copy.start()
    idx_copy.wait()

    def row_copy(r, k):
        return pltpu.make_async_copy(x_ref.at[pl.ds(r, 1)], xs_hbm.at[pl.ds(dsm[k * tm + r], 1)], sem)

    def start(r, c):
        for k in range(TOP_K):
            row_copy(r, k).start()
        return c

    def wait(r, c):
        for k in range(TOP_K):
            row_copy(r, k).wait()
        return c

    lax.fori_loop(0, tm, start, 0)
    lax.fori_loop(0, tm, wait, 0)


def _dispatch(dest_tiles, x, rows, tm):
    T = x.shape[0]
    zeros = jnp.zeros((rows, D_MODEL), F32)
    return pl.pallas_call(
        _dispatch_kernel,
        grid=(T // tm,),
        in_specs=[pl.BlockSpec(memory_space=pl.ANY), pl.BlockSpec((tm, D_MODEL), lambda i: (i, 0)),
                  pl.BlockSpec(memory_space=pl.ANY)],
        out_specs=pl.BlockSpec(memory_space=pl.ANY),
        out_shape=jax.ShapeDtypeStruct((rows, D_MODEL), F32),
        scratch_shapes=[pltpu.SMEM((TOP_K * tm,), I32), pltpu.SemaphoreType.DMA(()), pltpu.SemaphoreType.DMA(())],
        input_output_aliases={2: 0},
        compiler_params=_cparams(("arbitrary",)),
        name="moe_dispatch",
    )(dest_tiles, x, zeros)


def _ffn_kernel(be_ref, nu_ref, xs_ref, wg_ref, wu_ref, wd_ref, y_ref, h_ref):
    del be_ref
    i = pl.program_id(0)
    half = D_EXPERT // 2

    @pl.when(i < nu_ref[0])
    def _():
        xb = xs_ref[...].astype(BF16)
        for c in range(2):
            cs = slice(c * half, (c + 1) * half)
            gt = jnp.dot(xb, wg_ref[:, cs], preferred_element_type=F32)
            up = jnp.dot(xb, wu_ref[:, cs], preferred_element_type=F32)
            h_ref[:, cs] = (jax.nn.silu(gt) * up).astype(BF16)
        y_ref[...] = jnp.dot(h_ref[...], wd_ref[...], preferred_element_type=F32)

    @pl.when(i >= nu_ref[0])
    def _():
        y_ref[...] = jnp.zeros_like(y_ref)


def _ffn(blk_exp, n_used, xs, wg, wu, wd):
    rows = xs.shape[0]
    wspec = lambda: pl.BlockSpec((None, D_MODEL, D_EXPERT), lambda i, be, nu: (be[i], 0, 0))
    return pl.pallas_call(
        _ffn_kernel,
        grid_spec=pltpu.PrefetchScalarGridSpec(
            num_scalar_prefetch=2,
            grid=(rows // MOE_BLOCK,),
            in_specs=[pl.BlockSpec((MOE_BLOCK, D_MODEL), lambda i, be, nu: (i, 0)), wspec(), wspec(),
                      pl.BlockSpec((None, D_EXPERT, D_MODEL), lambda i, be, nu: (be[i], 0, 0))],
            out_specs=pl.BlockSpec((MOE_BLOCK, D_MODEL), lambda i, be, nu: (i, 0)),
            scratch_shapes=[pltpu.VMEM((MOE_BLOCK, D_EXPERT), BF16)]),
        out_shape=jax.ShapeDtypeStruct((rows, D_MODEL), F32),
        compiler_params=_cparams(("arbitrary",)),
        name="moe_ffn",
    )(blk_exp, n_used, xs, wg, wu, wd)


def _combine_kernel(dest_hbm, y_hbm, x_ref, wt_ref, lng_ref, lnb_ref, o_ref, dsm, buf, sem_idx, sem):
    i = pl.program_id(0)
    tm = x_ref.shape[0]
    idx_copy = pltpu.make_async_copy(dest_hbm.at[i], dsm, sem_idx)
    idx_copy.start()
    idx_copy.wait()

    def row_copy(r, k):
        return pltpu.make_async_copy(y_hbm.at[pl.ds(dsm[k * tm + r], 1)], buf.at[k, pl.ds(r, 1)], sem)

    def start(r, c):
        for k in range(TOP_K):
            row_copy(r, k).start()
        return c

    def wait(r, c):
        for k in range(TOP_K):
            row_copy(r, k).wait()
        return c

    lax.fori_loop(0, tm, start, 0)
    lax.fori_loop(0, tm, wait, 0)
    wt = wt_ref[...]
    z = ALPHA * x_ref[...] + wt[:, 0:1] * buf[0] + wt[:, 1:2] * buf[1]
    o_ref[...] = _layer_norm_rows(z, lng_ref[...], lnb_ref[...])


def _combine(dest_tiles, y, x, wt, lng, lnb, tm):
    T = x.shape[0]
    row = lambda i: (i, 0)
    vec = pl.BlockSpec((1, D_MODEL), lambda i: (0, 0))
    return pl.pallas_call(
        _combine_kernel,
        grid=(T // tm,),
        in_specs=[pl.BlockSpec(memory_space=pl.ANY), pl.BlockSpec(memory_space=pl.ANY),
                  pl.BlockSpec((tm, D_MODEL), row), pl.BlockSpec((tm, TOP_K), row), vec, vec],
        out_specs=pl.BlockSpec((tm, D_MODEL), row),
        out_shape=jax.ShapeDtypeStruct((T, D_MODEL), F32),
        scratch_shapes=[pltpu.SMEM((TOP_K * tm,), I32), pltpu.VMEM((TOP_K, tm, D_MODEL), F32),
                        pltpu.SemaphoreType.DMA(()), pltpu.SemaphoreType.DMA(())],
        compiler_params=_cparams(("arbitrary",)),
        name="moe_combine",
    )(dest_tiles, y, x, wt, lng, lnb)


def _moe(x1, e, wt, pos, cnt, wg, wu, wd, lng, lnb, tm=256):
    T = x1.shape[0]
    n_blocks = -(-(T * TOP_K) // MOE_BLOCK) + N_EXPERTS
    rows = n_blocks * MOE_BLOCK
    counts = cnt[:, 0].astype(I32)
    padded = (counts + MOE_BLOCK - 1) // MOE_BLOCK * MOE_BLOCK
    pad_end = jnp.cumsum(padded)
    pad_start = pad_end - padded
    dest = pad_start[e] + pos
    blk_start = jnp.arange(n_blocks, dtype=I32) * MOE_BLOCK
    blk_exp = jnp.minimum(jnp.sum((pad_end[None, :] <= blk_start[:, None]).astype(I32), axis=1), N_EXPERTS - 1)
    n_used = (pad_end[-1:] // MOE_BLOCK).astype(I32)
    dest_tiles = dest.reshape(TOP_K, T // tm, tm).transpose(1, 0, 2).reshape(T // tm, TOP_K * tm)
    xs = _dispatch(dest_tiles, x1, rows, tm)
    y = _ffn(blk_exp, n_used, xs, wg, wu, wd)
    return _combine(dest_tiles, y, x1, wt.T, lng, lnb, tm)


def _proj_rt_kernel(x_ref, w_ref, c_ref, s_ref, q_ref, k_ref, v_ref, g_ref):
    xb = x_ref[...].astype(BF16)
    cs, sn = c_ref[...], s_ref[...]
    rope = lambda y: y * cs + pltpu.roll(y, RT_QK // 2, 1) * sn
    for c in range(RT_HEADS):
        cols = slice(c * RT_QK, (c + 1) * RT_QK)
        q_ref[:, cols] = rope(jnp.dot(xb, w_ref[:, cols], preferred_element_type=F32)).astype(BF16)
    for c in range(RT_HEADS):
        cols = slice(c * RT_QK, (c + 1) * RT_QK)
        yk = jnp.dot(xb, w_ref[:, RT_QKW + c * RT_QK:RT_QKW + (c + 1) * RT_QK], preferred_element_type=F32)
        k_ref[:, cols] = (rope(yk) * RT_QK ** -0.5).astype(BF16)
    step = 1024
    for c in range(RT_VW // step):
        cols = slice(c * step, (c + 1) * step)
        v_ref[:, cols] = jnp.dot(xb, w_ref[:, 2 * RT_QKW + c * step:2 * RT_QKW + (c + 1) * step],
                                 preferred_element_type=F32).astype(BF16)
        base = 2 * RT_QKW + RT_VW
        g_ref[:, cols] = jax.nn.silu(jnp.dot(xb, w_ref[:, base + c * step:base + (c + 1) * step],
                                             preferred_element_type=F32))


def _proj_rt(xt, w, cs, sn, S, tm=256):
    T = xt.shape[0]
    nseq = S // tm
    row = lambda i: (i, 0)
    tab = lambda i: (i % nseq, 0)
    return pl.pallas_call(
        _proj_rt_kernel,
        grid=(T // tm,),
        in_specs=[pl.BlockSpec((tm, D_MODEL), row), pl.BlockSpec(w.shape, lambda i: (0, 0)),
                  pl.BlockSpec((tm, RT_QK), tab), pl.BlockSpec((tm, RT_QK), tab)],
        out_specs=[pl.BlockSpec((tm, RT_QKW), row), pl.BlockSpec((tm, RT_QKW), row),
                   pl.BlockSpec((tm, RT_VW), row), pl.BlockSpec((tm, RT_VW), row)],
        out_shape=[jax.ShapeDtypeStruct((T, RT_QKW), BF16), jax.ShapeDtypeStruct((T, RT_QKW), BF16),
                   jax.ShapeDtypeStruct((T, RT_VW), BF16), jax.ShapeDtypeStruct((T, RT_VW), F32)],
        compiler_params=_cparams(("parallel",)),
        name="proj_rt",
    )(xt, w, cs, sn)


def _retention_kernel(q_ref, k_ref, v_ref, sg_ref, dec_ref, qd_ref, kd_ref, cd_ref, gng_ref, gnb_ref, o_ref, state):
    j = pl.program_id(2)

    @pl.when(j == 0)
    def _():
        state[...] = jnp.zeros_like(state)

    q, k, v = q_ref[...], k_ref[...], v_ref[...]
    att = _dot_nt(q, k) * dec_ref[...]
    inner = jnp.dot(att.astype(BF16), v, preferred_element_type=F32)
    r_old = state[...]
    cross = jnp.dot(q, r_old.astype(BF16), preferred_element_type=F32) * qd_ref[...]
    kt = (k.astype(F32) * kd_ref[...]).T.astype(BF16)
    state[...] = r_old * cd_ref[:, 0:1] + jnp.dot(kt, v, preferred_element_type=F32)
    o = inner + cross
    mu = jnp.mean(o, axis=1, keepdims=True)
    oc = o - mu
    var = jnp.mean(oc * oc, axis=1, keepdims=True)
    on = oc * lax.rsqrt(var + RT_GN_EPS) * gng_ref[...] + gnb_ref[...]
    o_ref[...] = (sg_ref[...] * on).astype(BF16)


def _retention(q3, k3, v3, sg3, dec, qd, kd, cd, gng, gnb):
    B, S, _ = q3.shape
    C = RT_CHUNK
    qk = pl.BlockSpec((None, C, RT_QK), lambda b, h, j: (b, j, h))
    vv = pl.BlockSpec((None, C, RT_V), lambda b, h, j: (b, j, h))
    return pl.pallas_call(
        _retention_kernel,
        grid=(B, RT_HEADS, S // C),
        in_specs=[qk, qk, vv, vv,
                  pl.BlockSpec((None, C, C), lambda b, h, j: (h, 0, 0)),
                  pl.BlockSpec((None, C, 1), lambda b, h, j: (h, 0, 0)),
                  pl.BlockSpec((None, C, 1), lambda b, h, j: (h, 0, 0)),
                  pl.BlockSpec((None, 1, LANES), lambda b, h, j: (h, 0, 0)),
                  pl.BlockSpec((1, RT_V), lambda b, h, j: (0, h)),
                  pl.BlockSpec((1, RT_V), lambda b, h, j: (0, h))],
        out_specs=vv,
        out_shape=jax.ShapeDtypeStruct((B, S, RT_VW), BF16),
        scratch_shapes=[pltpu.VMEM((RT_QK, RT_V), F32)],
        compiler_params=_cparams(("parallel", "parallel", "arbitrary")),
        name="retention",
    )(q3, k3, v3, sg3, dec, qd, kd, cd, gng, gnb)


def _nsa_rope_tables(S):
    half = ROPE_DIM // 2
    inv = ROPE_THETA ** (-jnp.arange(half, dtype=F32) / half)
    ang = jnp.arange(S, dtype=F32)[:, None] * inv[None, :]
    cos, sin = jnp.cos(ang), jnp.sin(ang)
    zeros = lambda n: jnp.zeros((S, n), F32)
    cn = jnp.concatenate([cos, cos, jnp.ones((S, NS_HEAD - ROPE_DIM), F32)], axis=1)
    s1 = jnp.concatenate([-sin, zeros(NS_HEAD - half)], axis=1)
    s2 = jnp.concatenate([zeros(half), sin, zeros(NS_HEAD - ROPE_DIM)], axis=1)
    two = lambda a: jnp.concatenate([a, a], axis=1)
    return two(cn), two(s1), two(s2)


def _rt_rope_tables(S):
    inv = RT_THETA ** (-jnp.linspace(0.0, 1.0, RT_QK // 2, dtype=F32))
    ang = jnp.arange(S, dtype=F32)[:, None] * inv[None, :]
    cos, sin = jnp.cos(ang), jnp.sin(ang)
    return jnp.concatenate([cos, cos], axis=1), jnp.concatenate([-sin, sin], axis=1)


def _rt_decay_tables():
    log_g = jnp.log(1.0 - 2.0 ** (-5.0 - jnp.arange(RT_HEADS, dtype=F32)))
    idx = jnp.arange(RT_CHUNK, dtype=F32)
    diff = idx[:, None] - idx[None, :]
    dec = jnp.where(diff >= 0, jnp.exp(jnp.maximum(diff, 0.0) * log_g[:, None, None]), 0.0)
    qd = jnp.exp((idx + 1.0) * log_g[:, None])[..., None]
    kd = jnp.exp((RT_CHUNK - 1.0 - idx) * log_g[:, None])[..., None]
    cd = jnp.broadcast_to(jnp.exp(RT_CHUNK * log_g)[:, None, None], (RT_HEADS, 1, LANES))
    return dec, qd, kd, cd


def _overlap_table(S, ncp):
    n_cmp = (S - CMP_LEN) // CMP_STRIDE + 1
    n_sel = S // SEL_BLOCK
    cs = jnp.arange(ncp) * CMP_STRIDE
    ss = jnp.arange(LANES) * SEL_BLOCK
    ov = jnp.clip(jnp.minimum(cs[:, None] + CMP_LEN, ss[None, :] + SEL_BLOCK)
                  - jnp.maximum(cs[:, None], ss[None, :]), 0, None).astype(F32) / CMP_LEN
    keep = (jnp.arange(ncp)[:, None] < n_cmp) & (jnp.arange(LANES)[None, :] < n_sel)
    return jnp.where(keep, ov, 0.0)


def kernel(x, ab_w_in, ab_w_out, rk_mu, rk_w0, rk_w1, rk_w2, rk_a0, rk_a1, rk_a2, rk_g1, rk_g2, rk_kk, rk_ka, rk_rk,
           rk_ln, ns_pe, ns_c_w1, ns_c_w2, rt_w_in, rt_w_out, rt_gn, router_w, router_b, moe_w_gate, moe_w_up,
           moe_w_down, ln):
    B, S, D = x.shape
    T = B * S
    assert D == D_MODEL and S % 256 == 0 and S // SEL_BLOCK <= LANES and S >= WINDOW
    xt = x.reshape(T, D)
    rwt = router_w.T
    rb = router_b.reshape(N_EXPERTS, 1)
    vec = lambda a: a.reshape(1, -1)

    w_in = ab_w_in[0]
    n_gate = 3 * NS_HEADS
    w_cat = jnp.concatenate([w_in[:, :-n_gate], jnp.pad(w_in[:, -n_gate:], ((0, 0), (0, LANES - n_gate)))],
                            axis=1).astype(BF16)
    cn, s1, s2 = _nsa_rope_tables(S)
    prk, q, kv, gt = _proj_ab(xt, w_cat, cn, s1, s2, S)

    ones = (jnp.arange(RK_W)[:, None] // RK_HEAD == jnp.arange(RK_W)[None, :] // RK_HEAD).astype(F32)
    b16 = lambda a: a.astype(BF16)
    r, w, km, v, nkk, bb, g, bon = _rwkv_prep(
        prk.reshape(B, S, 4 * RK_W), rk_mu[0], vec(rk_w0[0]), b16(rk_w1[0]), b16(rk_w2[0]), vec(rk_a0[0]),
        b16(rk_a1[0]), b16(rk_a2[0]), b16(rk_g1[0]), b16(rk_g2[0]), vec(rk_kk[0]), vec(rk_ka[0]), vec(rk_rk[0]), ones)
    o_a = _rwkv_scan(r, w, km, v, nkk, bb, g, bon, rk_ln[0, 0:1], rk_ln[0, 1:2])

    ncp = S // CMP_STRIDE
    kv3 = kv.reshape(B, S, 6 * LANES)
    x16 = (kv3[:, :, :2 * LANES].reshape(B, ncp, CMP_STRIDE, 2, NS_KV, NS_HEAD)
           .transpose(0, 3, 4, 1, 2, 5).reshape(B, 2, NS_KV, ncp, CMP_STRIDE * NS_HEAD))
    kcv = _nsa_compress(x16, b16(ns_c_w1[0]), ns_c_w1[0], b16(ns_c_w2[0]),
                        ns_pe[0].reshape(2, 1, CMP_LEN * NS_HEAD))
    o_b = _nsa_attn(q.reshape(B, S, NS_W), kcv, kv3, gt.reshape(B, S, LANES), _overlap_table(S, ncp))

    w_out = b16(ab_w_out[0])
    x1, e, wt, pos, cnt = _outproj_router([o_a.reshape(T, RK_W), o_b.reshape(T, NS_W)], [w_out[:RK_W], w_out[RK_W:]],
                                          xt, ln[0, 0, 0:1], ln[0, 0, 1:2], rwt, rb)
    x2 = _moe(x1, e, wt, pos, cnt, b16(moe_w_gate[0]), b16(moe_w_up[0]), b16(moe_w_down[0]),
              ln[0, 1, 0:1], ln[0, 1, 1:2])

    cs, sn = _rt_rope_tables(S)
    qr, kr, vr, sg = _proj_rt(x2, b16(rt_w_in[0]), cs, sn, S)
    dec, qd, kd, cd = _rt_decay_tables()
    ret = _retention(qr.reshape(B, S, RT_QKW), kr.reshape(B, S, RT_QKW), vr.reshape(B, S, RT_VW),
                     sg.reshape(B, S, RT_VW), dec, qd, kd, cd, rt_gn[0, 0:1], rt_gn[0, 1:2])
    x3, e, wt, pos, cnt = _outproj_router([ret.reshape(T, RT_VW)], [b16(rt_w_out[0])], x2,
                                          ln[1, 0, 0:1], ln[1, 0, 1:2], rwt, rb)
    x4 = _moe(x3, e, wt, pos, cnt, b16(moe_w_gate[1]), b16(moe_w_up[1]), b16(moe_w_down[1]),
              ln[1, 1, 0:1], ln[1, 1, 1:2])
    return x4.reshape(B, S, D)
```

```python
import functools
import math

import jax
import jax.numpy as jnp
from jax import lax
from jax.experimental import pallas as pl
from jax.experimental.pallas import tpu as pltpu

F32 = jnp.float32
BF16 = jnp.bfloat16
I32 = jnp.int32
HI = lax.Precision.HIGHEST

LANES = 128
VMEM_LIMIT = 56 * 1024 * 1024

D_MODEL = 1024
RK_HEADS, RK_HEAD = 8, 64
RK_W = RK_HEADS * RK_HEAD
RK_DECAY_SCALE = 0.606531
RK_LN_EPS = 64e-5
NS_HEADS, NS_KV, NS_HPG, NS_HEAD = 8, 2, 4, 64
NS_W = NS_HEADS * NS_HEAD
CMP_LEN, CMP_STRIDE, SEL_BLOCK, SEL_TOPK, WINDOW = 32, 16, 64, 16, 512
ROPE_THETA = 500000.0
ROPE_DIM = NS_HEAD // 4
QT = 128
KT = 512
RT_HEADS, RT_QK, RT_V = 8, 128, 256
RT_QKW, RT_VW = RT_HEADS * RT_QK, RT_HEADS * RT_V
RT_CHUNK = 128
RT_THETA = 10000.0
RT_GN_EPS = 1e-5
N_EXPERTS, N_GROUPS, EXP_PER_GROUP, TOP_K = 16, 4, 4, 2
D_EXPERT = 1024
MOE_BLOCK = 512
DEPTH = 2
ALPHA = (2.0 * DEPTH) ** 0.25
LN_EPS = 1e-5
NEG = -1e30


def _cparams(sem):
    return pltpu.CompilerParams(dimension_semantics=sem, vmem_limit_bytes=VMEM_LIMIT)


def _bdot(a, w):
    return jnp.dot(a.astype(BF16), w, preferred_element_type=F32)


def _dot_nt(a, b):
    return lax.dot_general(a, b, (((1,), (1,)), ((), ())), preferred_element_type=F32)


def _layer_norm_rows(z, g, b):
    mu = jnp.mean(z, axis=1, keepdims=True)
    zc = z - mu
    var = jnp.mean(zc * zc, axis=1, keepdims=True)
    return zc * lax.rsqrt(var + LN_EPS) * g + b


def _proj_ab_kernel(x_ref, w_ref, cn_ref, s1_ref, s2_ref, prk_ref, q_ref, kv_ref, gt_ref):
    xb = x_ref[...].astype(BF16)
    for c in range(4):
        prk_ref[:, c * RK_W:(c + 1) * RK_W] = jnp.dot(xb, w_ref[:, c * RK_W:(c + 1) * RK_W],
                                                      preferred_element_type=F32)
    cn, s1, s2 = cn_ref[...], s1_ref[...], s2_ref[...]

    def rope(y):
        return y * cn + pltpu.roll(y, LANES - ROPE_DIM // 2, 1) * s1 + pltpu.roll(y, ROPE_DIM // 2, 1) * s2

    base = 4 * RK_W
    yq = jnp.dot(xb, w_ref[:, base:base + NS_W], preferred_element_type=F32)
    for c in range(NS_W // LANES):
        q_ref[:, c * LANES:(c + 1) * LANES] = (rope(yq[:, c * LANES:(c + 1) * LANES]) * NS_HEAD ** -0.5).astype(BF16)
    base += NS_W
    ykv = jnp.dot(xb, w_ref[:, base:base + 6 * LANES], preferred_element_type=F32)
    for c in range(6):
        y = ykv[:, c * LANES:(c + 1) * LANES]
        if c % 2 == 0:
            y = rope(y)
        kv_ref[:, c * LANES:(c + 1) * LANES] = y.astype(BF16)
    base += 6 * LANES
    gt_ref[...] = jax.nn.sigmoid(jnp.dot(xb, w_ref[:, base:base + LANES], preferred_element_type=F32))


def _proj_ab(xt, w, cn, s1, s2, S, tm=256):
    T = xt.shape[0]
    ncols = w.shape[1]
    nseq = S // tm
    row = lambda i: (i, 0)
    tab = lambda i: (i % nseq, 0)
    return pl.pallas_call(
        _proj_ab_kernel,
        grid=(T // tm,),
        in_specs=[pl.BlockSpec((tm, D_MODEL), row),
                  pl.BlockSpec((D_MODEL, ncols), lambda i: (0, 0)),
                  pl.BlockSpec((tm, LANES), tab), pl.BlockSpec((tm, LANES), tab), pl.BlockSpec((tm, LANES), tab)],
        out_specs=[pl.BlockSpec((tm, 4 * RK_W), row), pl.BlockSpec((tm, NS_W), row),
                   pl.BlockSpec((tm, 6 * LANES), row), pl.BlockSpec((tm, LANES), row)],
        out_shape=[jax.ShapeDtypeStruct((T, 4 * RK_W), F32), jax.ShapeDtypeStruct((T, NS_W), BF16),
                   jax.ShapeDtypeStruct((T, 6 * LANES), BF16), jax.ShapeDtypeStruct((T, LANES), F32)],
        compiler_params=_cparams(("parallel",)),
        name="proj_ab",
    )(xt, w, cn, s1, s2)


def _rwkv_prep_kernel(p_ref, mu_ref, w0_ref, w1_ref, w2_ref, a0_ref, a1_ref, a2_ref, g1_ref, g2_ref,
                      kk_ref, ka_ref, rk_ref, ones_ref,
                      r_o, w_o, k_o, v_o, kk_o, b_o, g_o, bon_o, carry):
    j = pl.program_id(1)
    p = p_ref[...]
    tm = p.shape[0]

    @pl.when(j == 0)
    def _():
        carry[...] = jnp.zeros_like(carry)

    rowi = lax.broadcasted_iota(I32, p.shape, 0)
    prev = jnp.where(rowi == 0, carry[...], pltpu.roll(p, 1, 0))
    carry[...] = p[tm - 1:tm, :]
    dp = prev - p
    sl = lambda a, c: a[:, c * RK_W:(c + 1) * RK_W]
    mu = mu_ref[...]
    r = sl(p, 0) + sl(dp, 0) * mu[0:1]
    k = sl(p, 1) + sl(dp, 1) * mu[1:2]
    v = sl(p, 2) + sl(dp, 2) * mu[2:3]
    xw = sl(p, 3) + sl(dp, 3) * mu[3:4]
    xa = sl(p, 3) + sl(dp, 3) * mu[4:5]
    xg = sl(p, 3) + sl(dp, 3) * mu[5:6]
    lw = -RK_DECAY_SCALE * jax.nn.sigmoid(w0_ref[...] + _bdot(jnp.tanh(_bdot(xw, w1_ref[...])), w2_ref[...]))
    a = jax.nn.sigmoid(a0_ref[...] + _bdot(_bdot(xa, a1_ref[...]), a2_ref[...]))
    g = _bdot(jax.nn.sigmoid(_bdot(xg, g1_ref[...])), g2_ref[...])
    ones = ones_ref[...]
    kk = k * kk_ref[...]
    ss = jnp.dot(kk * kk, ones, precision=HI, preferred_element_type=F32)
    kk = kk / jnp.maximum(jnp.sqrt(ss), 1e-12)
    km = k * (1.0 + (a - 1.0) * ka_ref[...])
    bon = jnp.dot(r * km * rk_ref[...], ones, precision=HI, preferred_element_type=F32) * v
    r_o[...] = r
    w_o[...] = lw
    k_o[...] = km
    v_o[...] = v
    kk_o[...] = kk
    b_o[...] = kk * a
    g_o[...] = g
    bon_o[...] = bon


def _rwkv_prep(prk3, mu, w0, w1, w2, a0, a1, a2, g1, g2, k_k, k_a, r_k, ones, tm=256):
    B, S, _ = prk3.shape
    full = lambda a: pl.BlockSpec(a.shape, lambda b, j: (0,) * a.ndim)
    params = [mu, w0, w1, w2, a0, a1, a2, g1, g2, k_k, k_a, r_k, ones]
    ospec = pl.BlockSpec((None, tm, RK_W), lambda b, j: (b, j, 0))
    return pl.pallas_call(
        _rwkv_prep_kernel,
        grid=(B, S // tm),
        in_specs=[pl.BlockSpec((None, tm, 4 * RK_W), lambda b, j: (b, j, 0))] + [full(a) for a in params],
        out_specs=[ospec] * 8,
        out_shape=[jax.ShapeDtypeStruct((B, S, RK_W), F32)] * 8,
        scratch_shapes=[pltpu.VMEM((1, 4 * RK_W), F32)],
        compiler_params=_cparams(("parallel", "arbitrary")),
        name="rwkv_prep",
    )(prk3, *params)


RK_CHUNK = 16


def _pdot(a, b, dims, precise):
    if precise:
        return lax.dot_general(a, b, (dims, ((), ())), precision=HI, preferred_element_type=F32)
    return lax.dot_general(a.astype(BF16), b.astype(BF16), (dims, ((), ())), preferred_element_type=F32)


def _rwkv_chunk_kernel(r_ref, lw_ref, k_ref, v_ref, kk_ref, b_ref, g_ref, bon_ref, lng_ref, lnb_ref, o_ref, ht,
                       *, precise):
    j = pl.program_id(1)

    @pl.when(j == 0)
    def _():
        ht[...] = jnp.zeros_like(ht)

    TT = r_ref.shape[0]
    C, N = RK_CHUNK, RK_HEAD
    mm = lambda a, b: _pdot(a, b, ((1,), (0,)), precise)
    mm_nt = lambda a, b: _pdot(a, b, ((1,), (1,)), precise)
    mm_tn = lambda a, b: _pdot(a, b, ((0,), (0,)), precise)

    lw = lw_ref[...]
    rowc = lax.broadcasted_iota(I32, lw.shape, 0) & (C - 1)
    linc, lrev = lw, lw
    sh = 1
    while sh < C:
        linc = linc + jnp.where(rowc >= sh, pltpu.roll(linc, sh, 0), 0.0)
        lrev = lrev + jnp.where(rowc < C - sh, pltpu.roll(lrev, TT - sh, 0), 0.0)
        sh *= 2
    lrev = lrev - lw
    r, k, v, kk, b = r_ref[...], k_ref[...], v_ref[...], kk_ref[...], b_ref[...]
    e_in, e_inv, e_rev = jnp.exp(linc), jnp.exp(-linc), jnp.exp(lrev)
    kkd = kk * jnp.exp(linc - lw)
    rd = r * e_in
    binv, kinv = b * e_inv, k * e_inv
    bd, kd = b * e_rev, k * e_rev
    gam = jnp.exp(linc + lrev)

    ti = lax.broadcasted_iota(I32, (TT, TT), 0)
    tj = lax.broadcasted_iota(I32, (TT, TT), 1)
    same = (ti // C) == (tj // C)
    strict = same & (tj < ti)
    incl = same & (tj <= ti)

    H = range(RK_HEADS)
    lo = lambda a: a if precise else a.astype(BF16)
    hs = lambda a: [a[:, h * N:(h + 1) * N] for h in H]
    rows2 = lambda a, b: jnp.concatenate([a, b], axis=0)
    kkd_h, rd_h, v_h = hs(lo(kkd)), hs(lo(rd)), hs(lo(v))
    binv_h, kinv_h, bd_h, kd_h = hs(lo(binv)), hs(lo(kinv)), hs(lo(bd)), hs(lo(kd))
    gam_h = hs(gam)
    gm = [mm_nt(rows2(kkd_h[h], rd_h[h]), rows2(binv_h[h], kinv_h[h])) for h in H]
    a_b = [lo(jnp.where(strict, gm[h][:TT, :TT], 0.0)) for h in H]
    b_rb = [lo(jnp.where(incl, gm[h][TT:, :TT], 0.0)) for h in H]
    akb = [lo(rows2(jnp.where(strict, gm[h][:TT, TT:], 0.0), jnp.where(incl, gm[h][TT:, TT:], 0.0))) for h in H]
    av = [mm(akb[h], v_h[h]) for h in H]
    x = [jnp.concatenate([kkd_h[h].astype(F32), av[h][:TT]], axis=1) for h in H]
    a2 = [lo(mm(a_b[h], a_b[h])) for h in H]
    a4 = [lo(mm(a2[h], a2[h])) for h in H]
    a8 = [lo(mm(a4[h], a4[h])) for h in H]
    x = [x[h] + mm(a8[h], lo(x[h])) for h in H]
    x = [x[h] + mm(a4[h], lo(x[h])) for h in H]
    x = [x[h] + mm(a2[h], lo(x[h])) for h in H]
    x = [x[h] - mm(a_b[h], lo(x[h])) for h in H]
    wt = [lo(x[h][:, :N]) for h in H]
    h_t = [ht[h] for h in H]
    us = [[] for _ in H]
    rhs = [[] for _ in H]
    for c in range(TT // C):
        rs = slice(c * C, (c + 1) * C)
        xh = [mm_nt(rows2(wt[h][rs], rd_h[h][rs]), lo(h_t[h])) for h in H]
        for h in H:
            u_c = -(xh[h][:C] + x[h][rs, N:])
            us[h].append(u_c)
            rhs[h].append(xh[h][C:])
        upd = [mm_tn(rows2(lo(us[h][c]), v_h[h][rs]), rows2(bd_h[h][rs], kd_h[h][rs])) for h in H]
        h_t = [h_t[h] * gam_h[h][c * C:c * C + 1] + upd[h] for h in H]
    outs = []
    for h in H:
        ht[h] = h_t[h]
        o = jnp.concatenate(rhs[h], axis=0) + mm(b_rb[h], lo(jnp.concatenate(us[h], axis=0))) + av[h][TT:]
        mu = jnp.mean(o, axis=1, keepdims=True)
        oc = o - mu
        var = jnp.mean(oc * oc, axis=1, keepdims=True)
        outs.append(oc * lax.rsqrt(var + RK_LN_EPS))
    on = jnp.concatenate(outs, axis=1)
    o_ref[...] = (on * lng_ref[...] + lnb_ref[...] + bon_ref[...]) * g_ref[...]


def _rwkv_chunk(r, lw, k, v, kk, bb, g, bon, lng, lnb, tt=128, precise=False):
    B, S, _ = r.shape
    blk = pl.BlockSpec((None, tt, RK_W), lambda b, j: (b, j, 0))
    vec = pl.BlockSpec((1, RK_W), lambda b, j: (0, 0))
    return pl.pallas_call(
        functools.partial(_rwkv_chunk_kernel, precise=precise),
        grid=(B, S // tt),
        in_specs=[blk] * 8 + [vec, vec],
        out_specs=blk,
        out_shape=jax.ShapeDtypeStruct((B, S, RK_W), F32),
        scratch_shapes=[pltpu.VMEM((RK_HEADS, RK_HEAD, RK_HEAD), F32)],
        compiler_params=_cparams(("parallel", "arbitrary")),
        name="rwkv_chunk",
    )(r, lw, k, v, kk, bb, g, bon, lng, lnb)


def _nsa_compress_kernel(x_ref, w1b_ref, w1f_ref, w2_ref, pe_ref, o_ref):
    half = CMP_STRIDE * NS_HEAD
    w1b = w1b_ref[...]
    bias = jnp.dot(jnp.broadcast_to(pe_ref[...], (8, CMP_LEN * NS_HEAD)), w1f_ref[...], precision=HI,
                   preferred_element_type=F32)[0:1]
    outs = []
    for g in range(NS_KV):
        x = x_ref[g]
        ya = jnp.dot(x, w1b[:half], preferred_element_type=F32)
        yb = jnp.dot(x, w1b[half:], preferred_element_type=F32)
        n = x.shape[0]
        h = ya + pltpu.roll(yb, n - 1, 0) + bias
        outs.append(_bdot(jax.nn.gelu(h), w2_ref[...]))
    o_ref[...] = jnp.concatenate(outs, axis=1).astype(BF16)


def _nsa_compress(x16, w1b, w1f, w2b, pe):
    B, _, _, ncp, width = x16.shape
    hid = w1b.shape[-1]
    return pl.pallas_call(
        _nsa_compress_kernel,
        grid=(B, 2),
        in_specs=[pl.BlockSpec((None, None, NS_KV, ncp, width), lambda b, c: (b, c, 0, 0, 0)),
                  pl.BlockSpec((None, 2 * width, hid), lambda b, c: (c, 0, 0)),
                  pl.BlockSpec((None, 2 * width, hid), lambda b, c: (c, 0, 0)),
                  pl.BlockSpec((None, hid, NS_HEAD), lambda b, c: (c, 0, 0)),
                  pl.BlockSpec((None, 1, 2 * width), lambda b, c: (c, 0, 0))],
        out_specs=pl.BlockSpec((None, None, ncp, LANES), lambda b, c: (b, c, 0, 0)),
        out_shape=jax.ShapeDtypeStruct((B, 2, ncp, LANES), BF16),
        compiler_params=_cparams(("parallel", "parallel")),
        name="nsa_compress",
    )(x16, w1b, w1f, w2b, pe)


def _nsa_attn_kernel(q_ref, kc_ref, vc_ref, ks_ref, vs_ref, kw_ref, vw_ref, gt_ref, ov_ref, ex_ref, o_ref,
                     m_s, l_s, acc_s, m_w, l_w, acc_w, *, ncp):
    i = pl.program_id(1)
    g = pl.program_id(2)
    s0 = i * QT
    R = NS_HPG * QT

    q4 = q_ref[...]
    lane_q = lax.broadcasted_iota(I32, (QT, LANES), 1)
    mine = (lane_q // NS_HEAD) == g
    rows = []
    for h in range(NS_HPG):
        qh = q4[:, h * NS_HEAD:(h + 1) * NS_HEAD]
        rows.append(jnp.where(mine, jnp.concatenate([qh, qh], axis=1), jnp.zeros((), BF16)))
    q = jnp.concatenate(rows, axis=0)

    sc = _dot_nt(q, kc_ref[...])
    tqc = s0 + (lax.broadcasted_iota(I32, (R, ncp), 0) & (QT - 1))
    ncol = lax.broadcasted_iota(I32, (R, ncp), 1)
    cmask = (ncol * CMP_STRIDE + (CMP_LEN - 1) <= tqc) & (ncol < ncp - 1)
    sc = jnp.where(cmask, sc, NEG)
    mc = jnp.max(sc, axis=1, keepdims=True)
    pc = jnp.where(cmask, jnp.exp(sc - mc), 0.0)
    lc = jnp.sum(pc, axis=1, keepdims=True)
    pc = pc / jnp.where(lc > 0.0, lc, 1.0)
    o_c = jnp.dot(pc.astype(BF16), vc_ref[...], preferred_element_type=F32)

    pcs = pc[0:QT] + pc[QT:2 * QT] + pc[2 * QT:3 * QT] + pc[3 * QT:4 * QT]
    imp = jnp.dot(pcs, ov_ref[...], precision=HI, preferred_element_type=F32)
    blk = lax.broadcasted_iota(I32, (QT, LANES), 1)
    cur = (s0 + lax.broadcasted_iota(I32, (QT, LANES), 0)) // SEL_BLOCK
    valid = blk <= cur
    forced = (blk == 0) | (blk == cur) | (blk == cur - 1)
    pri = jnp.where(valid & ~forced, imp, -jnp.inf)
    picked = forced
    blkf = blk.astype(F32)
    for _ in range(SEL_TOPK - 3):
        mx = jnp.max(pri, axis=1, keepdims=True)
        hit = blkf == jnp.min(jnp.where(pri == mx, blkf, float(LANES)), axis=1, keepdims=True)
        picked = picked | hit
        pri = jnp.where(hit, -jnp.inf, pri)
    sel = jnp.where(picked & valid, 1.0, 0.0).astype(BF16)

    def attend(s4, vb, ok, m_ref, l_ref, acc_ref):
        chunks = range(s4.shape[1] // LANES)
        pen = jnp.where(ok, 0.0, NEG)
        for h in range(NS_HPG):
            rs = slice(h * QT, (h + 1) * QT)
            s = [s4[rs, c * LANES:(c + 1) * LANES] + pen[:, c * LANES:(c + 1) * LANES] for c in chunks]
            m_old = m_ref[rs, :]
            m_new = jnp.maximum(m_old, jnp.max(functools.reduce(jnp.maximum, s), axis=1, keepdims=True))
            alpha = jnp.exp(m_old - m_new)
            p = [jnp.exp(sc - m_new) for sc in s]
            l_ref[rs, :] = alpha * l_ref[rs, :] + functools.reduce(jnp.add, p)
            pv = jnp.dot(jnp.concatenate(p, axis=1).astype(BF16), vb, preferred_element_type=F32)
            acc_ref[rs, :] = alpha * acc_ref[rs, :] + pv
            m_ref[rs, :] = m_new

    def reset(m_ref, l_ref, acc_ref):
        m_ref[...] = jnp.full(m_ref.shape, NEG, F32)
        l_ref[...] = jnp.zeros_like(l_ref)
        acc_ref[...] = jnp.zeros_like(acc_ref)

    tqk = s0 + lax.broadcasted_iota(I32, (QT, KT), 0)
    colk = lax.broadcasted_iota(I32, (QT, KT), 1)

    reset(m_s, l_s, acc_s)
    diag = i // (KT // QT)

    def sel_tile(kt, causal):
        k0 = pl.multiple_of(kt * KT, KT)
        s4 = _dot_nt(q, ks_ref[pl.ds(k0, KT), :])
        ok = jnp.dot(sel, ex_ref[kt], preferred_element_type=F32) > 0.5
        if causal:
            ok = ok & (k0 + colk <= tqk)
        attend(s4, vs_ref[pl.ds(k0, KT), :], ok, m_s, l_s, acc_s)

    def sel_body(kt, c):
        sel_tile(kt, False)
        return c

    lax.fori_loop(0, diag, sel_body, 0)
    sel_tile(diag, True)

    reset(m_w, l_w, acc_w)
    d0 = pl.multiple_of(s0, QT)
    ok_d = lax.broadcasted_iota(I32, (QT, QT), 1) <= lax.broadcasted_iota(I32, (QT, QT), 0)
    attend(_dot_nt(q, kw_ref[pl.ds(d0, QT), :]), vw_ref[pl.ds(d0, QT), :], ok_d, m_w, l_w, acc_w)

    @pl.when(i > 0)
    def _():
        a0 = pl.multiple_of(jnp.maximum(s0 - WINDOW, 0), QT)
        kpos = a0 + lax.broadcasted_iota(I32, (QT, WINDOW), 1)
        ok_w = (kpos < s0) & (kpos > s0 + lax.broadcasted_iota(I32, (QT, WINDOW), 0) - WINDOW)
        attend(_dot_nt(q, kw_ref[pl.ds(a0, WINDOW), :]), vw_ref[pl.ds(a0, WINDOW), :], ok_w, m_w, l_w, acc_w)

    gt = gt_ref[...]
    gcol = lax.broadcasted_iota(I32, (QT, LANES), 1)
    gsel = (g == 0).astype(F32)
    heads = []
    for h in range(NS_HPG):
        rs = slice(h * QT, (h + 1) * QT)
        gate = lambda br: jnp.sum(jnp.where(gcol == br * NS_HEADS + g * NS_HPG + h, gt, 0.0), axis=1, keepdims=True)
        norm = lambda acc_ref, l_ref: acc_ref[rs, :] / jnp.sum(l_ref[rs, :], axis=1, keepdims=True)
        o = gate(0) * o_c[rs] + gate(1) * norm(acc_s, l_s) + gate(2) * norm(acc_w, l_w)
        heads.append(o[:, :NS_HEAD] * gsel + o[:, NS_HEAD:] * (1.0 - gsel))
    o_ref[...] = jnp.concatenate(heads, axis=1).astype(BF16)


def _nsa_attn(q3, kcv, kv3, gt3, ov):
    B, S, _ = q3.shape
    ncp = kcv.shape[2]
    R = NS_HPG * QT
    gw = NS_HPG * NS_HEAD
    key_blk = (jnp.arange(S, dtype=I32) // SEL_BLOCK).reshape(S // KT, 1, KT)
    ex = (key_blk == jnp.arange(LANES, dtype=I32)[None, :, None]).astype(BF16)
    seq = lambda c: pl.BlockSpec((None, S, LANES), lambda b, i, g, c=c: (b, 0, c))
    return pl.pallas_call(
        functools.partial(_nsa_attn_kernel, ncp=ncp),
        grid=(B, S // QT, NS_KV),
        in_specs=[pl.BlockSpec((None, QT, gw), lambda b, i, g: (b, i, g)),
                  pl.BlockSpec((None, None, ncp, LANES), lambda b, i, g: (b, 0, 0, 0)),
                  pl.BlockSpec((None, None, ncp, LANES), lambda b, i, g: (b, 1, 0, 0)),
                  seq(2), seq(3), seq(4), seq(5),
                  pl.BlockSpec((None, QT, LANES), lambda b, i, g: (b, i, 0)),
                  pl.BlockSpec(ov.shape, lambda b, i, g: (0, 0)),
                  pl.BlockSpec(ex.shape, lambda b, i, g: (0, 0, 0))],
        out_specs=pl.BlockSpec((None, QT, gw), lambda b, i, g: (b, i, g)),
        out_shape=jax.ShapeDtypeStruct((B, S, NS_W), BF16),
        scratch_shapes=[pltpu.VMEM((R, LANES), F32)] * 6,
        compiler_params=_cparams(("parallel", "parallel", "arbitrary")),
        name="nsa_attn",
    )(q3, kcv, kcv, kv3, kv3, kv3, kv3, gt3, ov, ex)


def _first_argmax(vals):
    m = vals[0]
    for v in vals[1:]:
        m = jnp.maximum(m, v)
    idx = jnp.full(m.shape, len(vals) - 1, I32)
    for j in range(len(vals) - 2, -1, -1):
        idx = jnp.where(vals[j] == m, j, idx)
    return m, idx


def _outproj_router_kernel(*refs, n_in):
    acts, ws = refs[:n_in], refs[n_in:2 * n_in]
    x_ref, lng_ref, lnb_ref, rwt_ref, rb_ref, tri_ref = refs[2 * n_in:2 * n_in + 6]
    y_ref, e_ref, wt_ref, pos_ref, cnt_ref, cnt = refs[2 * n_in + 6:]
    i = pl.program_id(0)

    @pl.when(i == 0)
    def _():
        cnt[...] = jnp.zeros_like(cnt)

    mix = _bdot(acts[0][...], ws[0][...])
    for a, w in zip(acts[1:], ws[1:]):
        mix = mix + _bdot(a[...], w[...])
    y = _layer_norm_rows(ALPHA * x_ref[...] + mix, lng_ref[...], lnb_ref[...])
    y_ref[...] = y

    logit = lax.dot_general(rwt_ref[...], y, (((1,), (1,)), ((), ())), precision=HI, preferred_element_type=F32)
    aff = jax.nn.sigmoid(logit)
    biased = aff + rb_ref[...]
    neg_inf = -jnp.inf
    g_score, g_i1, g_i2 = [], [], []
    for gi in range(N_GROUPS):
        vals = [biased[gi * EXP_PER_GROUP + j:gi * EXP_PER_GROUP + j + 1, :] for j in range(EXP_PER_GROUP)]
        m1, i1 = _first_argmax(vals)
        m2, i2 = _first_argmax([jnp.where(i1 == j, neg_inf, vals[j]) for j in range(EXP_PER_GROUP)])
        g_score.append(m1 + m2)
        g_i1.append(i1)
        g_i2.append(i2)
    _, grp = _first_argmax(g_score)
    loc1, loc2 = g_i1[-1], g_i2[-1]
    for gi in range(N_GROUPS - 2, -1, -1):
        loc1 = jnp.where(grp == gi, g_i1[gi], loc1)
        loc2 = jnp.where(grp == gi, g_i2[gi], loc2)
    e1 = grp * EXP_PER_GROUP + loc1
    e2 = grp * EXP_PER_GROUP + loc2
    eio = lax.broadcasted_iota(I32, aff.shape, 0)
    oh1 = eio == e1
    oh2 = eio == e2
    a1 = jnp.sum(jnp.where(oh1, aff, 0.0), axis=0, keepdims=True)
    a2 = jnp.sum(jnp.where(oh2, aff, 0.0), axis=0, keepdims=True)
    tot = a1 + a2
    e_ref[...] = jnp.concatenate([e1, e2], axis=0)
    wt_ref[...] = jnp.concatenate([a1 / tot, a2 / tot], axis=0)

    ohs = oh1.astype(F32) + oh2.astype(F32)
    before = jnp.dot(ohs.astype(BF16), tri_ref[...], preferred_element_type=F32) + cnt[...]
    p1 = jnp.sum(jnp.where(oh1, before, 0.0), axis=0, keepdims=True)
    p2 = jnp.sum(jnp.where(oh2, before, 0.0), axis=0, keepdims=True)
    pos_ref[...] = jnp.concatenate([p1, p2], axis=0).astype(I32)
    cnt[...] = cnt[...] + jnp.sum(ohs, axis=1, keepdims=True)
    cnt_ref[...] = jnp.broadcast_to(cnt[...], cnt_ref.shape)


def _outproj_router(acts, ws, xres, lng, lnb, rwt, rb, tm=256):
    T = xres.shape[0]
    n_in = len(acts)
    tri = (lax.broadcasted_iota(I32, (tm, tm), 0) < lax.broadcasted_iota(I32, (tm, tm), 1)).astype(BF16)
    row = lambda i: (i, 0)
    const = lambda a: pl.BlockSpec(a.shape, lambda i: (0,) * a.ndim)
    lane_blk = pl.BlockSpec((TOP_K, tm), lambda i: (0, i))
    return pl.pallas_call(
        functools.partial(_outproj_router_kernel, n_in=n_in),
        grid=(T // tm,),
        in_specs=([pl.BlockSpec((tm, a.shape[1]), row) for a in acts] + [const(w) for w in ws]
                  + [pl.BlockSpec((tm, D_MODEL), row), const(lng), const(lnb), const(rwt), const(rb), const(tri)]),
        out_specs=[pl.BlockSpec((tm, D_MODEL), row), lane_blk, lane_blk, lane_blk,
                   pl.BlockSpec((N_EXPERTS, LANES), lambda i: (0, 0))],
        out_shape=[jax.ShapeDtypeStruct((T, D_MODEL), F32), jax.ShapeDtypeStruct((TOP_K, T), I32),
                   jax.ShapeDtypeStruct((TOP_K, T), F32), jax.ShapeDtypeStruct((TOP_K, T), I32),
                   jax.ShapeDtypeStruct((N_EXPERTS, LANES), F32)],
        scratch_shapes=[pltpu.VMEM((N_EXPERTS, 1), F32)],
        compiler_params=_cparams(("arbitrary",)),
        name="outproj_router",
    )(*acts, *ws, xres, lng, lnb, rwt, rb, tri)


def _dispatch_kernel(dest_hbm, x_ref, zero_hbm, xs_hbm, dsm, sem_idx, sem):
    del zero_hbm
    i = pl.program_id(0)
    tm = x_ref.shape[0]
    idx_copy = pltpu.make_async_copy(dest_hbm.at[i], dsm, sem_idx)
    idx_copy.start()
    idx_copy.wait()

    def row_copy(r, k):
        return pltpu.make_async_copy(x_ref.at[pl.ds(r, 1)], xs_hbm.at[pl.ds(dsm[k * tm + r], 1)], sem)

    def start(r, c):
        for k in range(TOP_K):
            row_copy(r, k).start()
        return c

    def wait(r, c):
        for k in range(TOP_K):
            row_copy(r, k).wait()
        return c

    lax.fori_loop(0, tm, start, 0)
    lax.fori_loop(0, tm, wait, 0)


def _dispatch(dest_tiles, x, rows, tm):
    T = x.shape[0]
    zeros = jnp.zeros((rows, D_MODEL), F32)
    return pl.pallas_call(
        _dispatch_kernel,
        grid=(T // tm,),
        in_specs=[pl.BlockSpec(memory_space=pl.ANY), pl.BlockSpec((tm, D_MODEL), lambda i: (i, 0)),
                  pl.BlockSpec(memory_space=pl.ANY)],
        out_specs=pl.BlockSpec(memory_space=pl.ANY),
        out_shape=jax.ShapeDtypeStruct((rows, D_MODEL), F32),
        scratch_shapes=[pltpu.SMEM((TOP_K * tm,), I32), pltpu.SemaphoreType.DMA(()), pltpu.SemaphoreType.DMA(())],
        input_output_aliases={2: 0},
        compiler_params=_cparams(("arbitrary",)),
        name="moe_dispatch",
    )(dest_tiles, x, zeros)


def _ffn_kernel(be_ref, nu_ref, xs_ref, wg_ref, wu_ref, wd_ref, y_ref, h_ref):
    del be_ref
    i = pl.program_id(0)
    half = D_EXPERT // 2

    @pl.when(i < nu_ref[0])
    def _():
        xb = xs_ref[...].astype(BF16)
        for c in range(2):
            cs = slice(c * half, (c + 1) * half)
            gt = jnp.dot(xb, wg_ref[:, cs], preferred_element_type=F32)
            up = jnp.dot(xb, wu_ref[:, cs], preferred_element_type=F32)
            h_ref[:, cs] = (jax.nn.silu(gt) * up).astype(BF16)
        y_ref[...] = jnp.dot(h_ref[...], wd_ref[...], preferred_element_type=F32)

    @pl.when(i >= nu_ref[0])
    def _():
        y_ref[...] = jnp.zeros_like(y_ref)


def _ffn(blk_exp, n_used, xs, wg, wu, wd):
    rows = xs.shape[0]
    wspec = lambda: pl.BlockSpec((None, D_MODEL, D_EXPERT), lambda i, be, nu: (be[i], 0, 0))
    return pl.pallas_call(
        _ffn_kernel,
        grid_spec=pltpu.PrefetchScalarGridSpec(
            num_scalar_prefetch=2,
            grid=(rows // MOE_BLOCK,),
            in_specs=[pl.BlockSpec((MOE_BLOCK, D_MODEL), lambda i, be, nu: (i, 0)), wspec(), wspec(),
                      pl.BlockSpec((None, D_EXPERT, D_MODEL), lambda i, be, nu: (be[i], 0, 0))],
            out_specs=pl.BlockSpec((MOE_BLOCK, D_MODEL), lambda i, be, nu: (i, 0)),
            scratch_shapes=[pltpu.VMEM((MOE_BLOCK, D_EXPERT), BF16)]),
        out_shape=jax.ShapeDtypeStruct((rows, D_MODEL), F32),
        compiler_params=_cparams(("arbitrary",)),
        name="moe_ffn",
    )(blk_exp, n_used, xs, wg, wu, wd)


def _combine_kernel(dest_hbm, y_hbm, x_ref, wt_ref, lng_ref, lnb_ref, o_ref, dsm, buf, sem_idx, sem):
    i = pl.program_id(0)
    tm = x_ref.shape[0]
    idx_copy = pltpu.make_async_copy(dest_hbm.at[i], dsm, sem_idx)
    idx_copy.start()
    idx_copy.wait()

    def row_copy(r, k):
        return pltpu.make_async_copy(y_hbm.at[pl.ds(dsm[k * tm + r], 1)], buf.at[k, pl.ds(r, 1)], sem)

    def start(r, c):
        for k in range(TOP_K):
            row_copy(r, k).start()
        return c

    def wait(r, c):
        for k in range(TOP_K):
            row_copy(r, k).wait()
        return c

    lax.fori_loop(0, tm, start, 0)
    lax.fori_loop(0, tm, wait, 0)
    wt = wt_ref[...]
    z = ALPHA * x_ref[...] + wt[:, 0:1] * buf[0] + wt[:, 1:2] * buf[1]
    o_ref[...] = _layer_norm_rows(z, lng_ref[...], lnb_ref[...])


def _combine(dest_tiles, y, x, wt, lng, lnb, tm):
    T = x.shape[0]
    row = lambda i: (i, 0)
    vec = pl.BlockSpec((1, D_MODEL), lambda i: (0, 0))
    return pl.pallas_call(
        _combine_kernel,
        grid=(T // tm,),
        in_specs=[pl.BlockSpec(memory_space=pl.ANY), pl.BlockSpec(memory_space=pl.ANY),
                  pl.BlockSpec((tm, D_MODEL), row), pl.BlockSpec((tm, TOP_K), row), vec, vec],
        out_specs=pl.BlockSpec((tm, D_MODEL), row),
        out_shape=jax.ShapeDtypeStruct((T, D_MODEL), F32),
        scratch_shapes=[pltpu.SMEM((TOP_K * tm,), I32), pltpu.VMEM((TOP_K, tm, D_MODEL), F32),
                        pltpu.SemaphoreType.DMA(()), pltpu.SemaphoreType.DMA(())],
        compiler_params=_cparams(("arbitrary",)),
        name="moe_combine",
    )(dest_tiles, y, x, wt, lng, lnb)


def _moe(x1, e, wt, pos, cnt, wg, wu, wd, lng, lnb, tm=256):
    T = x1.shape[0]
    n_blocks = -(-(T * TOP_K) // MOE_BLOCK) + N_EXPERTS
    rows = n_blocks * MOE_BLOCK
    counts = cnt[:, 0].astype(I32)
    padded = (counts + MOE_BLOCK - 1) // MOE_BLOCK * MOE_BLOCK
    pad_end = jnp.cumsum(padded)
    pad_start = pad_end - padded
    dest = pos
    for j in range(N_EXPERTS):
        dest = dest + jnp.where(e == j, pad_start[j], 0)
    blk_start = jnp.arange(n_blocks, dtype=I32) * MOE_BLOCK
    blk_exp = jnp.minimum(jnp.sum((pad_end[None, :] <= blk_start[:, None]).astype(I32), axis=1), N_EXPERTS - 1)
    n_used = (pad_end[-1:] // MOE_BLOCK).astype(I32)
    dest_tiles = dest.reshape(TOP_K, T // tm, tm).transpose(1, 0, 2).reshape(T // tm, TOP_K * tm)
    xs = _dispatch(dest_tiles, x1, rows, tm)
    y = _ffn(blk_exp, n_used, xs, wg, wu, wd)
    return _combine(dest_tiles, y, x1, wt.T, lng, lnb, tm)


def _proj_rt_kernel(x_ref, w_ref, c_ref, s_ref, q_ref, k_ref, v_ref, g_ref):
    xb = x_ref[...].astype(BF16)
    cs, sn = c_ref[...], s_ref[...]
    rope = lambda y: y * cs + pltpu.roll(y, RT_QK // 2, 1) * sn
    for c in range(RT_HEADS):
        cols = slice(c * RT_QK, (c + 1) * RT_QK)
        q_ref[:, cols] = rope(jnp.dot(xb, w_ref[:, cols], preferred_element_type=F32)).astype(BF16)
    for c in range(RT_HEADS):
        cols = slice(c * RT_QK, (c + 1) * RT_QK)
        yk = jnp.dot(xb, w_ref[:, RT_QKW + c * RT_QK:RT_QKW + (c + 1) * RT_QK], preferred_element_type=F32)
        k_ref[:, cols] = (rope(yk) * RT_QK ** -0.5).astype(BF16)
    step = 1024
    for c in range(RT_VW // step):
        cols = slice(c * step, (c + 1) * step)
        v_ref[:, cols] = jnp.dot(xb, w_ref[:, 2 * RT_QKW + c * step:2 * RT_QKW + (c + 1) * step],
                                 preferred_element_type=F32).astype(BF16)
        base = 2 * RT_QKW + RT_VW
        g_ref[:, cols] = jax.nn.silu(jnp.dot(xb, w_ref[:, base + c * step:base + (c + 1) * step],
                                             preferred_element_type=F32))


def _proj_rt(xt, w, cs, sn, S, tm=256):
    T = xt.shape[0]
    nseq = S // tm
    row = lambda i: (i, 0)
    tab = lambda i: (i % nseq, 0)
    return pl.pallas_call(
        _proj_rt_kernel,
        grid=(T // tm,),
        in_specs=[pl.BlockSpec((tm, D_MODEL), row), pl.BlockSpec(w.shape, lambda i: (0, 0)),
                  pl.BlockSpec((tm, RT_QK), tab), pl.BlockSpec((tm, RT_QK), tab)],
        out_specs=[pl.BlockSpec((tm, RT_QKW), row), pl.BlockSpec((tm, RT_QKW), row),
                   pl.BlockSpec((tm, RT_VW), row), pl.BlockSpec((tm, RT_VW), row)],
        out_shape=[jax.ShapeDtypeStruct((T, RT_QKW), BF16), jax.ShapeDtypeStruct((T, RT_QKW), BF16),
                   jax.ShapeDtypeStruct((T, RT_VW), BF16), jax.ShapeDtypeStruct((T, RT_VW), F32)],
        compiler_params=_cparams(("parallel",)),
        name="proj_rt",
    )(xt, w, cs, sn)


def _retention_kernel(q_ref, k_ref, v_ref, sg_ref, dec_ref, qd_ref, kd_ref, cd_ref, gng_ref, gnb_ref, o_ref, state):
    j = pl.program_id(1)

    @pl.when(j == 0)
    def _():
        state[...] = jnp.zeros_like(state)

    H = range(RT_HEADS)
    q = [q_ref[:, h * RT_QK:(h + 1) * RT_QK] for h in H]
    k = [k_ref[:, h * RT_QK:(h + 1) * RT_QK] for h in H]
    v = [v_ref[:, h * RT_V:(h + 1) * RT_V] for h in H]
    r_old = [state[h] for h in H]
    att = [(_dot_nt(q[h], k[h]) * dec_ref[h]).astype(BF16) for h in H]
    cross = [jnp.dot(q[h], r_old[h].astype(BF16), preferred_element_type=F32) * qd_ref[h] for h in H]
    inner = [jnp.dot(att[h], v[h], preferred_element_type=F32) for h in H]
    kdec = [(k[h].astype(F32) * kd_ref[h]).astype(BF16) for h in H]
    for h in H:
        upd = lax.dot_general(kdec[h], v[h], (((0,), (0,)), ((), ())), preferred_element_type=F32)
        state[h] = r_old[h] * cd_ref[h][:, 0:1] + upd
    for h in H:
        cols = slice(h * RT_V, (h + 1) * RT_V)
        o = inner[h] + cross[h]
        mu = jnp.mean(o, axis=1, keepdims=True)
        oc = o - mu
        var = jnp.mean(oc * oc, axis=1, keepdims=True)
        on = oc * lax.rsqrt(var + RT_GN_EPS) * gng_ref[:, cols] + gnb_ref[:, cols]
        o_ref[:, cols] = (sg_ref[:, cols] * on).astype(BF16)


def _retention(q3, k3, v3, sg3, dec, qd, kd, cd, gng, gnb):
    B, S, _ = q3.shape
    C = RT_CHUNK
    qk = pl.BlockSpec((None, C, RT_QKW), lambda b, j: (b, j, 0))
    vv = pl.BlockSpec((None, C, RT_VW), lambda b, j: (b, j, 0))
    const = lambda a: pl.BlockSpec(a.shape, lambda b, j: (0,) * a.ndim)
    return pl.pallas_call(
        _retention_kernel,
        grid=(B, S // C),
        in_specs=[qk, qk, vv, vv, const(dec), const(qd), const(kd), const(cd), const(gng), const(gnb)],
        out_specs=vv,
        out_shape=jax.ShapeDtypeStruct((B, S, RT_VW), BF16),
        scratch_shapes=[pltpu.VMEM((RT_HEADS, RT_QK, RT_V), F32)],
        compiler_params=_cparams(("parallel", "arbitrary")),
        name="retention",
    )(q3, k3, v3, sg3, dec, qd, kd, cd, gng, gnb)


def _nsa_rope_tables(S):
    half = ROPE_DIM // 2
    inv = ROPE_THETA ** (-jnp.arange(half, dtype=F32) / half)
    ang = jnp.arange(S, dtype=F32)[:, None] * inv[None, :]
    cos, sin = jnp.cos(ang), jnp.sin(ang)
    zeros = lambda n: jnp.zeros((S, n), F32)
    cn = jnp.concatenate([cos, cos, jnp.ones((S, NS_HEAD - ROPE_DIM), F32)], axis=1)
    s1 = jnp.concatenate([-sin, zeros(NS_HEAD - half)], axis=1)
    s2 = jnp.concatenate([zeros(half), sin, zeros(NS_HEAD - ROPE_DIM)], axis=1)
    two = lambda a: jnp.concatenate([a, a], axis=1)
    return two(cn), two(s1), two(s2)


def _rt_rope_tables(S):
    inv = RT_THETA ** (-jnp.linspace(0.0, 1.0, RT_QK // 2, dtype=F32))
    ang = jnp.arange(S, dtype=F32)[:, None] * inv[None, :]
    cos, sin = jnp.cos(ang), jnp.sin(ang)
    return jnp.concatenate([cos, cos], axis=1), jnp.concatenate([-sin, sin], axis=1)


def _rt_decay_tables():
    log_g = jnp.log(1.0 - 2.0 ** (-5.0 - jnp.arange(RT_HEADS, dtype=F32)))
    idx = jnp.arange(RT_CHUNK, dtype=F32)
    diff = idx[:, None] - idx[None, :]
    dec = jnp.where(diff >= 0, jnp.exp(jnp.maximum(diff, 0.0) * log_g[:, None, None]), 0.0)
    qd = jnp.exp((idx + 1.0) * log_g[:, None])[..., None]
    kd = jnp.exp((RT_CHUNK - 1.0 - idx) * log_g[:, None])[..., None]
    cd = jnp.broadcast_to(jnp.exp(RT_CHUNK * log_g)[:, None, None], (RT_HEADS, 1, LANES))
    return dec, qd, kd, cd


def _overlap_table(S, ncp):
    n_cmp = (S - CMP_LEN) // CMP_STRIDE + 1
    n_sel = S // SEL_BLOCK
    cs = jnp.arange(ncp) * CMP_STRIDE
    ss = jnp.arange(LANES) * SEL_BLOCK
    ov = jnp.clip(jnp.minimum(cs[:, None] + CMP_LEN, ss[None, :] + SEL_BLOCK)
                  - jnp.maximum(cs[:, None], ss[None, :]), 0, None).astype(F32) / CMP_LEN
    keep = (jnp.arange(ncp)[:, None] < n_cmp) & (jnp.arange(LANES)[None, :] < n_sel)
    return jnp.where(keep, ov, 0.0)


def kernel(x, ab_w_in, ab_w_out, rk_mu, rk_w0, rk_w1, rk_w2, rk_a0, rk_a1, rk_a2, rk_g1, rk_g2, rk_kk, rk_ka, rk_rk,
           rk_ln, ns_pe, ns_c_w1, ns_c_w2, rt_w_in, rt_w_out, rt_gn, router_w, router_b, moe_w_gate, moe_w_up,
           moe_w_down, ln):
    B, S, D = x.shape
    T = B * S
    assert D == D_MODEL and S % 256 == 0 and S // SEL_BLOCK <= LANES and S >= WINDOW
    xt = x.reshape(T, D)
    rwt = router_w.T
    rb = router_b.reshape(N_EXPERTS, 1)
    vec = lambda a: a.reshape(1, -1)

    w_in = ab_w_in[0]
    n_gate = 3 * NS_HEADS
    w_cat = jnp.concatenate([w_in[:, :-n_gate], jnp.pad(w_in[:, -n_gate:], ((0, 0), (0, LANES - n_gate)))],
                            axis=1).astype(BF16)
    cn, s1, s2 = _nsa_rope_tables(S)
    prk, q, kv, gt = _proj_ab(xt, w_cat, cn, s1, s2, S)

    ones = (jnp.arange(RK_W)[:, None] // RK_HEAD == jnp.arange(RK_W)[None, :] // RK_HEAD).astype(F32)
    b16 = lambda a: a.astype(BF16)
    r, lw, km, v, kk, bb, g, bon = _rwkv_prep(
        prk.reshape(B, S, 4 * RK_W), rk_mu[0], vec(rk_w0[0]), b16(rk_w1[0]), b16(rk_w2[0]), vec(rk_a0[0]),
        b16(rk_a1[0]), b16(rk_a2[0]), b16(rk_g1[0]), b16(rk_g2[0]), vec(rk_kk[0]), vec(rk_ka[0]), vec(rk_rk[0]), ones)
    o_a = _rwkv_chunk(r, lw, km, v, kk, bb, g, bon, rk_ln[0, 0:1], rk_ln[0, 1:2])

    ncp = S // CMP_STRIDE
    kv3 = kv.reshape(B, S, 6 * LANES)
    x16 = (kv3[:, :, :2 * LANES].reshape(B, ncp, CMP_STRIDE, 2, NS_KV, NS_HEAD)
           .transpose(0, 3, 4, 1, 2, 5).reshape(B, 2, NS_KV, ncp, CMP_STRIDE * NS_HEAD))
    kcv = _nsa_compress(x16, b16(ns_c_w1[0]), ns_c_w1[0], b16(ns_c_w2[0]),
                        ns_pe[0].reshape(2, 1, CMP_LEN * NS_HEAD))
    o_b = _nsa_attn(q.reshape(B, S, NS_W), kcv, kv3, gt.reshape(B, S, LANES), _overlap_table(S, ncp))

    w_out = b16(ab_w_out[0])
    x1, e, wt, pos, cnt = _outproj_router([o_a.reshape(T, RK_W), o_b.reshape(T, NS_W)], [w_out[:RK_W], w_out[RK_W:]],
                                          xt, ln[0, 0, 0:1], ln[0, 0, 1:2], rwt, rb)
    x2 = _moe(x1, e, wt, pos, cnt, b16(moe_w_gate[0]), b16(moe_w_up[0]), b16(moe_w_down[0]),
              ln[0, 1, 0:1], ln[0, 1, 1:2])

    cs, sn = _rt_rope_tables(S)
    qr, kr, vr, sg = _proj_rt(x2, b16(rt_w_in[0]), cs, sn, S)
    dec, qd, kd, cd = _rt_decay_tables()
    ret = _retention(qr.reshape(B, S, RT_QKW), kr.reshape(B, S, RT_QKW), vr.reshape(B, S, RT_VW),
                     sg.reshape(B, S, RT_VW), dec, qd, kd, cd, rt_gn[0, 0:1], rt_gn[0, 1:2])
    x3, e, wt, pos, cnt = _outproj_router([ret.reshape(T, RT_VW)], [b16(rt_w_out[0])], x2,
                                          ln[1, 0, 0:1], ln[1, 0, 1:2], rwt, rb)
    x4 = _moe(x3, e, wt, pos, cnt, b16(moe_w_gate[1]), b16(moe_w_up[1]), b16(moe_w_down[1]),
              ln[1, 1, 0:1], ln[1, 1, 1:2])
    return x4.reshape(B, S, D)
```

```python
import functools
import math

import jax
import jax.numpy as jnp
from jax import lax
from jax.experimental import pallas as pl
from jax.experimental.pallas import tpu as pltpu

F32 = jnp.float32
BF16 = jnp.bfloat16
I32 = jnp.int32
HI = lax.Precision.HIGHEST

LANES = 128
VMEM_LIMIT = 56 * 1024 * 1024

D_MODEL = 1024
RK_HEADS, RK_HEAD = 8, 64
RK_W = RK_HEADS * RK_HEAD
RK_DECAY_SCALE = 0.606531
RK_LN_EPS = 64e-5
NS_HEADS, NS_KV, NS_HPG, NS_HEAD = 8, 2, 4, 64
NS_W = NS_HEADS * NS_HEAD
CMP_LEN, CMP_STRIDE, SEL_BLOCK, SEL_TOPK, WINDOW = 32, 16, 64, 16, 512
ROPE_THETA = 500000.0
ROPE_DIM = NS_HEAD // 4
QT = 128
KT = 512
RT_HEADS, RT_QK, RT_V = 8, 128, 256
RT_QKW, RT_VW = RT_HEADS * RT_QK, RT_HEADS * RT_V
RT_CHUNK = 128
RT_THETA = 10000.0
RT_GN_EPS = 1e-5
N_EXPERTS, N_GROUPS, EXP_PER_GROUP, TOP_K = 16, 4, 4, 2
D_EXPERT = 1024
MOE_BLOCK = 512
ROW_DMA_UNROLL = 8
DEPTH = 2
ALPHA = (2.0 * DEPTH) ** 0.25
LN_EPS = 1e-5
NEG = -1e30


def _cparams(sem):
    return pltpu.CompilerParams(dimension_semantics=sem, vmem_limit_bytes=VMEM_LIMIT)


def _bdot(a, w):
    return jnp.dot(a.astype(BF16), w, preferred_element_type=F32)


def _dot_nt(a, b):
    return lax.dot_general(a, b, (((1,), (1,)), ((), ())), preferred_element_type=F32)


def _layer_norm_rows(z, g, b):
    mu = jnp.mean(z, axis=1, keepdims=True)
    zc = z - mu
    var = jnp.mean(zc * zc, axis=1, keepdims=True)
    return zc * lax.rsqrt(var + LN_EPS) * g + b


def _proj_ab_kernel(x_ref, w_ref, cn_ref, s1_ref, s2_ref, prk_ref, q_ref, kv_ref, gt_ref, vt_ref):
    xb = x_ref[...].astype(BF16)
    for c in range(4):
        prk_ref[:, c * RK_W:(c + 1) * RK_W] = jnp.dot(xb, w_ref[:, c * RK_W:(c + 1) * RK_W],
                                                      preferred_element_type=F32)
    cn, s1, s2 = cn_ref[...], s1_ref[...], s2_ref[...]

    def rope(y):
        return y * cn + pltpu.roll(y, LANES - ROPE_DIM // 2, 1) * s1 + pltpu.roll(y, ROPE_DIM // 2, 1) * s2

    base = 4 * RK_W
    yq = jnp.dot(xb, w_ref[:, base:base + NS_W], preferred_element_type=F32)
    for c in range(NS_W // LANES):
        q_ref[:, c * LANES:(c + 1) * LANES] = (rope(yq[:, c * LANES:(c + 1) * LANES]) * NS_HEAD ** -0.5).astype(BF16)
    base += NS_W
    ykv = jnp.dot(xb, w_ref[:, base:base + 6 * LANES], preferred_element_type=F32)
    for c in range(6):
        y = ykv[:, c * LANES:(c + 1) * LANES]
        if c % 2 == 0:
            y = rope(y)
        kv_ref[:, c * LANES:(c + 1) * LANES] = y.astype(BF16)
        if c in (3, 5):
            for t in range(y.shape[0] // QT):
                vt_ref[c // 2 - 1, t] = y[t * QT:(t + 1) * QT].T.astype(BF16)
    base += 6 * LANES
    gt_ref[...] = jax.nn.sigmoid(jnp.dot(xb, w_ref[:, base:base + LANES], preferred_element_type=F32))


def _proj_ab(xt, w, cn, s1, s2, S, tm=256):
    T = xt.shape[0]
    ncols = w.shape[1]
    nseq = S // tm
    row = lambda i: (i, 0)
    tab = lambda i: (i % nseq, 0)
    return pl.pallas_call(
        _proj_ab_kernel,
        grid=(T // tm,),
        in_specs=[pl.BlockSpec((tm, D_MODEL), row),
                  pl.BlockSpec((D_MODEL, ncols), lambda i: (0, 0)),
                  pl.BlockSpec((tm, LANES), tab), pl.BlockSpec((tm, LANES), tab), pl.BlockSpec((tm, LANES), tab)],
        out_specs=[pl.BlockSpec((tm, 4 * RK_W), row), pl.BlockSpec((tm, NS_W), row),
                   pl.BlockSpec((tm, 6 * LANES), row), pl.BlockSpec((tm, LANES), row),
                   pl.BlockSpec((2, tm // QT, LANES, QT), lambda i: (0, i, 0, 0))],
        out_shape=[jax.ShapeDtypeStruct((T, 4 * RK_W), F32), jax.ShapeDtypeStruct((T, NS_W), BF16),
                   jax.ShapeDtypeStruct((T, 6 * LANES), BF16), jax.ShapeDtypeStruct((T, LANES), F32),
                   jax.ShapeDtypeStruct((2, T // QT, LANES, QT), BF16)],
        compiler_params=_cparams(("parallel",)),
        name="proj_ab",
    )(xt, w, cn, s1, s2)


def _rwkv_prep_kernel(p_ref, mu_ref, w0_ref, w1_ref, w2_ref, a0_ref, a1_ref, a2_ref, g1_ref, g2_ref,
                      kk_ref, ka_ref, rk_ref, ones_ref,
                      r_o, w_o, k_o, v_o, kk_o, b_o, g_o, bon_o, carry):
    j = pl.program_id(1)
    p = p_ref[...]
    tm = p.shape[0]

    @pl.when(j == 0)
    def _():
        carry[...] = jnp.zeros_like(carry)

    rowi = lax.broadcasted_iota(I32, p.shape, 0)
    prev = jnp.where(rowi == 0, carry[...], pltpu.roll(p, 1, 0))
    carry[...] = p[tm - 1:tm, :]
    dp = prev - p
    sl = lambda a, c: a[:, c * RK_W:(c + 1) * RK_W]
    mu = mu_ref[...]
    r = sl(p, 0) + sl(dp, 0) * mu[0:1]
    k = sl(p, 1) + sl(dp, 1) * mu[1:2]
    v = sl(p, 2) + sl(dp, 2) * mu[2:3]
    xw = sl(p, 3) + sl(dp, 3) * mu[3:4]
    xa = sl(p, 3) + sl(dp, 3) * mu[4:5]
    xg = sl(p, 3) + sl(dp, 3) * mu[5:6]
    lw = -RK_DECAY_SCALE * jax.nn.sigmoid(w0_ref[...] + _bdot(jnp.tanh(_bdot(xw, w1_ref[...])), w2_ref[...]))
    a = jax.nn.sigmoid(a0_ref[...] + _bdot(_bdot(xa, a1_ref[...]), a2_ref[...]))
    g = _bdot(jax.nn.sigmoid(_bdot(xg, g1_ref[...])), g2_ref[...])
    ones = ones_ref[...]
    kk = k * kk_ref[...]
    ss = jnp.dot(kk * kk, ones, precision=HI, preferred_element_type=F32)
    kk = kk / jnp.maximum(jnp.sqrt(ss), 1e-12)
    km = k * (1.0 + (a - 1.0) * ka_ref[...])
    bon = jnp.dot(r * km * rk_ref[...], ones, precision=HI, preferred_element_type=F32) * v
    r_o[...] = r
    w_o[...] = lw
    k_o[...] = km
    v_o[...] = v
    kk_o[...] = kk
    b_o[...] = kk * a
    g_o[...] = g
    bon_o[...] = bon


def _rwkv_prep(prk3, mu, w0, w1, w2, a0, a1, a2, g1, g2, k_k, k_a, r_k, ones, tm=256):
    B, S, _ = prk3.shape
    full = lambda a: pl.BlockSpec(a.shape, lambda b, j: (0,) * a.ndim)
    params = [mu, w0, w1, w2, a0, a1, a2, g1, g2, k_k, k_a, r_k, ones]
    ospec = pl.BlockSpec((None, tm, RK_W), lambda b, j: (b, j, 0))
    return pl.pallas_call(
        _rwkv_prep_kernel,
        grid=(B, S // tm),
        in_specs=[pl.BlockSpec((None, tm, 4 * RK_W), lambda b, j: (b, j, 0))] + [full(a) for a in params],
        out_specs=[ospec] * 8,
        out_shape=[jax.ShapeDtypeStruct((B, S, RK_W), F32)] * 8,
        scratch_shapes=[pltpu.VMEM((1, 4 * RK_W), F32)],
        compiler_params=_cparams(("parallel", "arbitrary")),
        name="rwkv_prep",
    )(prk3, *params)


RK_CHUNK = 16


def _pdot(a, b, dims, precise):
    if precise:
        return lax.dot_general(a, b, (dims, ((), ())), precision=HI, preferred_element_type=F32)
    return lax.dot_general(a.astype(BF16), b.astype(BF16), (dims, ((), ())), preferred_element_type=F32)


def _rwkv_chunk_kernel(r_ref, lw_ref, k_ref, v_ref, kk_ref, b_ref, g_ref, bon_ref, lng_ref, lnb_ref, o_ref, ht,
                       *, precise):
    j = pl.program_id(1)

    @pl.when(j == 0)
    def _():
        ht[...] = jnp.zeros_like(ht)

    TT = r_ref.shape[0]
    C, N = RK_CHUNK, RK_HEAD
    mm = lambda a, b: _pdot(a, b, ((1,), (0,)), precise)
    mm_nt = lambda a, b: _pdot(a, b, ((1,), (1,)), precise)
    mm_tn = lambda a, b: _pdot(a, b, ((0,), (0,)), precise)

    lw = lw_ref[...]
    rowc = lax.broadcasted_iota(I32, lw.shape, 0) & (C - 1)
    linc, lrev = lw, lw
    sh = 1
    while sh < C:
        linc = linc + jnp.where(rowc >= sh, pltpu.roll(linc, sh, 0), 0.0)
        lrev = lrev + jnp.where(rowc < C - sh, pltpu.roll(lrev, TT - sh, 0), 0.0)
        sh *= 2
    lrev = lrev - lw
    r, k, v, kk, b = r_ref[...], k_ref[...], v_ref[...], kk_ref[...], b_ref[...]
    e_in, e_inv, e_rev = jnp.exp(linc), jnp.exp(-linc), jnp.exp(lrev)
    kkd = kk * jnp.exp(linc - lw)
    rd = r * e_in
    binv, kinv = b * e_inv, k * e_inv
    bd, kd = b * e_rev, k * e_rev
    gam = jnp.exp(linc + lrev)

    ti = lax.broadcasted_iota(I32, (TT, TT), 0)
    tj = lax.broadcasted_iota(I32, (TT, TT), 1)
    same = (ti // C) == (tj // C)
    strict = same & (tj < ti)
    incl = same & (tj <= ti)

    H = range(RK_HEADS)
    lo = lambda a: a if precise else a.astype(BF16)
    hs = lambda a: [a[:, h * N:(h + 1) * N] for h in H]
    rows2 = lambda a, b: jnp.concatenate([a, b], axis=0)
    kkd_h, rd_h, v_h = hs(lo(kkd)), hs(lo(rd)), hs(lo(v))
    binv_h, kinv_h, bd_h, kd_h = hs(lo(binv)), hs(lo(kinv)), hs(lo(bd)), hs(lo(kd))
    gam_h = hs(gam)
    gm = [mm_nt(rows2(kkd_h[h], rd_h[h]), rows2(binv_h[h], kinv_h[h])) for h in H]
    a_b = [lo(jnp.where(strict, gm[h][:TT, :TT], 0.0)) for h in H]
    b_rb = [lo(jnp.where(incl, gm[h][TT:, :TT], 0.0)) for h in H]
    akb = [lo(rows2(jnp.where(strict, gm[h][:TT, TT:], 0.0), jnp.where(incl, gm[h][TT:, TT:], 0.0))) for h in H]
    av = [mm(akb[h], v_h[h]) for h in H]
    x = [jnp.concatenate([kkd_h[h].astype(F32), av[h][:TT]], axis=1) for h in H]
    a2 = [lo(mm(a_b[h], a_b[h])) for h in H]
    a4 = [lo(mm(a2[h], a2[h])) for h in H]
    a8 = [lo(mm(a4[h], a4[h])) for h in H]
    x = [x[h] + mm(a8[h], lo(x[h])) for h in H]
    x = [x[h] + mm(a4[h], lo(x[h])) for h in H]
    x = [x[h] + mm(a2[h], lo(x[h])) for h in H]
    x = [x[h] - mm(a_b[h], lo(x[h])) for h in H]
    wt = [lo(x[h][:, :N]) for h in H]
    h_t = [ht[h] for h in H]
    us = [[] for _ in H]
    rhs = [[] for _ in H]
    for c in range(TT // C):
        rs = slice(c * C, (c + 1) * C)
        xh = [mm_nt(rows2(wt[h][rs], rd_h[h][rs]), lo(h_t[h])) for h in H]
        for h in H:
            u_c = -(xh[h][:C] + x[h][rs, N:])
            us[h].append(u_c)
            rhs[h].append(xh[h][C:])
        upd = [mm_tn(rows2(lo(us[h][c]), v_h[h][rs]), rows2(bd_h[h][rs], kd_h[h][rs])) for h in H]
        h_t = [h_t[h] * gam_h[h][c * C:c * C + 1] + upd[h] for h in H]
    outs = []
    for h in H:
        ht[h] = h_t[h]
        o = jnp.concatenate(rhs[h], axis=0) + mm(b_rb[h], lo(jnp.concatenate(us[h], axis=0))) + av[h][TT:]
        mu = jnp.mean(o, axis=1, keepdims=True)
        oc = o - mu
        var = jnp.mean(oc * oc, axis=1, keepdims=True)
        outs.append(oc * lax.rsqrt(var + RK_LN_EPS))
    on = jnp.concatenate(outs, axis=1)
    o_ref[...] = (on * lng_ref[...] + lnb_ref[...] + bon_ref[...]) * g_ref[...]


def _rwkv_chunk(r, lw, k, v, kk, bb, g, bon, lng, lnb, tt=128, precise=False):
    B, S, _ = r.shape
    blk = pl.BlockSpec((None, tt, RK_W), lambda b, j: (b, j, 0))
    vec = pl.BlockSpec((1, RK_W), lambda b, j: (0, 0))
    return pl.pallas_call(
        functools.partial(_rwkv_chunk_kernel, precise=precise),
        grid=(B, S // tt),
        in_specs=[blk] * 8 + [vec, vec],
        out_specs=blk,
        out_shape=jax.ShapeDtypeStruct((B, S, RK_W), F32),
        scratch_shapes=[pltpu.VMEM((RK_HEADS, RK_HEAD, RK_HEAD), F32)],
        compiler_params=_cparams(("parallel", "arbitrary")),
        name="rwkv_chunk",
    )(r, lw, k, v, kk, bb, g, bon, lng, lnb)


def _nsa_compress_kernel(x_ref, w1b_ref, w1f_ref, w2_ref, pe_ref, o_ref, ot_ref):
    half = CMP_STRIDE * NS_HEAD
    w1b = w1b_ref[...]
    bias = jnp.dot(jnp.broadcast_to(pe_ref[...], (8, CMP_LEN * NS_HEAD)), w1f_ref[...], precision=HI,
                   preferred_element_type=F32)[0:1]
    outs = []
    for g in range(NS_KV):
        x = x_ref[g]
        ya = jnp.dot(x, w1b[:half], preferred_element_type=F32)
        yb = jnp.dot(x, w1b[half:], preferred_element_type=F32)
        n = x.shape[0]
        h = ya + pltpu.roll(yb, n - 1, 0) + bias
        outs.append(_bdot(jax.nn.gelu(h), w2_ref[...]))
    out = jnp.concatenate(outs, axis=1)
    o_ref[...] = out.astype(BF16)
    ot_ref[...] = out.T.astype(BF16)


def _nsa_compress(x16, w1b, w1f, w2b, pe):
    B, _, _, ncp, width = x16.shape
    hid = w1b.shape[-1]
    return pl.pallas_call(
        _nsa_compress_kernel,
        grid=(B, 2),
        in_specs=[pl.BlockSpec((None, None, NS_KV, ncp, width), lambda b, c: (b, c, 0, 0, 0)),
                  pl.BlockSpec((None, 2 * width, hid), lambda b, c: (c, 0, 0)),
                  pl.BlockSpec((None, 2 * width, hid), lambda b, c: (c, 0, 0)),
                  pl.BlockSpec((None, hid, NS_HEAD), lambda b, c: (c, 0, 0)),
                  pl.BlockSpec((None, 1, 2 * width), lambda b, c: (c, 0, 0))],
        out_specs=[pl.BlockSpec((None, None, ncp, LANES), lambda b, c: (b, c, 0, 0)),
                   pl.BlockSpec((None, None, LANES, ncp), lambda b, c: (b, c, 0, 0))],
        out_shape=[jax.ShapeDtypeStruct((B, 2, ncp, LANES), BF16), jax.ShapeDtypeStruct((B, 2, LANES, ncp), BF16)],
        compiler_params=_cparams(("parallel", "parallel")),
        name="nsa_compress",
    )(x16, w1b, w1f, w2b, pe)


def _nsa_attn_kernel(q_ref, kc_ref, vct_ref, ks_ref, vst_ref, kw_ref, vwt_ref, gt_ref, ovt_ref, ext_ref, o_ref,
                     m_s, l_s, acc_s, m_w, l_w, acc_w, s_buf, p_buf, *, ncp):
    i = pl.program_id(1)
    g = pl.program_id(2)
    s0 = i * QT
    gsel = (g == 0).astype(F32)
    heads = range(NS_HPG)
    hcols = lambda h: slice(h * QT, (h + 1) * QT)
    iota = lambda shape, d: lax.broadcasted_iota(I32, shape, d)

    def col_reduce(x, op, final):
        n = x.shape[0]
        while n > 8:
            n //= 2
            x = op(x[:n], x[n:])
        return final(x, axis=0, keepdims=True)

    q4 = q_ref[...].astype(F32)
    mine = (iota((QT, LANES), 1) // NS_HEAD) == g
    qt = []
    for h in heads:
        qh = q4[:, h * NS_HEAD:(h + 1) * NS_HEAD]
        qt.append(jnp.where(mine, jnp.concatenate([qh, qh], axis=1), 0.0).T)
    qt = jnp.concatenate(qt, axis=1).astype(BF16)

    sc = jnp.dot(kc_ref[...], qt, preferred_element_type=F32)
    n_row = iota((ncp, QT), 0)
    cmask = (n_row * CMP_STRIDE + (CMP_LEN - 1) <= s0 + iota((ncp, QT), 1)) & (n_row < ncp - 1)
    cpen = jnp.where(cmask, 0.0, NEG)
    cok = jnp.where(cmask, 1.0, 0.0)
    sc = jnp.concatenate([sc[:, hcols(h)] + cpen for h in heads], axis=1)
    pc = jnp.exp(sc - col_reduce(sc, jnp.maximum, jnp.max))
    pc = jnp.concatenate([pc[:, hcols(h)] * cok for h in heads], axis=1)
    lc = col_reduce(pc, jnp.add, jnp.sum)
    pc = pc / jnp.where(lc > 0.0, lc, 1.0)
    pcs = functools.reduce(jnp.add, [pc[:, hcols(h)] for h in heads])
    o_c = jnp.dot(vct_ref[...], pc.astype(BF16), preferred_element_type=F32)

    imp = jnp.dot(ovt_ref[...], pcs, precision=HI, preferred_element_type=F32)
    blk = iota((LANES, QT), 0)
    cur = (s0 + iota((LANES, QT), 1)) // SEL_BLOCK
    valid = blk <= cur
    forced = (blk == 0) | (blk == cur) | (blk == cur - 1)
    pri = jnp.where(valid & ~forced, imp, -jnp.inf)
    picked = forced
    blkf = blk.astype(F32)
    for _ in range(SEL_TOPK - 3):
        mx = col_reduce(pri, jnp.maximum, jnp.max)
        hit = blkf == col_reduce(jnp.where(pri == mx, blkf, float(LANES)), jnp.minimum, jnp.min)
        picked = picked | hit
        pri = jnp.where(hit, -jnp.inf, pri)
    sel = jnp.where(picked & valid, 1.0, 0.0).astype(BF16)

    def attend(s, vt, pen, m_ref, l_ref, acc_ref):
        s = jnp.concatenate([s[:, hcols(h)] + pen for h in heads], axis=1)
        m_old = m_ref[...]
        m_new = jnp.maximum(m_old, col_reduce(s, jnp.maximum, jnp.max))
        alpha = jnp.exp(m_old - m_new)
        p = jnp.exp(s - m_new)
        m_ref[...] = m_new
        l_ref[...] = alpha * l_ref[...] + col_reduce(p, jnp.add, jnp.sum)
        acc_ref[...] = alpha * acc_ref[...] + jnp.dot(vt, p.astype(BF16), preferred_element_type=F32)

    def reset(m_ref, l_ref, acc_ref):
        m_ref[...] = jnp.full(m_ref.shape, NEG, F32)
        l_ref[...] = jnp.zeros_like(l_ref)
        acc_ref[...] = jnp.zeros_like(acc_ref)

    tiles = lambda ref, t0, n: jnp.concatenate([ref[t0 + c] for c in range(n)], axis=1)

    reset(m_s, l_s, acc_s)
    diag = i // (KT // QT)

    def sel_scores(kt, slot):
        k0 = pl.multiple_of(kt * KT, KT)
        s_buf[slot] = jnp.dot(ks_ref[pl.ds(k0, KT), :], qt, preferred_element_type=F32)
        ok = jnp.dot(ext_ref[pl.ds(k0, KT), :], sel, preferred_element_type=F32) > 0.5
        ok = ok & (k0 + iota((KT, QT), 0) <= s0 + iota((KT, QT), 1))
        p_buf[slot] = jnp.where(ok, 0.0, NEG)

    def sel_attend(kt, slot):
        attend(s_buf[slot], tiles(vst_ref, kt * (KT // QT), KT // QT), p_buf[slot], m_s, l_s, acc_s)

    def sel_pair(j, c):
        sel_scores(2 * j + 1, 1)
        sel_attend(2 * j, 0)
        sel_scores(2 * j + 2, 0)
        sel_attend(2 * j + 1, 1)
        return c

    d0 = pl.multiple_of(s0, QT)
    a0 = pl.multiple_of(jnp.maximum(s0 - WINDOW, 0), QT)
    s_d = jnp.dot(kw_ref[pl.ds(d0, QT), :], qt, preferred_element_type=F32)
    s_a = jnp.dot(kw_ref[pl.ds(a0, WINDOW), :], qt, preferred_element_type=F32)
    pen_d = jnp.where(iota((QT, QT), 0) <= iota((QT, QT), 1), 0.0, NEG)
    kpos = a0 + iota((WINDOW, QT), 0)
    pen_a = jnp.where((kpos < s0) & (kpos > s0 + iota((WINDOW, QT), 1) - WINDOW), 0.0, NEG)

    sel_scores(0, 0)
    reset(m_w, l_w, acc_w)
    attend(s_d, vwt_ref[i], pen_d, m_w, l_w, acc_w)
    attend(s_a, tiles(vwt_ref, a0 // QT, WINDOW // QT), pen_a, m_w, l_w, acc_w)
    lax.fori_loop(0, diag // 2, sel_pair, 0)

    @pl.when(diag % 2 == 1)
    def _():
        sel_scores(diag, 1)
        sel_attend(diag - 1, 0)
        sel_attend(diag, 1)

    @pl.when(diag % 2 == 0)
    def _():
        sel_attend(diag, 0)

    gtt = gt_ref[...].T
    o_s = acc_s[...] / l_s[...]
    o_w = acc_w[...] / l_w[...]
    outs = []
    for h in heads:
        def gate(br):
            r0 = br * NS_HEADS + h
            return gtt[r0:r0 + 1] * gsel + gtt[r0 + NS_HPG:r0 + NS_HPG + 1] * (1.0 - gsel)
        o = gate(0) * o_c[:, hcols(h)] + gate(1) * o_s[:, hcols(h)] + gate(2) * o_w[:, hcols(h)]
        outs.append(o[:NS_HEAD] * gsel + o[NS_HEAD:] * (1.0 - gsel))
    pairs = [jnp.concatenate(outs[p:p + 2], axis=0).T for p in range(0, NS_HPG, 2)]
    o_ref[...] = jnp.concatenate(pairs, axis=1).astype(BF16)


def _nsa_attn(q3, kcv, kcvt, kv3, vt, gt3, ovt):
    B, S, _ = q3.shape
    ncp = kcv.shape[2]
    R = NS_HPG * QT
    gw = NS_HPG * NS_HEAD
    ext = (jnp.arange(S, dtype=I32)[:, None] // SEL_BLOCK == jnp.arange(LANES, dtype=I32)[None, :]).astype(BF16)
    seq = lambda c: pl.BlockSpec((None, S, LANES), lambda b, i, g, c=c: (b, 0, c))
    seqt = lambda c: pl.BlockSpec((None, None, S // QT, LANES, QT), lambda b, i, g, c=c: (c, b, 0, 0, 0))
    return pl.pallas_call(
        functools.partial(_nsa_attn_kernel, ncp=ncp),
        grid=(B, S // QT, NS_KV),
        in_specs=[pl.BlockSpec((None, QT, gw), lambda b, i, g: (b, i, g)),
                  pl.BlockSpec((None, None, ncp, LANES), lambda b, i, g: (b, 0, 0, 0)),
                  pl.BlockSpec((None, None, LANES, ncp), lambda b, i, g: (b, 1, 0, 0)),
                  seq(2), seqt(0), seq(4), seqt(1),
                  pl.BlockSpec((None, QT, LANES), lambda b, i, g: (b, i, 0)),
                  pl.BlockSpec(ovt.shape, lambda b, i, g: (0, 0)),
                  pl.BlockSpec(ext.shape, lambda b, i, g: (0, 0))],
        out_specs=pl.BlockSpec((None, QT, gw), lambda b, i, g: (b, i, g)),
        out_shape=jax.ShapeDtypeStruct((B, S, NS_W), BF16),
        scratch_shapes=([pltpu.VMEM((1, R), F32), pltpu.VMEM((1, R), F32), pltpu.VMEM((LANES, R), F32)] * 2
                        + [pltpu.VMEM((2, KT, R), F32), pltpu.VMEM((2, KT, QT), F32)]),
        compiler_params=_cparams(("parallel", "parallel", "arbitrary")),
        name="nsa_attn",
    )(q3, kcv, kcvt, kv3, vt, kv3, vt, gt3, ovt, ext)


def _first_argmax(vals):
    m = vals[0]
    for v in vals[1:]:
        m = jnp.maximum(m, v)
    idx = jnp.full(m.shape, len(vals) - 1, I32)
    for j in range(len(vals) - 2, -1, -1):
        idx = jnp.where(vals[j] == m, j, idx)
    return m, idx


def _outproj_router_kernel(*refs, n_in):
    acts, ws = refs[:n_in], refs[n_in:2 * n_in]
    x_ref, lng_ref, lnb_ref, rwt_ref, rb_ref, tri_ref = refs[2 * n_in:2 * n_in + 6]
    y_ref, e_ref, wt_ref, pos_ref, cnt_ref, cnt = refs[2 * n_in + 6:]
    i = pl.program_id(0)

    @pl.when(i == 0)
    def _():
        cnt[...] = jnp.zeros_like(cnt)

    mix = _bdot(acts[0][...], ws[0][...])
    for a, w in zip(acts[1:], ws[1:]):
        mix = mix + _bdot(a[...], w[...])
    y = _layer_norm_rows(ALPHA * x_ref[...] + mix, lng_ref[...], lnb_ref[...])
    y_ref[...] = y

    logit = lax.dot_general(rwt_ref[...], y, (((1,), (1,)), ((), ())), precision=HI, preferred_element_type=F32)
    aff = jax.nn.sigmoid(logit)
    biased = aff + rb_ref[...]
    neg_inf = -jnp.inf
    g_score, g_i1, g_i2 = [], [], []
    for gi in range(N_GROUPS):
        vals = [biased[gi * EXP_PER_GROUP + j:gi * EXP_PER_GROUP + j + 1, :] for j in range(EXP_PER_GROUP)]
        m1, i1 = _first_argmax(vals)
        m2, i2 = _first_argmax([jnp.where(i1 == j, neg_inf, vals[j]) for j in range(EXP_PER_GROUP)])
        g_score.append(m1 + m2)
        g_i1.append(i1)
        g_i2.append(i2)
    _, grp = _first_argmax(g_score)
    loc1, loc2 = g_i1[-1], g_i2[-1]
    for gi in range(N_GROUPS - 2, -1, -1):
        loc1 = jnp.where(grp == gi, g_i1[gi], loc1)
        loc2 = jnp.where(grp == gi, g_i2[gi], loc2)
    e1 = grp * EXP_PER_GROUP + loc1
    e2 = grp * EXP_PER_GROUP + loc2
    eio = lax.broadcasted_iota(I32, aff.shape, 0)
    oh1 = eio == e1
    oh2 = eio == e2
    a1 = jnp.sum(jnp.where(oh1, aff, 0.0), axis=0, keepdims=True)
    a2 = jnp.sum(jnp.where(oh2, aff, 0.0), axis=0, keepdims=True)
    tot = a1 + a2
    e_ref[...] = jnp.concatenate([e1, e2], axis=0)
    wt_ref[...] = jnp.concatenate([a1 / tot, a2 / tot], axis=0)

    ohs = oh1.astype(F32) + oh2.astype(F32)
    before = jnp.dot(ohs.astype(BF16), tri_ref[...], preferred_element_type=F32) + cnt[...]
    p1 = jnp.sum(jnp.where(oh1, before, 0.0), axis=0, keepdims=True)
    p2 = jnp.sum(jnp.where(oh2, before, 0.0), axis=0, keepdims=True)
    pos_ref[...] = jnp.concatenate([p1, p2], axis=0).astype(I32)
    cnt[...] = cnt[...] + jnp.sum(ohs, axis=1, keepdims=True)
    cnt_ref[...] = jnp.broadcast_to(cnt[...], cnt_ref.shape)


def _outproj_router(acts, ws, xres, lng, lnb, rwt, rb, tm=256):
    T = xres.shape[0]
    n_in = len(acts)
    tri = (lax.broadcasted_iota(I32, (tm, tm), 0) < lax.broadcasted_iota(I32, (tm, tm), 1)).astype(BF16)
    row = lambda i: (i, 0)
    const = lambda a: pl.BlockSpec(a.shape, lambda i: (0,) * a.ndim)
    lane_blk = pl.BlockSpec((TOP_K, tm), lambda i: (0, i))
    return pl.pallas_call(
        functools.partial(_outproj_router_kernel, n_in=n_in),
        grid=(T // tm,),
        in_specs=([pl.BlockSpec((tm, a.shape[1]), row) for a in acts] + [const(w) for w in ws]
                  + [pl.BlockSpec((tm, D_MODEL), row), const(lng), const(lnb), const(rwt), const(rb), const(tri)]),
        out_specs=[pl.BlockSpec((tm, D_MODEL), row), lane_blk, lane_blk, lane_blk,
                   pl.BlockSpec((N_EXPERTS, LANES), lambda i: (0, 0))],
        out_shape=[jax.ShapeDtypeStruct((T, D_MODEL), F32), jax.ShapeDtypeStruct((TOP_K, T), I32),
                   jax.ShapeDtypeStruct((TOP_K, T), F32), jax.ShapeDtypeStruct((TOP_K, T), I32),
                   jax.ShapeDtypeStruct((N_EXPERTS, LANES), F32)],
        scratch_shapes=[pltpu.VMEM((N_EXPERTS, 1), F32)],
        compiler_params=_cparams(("arbitrary",)),
        name="outproj_router",
    )(*acts, *ws, xres, lng, lnb, rwt, rb, tri)


def _dispatch_kernel(dest_hbm, x_ref, zero_hbm, xs_hbm, dsm, sem_idx, sem):
    del zero_hbm
    i = pl.program_id(0)
    tm = x_ref.shape[0]
    idx_copy = pltpu.make_async_copy(dest_hbm.at[i], dsm, sem_idx)
    idx_copy.start()
    idx_copy.wait()

    def row_copy(r, k):
        return pltpu.make_async_copy(x_ref.at[pl.ds(r, 1)], xs_hbm.at[pl.ds(dsm[k * tm + r], 1)], sem)

    def start(r, c):
        for k in range(TOP_K):
            row_copy(r, k).start()
        return c

    lax.fori_loop(0, tm, start, 0, unroll=ROW_DMA_UNROLL)
    for k in range(TOP_K):
        pltpu.make_async_copy(x_ref, xs_hbm.at[pl.ds(0, tm)], sem).wait()


def _dispatch(dest_tiles, x, rows, tm):
    T = x.shape[0]
    zeros = jnp.zeros((rows, D_MODEL), F32)
    return pl.pallas_call(
        _dispatch_kernel,
        grid=(T // tm,),
        in_specs=[pl.BlockSpec(memory_space=pl.ANY), pl.BlockSpec((tm, D_MODEL), lambda i: (i, 0)),
                  pl.BlockSpec(memory_space=pl.ANY)],
        out_specs=pl.BlockSpec(memory_space=pl.ANY),
        out_shape=jax.ShapeDtypeStruct((rows, D_MODEL), F32),
        scratch_shapes=[pltpu.SMEM((TOP_K * tm,), I32), pltpu.SemaphoreType.DMA(()), pltpu.SemaphoreType.DMA(())],
        input_output_aliases={2: 0},
        compiler_params=_cparams(("arbitrary",)),
        name="moe_dispatch",
    )(dest_tiles, x, zeros)


def _ffn_kernel(be_ref, nu_ref, xs_ref, wg_ref, wu_ref, wd_ref, y_ref, h_ref):
    del be_ref
    i = pl.program_id(0)
    half = D_EXPERT // 2

    @pl.when(i < nu_ref[0])
    def _():
        xb = xs_ref[...].astype(BF16)
        for c in range(2):
            cs = slice(c * half, (c + 1) * half)
            gt = jnp.dot(xb, wg_ref[:, cs], preferred_element_type=F32)
            up = jnp.dot(xb, wu_ref[:, cs], preferred_element_type=F32)
            h_ref[:, cs] = (jax.nn.silu(gt) * up).astype(BF16)
        y_ref[...] = jnp.dot(h_ref[...], wd_ref[...], preferred_element_type=F32)

    @pl.when(i >= nu_ref[0])
    def _():
        y_ref[...] = jnp.zeros_like(y_ref)


def _ffn(blk_exp, n_used, xs, wg, wu, wd):
    rows = xs.shape[0]
    wspec = lambda: pl.BlockSpec((None, D_MODEL, D_EXPERT), lambda i, be, nu: (be[i], 0, 0))
    return pl.pallas_call(
        _ffn_kernel,
        grid_spec=pltpu.PrefetchScalarGridSpec(
            num_scalar_prefetch=2,
            grid=(rows // MOE_BLOCK,),
            in_specs=[pl.BlockSpec((MOE_BLOCK, D_MODEL), lambda i, be, nu: (i, 0)), wspec(), wspec(),
                      pl.BlockSpec((None, D_EXPERT, D_MODEL), lambda i, be, nu: (be[i], 0, 0))],
            out_specs=pl.BlockSpec((MOE_BLOCK, D_MODEL), lambda i, be, nu: (i, 0)),
            scratch_shapes=[pltpu.VMEM((MOE_BLOCK, D_EXPERT), BF16)]),
        out_shape=jax.ShapeDtypeStruct((rows, D_MODEL), F32),
        compiler_params=_cparams(("arbitrary",)),
        name="moe_ffn",
    )(blk_exp, n_used, xs, wg, wu, wd)


def _combine_kernel(dest_hbm, y_hbm, x_ref, wt_ref, lng_ref, lnb_ref, o_ref, dsm, buf, sem_idx, sem):
    i = pl.program_id(0)
    tm = x_ref.shape[0]
    idx_copy = pltpu.make_async_copy(dest_hbm.at[i], dsm, sem_idx)
    idx_copy.start()
    idx_copy.wait()

    def row_copy(r, k):
        return pltpu.make_async_copy(y_hbm.at[pl.ds(dsm[k * tm + r], 1)], buf.at[k, pl.ds(r, 1)], sem)

    def start(r, c):
        for k in range(TOP_K):
            row_copy(r, k).start()
        return c

    lax.fori_loop(0, tm, start, 0, unroll=ROW_DMA_UNROLL)
    for k in range(TOP_K):
        pltpu.make_async_copy(y_hbm.at[pl.ds(0, tm)], buf.at[k], sem).wait()
    wt = wt_ref[...]
    z = ALPHA * x_ref[...] + wt[:, 0:1] * buf[0] + wt[:, 1:2] * buf[1]
    o_ref[...] = _layer_norm_rows(z, lng_ref[...], lnb_ref[...])


def _combine(dest_tiles, y, x, wt, lng, lnb, tm):
    T = x.shape[0]
    row = lambda i: (i, 0)
    vec = pl.BlockSpec((1, D_MODEL), lambda i: (0, 0))
    return pl.pallas_call(
        _combine_kernel,
        grid=(T // tm,),
        in_specs=[pl.BlockSpec(memory_space=pl.ANY), pl.BlockSpec(memory_space=pl.ANY),
                  pl.BlockSpec((tm, D_MODEL), row), pl.BlockSpec((tm, TOP_K), row), vec, vec],
        out_specs=pl.BlockSpec((tm, D_MODEL), row),
        out_shape=jax.ShapeDtypeStruct((T, D_MODEL), F32),
        scratch_shapes=[pltpu.SMEM((TOP_K * tm,), I32), pltpu.VMEM((TOP_K, tm, D_MODEL), F32),
                        pltpu.SemaphoreType.DMA(()), pltpu.SemaphoreType.DMA(())],
        compiler_params=_cparams(("arbitrary",)),
        name="moe_combine",
    )(dest_tiles, y, x, wt, lng, lnb)


def _moe(x1, e, wt, pos, cnt, wg, wu, wd, lng, lnb, tm=256):
    T = x1.shape[0]
    n_blocks = -(-(T * TOP_K) // MOE_BLOCK) + N_EXPERTS
    rows = n_blocks * MOE_BLOCK
    counts = cnt[:, 0].astype(I32)
    padded = (counts + MOE_BLOCK - 1) // MOE_BLOCK * MOE_BLOCK
    pad_end = jnp.cumsum(padded)
    pad_start = pad_end - padded
    dest = pos
    for j in range(N_EXPERTS):
        dest = dest + jnp.where(e == j, pad_start[j], 0)
    blk_start = jnp.arange(n_blocks, dtype=I32) * MOE_BLOCK
    blk_exp = jnp.minimum(jnp.sum((pad_end[None, :] <= blk_start[:, None]).astype(I32), axis=1), N_EXPERTS - 1)
    n_used = (pad_end[-1:] // MOE_BLOCK).astype(I32)
    dest_tiles = dest.reshape(TOP_K, T // tm, tm).transpose(1, 0, 2).reshape(T // tm, TOP_K * tm)
    xs = _dispatch(dest_tiles, x1, rows, tm)
    y = _ffn(blk_exp, n_used, xs, wg, wu, wd)
    return _combine(dest_tiles, y, x1, wt.T, lng, lnb, tm)


def _proj_rt_kernel(x_ref, w_ref, c_ref, s_ref, q_ref, k_ref, v_ref, g_ref):
    xb = x_ref[...].astype(BF16)
    cs, sn = c_ref[...], s_ref[...]
    rope = lambda y: y * cs + pltpu.roll(y, RT_QK // 2, 1) * sn
    for c in range(RT_HEADS):
        cols = slice(c * RT_QK, (c + 1) * RT_QK)
        q_ref[:, cols] = rope(jnp.dot(xb, w_ref[:, cols], preferred_element_type=F32)).astype(BF16)
    for c in range(RT_HEADS):
        cols = slice(c * RT_QK, (c + 1) * RT_QK)
        yk = jnp.dot(xb, w_ref[:, RT_QKW + c * RT_QK:RT_QKW + (c + 1) * RT_QK], preferred_element_type=F32)
        k_ref[:, cols] = (rope(yk) * RT_QK ** -0.5).astype(BF16)
    step = 1024
    for c in range(RT_VW // step):
        cols = slice(c * step, (c + 1) * step)
        v_ref[:, cols] = jnp.dot(xb, w_ref[:, 2 * RT_QKW + c * step:2 * RT_QKW + (c + 1) * step],
                                 preferred_element_type=F32).astype(BF16)
        base = 2 * RT_QKW + RT_VW
        g_ref[:, cols] = jax.nn.silu(jnp.dot(xb, w_ref[:, base + c * step:base + (c + 1) * step],
                                             preferred_element_type=F32))


def _proj_rt(xt, w, cs, sn, S, tm=256):
    T = xt.shape[0]
    nseq = S // tm
    row = lambda i: (i, 0)
    tab = lambda i: (i % nseq, 0)
    return pl.pallas_call(
        _proj_rt_kernel,
        grid=(T // tm,),
        in_specs=[pl.BlockSpec((tm, D_MODEL), row), pl.BlockSpec(w.shape, lambda i: (0, 0)),
                  pl.BlockSpec((tm, RT_QK), tab), pl.BlockSpec((tm, RT_QK), tab)],
        out_specs=[pl.BlockSpec((tm, RT_QKW), row), pl.BlockSpec((tm, RT_QKW), row),
                   pl.BlockSpec((tm, RT_VW), row), pl.BlockSpec((tm, RT_VW), row)],
        out_shape=[jax.ShapeDtypeStruct((T, RT_QKW), BF16), jax.ShapeDtypeStruct((T, RT_QKW), BF16),
                   jax.ShapeDtypeStruct((T, RT_VW), BF16), jax.ShapeDtypeStruct((T, RT_VW), F32)],
        compiler_params=_cparams(("parallel",)),
        name="proj_rt",
    )(xt, w, cs, sn)


def _retention_kernel(q_ref, k_ref, v_ref, sg_ref, dec_ref, qd_ref, kd_ref, cd_ref, gng_ref, gnb_ref, o_ref, state):
    j = pl.program_id(1)

    @pl.when(j == 0)
    def _():
        state[...] = jnp.zeros_like(state)

    H = range(RT_HEADS)
    q = [q_ref[:, h * RT_QK:(h + 1) * RT_QK] for h in H]
    k = [k_ref[:, h * RT_QK:(h + 1) * RT_QK] for h in H]
    v = [v_ref[:, h * RT_V:(h + 1) * RT_V] for h in H]
    r_old = [state[h] for h in H]
    att = [(_dot_nt(q[h], k[h]) * dec_ref[h]).astype(BF16) for h in H]
    cross = [jnp.dot(q[h], r_old[h].astype(BF16), preferred_element_type=F32) * qd_ref[h] for h in H]
    inner = [jnp.dot(att[h], v[h], preferred_element_type=F32) for h in H]
    kdec = [(k[h].astype(F32) * kd_ref[h]).astype(BF16) for h in H]
    for h in H:
        upd = lax.dot_general(kdec[h], v[h], (((0,), (0,)), ((), ())), preferred_element_type=F32)
        state[h] = r_old[h] * cd_ref[h][:, 0:1] + upd
    for h in H:
        cols = slice(h * RT_V, (h + 1) * RT_V)
        o = inner[h] + cross[h]
        mu = jnp.mean(o, axis=1, keepdims=True)
        oc = o - mu
        var = jnp.mean(oc * oc, axis=1, keepdims=True)
        on = oc * lax.rsqrt(var + RT_GN_EPS) * gng_ref[:, cols] + gnb_ref[:, cols]
        o_ref[:, cols] = (sg_ref[:, cols] * on).astype(BF16)


def _retention(q3, k3, v3, sg3, dec, qd, kd, cd, gng, gnb):
    B, S, _ = q3.shape
    C = RT_CHUNK
    qk = pl.BlockSpec((None, C, RT_QKW), lambda b, j: (b, j, 0))
    vv = pl.BlockSpec((None, C, RT_VW), lambda b, j: (b, j, 0))
    const = lambda a: pl.BlockSpec(a.shape, lambda b, j: (0,) * a.ndim)
    return pl.pallas_call(
        _retention_kernel,
        grid=(B, S // C),
        in_specs=[qk, qk, vv, vv, const(dec), const(qd), const(kd), const(cd), const(gng), const(gnb)],
        out_specs=vv,
        out_shape=jax.ShapeDtypeStruct((B, S, RT_VW), BF16),
        scratch_shapes=[pltpu.VMEM((RT_HEADS, RT_QK, RT_V), F32)],
        compiler_params=_cparams(("parallel", "arbitrary")),
        name="retention",
    )(q3, k3, v3, sg3, dec, qd, kd, cd, gng, gnb)


def _nsa_rope_tables(S):
    half = ROPE_DIM // 2
    inv = ROPE_THETA ** (-jnp.arange(half, dtype=F32) / half)
    ang = jnp.arange(S, dtype=F32)[:, None] * inv[None, :]
    cos, sin = jnp.cos(ang), jnp.sin(ang)
    zeros = lambda n: jnp.zeros((S, n), F32)
    cn = jnp.concatenate([cos, cos, jnp.ones((S, NS_HEAD - ROPE_DIM), F32)], axis=1)
    s1 = jnp.concatenate([-sin, zeros(NS_HEAD - half)], axis=1)
    s2 = jnp.concatenate([zeros(half), sin, zeros(NS_HEAD - ROPE_DIM)], axis=1)
    two = lambda a: jnp.concatenate([a, a], axis=1)
    return two(cn), two(s1), two(s2)


def _rt_rope_tables(S):
    inv = RT_THETA ** (-jnp.linspace(0.0, 1.0, RT_QK // 2, dtype=F32))
    ang = jnp.arange(S, dtype=F32)[:, None] * inv[None, :]
    cos, sin = jnp.cos(ang), jnp.sin(ang)
    return jnp.concatenate([cos, cos], axis=1), jnp.concatenate([-sin, sin], axis=1)


def _rt_decay_tables():
    log_g = jnp.log(1.0 - 2.0 ** (-5.0 - jnp.arange(RT_HEADS, dtype=F32)))
    idx = jnp.arange(RT_CHUNK, dtype=F32)
    diff = idx[:, None] - idx[None, :]
    dec = jnp.where(diff >= 0, jnp.exp(jnp.maximum(diff, 0.0) * log_g[:, None, None]), 0.0)
    qd = jnp.exp((idx + 1.0) * log_g[:, None])[..., None]
    kd = jnp.exp((RT_CHUNK - 1.0 - idx) * log_g[:, None])[..., None]
    cd = jnp.broadcast_to(jnp.exp(RT_CHUNK * log_g)[:, None, None], (RT_HEADS, 1, LANES))
    return dec, qd, kd, cd


def _overlap_table(S, ncp):
    n_cmp = (S - CMP_LEN) // CMP_STRIDE + 1
    n_sel = S // SEL_BLOCK
    cs = jnp.arange(ncp) * CMP_STRIDE
    ss = jnp.arange(LANES) * SEL_BLOCK
    ov = jnp.clip(jnp.minimum(cs[:, None] + CMP_LEN, ss[None, :] + SEL_BLOCK)
                  - jnp.maximum(cs[:, None], ss[None, :]), 0, None).astype(F32) / CMP_LEN
    keep = (jnp.arange(ncp)[:, None] < n_cmp) & (jnp.arange(LANES)[None, :] < n_sel)
    return jnp.where(keep, ov, 0.0)


def kernel(x, ab_w_in, ab_w_out, rk_mu, rk_w0, rk_w1, rk_w2, rk_a0, rk_a1, rk_a2, rk_g1, rk_g2, rk_kk, rk_ka, rk_rk,
           rk_ln, ns_pe, ns_c_w1, ns_c_w2, rt_w_in, rt_w_out, rt_gn, router_w, router_b, moe_w_gate, moe_w_up,
           moe_w_down, ln):
    B, S, D = x.shape
    T = B * S
    assert D == D_MODEL and S % 256 == 0 and S // SEL_BLOCK <= LANES and S >= WINDOW
    xt = x.reshape(T, D)
    rwt = router_w.T
    rb = router_b.reshape(N_EXPERTS, 1)
    vec = lambda a: a.reshape(1, -1)

    w_in = ab_w_in[0]
    n_gate = 3 * NS_HEADS
    w_cat = jnp.concatenate([w_in[:, :-n_gate], jnp.pad(w_in[:, -n_gate:], ((0, 0), (0, LANES - n_gate)))],
                            axis=1).astype(BF16)
    cn, s1, s2 = _nsa_rope_tables(S)
    prk, q, kv, gt, vt = _proj_ab(xt, w_cat, cn, s1, s2, S)

    ones = (jnp.arange(RK_W)[:, None] // RK_HEAD == jnp.arange(RK_W)[None, :] // RK_HEAD).astype(F32)
    b16 = lambda a: a.astype(BF16)
    r, lw, km, v, kk, bb, g, bon = _rwkv_prep(
        prk.reshape(B, S, 4 * RK_W), rk_mu[0], vec(rk_w0[0]), b16(rk_w1[0]), b16(rk_w2[0]), vec(rk_a0[0]),
        b16(rk_a1[0]), b16(rk_a2[0]), b16(rk_g1[0]), b16(rk_g2[0]), vec(rk_kk[0]), vec(rk_ka[0]), vec(rk_rk[0]), ones)
    o_a = _rwkv_chunk(r, lw, km, v, kk, bb, g, bon, rk_ln[0, 0:1], rk_ln[0, 1:2])

    ncp = S // CMP_STRIDE
    kv3 = kv.reshape(B, S, 6 * LANES)
    x16 = (kv3[:, :, :2 * LANES].reshape(B, ncp, CMP_STRIDE, 2, NS_KV, NS_HEAD)
           .transpose(0, 3, 4, 1, 2, 5).reshape(B, 2, NS_KV, ncp, CMP_STRIDE * NS_HEAD))
    kcv, kcvt = _nsa_compress(x16, b16(ns_c_w1[0]), ns_c_w1[0], b16(ns_c_w2[0]),
                              ns_pe[0].reshape(2, 1, CMP_LEN * NS_HEAD))
    o_b = _nsa_attn(q.reshape(B, S, NS_W), kcv, kcvt, kv3, vt.reshape(2, B, S // QT, LANES, QT),
                    gt.reshape(B, S, LANES), _overlap_table(S, ncp).T)

    w_out = b16(ab_w_out[0])
    x1, e, wt, pos, cnt = _outproj_router([o_a.reshape(T, RK_W), o_b.reshape(T, NS_W)], [w_out[:RK_W], w_out[RK_W:]],
                                          xt, ln[0, 0, 0:1], ln[0, 0, 1:2], rwt, rb)
    x2 = _moe(x1, e, wt, pos, cnt, b16(moe_w_gate[0]), b16(moe_w_up[0]), b16(moe_w_down[0]),
              ln[0, 1, 0:1], ln[0, 1, 1:2])

    cs, sn = _rt_rope_tables(S)
    qr, kr, vr, sg = _proj_rt(x2, b16(rt_w_in[0]), cs, sn, S)
    dec, qd, kd, cd = _rt_decay_tables()
    ret = _retention(qr.reshape(B, S, RT_QKW), kr.reshape(B, S, RT_QKW), vr.reshape(B, S, RT_VW),
                     sg.reshape(B, S, RT_VW), dec, qd, kd, cd, rt_gn[0, 0:1], rt_gn[0, 1:2])
    x3, e, wt, pos, cnt = _outproj_router([ret.reshape(T, RT_VW)], [b16(rt_w_out[0])], x2,
                                          ln[1, 0, 0:1], ln[1, 0, 1:2], rwt, rb)
    x4 = _moe(x3, e, wt, pos, cnt, b16(moe_w_gate[1]), b16(moe_w_up[1]), b16(moe_w_down[1]),
              ln[1, 1, 0:1], ln[1, 1, 1:2])
    return x4.reshape(B, S, D)
```

```python
import functools
import math

import jax
import jax.numpy as jnp
from jax import lax
from jax.experimental import pallas as pl
from jax.experimental.pallas import tpu as pltpu

F32 = jnp.float32
BF16 = jnp.bfloat16
I32 = jnp.int32
HI = lax.Precision.HIGHEST

LANES = 128
VMEM_LIMIT = 56 * 1024 * 1024

D_MODEL = 1024
RK_HEADS, RK_HEAD = 8, 64
RK_W = RK_HEADS * RK_HEAD
RK_DECAY_SCALE = 0.606531
RK_LN_EPS = 64e-5
NS_HEADS, NS_KV, NS_HPG, NS_HEAD = 8, 2, 4, 64
NS_W = NS_HEADS * NS_HEAD
CMP_LEN, CMP_STRIDE, SEL_BLOCK, SEL_TOPK, WINDOW = 32, 16, 64, 16, 512
ROPE_THETA = 500000.0
ROPE_DIM = NS_HEAD // 4
Q_SCALE = NS_HEAD ** -0.5 * math.log2(math.e)
QT = 128
KT = 512
RT_HEADS, RT_QK, RT_V = 8, 128, 256
RT_QKW, RT_VW = RT_HEADS * RT_QK, RT_HEADS * RT_V
RT_CHUNK = 128
RT_THETA = 10000.0
RT_GN_EPS = 1e-5
N_EXPERTS, N_GROUPS, EXP_PER_GROUP, TOP_K = 16, 4, 4, 2
D_EXPERT = 1024
MOE_BLOCK = 512
ROW_DMA_UNROLL = 8
DEPTH = 2
ALPHA = (2.0 * DEPTH) ** 0.25
LN_EPS = 1e-5
NEG = -1e30


def _cparams(sem):
    return pltpu.CompilerParams(dimension_semantics=sem, vmem_limit_bytes=VMEM_LIMIT)


def _bdot(a, w):
    return jnp.dot(a.astype(BF16), w, preferred_element_type=F32)


def _dot_nt(a, b):
    return lax.dot_general(a, b, (((1,), (1,)), ((), ())), preferred_element_type=F32)


def _layer_norm_rows(z, g, b):
    mu = jnp.mean(z, axis=1, keepdims=True)
    zc = z - mu
    var = jnp.mean(zc * zc, axis=1, keepdims=True)
    return zc * lax.rsqrt(var + LN_EPS) * g + b


def _proj_ab_kernel(x_ref, w_ref, cn_ref, s1_ref, s2_ref, prk_ref, q_ref, kv_ref, gt_ref, vt_ref):
    xb = x_ref[...].astype(BF16)
    for c in range(4):
        prk_ref[:, c * RK_W:(c + 1) * RK_W] = jnp.dot(xb, w_ref[:, c * RK_W:(c + 1) * RK_W],
                                                      preferred_element_type=F32)
    cn, s1, s2 = cn_ref[...], s1_ref[...], s2_ref[...]

    def rope(y):
        return y * cn + pltpu.roll(y, LANES - ROPE_DIM // 2, 1) * s1 + pltpu.roll(y, ROPE_DIM // 2, 1) * s2

    base = 4 * RK_W
    yq = jnp.dot(xb, w_ref[:, base:base + NS_W], preferred_element_type=F32)
    for c in range(NS_W // LANES):
        q_ref[:, c * LANES:(c + 1) * LANES] = (rope(yq[:, c * LANES:(c + 1) * LANES]) * Q_SCALE).astype(BF16)
    base += NS_W
    ykv = jnp.dot(xb, w_ref[:, base:base + 6 * LANES], preferred_element_type=F32)
    for c in range(6):
        y = ykv[:, c * LANES:(c + 1) * LANES]
        if c % 2 == 0:
            y = rope(y)
        kv_ref[:, c * LANES:(c + 1) * LANES] = y.astype(BF16)
        if c in (3, 5):
            for t in range(y.shape[0] // QT):
                vt_ref[c // 2 - 1, t] = y[t * QT:(t + 1) * QT].T.astype(BF16)
    base += 6 * LANES
    gt_ref[...] = jax.nn.sigmoid(jnp.dot(xb, w_ref[:, base:base + LANES], preferred_element_type=F32))


def _proj_ab(xt, w, cn, s1, s2, S, tm=256):
    T = xt.shape[0]
    ncols = w.shape[1]
    nseq = S // tm
    row = lambda i: (i, 0)
    tab = lambda i: (i % nseq, 0)
    return pl.pallas_call(
        _proj_ab_kernel,
        grid=(T // tm,),
        in_specs=[pl.BlockSpec((tm, D_MODEL), row),
                  pl.BlockSpec((D_MODEL, ncols), lambda i: (0, 0)),
                  pl.BlockSpec((tm, LANES), tab), pl.BlockSpec((tm, LANES), tab), pl.BlockSpec((tm, LANES), tab)],
        out_specs=[pl.BlockSpec((tm, 4 * RK_W), row), pl.BlockSpec((tm, NS_W), row),
                   pl.BlockSpec((tm, 6 * LANES), row), pl.BlockSpec((tm, LANES), row),
                   pl.BlockSpec((2, tm // QT, LANES, QT), lambda i: (0, i, 0, 0))],
        out_shape=[jax.ShapeDtypeStruct((T, 4 * RK_W), F32), jax.ShapeDtypeStruct((T, NS_W), BF16),
                   jax.ShapeDtypeStruct((T, 6 * LANES), BF16), jax.ShapeDtypeStruct((T, LANES), F32),
                   jax.ShapeDtypeStruct((2, T // QT, LANES, QT), BF16)],
        compiler_params=_cparams(("parallel",)),
        name="proj_ab",
    )(xt, w, cn, s1, s2)


def _rwkv_prep_kernel(p_ref, mu_ref, w0_ref, w1_ref, w2_ref, a0_ref, a1_ref, a2_ref, g1_ref, g2_ref,
                      kk_ref, ka_ref, rk_ref, ones_ref,
                      r_o, w_o, k_o, v_o, kk_o, b_o, g_o, bon_o, carry):
    j = pl.program_id(1)
    p = p_ref[...]
    tm = p.shape[0]

    @pl.when(j == 0)
    def _():
        carry[...] = jnp.zeros_like(carry)

    rowi = lax.broadcasted_iota(I32, p.shape, 0)
    prev = jnp.where(rowi == 0, carry[...], pltpu.roll(p, 1, 0))
    carry[...] = p[tm - 1:tm, :]
    dp = prev - p
    sl = lambda a, c: a[:, c * RK_W:(c + 1) * RK_W]
    mu = mu_ref[...]
    r = sl(p, 0) + sl(dp, 0) * mu[0:1]
    k = sl(p, 1) + sl(dp, 1) * mu[1:2]
    v = sl(p, 2) + sl(dp, 2) * mu[2:3]
    xw = sl(p, 3) + sl(dp, 3) * mu[3:4]
    xa = sl(p, 3) + sl(dp, 3) * mu[4:5]
    xg = sl(p, 3) + sl(dp, 3) * mu[5:6]
    lw = -RK_DECAY_SCALE * jax.nn.sigmoid(w0_ref[...] + _bdot(jnp.tanh(_bdot(xw, w1_ref[...])), w2_ref[...]))
    a = jax.nn.sigmoid(a0_ref[...] + _bdot(_bdot(xa, a1_ref[...]), a2_ref[...]))
    g = _bdot(jax.nn.sigmoid(_bdot(xg, g1_ref[...])), g2_ref[...])
    ones = ones_ref[...]
    kk = k * kk_ref[...]
    ss = jnp.dot(kk * kk, ones, precision=HI, preferred_element_type=F32)
    kk = kk / jnp.maximum(jnp.sqrt(ss), 1e-12)
    km = k * (1.0 + (a - 1.0) * ka_ref[...])
    bon = jnp.dot(r * km * rk_ref[...], ones, precision=HI, preferred_element_type=F32) * v
    r_o[...] = r
    w_o[...] = lw
    k_o[...] = km
    v_o[...] = v
    kk_o[...] = kk
    b_o[...] = kk * a
    g_o[...] = g
    bon_o[...] = bon


def _rwkv_prep(prk3, mu, w0, w1, w2, a0, a1, a2, g1, g2, k_k, k_a, r_k, ones, tm=256):
    B, S, _ = prk3.shape
    full = lambda a: pl.BlockSpec(a.shape, lambda b, j: (0,) * a.ndim)
    params = [mu, w0, w1, w2, a0, a1, a2, g1, g2, k_k, k_a, r_k, ones]
    ospec = pl.BlockSpec((None, tm, RK_W), lambda b, j: (b, j, 0))
    return pl.pallas_call(
        _rwkv_prep_kernel,
        grid=(B, S // tm),
        in_specs=[pl.BlockSpec((None, tm, 4 * RK_W), lambda b, j: (b, j, 0))] + [full(a) for a in params],
        out_specs=[ospec] * 8,
        out_shape=[jax.ShapeDtypeStruct((B, S, RK_W), F32)] * 8,
        scratch_shapes=[pltpu.VMEM((1, 4 * RK_W), F32)],
        compiler_params=_cparams(("parallel", "arbitrary")),
        name="rwkv_prep",
    )(prk3, *params)


RK_CHUNK = 16


def _pdot(a, b, dims, precise):
    if precise:
        return lax.dot_general(a, b, (dims, ((), ())), precision=HI, preferred_element_type=F32)
    return lax.dot_general(a.astype(BF16), b.astype(BF16), (dims, ((), ())), preferred_element_type=F32)


def _rwkv_chunk_kernel(r_ref, lw_ref, k_ref, v_ref, kk_ref, b_ref, g_ref, bon_ref, lng_ref, lnb_ref, o_ref, ht,
                       *, precise):
    j = pl.program_id(1)

    @pl.when(j == 0)
    def _():
        ht[...] = jnp.zeros_like(ht)

    NB, TT = r_ref.shape[0], r_ref.shape[1]
    C, N = RK_CHUNK, RK_HEAD
    mm = lambda a, b: _pdot(a, b, ((1,), (0,)), precise)
    mm_nt = lambda a, b: _pdot(a, b, ((1,), (1,)), precise)
    mm_tn = lambda a, b: _pdot(a, b, ((0,), (0,)), precise)
    wide = lambda ref: jnp.concatenate([ref[n] for n in range(NB)], axis=1)

    lw = wide(lw_ref)
    rowc = lax.broadcasted_iota(I32, lw.shape, 0) & (C - 1)
    linc, lrev = lw, lw
    sh = 1
    while sh < C:
        linc = linc + jnp.where(rowc >= sh, pltpu.roll(linc, sh, 0), 0.0)
        lrev = lrev + jnp.where(rowc < C - sh, pltpu.roll(lrev, TT - sh, 0), 0.0)
        sh *= 2
    lrev = lrev - lw
    r, k, v, kk, b = wide(r_ref), wide(k_ref), wide(v_ref), wide(kk_ref), wide(b_ref)
    e_in, e_inv, e_rev = jnp.exp(linc), jnp.exp(-linc), jnp.exp(lrev)
    kkd = kk * jnp.exp(linc - lw)
    rd = r * e_in
    binv, kinv = b * e_inv, k * e_inv
    bd, kd = b * e_rev, k * e_rev
    gam = jnp.exp(linc + lrev)

    ti = lax.broadcasted_iota(I32, (TT, TT), 0)
    tj = lax.broadcasted_iota(I32, (TT, TT), 1)
    same = (ti // C) == (tj // C)
    strict = same & (tj < ti)
    incl = same & (tj <= ti)

    H = range(NB * RK_HEADS)
    lo = lambda a: a if precise else a.astype(BF16)
    hs = lambda a: [a[:, h * N:(h + 1) * N] for h in H]
    rows2 = lambda a, b: jnp.concatenate([a, b], axis=0)
    kkd_h, rd_h, v_h = hs(lo(kkd)), hs(lo(rd)), hs(lo(v))
    binv_h, kinv_h, bd_h, kd_h = hs(lo(binv)), hs(lo(kinv)), hs(lo(bd)), hs(lo(kd))
    gam_h = hs(gam)
    gm = [mm_nt(rows2(kkd_h[h], rd_h[h]), rows2(binv_h[h], kinv_h[h])) for h in H]
    a_b = [lo(jnp.where(strict, gm[h][:TT, :TT], 0.0)) for h in H]
    b_rb = [lo(jnp.where(incl, gm[h][TT:, :TT], 0.0)) for h in H]
    akb = [lo(rows2(jnp.where(strict, gm[h][:TT, TT:], 0.0), jnp.where(incl, gm[h][TT:, TT:], 0.0))) for h in H]
    av = [mm(akb[h], v_h[h]) for h in H]
    x = [jnp.concatenate([kkd_h[h].astype(F32), av[h][:TT]], axis=1) for h in H]
    a2 = [lo(mm(a_b[h], a_b[h])) for h in H]
    a4 = [lo(mm(a2[h], a2[h])) for h in H]
    a8 = [lo(mm(a4[h], a4[h])) for h in H]
    x = [x[h] + mm(a8[h], lo(x[h])) for h in H]
    x = [x[h] + mm(a4[h], lo(x[h])) for h in H]
    x = [x[h] + mm(a2[h], lo(x[h])) for h in H]
    x = [x[h] - mm(a_b[h], lo(x[h])) for h in H]
    wt = [lo(x[h][:, :N]) for h in H]
    h_t = [ht[h] for h in H]
    us = [[] for _ in H]
    rhs = [[] for _ in H]
    for c in range(TT // C):
        rs = slice(c * C, (c + 1) * C)
        xh = [mm_nt(rows2(wt[h][rs], rd_h[h][rs]), lo(h_t[h])) for h in H]
        for h in H:
            u_c = -(xh[h][:C] + x[h][rs, N:])
            us[h].append(u_c)
            rhs[h].append(xh[h][C:])
        upd = [mm_tn(rows2(lo(us[h][c]), v_h[h][rs]), rows2(bd_h[h][rs], kd_h[h][rs])) for h in H]
        h_t = [h_t[h] * gam_h[h][c * C:c * C + 1] + upd[h] for h in H]
    outs = []
    for h in H:
        ht[h] = h_t[h]
        o = jnp.concatenate(rhs[h], axis=0) + mm(b_rb[h], lo(jnp.concatenate(us[h], axis=0))) + av[h][TT:]
        mu = jnp.mean(o, axis=1, keepdims=True)
        oc = o - mu
        var = jnp.mean(oc * oc, axis=1, keepdims=True)
        outs.append(oc * lax.rsqrt(var + RK_LN_EPS))
    for n in range(NB):
        on = jnp.concatenate(outs[n * RK_HEADS:(n + 1) * RK_HEADS], axis=1)
        o_ref[n] = (on * lng_ref[...] + lnb_ref[...] + bon_ref[n]) * g_ref[n]


def _rwkv_chunk(r, lw, k, v, kk, bb, g, bon, lng, lnb, tt=128, nb=4, precise=False):
    B, S, _ = r.shape
    nb = nb if B % nb == 0 else 1
    blk = pl.BlockSpec((nb, tt, RK_W), lambda b, j: (b, j, 0))
    vec = pl.BlockSpec((1, RK_W), lambda b, j: (0, 0))
    return pl.pallas_call(
        functools.partial(_rwkv_chunk_kernel, precise=precise),
        grid=(B // nb, S // tt),
        in_specs=[blk] * 8 + [vec, vec],
        out_specs=blk,
        out_shape=jax.ShapeDtypeStruct((B, S, RK_W), F32),
        scratch_shapes=[pltpu.VMEM((nb * RK_HEADS, RK_HEAD, RK_HEAD), F32)],
        compiler_params=_cparams(("parallel", "arbitrary")),
        name="rwkv_chunk",
    )(r, lw, k, v, kk, bb, g, bon, lng, lnb)


def _nsa_compress_kernel(x_ref, w1b_ref, w1f_ref, w2_ref, pe_ref, o_ref, ot_ref):
    half = CMP_STRIDE * NS_HEAD
    w1b = w1b_ref[...]
    bias = jnp.dot(jnp.broadcast_to(pe_ref[...], (8, CMP_LEN * NS_HEAD)), w1f_ref[...], precision=HI,
                   preferred_element_type=F32)[0:1]
    outs = []
    for g in range(NS_KV):
        x = x_ref[g]
        ya = jnp.dot(x, w1b[:half], preferred_element_type=F32)
        yb = jnp.dot(x, w1b[half:], preferred_element_type=F32)
        n = x.shape[0]
        h = ya + pltpu.roll(yb, n - 1, 0) + bias
        outs.append(_bdot(jax.nn.gelu(h), w2_ref[...]))
    out = jnp.concatenate(outs, axis=1)
    o_ref[...] = out.astype(BF16)
    ot_ref[...] = out.T.astype(BF16)


def _nsa_compress(x16, w1b, w1f, w2b, pe):
    B, _, _, ncp, width = x16.shape
    hid = w1b.shape[-1]
    return pl.pallas_call(
        _nsa_compress_kernel,
        grid=(B, 2),
        in_specs=[pl.BlockSpec((None, None, NS_KV, ncp, width), lambda b, c: (b, c, 0, 0, 0)),
                  pl.BlockSpec((None, 2 * width, hid), lambda b, c: (c, 0, 0)),
                  pl.BlockSpec((None, 2 * width, hid), lambda b, c: (c, 0, 0)),
                  pl.BlockSpec((None, hid, NS_HEAD), lambda b, c: (c, 0, 0)),
                  pl.BlockSpec((None, 1, 2 * width), lambda b, c: (c, 0, 0))],
        out_specs=[pl.BlockSpec((None, None, ncp, LANES), lambda b, c: (b, c, 0, 0)),
                   pl.BlockSpec((None, None, LANES, ncp), lambda b, c: (b, c, 0, 0))],
        out_shape=[jax.ShapeDtypeStruct((B, 2, ncp, LANES), BF16), jax.ShapeDtypeStruct((B, 2, LANES, ncp), BF16)],
        compiler_params=_cparams(("parallel", "parallel")),
        name="nsa_compress",
    )(x16, w1b, w1f, w2b, pe)


def _nsa_attn_kernel(q_ref, kc_ref, vct_ref, ks_ref, vst_ref, kw_ref, vwt_ref, gt_ref, ovt_ref, ext_ref, o_ref,
                     m_s, l_s, acc_s, m_w, l_w, acc_w, s_buf, *, ncp):
    i = pl.program_id(1)
    s0 = i * QT
    heads = range(NS_HEADS)
    hcols = lambda h: slice(h * QT, (h + 1) * QT)
    gcols = lambda h: slice(h // NS_HPG * QT, (h // NS_HPG + 1) * QT)
    iota = lambda shape, d: lax.broadcasted_iota(I32, shape, d)
    qpos = lambda shape: s0 + (iota(shape, 1) & (QT - 1))

    def col_reduce(x, op, final):
        n = x.shape[0]
        while n > 8:
            n //= 2
            x = op(x[:n], x[n:])
        return final(x, axis=0, keepdims=True)

    q8 = q_ref[...].astype(F32)
    zeros = jnp.zeros((QT, NS_HEAD), F32)
    qt = []
    for h in heads:
        qh = q8[:, h * NS_HEAD:(h + 1) * NS_HEAD]
        qt.append(jnp.concatenate([qh, zeros] if h < NS_HPG else [zeros, qh], axis=1).T)
    qt = jnp.concatenate(qt, axis=1).astype(BF16)

    sc = jnp.dot(kc_ref[...], qt, preferred_element_type=F32)
    n_row = iota((ncp, QT), 0)
    cmask = (n_row * CMP_STRIDE + (CMP_LEN - 1) <= s0 + iota((ncp, QT), 1)) & (n_row < ncp - 1)
    cpen = jnp.where(cmask, 0.0, NEG)
    cok = jnp.where(cmask, 1.0, 0.0)
    sc = jnp.concatenate([sc[:, hcols(h)] + cpen for h in heads], axis=1)
    pc = jnp.exp2(sc - col_reduce(sc, jnp.maximum, jnp.max))
    pc = jnp.concatenate([pc[:, hcols(h)] * cok for h in heads], axis=1)
    lc = col_reduce(pc, jnp.add, jnp.sum)
    pc = pc / jnp.where(lc > 0.0, lc, 1.0)
    pcs = jnp.concatenate([functools.reduce(jnp.add, [pc[:, hcols(h)] for h in range(g * NS_HPG, (g + 1) * NS_HPG)])
                           for g in range(NS_KV)], axis=1)
    o_c = jnp.dot(vct_ref[...], pc.astype(BF16), preferred_element_type=F32)

    imp = jnp.dot(ovt_ref[...], pcs, precision=HI, preferred_element_type=F32)
    blk = iota((LANES, NS_KV * QT), 0)
    cur = qpos((LANES, NS_KV * QT)) // SEL_BLOCK
    valid = blk <= cur
    forced = (blk == 0) | (blk == cur) | (blk == cur - 1)
    pri = jnp.where(valid & ~forced, imp, -jnp.inf)
    picked = forced
    blkf = blk.astype(F32)
    for _ in range(SEL_TOPK - 3):
        mx = col_reduce(pri, jnp.maximum, jnp.max)
        hit = blkf == col_reduce(jnp.where(pri == mx, blkf, float(LANES)), jnp.minimum, jnp.min)
        picked = picked | hit
        pri = jnp.where(hit, -jnp.inf, pri)
    selpen = jnp.where(picked & valid, 0.0, NEG).astype(BF16)
    wq = jnp.concatenate([qt, jnp.concatenate([selpen[:, gcols(h)] for h in heads], axis=1)], axis=0)

    def attend(s, vt, pen, m_ref, l_ref, acc_ref):
        if pen is not None:
            s = jnp.concatenate([s[:, hcols(h)] + pen for h in heads], axis=1)
        m_old = m_ref[...]
        m_new = jnp.maximum(m_old, col_reduce(s, jnp.maximum, jnp.max))
        alpha = jnp.exp2(m_old - m_new)
        p = jnp.exp2(s - m_new)
        m_ref[...] = m_new
        l_ref[...] = alpha * l_ref[...] + col_reduce(p, jnp.add, jnp.sum)
        acc_ref[...] = alpha * acc_ref[...] + jnp.dot(vt, p.astype(BF16), preferred_element_type=F32)

    def reset(m_ref, l_ref, acc_ref):
        m_ref[...] = jnp.full(m_ref.shape, NEG, F32)
        l_ref[...] = jnp.zeros_like(l_ref)
        acc_ref[...] = jnp.zeros_like(acc_ref)

    tiles = lambda ref, t0, n: jnp.concatenate([ref[t0 + c] for c in range(n)], axis=1)

    reset(m_s, l_s, acc_s)
    diag = i // (KT // QT)

    def sel_scores(kt, slot):
        k0 = pl.multiple_of(kt * KT, KT)
        keys = jnp.concatenate([ks_ref[pl.ds(k0, KT), :], ext_ref[pl.ds(k0, KT), :]], axis=1)
        s_buf[slot] = jnp.dot(keys, wq, preferred_element_type=F32)

    def sel_attend(kt, slot, causal):
        pen = None
        if causal:
            pen = jnp.where(kt * KT + iota((KT, QT), 0) <= s0 + iota((KT, QT), 1), 0.0, NEG)
        attend(s_buf[slot], tiles(vst_ref, kt * (KT // QT), KT // QT), pen, m_s, l_s, acc_s)

    def sel_pair(j, c):
        sel_scores(2 * j + 1, 1)
        sel_attend(2 * j, 0, False)
        sel_scores(2 * j + 2, 0)
        sel_attend(2 * j + 1, 1, False)
        return c

    d0 = pl.multiple_of(s0, QT)
    a0 = pl.multiple_of(jnp.maximum(s0 - WINDOW, 0), QT)
    s_d = jnp.dot(kw_ref[pl.ds(d0, QT), :], qt, preferred_element_type=F32)
    s_a = jnp.dot(kw_ref[pl.ds(a0, WINDOW), :], qt, preferred_element_type=F32)
    pen_d = jnp.where(iota((QT, QT), 0) <= iota((QT, QT), 1), 0.0, NEG)
    kpos = a0 + iota((WINDOW, QT), 0)
    pen_a = jnp.where((kpos < s0) & (kpos > s0 + iota((WINDOW, QT), 1) - WINDOW), 0.0, NEG)

    sel_scores(0, 0)
    reset(m_w, l_w, acc_w)
    attend(s_d, vwt_ref[i], pen_d, m_w, l_w, acc_w)
    attend(s_a, tiles(vwt_ref, a0 // QT, WINDOW // QT), pen_a, m_w, l_w, acc_w)
    lax.fori_loop(0, diag // 2, sel_pair, 0)

    @pl.when(diag % 2 == 1)
    def _():
        sel_scores(diag, 1)
        sel_attend(diag - 1, 0, False)
        sel_attend(diag, 1, True)

    @pl.when(diag % 2 == 0)
    def _():
        sel_attend(diag, 0, True)

    gtt = gt_ref[...].T
    o_s = acc_s[...] / l_s[...]
    o_w = acc_w[...] / l_w[...]
    outs = []
    for h in heads:
        gate = lambda br: gtt[br * NS_HEADS + h:br * NS_HEADS + h + 1]
        o = gate(0) * o_c[:, hcols(h)] + gate(1) * o_s[:, hcols(h)] + gate(2) * o_w[:, hcols(h)]
        outs.append(o[:NS_HEAD] if h < NS_HPG else o[NS_HEAD:])
    pairs = [jnp.concatenate(outs[p:p + 2], axis=0).T for p in range(0, NS_HEADS, 2)]
    o_ref[...] = jnp.concatenate(pairs, axis=1).astype(BF16)


def _nsa_attn(q3, kcv, kcvt, kv3, vt, gt3, ovt):
    B, S, _ = q3.shape
    ncp = kcv.shape[2]
    R = NS_HEADS * QT
    ext = (jnp.arange(S, dtype=I32)[:, None] // SEL_BLOCK == jnp.arange(LANES, dtype=I32)[None, :]).astype(BF16)
    seq = lambda c: pl.BlockSpec((None, S, LANES), lambda b, i, c=c: (b, 0, c))
    seqt = lambda c: pl.BlockSpec((None, None, S // QT, LANES, QT), lambda b, i, c=c: (c, b, 0, 0, 0))
    return pl.pallas_call(
        functools.partial(_nsa_attn_kernel, ncp=ncp),
        grid=(B, S // QT),
        in_specs=[pl.BlockSpec((None, QT, NS_W), lambda b, i: (b, i, 0)),
                  pl.BlockSpec((None, None, ncp, LANES), lambda b, i: (b, 0, 0, 0)),
                  pl.BlockSpec((None, None, LANES, ncp), lambda b, i: (b, 1, 0, 0)),
                  seq(2), seqt(0), seq(4), seqt(1),
                  pl.BlockSpec((None, QT, LANES), lambda b, i: (b, i, 0)),
                  pl.BlockSpec(ovt.shape, lambda b, i: (0, 0)),
                  pl.BlockSpec(ext.shape, lambda b, i: (0, 0))],
        out_specs=pl.BlockSpec((None, QT, NS_W), lambda b, i: (b, i, 0)),
        out_shape=jax.ShapeDtypeStruct((B, S, NS_W), BF16),
        scratch_shapes=([pltpu.VMEM((1, R), F32), pltpu.VMEM((1, R), F32), pltpu.VMEM((LANES, R), F32)] * 2
                        + [pltpu.VMEM((2, KT, R), F32)]),
        compiler_params=_cparams(("parallel", "arbitrary")),
        name="nsa_attn",
    )(q3, kcv, kcvt, kv3, vt, kv3, vt, gt3, ovt, ext)


def _first_argmax(vals):
    m = vals[0]
    for v in vals[1:]:
        m = jnp.maximum(m, v)
    idx = jnp.full(m.shape, len(vals) - 1, I32)
    for j in range(len(vals) - 2, -1, -1):
        idx = jnp.where(vals[j] == m, j, idx)
    return m, idx


def _outproj_router_kernel(*refs, n_in):
    acts, ws = refs[:n_in], refs[n_in:2 * n_in]
    x_ref, lng_ref, lnb_ref, rwt_ref, rb_ref, tri_ref = refs[2 * n_in:2 * n_in + 6]
    y_ref, e_ref, wt_ref, pos_ref, cnt_ref, cnt = refs[2 * n_in + 6:]
    i = pl.program_id(0)

    @pl.when(i == 0)
    def _():
        cnt[...] = jnp.zeros_like(cnt)

    mix = _bdot(acts[0][...], ws[0][...])
    for a, w in zip(acts[1:], ws[1:]):
        mix = mix + _bdot(a[...], w[...])
    y = _layer_norm_rows(ALPHA * x_ref[...] + mix, lng_ref[...], lnb_ref[...])
    y_ref[...] = y

    logit = lax.dot_general(rwt_ref[...], y, (((1,), (1,)), ((), ())), precision=HI, preferred_element_type=F32)
    aff = jax.nn.sigmoid(logit)
    biased = aff + rb_ref[...]
    neg_inf = -jnp.inf
    g_score, g_i1, g_i2 = [], [], []
    for gi in range(N_GROUPS):
        vals = [biased[gi * EXP_PER_GROUP + j:gi * EXP_PER_GROUP + j + 1, :] for j in range(EXP_PER_GROUP)]
        m1, i1 = _first_argmax(vals)
        m2, i2 = _first_argmax([jnp.where(i1 == j, neg_inf, vals[j]) for j in range(EXP_PER_GROUP)])
        g_score.append(m1 + m2)
        g_i1.append(i1)
        g_i2.append(i2)
    _, grp = _first_argmax(g_score)
    loc1, loc2 = g_i1[-1], g_i2[-1]
    for gi in range(N_GROUPS - 2, -1, -1):
        loc1 = jnp.where(grp == gi, g_i1[gi], loc1)
        loc2 = jnp.where(grp == gi, g_i2[gi], loc2)
    e1 = grp * EXP_PER_GROUP + loc1
    e2 = grp * EXP_PER_GROUP + loc2
    eio = lax.broadcasted_iota(I32, aff.shape, 0)
    oh1 = eio == e1
    oh2 = eio == e2
    a1 = jnp.sum(jnp.where(oh1, aff, 0.0), axis=0, keepdims=True)
    a2 = jnp.sum(jnp.where(oh2, aff, 0.0), axis=0, keepdims=True)
    tot = a1 + a2
    e_ref[...] = jnp.concatenate([e1, e2], axis=0)
    wt_ref[...] = jnp.concatenate([a1 / tot, a2 / tot], axis=0)

    ohs = oh1.astype(F32) + oh2.astype(F32)
    before = jnp.dot(ohs.astype(BF16), tri_ref[...], preferred_element_type=F32) + cnt[...]
    p1 = jnp.sum(jnp.where(oh1, before, 0.0), axis=0, keepdims=True)
    p2 = jnp.sum(jnp.where(oh2, before, 0.0), axis=0, keepdims=True)
    pos_ref[...] = jnp.concatenate([p1, p2], axis=0).astype(I32)
    cnt[...] = cnt[...] + jnp.sum(ohs, axis=1, keepdims=True)
    cnt_ref[...] = jnp.broadcast_to(cnt[...], cnt_ref.shape)


def _outproj_router(acts, ws, xres, lng, lnb, rwt, rb, tm=256):
    T = xres.shape[0]
    n_in = len(acts)
    tri = (lax.broadcasted_iota(I32, (tm, tm), 0) < lax.broadcasted_iota(I32, (tm, tm), 1)).astype(BF16)
    row = lambda i: (i, 0)
    const = lambda a: pl.BlockSpec(a.shape, lambda i: (0,) * a.ndim)
    lane_blk = pl.BlockSpec((TOP_K, tm), lambda i: (0, i))
    return pl.pallas_call(
        functools.partial(_outproj_router_kernel, n_in=n_in),
        grid=(T // tm,),
        in_specs=([pl.BlockSpec((tm, a.shape[1]), row) for a in acts] + [const(w) for w in ws]
                  + [pl.BlockSpec((tm, D_MODEL), row), const(lng), const(lnb), const(rwt), const(rb), const(tri)]),
        out_specs=[pl.BlockSpec((tm, D_MODEL), row), lane_blk, lane_blk, lane_blk,
                   pl.BlockSpec((N_EXPERTS, LANES), lambda i: (0, 0))],
        out_shape=[jax.ShapeDtypeStruct((T, D_MODEL), F32), jax.ShapeDtypeStruct((TOP_K, T), I32),
                   jax.ShapeDtypeStruct((TOP_K, T), F32), jax.ShapeDtypeStruct((TOP_K, T), I32),
                   jax.ShapeDtypeStruct((N_EXPERTS, LANES), F32)],
        scratch_shapes=[pltpu.VMEM((N_EXPERTS, 1), F32)],
        compiler_params=_cparams(("arbitrary",)),
        name="outproj_router",
    )(*acts, *ws, xres, lng, lnb, rwt, rb, tri)


def _dispatch_kernel(dest_hbm, x_ref, zero_hbm, xs_hbm, dsm, sem_idx, sem):
    del zero_hbm
    i = pl.program_id(0)
    tm = x_ref.shape[0]
    idx_copy = pltpu.make_async_copy(dest_hbm.at[i], dsm, sem_idx)
    idx_copy.start()
    idx_copy.wait()

    def row_copy(r, k):
        return pltpu.make_async_copy(x_ref.at[pl.ds(r, 1)], xs_hbm.at[pl.ds(dsm[k * tm + r], 1)], sem)

    def start(r, c):
        for k in range(TOP_K):
            row_copy(r, k).start()
        return c

    lax.fori_loop(0, tm, start, 0, unroll=ROW_DMA_UNROLL)
    for k in range(TOP_K):
        pltpu.make_async_copy(x_ref, xs_hbm.at[pl.ds(0, tm)], sem).wait()


def _dispatch(dest_tiles, x, rows, tm):
    T = x.shape[0]
    zeros = jnp.zeros((rows, D_MODEL), F32)
    return pl.pallas_call(
        _dispatch_kernel,
        grid=(T // tm,),
        in_specs=[pl.BlockSpec(memory_space=pl.ANY), pl.BlockSpec((tm, D_MODEL), lambda i: (i, 0)),
                  pl.BlockSpec(memory_space=pl.ANY)],
        out_specs=pl.BlockSpec(memory_space=pl.ANY),
        out_shape=jax.ShapeDtypeStruct((rows, D_MODEL), F32),
        scratch_shapes=[pltpu.SMEM((TOP_K * tm,), I32), pltpu.SemaphoreType.DMA(()), pltpu.SemaphoreType.DMA(())],
        input_output_aliases={2: 0},
        compiler_params=_cparams(("arbitrary",)),
        name="moe_dispatch",
    )(dest_tiles, x, zeros)


def _ffn_kernel(be_ref, nu_ref, xs_ref, wg_ref, wu_ref, wd_ref, y_ref, h_ref):
    del be_ref
    i = pl.program_id(0)
    half = D_EXPERT // 2

    @pl.when(i < nu_ref[0])
    def _():
        xb = xs_ref[...].astype(BF16)
        for c in range(2):
            cs = slice(c * half, (c + 1) * half)
            gt = jnp.dot(xb, wg_ref[:, cs], preferred_element_type=F32)
            up = jnp.dot(xb, wu_ref[:, cs], preferred_element_type=F32)
            h_ref[:, cs] = (jax.nn.silu(gt) * up).astype(BF16)
        y_ref[...] = jnp.dot(h_ref[...], wd_ref[...], preferred_element_type=F32)

    @pl.when(i >= nu_ref[0])
    def _():
        y_ref[...] = jnp.zeros_like(y_ref)


def _ffn(blk_exp, n_used, xs, wg, wu, wd):
    rows = xs.shape[0]
    wspec = lambda: pl.BlockSpec((None, D_MODEL, D_EXPERT), lambda i, be, nu: (be[i], 0, 0))
    return pl.pallas_call(
        _ffn_kernel,
        grid_spec=pltpu.PrefetchScalarGridSpec(
            num_scalar_prefetch=2,
            grid=(rows // MOE_BLOCK,),
            in_specs=[pl.BlockSpec((MOE_BLOCK, D_MODEL), lambda i, be, nu: (i, 0)), wspec(), wspec(),
                      pl.BlockSpec((None, D_EXPERT, D_MODEL), lambda i, be, nu: (be[i], 0, 0))],
            out_specs=pl.BlockSpec((MOE_BLOCK, D_MODEL), lambda i, be, nu: (i, 0)),
            scratch_shapes=[pltpu.VMEM((MOE_BLOCK, D_EXPERT), BF16)]),
        out_shape=jax.ShapeDtypeStruct((rows, D_MODEL), F32),
        compiler_params=_cparams(("arbitrary",)),
        name="moe_ffn",
    )(blk_exp, n_used, xs, wg, wu, wd)


def _combine_kernel(dest_hbm, y_hbm, x_ref, wt_ref, lng_ref, lnb_ref, o_ref, dsm, buf, sem_idx, sem):
    i = pl.program_id(0)
    tm = x_ref.shape[0]
    idx_copy = pltpu.make_async_copy(dest_hbm.at[i], dsm, sem_idx)
    idx_copy.start()
    idx_copy.wait()

    def row_copy(r, k):
        return pltpu.make_async_copy(y_hbm.at[pl.ds(dsm[k * tm + r], 1)], buf.at[k, pl.ds(r, 1)], sem)

    def start(r, c):
        for k in range(TOP_K):
            row_copy(r, k).start()
        return c

    lax.fori_loop(0, tm, start, 0, unroll=ROW_DMA_UNROLL)
    for k in range(TOP_K):
        pltpu.make_async_copy(y_hbm.at[pl.ds(0, tm)], buf.at[k], sem).wait()
    wt = wt_ref[...]
    z = ALPHA * x_ref[...] + wt[:, 0:1] * buf[0] + wt[:, 1:2] * buf[1]
    o_ref[...] = _layer_norm_rows(z, lng_ref[...], lnb_ref[...])


def _combine(dest_tiles, y, x, wt, lng, lnb, tm):
    T = x.shape[0]
    row = lambda i: (i, 0)
    vec = pl.BlockSpec((1, D_MODEL), lambda i: (0, 0))
    return pl.pallas_call(
        _combine_kernel,
        grid=(T // tm,),
        in_specs=[pl.BlockSpec(memory_space=pl.ANY), pl.BlockSpec(memory_space=pl.ANY),
                  pl.BlockSpec((tm, D_MODEL), row), pl.BlockSpec((tm, TOP_K), row), vec, vec],
        out_specs=pl.BlockSpec((tm, D_MODEL), row),
        out_shape=jax.ShapeDtypeStruct((T, D_MODEL), F32),
        scratch_shapes=[pltpu.SMEM((TOP_K * tm,), I32), pltpu.VMEM((TOP_K, tm, D_MODEL), F32),
                        pltpu.SemaphoreType.DMA(()), pltpu.SemaphoreType.DMA(())],
        compiler_params=_cparams(("arbitrary",)),
        name="moe_combine",
    )(dest_tiles, y, x, wt, lng, lnb)


def _moe(x1, e, wt, pos, cnt, wg, wu, wd, lng, lnb, tm=256):
    T = x1.shape[0]
    n_blocks = -(-(T * TOP_K) // MOE_BLOCK) + N_EXPERTS
    rows = n_blocks * MOE_BLOCK
    counts = cnt[:, 0].astype(I32)
    padded = (counts + MOE_BLOCK - 1) // MOE_BLOCK * MOE_BLOCK
    pad_end = jnp.cumsum(padded)
    pad_start = pad_end - padded
    dest = pos
    for j in range(N_EXPERTS):
        dest = dest + jnp.where(e == j, pad_start[j], 0)
    blk_start = jnp.arange(n_blocks, dtype=I32) * MOE_BLOCK
    blk_exp = jnp.minimum(jnp.sum((pad_end[None, :] <= blk_start[:, None]).astype(I32), axis=1), N_EXPERTS - 1)
    n_used = (pad_end[-1:] // MOE_BLOCK).astype(I32)
    dest_tiles = dest.reshape(TOP_K, T // tm, tm).transpose(1, 0, 2).reshape(T // tm, TOP_K * tm)
    xs = _dispatch(dest_tiles, x1, rows, tm)
    y = _ffn(blk_exp, n_used, xs, wg, wu, wd)
    return _combine(dest_tiles, y, x1, wt.T, lng, lnb, tm)


def _proj_rt_kernel(x_ref, w_ref, c_ref, s_ref, q_ref, k_ref, v_ref, g_ref):
    xb = x_ref[...].astype(BF16)
    cs, sn = c_ref[...], s_ref[...]
    rope = lambda y: y * cs + pltpu.roll(y, RT_QK // 2, 1) * sn
    for c in range(RT_HEADS):
        cols = slice(c * RT_QK, (c + 1) * RT_QK)
        q_ref[:, cols] = rope(jnp.dot(xb, w_ref[:, cols], preferred_element_type=F32)).astype(BF16)
    for c in range(RT_HEADS):
        cols = slice(c * RT_QK, (c + 1) * RT_QK)
        yk = jnp.dot(xb, w_ref[:, RT_QKW + c * RT_QK:RT_QKW + (c + 1) * RT_QK], preferred_element_type=F32)
        k_ref[:, cols] = (rope(yk) * RT_QK ** -0.5).astype(BF16)
    step = 1024
    for c in range(RT_VW // step):
        cols = slice(c * step, (c + 1) * step)
        v_ref[:, cols] = jnp.dot(xb, w_ref[:, 2 * RT_QKW + c * step:2 * RT_QKW + (c + 1) * step],
                                 preferred_element_type=F32).astype(BF16)
        base = 2 * RT_QKW + RT_VW
        g_ref[:, cols] = jax.nn.silu(jnp.dot(xb, w_ref[:, base + c * step:base + (c + 1) * step],
                                             preferred_element_type=F32))


def _proj_rt(xt, w, cs, sn, S, tm=256):
    T = xt.shape[0]
    nseq = S // tm
    row = lambda i: (i, 0)
    tab = lambda i: (i % nseq, 0)
    return pl.pallas_call(
        _proj_rt_kernel,
        grid=(T // tm,),
        in_specs=[pl.BlockSpec((tm, D_MODEL), row), pl.BlockSpec(w.shape, lambda i: (0, 0)),
                  pl.BlockSpec((tm, RT_QK), tab), pl.BlockSpec((tm, RT_QK), tab)],
        out_specs=[pl.BlockSpec((tm, RT_QKW), row), pl.BlockSpec((tm, RT_QKW), row),
                   pl.BlockSpec((tm, RT_VW), row), pl.BlockSpec((tm, RT_VW), row)],
        out_shape=[jax.ShapeDtypeStruct((T, RT_QKW), BF16), jax.ShapeDtypeStruct((T, RT_QKW), BF16),
                   jax.ShapeDtypeStruct((T, RT_VW), BF16), jax.ShapeDtypeStruct((T, RT_VW), F32)],
        compiler_params=_cparams(("parallel",)),
        name="proj_rt",
    )(xt, w, cs, sn)


def _retention_kernel(q_ref, k_ref, v_ref, sg_ref, dec_ref, qd_ref, kd_ref, cd_ref, gng_ref, gnb_ref, o_ref, state):
    j = pl.program_id(1)

    @pl.when(j == 0)
    def _():
        state[...] = jnp.zeros_like(state)

    H = range(RT_HEADS)
    q = [q_ref[:, h * RT_QK:(h + 1) * RT_QK] for h in H]
    k = [k_ref[:, h * RT_QK:(h + 1) * RT_QK] for h in H]
    v = [v_ref[:, h * RT_V:(h + 1) * RT_V] for h in H]
    r_old = [state[h] for h in H]
    att = [(_dot_nt(q[h], k[h]) * dec_ref[h]).astype(BF16) for h in H]
    cross = [jnp.dot(q[h], r_old[h].astype(BF16), preferred_element_type=F32) * qd_ref[h] for h in H]
    inner = [jnp.dot(att[h], v[h], preferred_element_type=F32) for h in H]
    kdec = [(k[h].astype(F32) * kd_ref[h]).astype(BF16) for h in H]
    for h in H:
        upd = lax.dot_general(kdec[h], v[h], (((0,), (0,)), ((), ())), preferred_element_type=F32)
        state[h] = r_old[h] * cd_ref[h][:, 0:1] + upd
    for h in H:
        cols = slice(h * RT_V, (h + 1) * RT_V)
        o = inner[h] + cross[h]
        mu = jnp.mean(o, axis=1, keepdims=True)
        oc = o - mu
        var = jnp.mean(oc * oc, axis=1, keepdims=True)
        on = oc * lax.rsqrt(var + RT_GN_EPS) * gng_ref[:, cols] + gnb_ref[:, cols]
        o_ref[:, cols] = (sg_ref[:, cols] * on).astype(BF16)


def _retention(q3, k3, v3, sg3, dec, qd, kd, cd, gng, gnb):
    B, S, _ = q3.shape
    C = RT_CHUNK
    qk = pl.BlockSpec((None, C, RT_QKW), lambda b, j: (b, j, 0))
    vv = pl.BlockSpec((None, C, RT_VW), lambda b, j: (b, j, 0))
    const = lambda a: pl.BlockSpec(a.shape, lambda b, j: (0,) * a.ndim)
    return pl.pallas_call(
        _retention_kernel,
        grid=(B, S // C),
        in_specs=[qk, qk, vv, vv, const(dec), const(qd), const(kd), const(cd), const(gng), const(gnb)],
        out_specs=vv,
        out_shape=jax.ShapeDtypeStruct((B, S, RT_VW), BF16),
        scratch_shapes=[pltpu.VMEM((RT_HEADS, RT_QK, RT_V), F32)],
        compiler_params=_cparams(("parallel", "arbitrary")),
        name="retention",
    )(q3, k3, v3, sg3, dec, qd, kd, cd, gng, gnb)


def _nsa_rope_tables(S):
    half = ROPE_DIM // 2
    inv = ROPE_THETA ** (-jnp.arange(half, dtype=F32) / half)
    ang = jnp.arange(S, dtype=F32)[:, None] * inv[None, :]
    cos, sin = jnp.cos(ang), jnp.sin(ang)
    zeros = lambda n: jnp.zeros((S, n), F32)
    cn = jnp.concatenate([cos, cos, jnp.ones((S, NS_HEAD - ROPE_DIM), F32)], axis=1)
    s1 = jnp.concatenate([-sin, zeros(NS_HEAD - half)], axis=1)
    s2 = jnp.concatenate([zeros(half), sin, zeros(NS_HEAD - ROPE_DIM)], axis=1)
    two = lambda a: jnp.concatenate([a, a], axis=1)
    return two(cn), two(s1), two(s2)


def _rt_rope_tables(S):
    inv = RT_THETA ** (-jnp.linspace(0.0, 1.0, RT_QK // 2, dtype=F32))
    ang = jnp.arange(S, dtype=F32)[:, None] * inv[None, :]
    cos, sin = jnp.cos(ang), jnp.sin(ang)
    return jnp.concatenate([cos, cos], axis=1), jnp.concatenate([-sin, sin], axis=1)


def _rt_decay_tables():
    log_g = jnp.log(1.0 - 2.0 ** (-5.0 - jnp.arange(RT_HEADS, dtype=F32)))
    idx = jnp.arange(RT_CHUNK, dtype=F32)
    diff = idx[:, None] - idx[None, :]
    dec = jnp.where(diff >= 0, jnp.exp(jnp.maximum(diff, 0.0) * log_g[:, None, None]), 0.0)
    qd = jnp.exp((idx + 1.0) * log_g[:, None])[..., None]
    kd = jnp.exp((RT_CHUNK - 1.0 - idx) * log_g[:, None])[..., None]
    cd = jnp.broadcast_to(jnp.exp(RT_CHUNK * log_g)[:, None, None], (RT_HEADS, 1, LANES))
    return dec, qd, kd, cd


def _overlap_table(S, ncp):
    n_cmp = (S - CMP_LEN) // CMP_STRIDE + 1
    n_sel = S // SEL_BLOCK
    cs = jnp.arange(ncp) * CMP_STRIDE
    ss = jnp.arange(LANES) * SEL_BLOCK
    ov = jnp.clip(jnp.minimum(cs[:, None] + CMP_LEN, ss[None, :] + SEL_BLOCK)
                  - jnp.maximum(cs[:, None], ss[None, :]), 0, None).astype(F32) / CMP_LEN
    keep = (jnp.arange(ncp)[:, None] < n_cmp) & (jnp.arange(LANES)[None, :] < n_sel)
    return jnp.where(keep, ov, 0.0)


def kernel(x, ab_w_in, ab_w_out, rk_mu, rk_w0, rk_w1, rk_w2, rk_a0, rk_a1, rk_a2, rk_g1, rk_g2, rk_kk, rk_ka, rk_rk,
           rk_ln, ns_pe, ns_c_w1, ns_c_w2, rt_w_in, rt_w_out, rt_gn, router_w, router_b, moe_w_gate, moe_w_up,
           moe_w_down, ln):
    B, S, D = x.shape
    T = B * S
    assert D == D_MODEL and S % 256 == 0 and S // SEL_BLOCK <= LANES and S >= WINDOW
    xt = x.reshape(T, D)
    rwt = router_w.T
    rb = router_b.reshape(N_EXPERTS, 1)
    vec = lambda a: a.reshape(1, -1)

    w_in = ab_w_in[0]
    n_gate = 3 * NS_HEADS
    w_cat = jnp.concatenate([w_in[:, :-n_gate], jnp.pad(w_in[:, -n_gate:], ((0, 0), (0, LANES - n_gate)))],
                            axis=1).astype(BF16)
    cn, s1, s2 = _nsa_rope_tables(S)
    prk, q, kv, gt, vt = _proj_ab(xt, w_cat, cn, s1, s2, S)

    ones = (jnp.arange(RK_W)[:, None] // RK_HEAD == jnp.arange(RK_W)[None, :] // RK_HEAD).astype(F32)
    b16 = lambda a: a.astype(BF16)
    r, lw, km, v, kk, bb, g, bon = _rwkv_prep(
        prk.reshape(B, S, 4 * RK_W), rk_mu[0], vec(rk_w0[0]), b16(rk_w1[0]), b16(rk_w2[0]), vec(rk_a0[0]),
        b16(rk_a1[0]), b16(rk_a2[0]), b16(rk_g1[0]), b16(rk_g2[0]), vec(rk_kk[0]), vec(rk_ka[0]), vec(rk_rk[0]), ones)
    o_a = _rwkv_chunk(r, lw, km, v, kk, bb, g, bon, rk_ln[0, 0:1], rk_ln[0, 1:2])

    ncp = S // CMP_STRIDE
    kv3 = kv.reshape(B, S, 6 * LANES)
    x16 = (kv3[:, :, :2 * LANES].reshape(B, ncp, CMP_STRIDE, 2, NS_KV, NS_HEAD)
           .transpose(0, 3, 4, 1, 2, 5).reshape(B, 2, NS_KV, ncp, CMP_STRIDE * NS_HEAD))
    kcv, kcvt = _nsa_compress(x16, b16(ns_c_w1[0]), ns_c_w1[0], b16(ns_c_w2[0]),
                              ns_pe[0].reshape(2, 1, CMP_LEN * NS_HEAD))
    o_b = _nsa_attn(q.reshape(B, S, NS_W), kcv, kcvt, kv3, vt.reshape(2, B, S // QT, LANES, QT),
                    gt.reshape(B, S, LANES), _overlap_table(S, ncp).T)

    w_out = b16(ab_w_out[0])
    x1, e, wt, pos, cnt = _outproj_router([o_a.reshape(T, RK_W), o_b.reshape(T, NS_W)], [w_out[:RK_W], w_out[RK_W:]],
                                          xt, ln[0, 0, 0:1], ln[0, 0, 1:2], rwt, rb)
    x2 = _moe(x1, e, wt, pos, cnt, b16(moe_w_gate[0]), b16(moe_w_up[0]), b16(moe_w_down[0]),
              ln[0, 1, 0:1], ln[0, 1, 1:2])

    cs, sn = _rt_rope_tables(S)
    qr, kr, vr, sg = _proj_rt(x2, b16(rt_w_in[0]), cs, sn, S)
    dec, qd, kd, cd = _rt_decay_tables()
    ret = _retention(qr.reshape(B, S, RT_QKW), kr.reshape(B, S, RT_QKW), vr.reshape(B, S, RT_VW),
                     sg.reshape(B, S, RT_VW), dec, qd, kd, cd, rt_gn[0, 0:1], rt_gn[0, 1:2])
    x3, e, wt, pos, cnt = _outproj_router([ret.reshape(T, RT_VW)], [b16(rt_w_out[0])], x2,
                                          ln[1, 0, 0:1], ln[1, 0, 1:2], rwt, rb)
    x4 = _moe(x3, e, wt, pos, cnt, b16(moe_w_gate[1]), b16(moe_w_up[1]), b16(moe_w_down[1]),
              ln[1, 1, 0:1], ln[1, 1, 1:2])
    return x4.reshape(B, S, D)
```

```python
import functools
import math

import jax
import jax.numpy as jnp
from jax import lax
from jax.experimental import pallas as pl
from jax.experimental.pallas import tpu as pltpu

F32 = jnp.float32
BF16 = jnp.bfloat16
I32 = jnp.int32
HI = lax.Precision.HIGHEST

LANES = 128
VMEM_LIMIT = 56 * 1024 * 1024

D_MODEL = 1024
RK_HEADS, RK_HEAD = 8, 64
RK_W = RK_HEADS * RK_HEAD
RK_DECAY_SCALE = 0.606531
RK_LN_EPS = 64e-5
NS_HEADS, NS_KV, NS_HPG, NS_HEAD = 8, 2, 4, 64
NS_W = NS_HEADS * NS_HEAD
CMP_LEN, CMP_STRIDE, SEL_BLOCK, SEL_TOPK, WINDOW = 32, 16, 64, 16, 512
ROPE_THETA = 500000.0
ROPE_DIM = NS_HEAD // 4
Q_SCALE = NS_HEAD ** -0.5 * math.log2(math.e)
QT = 128
KT = 512
RT_HEADS, RT_QK, RT_V = 8, 128, 256
RT_QKW, RT_VW = RT_HEADS * RT_QK, RT_HEADS * RT_V
RT_CHUNK = 128
RT_THETA = 10000.0
RT_GN_EPS = 1e-5
N_EXPERTS, N_GROUPS, EXP_PER_GROUP, TOP_K = 16, 4, 4, 2
D_EXPERT = 1024
MOE_BLOCK = 512
ROW_DMA_UNROLL = 8
DEPTH = 2
ALPHA = (2.0 * DEPTH) ** 0.25
LN_EPS = 1e-5
NEG = -1e30


def _cparams(sem):
    return pltpu.CompilerParams(dimension_semantics=sem, vmem_limit_bytes=VMEM_LIMIT)


def _bdot(a, w):
    return jnp.dot(a.astype(BF16), w, preferred_element_type=F32)


def _dot_nt(a, b):
    return lax.dot_general(a, b, (((1,), (1,)), ((), ())), preferred_element_type=F32)


def _layer_norm_rows(z, g, b):
    mu = jnp.mean(z, axis=1, keepdims=True)
    zc = z - mu
    var = jnp.mean(zc * zc, axis=1, keepdims=True)
    return zc * lax.rsqrt(var + LN_EPS) * g + b


def _proj_ab_kernel(x_ref, w_ref, cn_ref, s1_ref, s2_ref, prk_ref, q_ref, kv_ref, gt_ref, vt_ref):
    xb = x_ref[...].astype(BF16)
    for c in range(4):
        prk_ref[:, c * RK_W:(c + 1) * RK_W] = jnp.dot(xb, w_ref[:, c * RK_W:(c + 1) * RK_W],
                                                      preferred_element_type=F32)
    cn, s1, s2 = cn_ref[...], s1_ref[...], s2_ref[...]

    def rope(y):
        return y * cn + pltpu.roll(y, LANES - ROPE_DIM // 2, 1) * s1 + pltpu.roll(y, ROPE_DIM // 2, 1) * s2

    base = 4 * RK_W
    yq = jnp.dot(xb, w_ref[:, base:base + NS_W], preferred_element_type=F32)
    for c in range(NS_W // LANES):
        q_ref[:, c * LANES:(c + 1) * LANES] = (rope(yq[:, c * LANES:(c + 1) * LANES]) * Q_SCALE).astype(BF16)
    base += NS_W
    ykv = jnp.dot(xb, w_ref[:, base:base + 6 * LANES], preferred_element_type=F32)
    for c in range(6):
        y = ykv[:, c * LANES:(c + 1) * LANES]
        if c % 2 == 0:
            y = rope(y)
        kv_ref[:, c * LANES:(c + 1) * LANES] = y.astype(BF16)
        if c in (3, 5):
            for t in range(y.shape[0] // QT):
                vt_ref[c // 2 - 1, t] = y[t * QT:(t + 1) * QT].T.astype(BF16)
    base += 6 * LANES
    gt_ref[...] = jax.nn.sigmoid(jnp.dot(xb, w_ref[:, base:base + LANES], preferred_element_type=F32))


def _proj_ab(xt, w, cn, s1, s2, S, tm=512):
    T = xt.shape[0]
    ncols = w.shape[1]
    nseq = S // tm
    row = lambda i: (i, 0)
    tab = lambda i: (i % nseq, 0)
    return pl.pallas_call(
        _proj_ab_kernel,
        grid=(T // tm,),
        in_specs=[pl.BlockSpec((tm, D_MODEL), row),
                  pl.BlockSpec((D_MODEL, ncols), lambda i: (0, 0)),
                  pl.BlockSpec((tm, LANES), tab), pl.BlockSpec((tm, LANES), tab), pl.BlockSpec((tm, LANES), tab)],
        out_specs=[pl.BlockSpec((tm, 4 * RK_W), row), pl.BlockSpec((tm, NS_W), row),
                   pl.BlockSpec((tm, 6 * LANES), row), pl.BlockSpec((tm, LANES), row),
                   pl.BlockSpec((2, tm // QT, LANES, QT), lambda i: (0, i, 0, 0))],
        out_shape=[jax.ShapeDtypeStruct((T, 4 * RK_W), F32), jax.ShapeDtypeStruct((T, NS_W), BF16),
                   jax.ShapeDtypeStruct((T, 6 * LANES), BF16), jax.ShapeDtypeStruct((T, LANES), F32),
                   jax.ShapeDtypeStruct((2, T // QT, LANES, QT), BF16)],
        compiler_params=_cparams(("parallel",)),
        name="proj_ab",
    )(xt, w, cn, s1, s2)


def _rwkv_prep_kernel(p_ref, mu_ref, w0_ref, w1_ref, w2_ref, a0_ref, a1_ref, a2_ref, g1_ref, g2_ref,
                      kk_ref, ka_ref, rk_ref, ones_ref,
                      r_o, w_o, k_o, v_o, kk_o, b_o, g_o, bon_o, carry):
    j = pl.program_id(1)
    p = p_ref[...]
    tm = p.shape[0]

    @pl.when(j == 0)
    def _():
        carry[...] = jnp.zeros_like(carry)

    rowi = lax.broadcasted_iota(I32, p.shape, 0)
    prev = jnp.where(rowi == 0, carry[...], pltpu.roll(p, 1, 0))
    carry[...] = p[tm - 1:tm, :]
    dp = prev - p
    sl = lambda a, c: a[:, c * RK_W:(c + 1) * RK_W]
    mu = mu_ref[...]
    r = sl(p, 0) + sl(dp, 0) * mu[0:1]
    k = sl(p, 1) + sl(dp, 1) * mu[1:2]
    v = sl(p, 2) + sl(dp, 2) * mu[2:3]
    xw = sl(p, 3) + sl(dp, 3) * mu[3:4]
    xa = sl(p, 3) + sl(dp, 3) * mu[4:5]
    xg = sl(p, 3) + sl(dp, 3) * mu[5:6]
    lw = -RK_DECAY_SCALE * jax.nn.sigmoid(w0_ref[...] + _bdot(jnp.tanh(_bdot(xw, w1_ref[...])), w2_ref[...]))
    a = jax.nn.sigmoid(a0_ref[...] + _bdot(_bdot(xa, a1_ref[...]), a2_ref[...]))
    g = _bdot(jax.nn.sigmoid(_bdot(xg, g1_ref[...])), g2_ref[...])
    ones = ones_ref[...]
    kk = k * kk_ref[...]
    ss = jnp.dot(kk * kk, ones, precision=HI, preferred_element_type=F32)
    kk = kk / jnp.maximum(jnp.sqrt(ss), 1e-12)
    km = k * (1.0 + (a - 1.0) * ka_ref[...])
    bon = jnp.dot(r * km * rk_ref[...], ones, precision=HI, preferred_element_type=F32) * v
    r_o[...] = r
    w_o[...] = lw
    k_o[...] = km
    v_o[...] = v
    kk_o[...] = kk
    b_o[...] = kk * a
    g_o[...] = g
    bon_o[...] = bon


def _rwkv_prep(prk3, mu, w0, w1, w2, a0, a1, a2, g1, g2, k_k, k_a, r_k, ones, tm=512):
    B, S, _ = prk3.shape
    full = lambda a: pl.BlockSpec(a.shape, lambda b, j: (0,) * a.ndim)
    params = [mu, w0, w1, w2, a0, a1, a2, g1, g2, k_k, k_a, r_k, ones]
    ospec = pl.BlockSpec((None, tm, RK_W), lambda b, j: (b, j, 0))
    return pl.pallas_call(
        _rwkv_prep_kernel,
        grid=(B, S // tm),
        in_specs=[pl.BlockSpec((None, tm, 4 * RK_W), lambda b, j: (b, j, 0))] + [full(a) for a in params],
        out_specs=[ospec] * 8,
        out_shape=[jax.ShapeDtypeStruct((B, S, RK_W), F32)] * 8,
        scratch_shapes=[pltpu.VMEM((1, 4 * RK_W), F32)],
        compiler_params=_cparams(("parallel", "arbitrary")),
        name="rwkv_prep",
    )(prk3, *params)


RK_CHUNK = 16


def _pdot(a, b, dims, precise):
    if precise:
        return lax.dot_general(a, b, (dims, ((), ())), precision=HI, preferred_element_type=F32)
    return lax.dot_general(a.astype(BF16), b.astype(BF16), (dims, ((), ())), preferred_element_type=F32)


def _rwkv_chunk_kernel(r_ref, lw_ref, k_ref, v_ref, kk_ref, b_ref, g_ref, bon_ref, lng_ref, lnb_ref, o_ref, ht,
                       *, precise):
    j = pl.program_id(1)

    @pl.when(j == 0)
    def _():
        ht[...] = jnp.zeros_like(ht)

    NB, TT = r_ref.shape[0], r_ref.shape[1]
    C, N = RK_CHUNK, RK_HEAD
    mm = lambda a, b: _pdot(a, b, ((1,), (0,)), precise)
    mm_nt = lambda a, b: _pdot(a, b, ((1,), (1,)), precise)
    mm_tn = lambda a, b: _pdot(a, b, ((0,), (0,)), precise)
    wide = lambda ref: jnp.concatenate([ref[n] for n in range(NB)], axis=1)

    lw = wide(lw_ref)
    rowc = lax.broadcasted_iota(I32, lw.shape, 0) & (C - 1)
    linc, lrev = lw, lw
    sh = 1
    while sh < C:
        linc = linc + jnp.where(rowc >= sh, pltpu.roll(linc, sh, 0), 0.0)
        lrev = lrev + jnp.where(rowc < C - sh, pltpu.roll(lrev, TT - sh, 0), 0.0)
        sh *= 2
    lrev = lrev - lw
    r, k, v, kk, b = wide(r_ref), wide(k_ref), wide(v_ref), wide(kk_ref), wide(b_ref)
    e_in, e_inv, e_rev = jnp.exp(linc), jnp.exp(-linc), jnp.exp(lrev)
    kkd = kk * jnp.exp(linc - lw)
    rd = r * e_in
    binv, kinv = b * e_inv, k * e_inv
    bd, kd = b * e_rev, k * e_rev
    gam = jnp.exp(linc + lrev)

    ti = lax.broadcasted_iota(I32, (TT, TT), 0)
    tj = lax.broadcasted_iota(I32, (TT, TT), 1)
    same = (ti // C) == (tj // C)
    strict = same & (tj < ti)
    incl = same & (tj <= ti)

    H = range(NB * RK_HEADS)
    lo = lambda a: a if precise else a.astype(BF16)
    hs = lambda a: [a[:, h * N:(h + 1) * N] for h in H]
    rows2 = lambda a, b: jnp.concatenate([a, b], axis=0)
    kkd_h, rd_h, v_h = hs(lo(kkd)), hs(lo(rd)), hs(lo(v))
    binv_h, kinv_h, bd_h, kd_h = hs(lo(binv)), hs(lo(kinv)), hs(lo(bd)), hs(lo(kd))
    gam_h = hs(gam)
    gm = [mm_nt(rows2(kkd_h[h], rd_h[h]), rows2(binv_h[h], kinv_h[h])) for h in H]
    a_b = [lo(jnp.where(strict, gm[h][:TT, :TT], 0.0)) for h in H]
    b_rb = [lo(jnp.where(incl, gm[h][TT:, :TT], 0.0)) for h in H]
    akb = [lo(rows2(jnp.where(strict, gm[h][:TT, TT:], 0.0), jnp.where(incl, gm[h][TT:, TT:], 0.0))) for h in H]
    av = [mm(akb[h], v_h[h]) for h in H]
    x = [jnp.concatenate([kkd_h[h].astype(F32), av[h][:TT]], axis=1) for h in H]
    a2 = [lo(mm(a_b[h], a_b[h])) for h in H]
    a4 = [lo(mm(a2[h], a2[h])) for h in H]
    a8 = [lo(mm(a4[h], a4[h])) for h in H]
    x = [x[h] + mm(a8[h], lo(x[h])) for h in H]
    x = [x[h] + mm(a4[h], lo(x[h])) for h in H]
    x = [x[h] + mm(a2[h], lo(x[h])) for h in H]
    x = [x[h] - mm(a_b[h], lo(x[h])) for h in H]
    wt = [lo(x[h][:, :N]) for h in H]
    h_t = [ht[h] for h in H]
    us = [[] for _ in H]
    rhs = [[] for _ in H]
    for c in range(TT // C):
        rs = slice(c * C, (c + 1) * C)
        xh = [mm_nt(rows2(wt[h][rs], rd_h[h][rs]), lo(h_t[h])) for h in H]
        for h in H:
            u_c = -(xh[h][:C] + x[h][rs, N:])
            us[h].append(u_c)
            rhs[h].append(xh[h][C:])
        upd = [mm_tn(rows2(lo(us[h][c]), v_h[h][rs]), rows2(bd_h[h][rs], kd_h[h][rs])) for h in H]
        h_t = [h_t[h] * gam_h[h][c * C:c * C + 1] + upd[h] for h in H]
    outs = []
    for h in H:
        ht[h] = h_t[h]
        o = jnp.concatenate(rhs[h], axis=0) + mm(b_rb[h], lo(jnp.concatenate(us[h], axis=0))) + av[h][TT:]
        mu = jnp.mean(o, axis=1, keepdims=True)
        oc = o - mu
        var = jnp.mean(oc * oc, axis=1, keepdims=True)
        outs.append(oc * lax.rsqrt(var + RK_LN_EPS))
    for n in range(NB):
        on = jnp.concatenate(outs[n * RK_HEADS:(n + 1) * RK_HEADS], axis=1)
        o_ref[n] = (on * lng_ref[...] + lnb_ref[...] + bon_ref[n]) * g_ref[n]


def _rwkv_chunk(r, lw, k, v, kk, bb, g, bon, lng, lnb, tt=128, nb=4, precise=False):
    B, S, _ = r.shape
    nb = nb if B % nb == 0 else 1
    blk = pl.BlockSpec((nb, tt, RK_W), lambda b, j: (b, j, 0))
    vec = pl.BlockSpec((1, RK_W), lambda b, j: (0, 0))
    return pl.pallas_call(
        functools.partial(_rwkv_chunk_kernel, precise=precise),
        grid=(B // nb, S // tt),
        in_specs=[blk] * 8 + [vec, vec],
        out_specs=blk,
        out_shape=jax.ShapeDtypeStruct((B, S, RK_W), F32),
        scratch_shapes=[pltpu.VMEM((nb * RK_HEADS, RK_HEAD, RK_HEAD), F32)],
        compiler_params=_cparams(("parallel", "arbitrary")),
        name="rwkv_chunk",
    )(r, lw, k, v, kk, bb, g, bon, lng, lnb)


def _nsa_compress_kernel(x_ref, w1b_ref, w1f_ref, w2_ref, pe_ref, o_ref, ot_ref):
    half = CMP_STRIDE * NS_HEAD
    w1b = w1b_ref[...]
    bias = jnp.dot(jnp.broadcast_to(pe_ref[...], (8, CMP_LEN * NS_HEAD)), w1f_ref[...], precision=HI,
                   preferred_element_type=F32)[0:1]
    outs = []
    for g in range(NS_KV):
        x = x_ref[g]
        ya = jnp.dot(x, w1b[:half], preferred_element_type=F32)
        yb = jnp.dot(x, w1b[half:], preferred_element_type=F32)
        n = x.shape[0]
        h = ya + pltpu.roll(yb, n - 1, 0) + bias
        outs.append(_bdot(jax.nn.gelu(h), w2_ref[...]))
    out = jnp.concatenate(outs, axis=1)
    o_ref[...] = out.astype(BF16)
    ot_ref[...] = out.T.astype(BF16)


def _nsa_compress(x16, w1b, w1f, w2b, pe):
    B, _, _, ncp, width = x16.shape
    hid = w1b.shape[-1]
    return pl.pallas_call(
        _nsa_compress_kernel,
        grid=(B, 2),
        in_specs=[pl.BlockSpec((None, None, NS_KV, ncp, width), lambda b, c: (b, c, 0, 0, 0)),
                  pl.BlockSpec((None, 2 * width, hid), lambda b, c: (c, 0, 0)),
                  pl.BlockSpec((None, 2 * width, hid), lambda b, c: (c, 0, 0)),
                  pl.BlockSpec((None, hid, NS_HEAD), lambda b, c: (c, 0, 0)),
                  pl.BlockSpec((None, 1, 2 * width), lambda b, c: (c, 0, 0))],
        out_specs=[pl.BlockSpec((None, None, ncp, LANES), lambda b, c: (b, c, 0, 0)),
                   pl.BlockSpec((None, None, LANES, ncp), lambda b, c: (b, c, 0, 0))],
        out_shape=[jax.ShapeDtypeStruct((B, 2, ncp, LANES), BF16), jax.ShapeDtypeStruct((B, 2, LANES, ncp), BF16)],
        compiler_params=_cparams(("parallel", "parallel")),
        name="nsa_compress",
    )(x16, w1b, w1f, w2b, pe)


def _nsa_attn_kernel(q_ref, kc_ref, vct_ref, ks_ref, vst_ref, kw_ref, vwt_ref, gt_ref, ovt_ref, ext_ref, o_ref,
                     m_s, l_s, acc_s, m_w, l_w, acc_w, s_buf, *, ncp):
    i = pl.program_id(1)
    s0 = i * QT
    heads = range(NS_HEADS)
    hcols = lambda h: slice(h * QT, (h + 1) * QT)
    gcols = lambda h: slice(h // NS_HPG * QT, (h // NS_HPG + 1) * QT)
    iota = lambda shape, d: lax.broadcasted_iota(I32, shape, d)
    qpos = lambda shape: s0 + (iota(shape, 1) & (QT - 1))

    def col_reduce(x, op, final):
        n = x.shape[0]
        while n > 8:
            n //= 2
            x = op(x[:n], x[n:])
        return final(x, axis=0, keepdims=True)

    q8 = q_ref[...].astype(F32)
    zeros = jnp.zeros((QT, NS_HEAD), F32)
    qt = []
    for h in heads:
        qh = q8[:, h * NS_HEAD:(h + 1) * NS_HEAD]
        qt.append(jnp.concatenate([qh, zeros] if h < NS_HPG else [zeros, qh], axis=1).T)
    qt = jnp.concatenate(qt, axis=1).astype(BF16)

    sc = jnp.dot(kc_ref[...], qt, preferred_element_type=F32)
    n_row = iota((ncp, QT), 0)
    cmask = (n_row * CMP_STRIDE + (CMP_LEN - 1) <= s0 + iota((ncp, QT), 1)) & (n_row < ncp - 1)
    cpen = jnp.where(cmask, 0.0, NEG)
    cok = jnp.where(cmask, 1.0, 0.0)
    sc = jnp.concatenate([sc[:, hcols(h)] + cpen for h in heads], axis=1)
    pc = jnp.exp2(sc - col_reduce(sc, jnp.maximum, jnp.max))
    pc = jnp.concatenate([pc[:, hcols(h)] * cok for h in heads], axis=1)
    lc = col_reduce(pc, jnp.add, jnp.sum)
    pc = pc / jnp.where(lc > 0.0, lc, 1.0)
    pcs = jnp.concatenate([functools.reduce(jnp.add, [pc[:, hcols(h)] for h in range(g * NS_HPG, (g + 1) * NS_HPG)])
                           for g in range(NS_KV)], axis=1)
    o_c = jnp.dot(vct_ref[...], pc.astype(BF16), preferred_element_type=F32)

    imp = jnp.dot(ovt_ref[...], pcs, precision=HI, preferred_element_type=F32)
    blk = iota((LANES, NS_KV * QT), 0)
    cur = qpos((LANES, NS_KV * QT)) // SEL_BLOCK
    valid = blk <= cur
    forced = (blk == 0) | (blk == cur) | (blk == cur - 1)
    pri = jnp.where(valid & ~forced, imp, -jnp.inf)
    picked = forced
    blkf = blk.astype(F32)
    for _ in range(SEL_TOPK - 3):
        mx = col_reduce(pri, jnp.maximum, jnp.max)
        hit = blkf == col_reduce(jnp.where(pri == mx, blkf, float(LANES)), jnp.minimum, jnp.min)
        picked = picked | hit
        pri = jnp.where(hit, -jnp.inf, pri)
    selpen = jnp.where(picked & valid, 0.0, NEG).astype(BF16)
    wq = jnp.concatenate([qt, jnp.concatenate([selpen[:, gcols(h)] for h in heads], axis=1)], axis=0)

    def attend(s, vt, pen, m_ref, l_ref, acc_ref):
        if pen is not None:
            s = jnp.concatenate([s[:, hcols(h)] + pen for h in heads], axis=1)
        m_old = m_ref[...]
        m_new = jnp.maximum(m_old, col_reduce(s, jnp.maximum, jnp.max))
        alpha = jnp.exp2(m_old - m_new)
        p = jnp.exp2(s - m_new)
        m_ref[...] = m_new
        l_ref[...] = alpha * l_ref[...] + col_reduce(p, jnp.add, jnp.sum)
        acc_ref[...] = alpha * acc_ref[...] + jnp.dot(vt, p.astype(BF16), preferred_element_type=F32)

    def reset(m_ref, l_ref, acc_ref):
        m_ref[...] = jnp.full(m_ref.shape, NEG, F32)
        l_ref[...] = jnp.zeros_like(l_ref)
        acc_ref[...] = jnp.zeros_like(acc_ref)

    tiles = lambda ref, t0, n: jnp.concatenate([ref[t0 + c] for c in range(n)], axis=1)

    reset(m_s, l_s, acc_s)
    diag = i // (KT // QT)

    def sel_scores(kt, slot):
        k0 = pl.multiple_of(kt * KT, KT)
        keys = jnp.concatenate([ks_ref[pl.ds(k0, KT), :], ext_ref[pl.ds(k0, KT), :]], axis=1)
        s_buf[slot] = jnp.dot(keys, wq, preferred_element_type=F32)

    def sel_attend(kt, slot, causal):
        pen = None
        if causal:
            pen = jnp.where(kt * KT + iota((KT, QT), 0) <= s0 + iota((KT, QT), 1), 0.0, NEG)
        attend(s_buf[slot], tiles(vst_ref, kt * (KT // QT), KT // QT), pen, m_s, l_s, acc_s)

    def sel_pair(j, c):
        sel_scores(2 * j + 1, 1)
        sel_attend(2 * j, 0, False)
        sel_scores(2 * j + 2, 0)
        sel_attend(2 * j + 1, 1, False)
        return c

    d0 = pl.multiple_of(s0, QT)
    a0 = pl.multiple_of(jnp.maximum(s0 - WINDOW, 0), QT)
    s_d = jnp.dot(kw_ref[pl.ds(d0, QT), :], qt, preferred_element_type=F32)
    s_a = jnp.dot(kw_ref[pl.ds(a0, WINDOW), :], qt, preferred_element_type=F32)
    pen_d = jnp.where(iota((QT, QT), 0) <= iota((QT, QT), 1), 0.0, NEG)
    kpos = a0 + iota((WINDOW, QT), 0)
    pen_a = jnp.where((kpos < s0) & (kpos > s0 + iota((WINDOW, QT), 1) - WINDOW), 0.0, NEG)

    sel_scores(0, 0)
    reset(m_w, l_w, acc_w)
    attend(s_d, vwt_ref[i], pen_d, m_w, l_w, acc_w)
    attend(s_a, tiles(vwt_ref, a0 // QT, WINDOW // QT), pen_a, m_w, l_w, acc_w)
    lax.fori_loop(0, diag // 2, sel_pair, 0)

    @pl.when(diag % 2 == 1)
    def _():
        sel_scores(diag, 1)
        sel_attend(diag - 1, 0, False)
        sel_attend(diag, 1, True)

    @pl.when(diag % 2 == 0)
    def _():
        sel_attend(diag, 0, True)

    gtt = gt_ref[...].T
    o_s = acc_s[...] / l_s[...]
    o_w = acc_w[...] / l_w[...]
    outs = []
    for h in heads:
        gate = lambda br: gtt[br * NS_HEADS + h:br * NS_HEADS + h + 1]
        o = gate(0) * o_c[:, hcols(h)] + gate(1) * o_s[:, hcols(h)] + gate(2) * o_w[:, hcols(h)]
        outs.append(o[:NS_HEAD] if h < NS_HPG else o[NS_HEAD:])
    pairs = [jnp.concatenate(outs[p:p + 2], axis=0).T for p in range(0, NS_HEADS, 2)]
    o_ref[...] = jnp.concatenate(pairs, axis=1).astype(BF16)


def _nsa_attn(q3, kcv, kcvt, kv3, vt, gt3, ovt):
    B, S, _ = q3.shape
    ncp = kcv.shape[2]
    R = NS_HEADS * QT
    ext = (jnp.arange(S, dtype=I32)[:, None] // SEL_BLOCK == jnp.arange(LANES, dtype=I32)[None, :]).astype(BF16)
    seq = lambda c: pl.BlockSpec((None, S, LANES), lambda b, i, c=c: (b, 0, c))
    seqt = lambda c: pl.BlockSpec((None, None, S // QT, LANES, QT), lambda b, i, c=c: (c, b, 0, 0, 0))
    return pl.pallas_call(
        functools.partial(_nsa_attn_kernel, ncp=ncp),
        grid=(B, S // QT),
        in_specs=[pl.BlockSpec((None, QT, NS_W), lambda b, i: (b, i, 0)),
                  pl.BlockSpec((None, None, ncp, LANES), lambda b, i: (b, 0, 0, 0)),
                  pl.BlockSpec((None, None, LANES, ncp), lambda b, i: (b, 1, 0, 0)),
                  seq(2), seqt(0), seq(4), seqt(1),
                  pl.BlockSpec((None, QT, LANES), lambda b, i: (b, i, 0)),
                  pl.BlockSpec(ovt.shape, lambda b, i: (0, 0)),
                  pl.BlockSpec(ext.shape, lambda b, i: (0, 0))],
        out_specs=pl.BlockSpec((None, QT, NS_W), lambda b, i: (b, i, 0)),
        out_shape=jax.ShapeDtypeStruct((B, S, NS_W), BF16),
        scratch_shapes=([pltpu.VMEM((1, R), F32), pltpu.VMEM((1, R), F32), pltpu.VMEM((LANES, R), F32)] * 2
                        + [pltpu.VMEM((2, KT, R), F32)]),
        compiler_params=_cparams(("parallel", "arbitrary")),
        name="nsa_attn",
    )(q3, kcv, kcvt, kv3, vt, kv3, vt, gt3, ovt, ext)


def _first_argmax(vals):
    m = vals[0]
    for v in vals[1:]:
        m = jnp.maximum(m, v)
    idx = jnp.full(m.shape, len(vals) - 1, I32)
    for j in range(len(vals) - 2, -1, -1):
        idx = jnp.where(vals[j] == m, j, idx)
    return m, idx


def _outproj_router_kernel(*refs, n_in):
    acts, ws = refs[:n_in], refs[n_in:2 * n_in]
    x_ref, lng_ref, lnb_ref, rwt_ref, rb_ref, tri_ref = refs[2 * n_in:2 * n_in + 6]
    y_ref, e_ref, wt_ref, pos_ref, cnt_ref, cnt = refs[2 * n_in + 6:]
    i = pl.program_id(0)

    @pl.when(i == 0)
    def _():
        cnt[...] = jnp.zeros_like(cnt)

    mix = _bdot(acts[0][...], ws[0][...])
    for a, w in zip(acts[1:], ws[1:]):
        mix = mix + _bdot(a[...], w[...])
    y = _layer_norm_rows(ALPHA * x_ref[...] + mix, lng_ref[...], lnb_ref[...])
    y_ref[...] = y

    logit = lax.dot_general(rwt_ref[...], y, (((1,), (1,)), ((), ())), precision=HI, preferred_element_type=F32)
    aff = jax.nn.sigmoid(logit)
    biased = aff + rb_ref[...]
    neg_inf = -jnp.inf
    g_score, g_i1, g_i2 = [], [], []
    for gi in range(N_GROUPS):
        vals = [biased[gi * EXP_PER_GROUP + j:gi * EXP_PER_GROUP + j + 1, :] for j in range(EXP_PER_GROUP)]
        m1, i1 = _first_argmax(vals)
        m2, i2 = _first_argmax([jnp.where(i1 == j, neg_inf, vals[j]) for j in range(EXP_PER_GROUP)])
        g_score.append(m1 + m2)
        g_i1.append(i1)
        g_i2.append(i2)
    _, grp = _first_argmax(g_score)
    loc1, loc2 = g_i1[-1], g_i2[-1]
    for gi in range(N_GROUPS - 2, -1, -1):
        loc1 = jnp.where(grp == gi, g_i1[gi], loc1)
        loc2 = jnp.where(grp == gi, g_i2[gi], loc2)
    e1 = grp * EXP_PER_GROUP + loc1
    e2 = grp * EXP_PER_GROUP + loc2
    eio = lax.broadcasted_iota(I32, aff.shape, 0)
    oh1 = eio == e1
    oh2 = eio == e2
    a1 = jnp.sum(jnp.where(oh1, aff, 0.0), axis=0, keepdims=True)
    a2 = jnp.sum(jnp.where(oh2, aff, 0.0), axis=0, keepdims=True)
    tot = a1 + a2
    e_ref[...] = jnp.concatenate([e1, e2], axis=0)
    wt_ref[...] = jnp.concatenate([a1 / tot, a2 / tot], axis=0)

    ohs = oh1.astype(F32) + oh2.astype(F32)
    before = jnp.dot(ohs.astype(BF16), tri_ref[...], preferred_element_type=F32) + cnt[...]
    p1 = jnp.sum(jnp.where(oh1, before, 0.0), axis=0, keepdims=True)
    p2 = jnp.sum(jnp.where(oh2, before, 0.0), axis=0, keepdims=True)
    pos_ref[...] = jnp.concatenate([p1, p2], axis=0).astype(I32)
    cnt[...] = cnt[...] + jnp.sum(ohs, axis=1, keepdims=True)
    cnt_ref[...] = jnp.broadcast_to(cnt[...], cnt_ref.shape)


def _outproj_router(acts, ws, xres, lng, lnb, rwt, rb, tm=1024):
    T = xres.shape[0]
    n_in = len(acts)
    tri = (lax.broadcasted_iota(I32, (tm, tm), 0) < lax.broadcasted_iota(I32, (tm, tm), 1)).astype(BF16)
    row = lambda i: (i, 0)
    const = lambda a: pl.BlockSpec(a.shape, lambda i: (0,) * a.ndim)
    lane_blk = pl.BlockSpec((TOP_K, tm), lambda i: (0, i))
    return pl.pallas_call(
        functools.partial(_outproj_router_kernel, n_in=n_in),
        grid=(T // tm,),
        in_specs=([pl.BlockSpec((tm, a.shape[1]), row) for a in acts] + [const(w) for w in ws]
                  + [pl.BlockSpec((tm, D_MODEL), row), const(lng), const(lnb), const(rwt), const(rb), const(tri)]),
        out_specs=[pl.BlockSpec((tm, D_MODEL), row), lane_blk, lane_blk, lane_blk,
                   pl.BlockSpec((N_EXPERTS, LANES), lambda i: (0, 0))],
        out_shape=[jax.ShapeDtypeStruct((T, D_MODEL), F32), jax.ShapeDtypeStruct((TOP_K, T), I32),
                   jax.ShapeDtypeStruct((TOP_K, T), F32), jax.ShapeDtypeStruct((TOP_K, T), I32),
                   jax.ShapeDtypeStruct((N_EXPERTS, LANES), F32)],
        scratch_shapes=[pltpu.VMEM((N_EXPERTS, 1), F32)],
        compiler_params=_cparams(("arbitrary",)),
        name="outproj_router",
    )(*acts, *ws, xres, lng, lnb, rwt, rb, tri)


def _dispatch_kernel(dest_hbm, x_ref, zero_hbm, xs_hbm, dsm, sem_idx, sem):
    del zero_hbm
    i = pl.program_id(0)
    tm = x_ref.shape[0]
    idx_copy = pltpu.make_async_copy(dest_hbm.at[i], dsm, sem_idx)
    idx_copy.start()
    idx_copy.wait()

    def row_copy(r, k):
        return pltpu.make_async_copy(x_ref.at[pl.ds(r, 1)], xs_hbm.at[pl.ds(dsm[k * tm + r], 1)], sem)

    def start(r, c):
        for k in range(TOP_K):
            row_copy(r, k).start()
        return c

    lax.fori_loop(0, tm, start, 0, unroll=ROW_DMA_UNROLL)
    for k in range(TOP_K):
        pltpu.make_async_copy(x_ref, xs_hbm.at[pl.ds(0, tm)], sem).wait()


def _dispatch(dest_tiles, x, rows, tm):
    T = x.shape[0]
    zeros = jnp.zeros((rows, D_MODEL), F32)
    return pl.pallas_call(
        _dispatch_kernel,
        grid=(T // tm,),
        in_specs=[pl.BlockSpec(memory_space=pl.ANY), pl.BlockSpec((tm, D_MODEL), lambda i: (i, 0)),
                  pl.BlockSpec(memory_space=pl.ANY)],
        out_specs=pl.BlockSpec(memory_space=pl.ANY),
        out_shape=jax.ShapeDtypeStruct((rows, D_MODEL), F32),
        scratch_shapes=[pltpu.SMEM((TOP_K * tm,), I32), pltpu.SemaphoreType.DMA(()), pltpu.SemaphoreType.DMA(())],
        input_output_aliases={2: 0},
        compiler_params=_cparams(("arbitrary",)),
        name="moe_dispatch",
    )(dest_tiles, x, zeros)


def _ffn_kernel(be_ref, nu_ref, xs_ref, wg_ref, wu_ref, wd_ref, y_ref, h_ref):
    del be_ref
    i = pl.program_id(0)
    half = D_EXPERT // 2

    @pl.when(i < nu_ref[0])
    def _():
        xb = xs_ref[...].astype(BF16)
        for c in range(2):
            cs = slice(c * half, (c + 1) * half)
            gt = jnp.dot(xb, wg_ref[:, cs], preferred_element_type=F32)
            up = jnp.dot(xb, wu_ref[:, cs], preferred_element_type=F32)
            h_ref[:, cs] = (jax.nn.silu(gt) * up).astype(BF16)
        y_ref[...] = jnp.dot(h_ref[...], wd_ref[...], preferred_element_type=F32)

    @pl.when(i >= nu_ref[0])
    def _():
        y_ref[...] = jnp.zeros_like(y_ref)


def _ffn(blk_exp, n_used, xs, wg, wu, wd):
    rows = xs.shape[0]
    wspec = lambda: pl.BlockSpec((None, D_MODEL, D_EXPERT), lambda i, be, nu: (be[i], 0, 0))
    return pl.pallas_call(
        _ffn_kernel,
        grid_spec=pltpu.PrefetchScalarGridSpec(
            num_scalar_prefetch=2,
            grid=(rows // MOE_BLOCK,),
            in_specs=[pl.BlockSpec((MOE_BLOCK, D_MODEL), lambda i, be, nu: (i, 0)), wspec(), wspec(),
                      pl.BlockSpec((None, D_EXPERT, D_MODEL), lambda i, be, nu: (be[i], 0, 0))],
            out_specs=pl.BlockSpec((MOE_BLOCK, D_MODEL), lambda i, be, nu: (i, 0)),
            scratch_shapes=[pltpu.VMEM((MOE_BLOCK, D_EXPERT), BF16)]),
        out_shape=jax.ShapeDtypeStruct((rows, D_MODEL), F32),
        compiler_params=_cparams(("arbitrary",)),
        name="moe_ffn",
    )(blk_exp, n_used, xs, wg, wu, wd)


def _combine_kernel(dest_hbm, y_hbm, x_ref, wt_ref, lng_ref, lnb_ref, o_ref, dsm, buf, sem_idx, sem):
    i = pl.program_id(0)
    tm = x_ref.shape[0]
    idx_copy = pltpu.make_async_copy(dest_hbm.at[i], dsm, sem_idx)
    idx_copy.start()
    idx_copy.wait()

    def row_copy(r, k):
        return pltpu.make_async_copy(y_hbm.at[pl.ds(dsm[k * tm + r], 1)], buf.at[k, pl.ds(r, 1)], sem)

    def start(r, c):
        for k in range(TOP_K):
            row_copy(r, k).start()
        return c

    lax.fori_loop(0, tm, start, 0, unroll=ROW_DMA_UNROLL)
    for k in range(TOP_K):
        pltpu.make_async_copy(y_hbm.at[pl.ds(0, tm)], buf.at[k], sem).wait()
    wt = wt_ref[...]
    z = ALPHA * x_ref[...] + wt[:, 0:1] * buf[0] + wt[:, 1:2] * buf[1]
    o_ref[...] = _layer_norm_rows(z, lng_ref[...], lnb_ref[...])


def _combine(dest_tiles, y, x, wt, lng, lnb, tm):
    T = x.shape[0]
    row = lambda i: (i, 0)
    vec = pl.BlockSpec((1, D_MODEL), lambda i: (0, 0))
    return pl.pallas_call(
        _combine_kernel,
        grid=(T // tm,),
        in_specs=[pl.BlockSpec(memory_space=pl.ANY), pl.BlockSpec(memory_space=pl.ANY),
                  pl.BlockSpec((tm, D_MODEL), row), pl.BlockSpec((tm, TOP_K), row), vec, vec],
        out_specs=pl.BlockSpec((tm, D_MODEL), row),
        out_shape=jax.ShapeDtypeStruct((T, D_MODEL), F32),
        scratch_shapes=[pltpu.SMEM((TOP_K * tm,), I32), pltpu.VMEM((TOP_K, tm, D_MODEL), F32),
                        pltpu.SemaphoreType.DMA(()), pltpu.SemaphoreType.DMA(())],
        compiler_params=_cparams(("arbitrary",)),
        name="moe_combine",
    )(dest_tiles, y, x, wt, lng, lnb)


def _moe(x1, e, wt, pos, cnt, wg, wu, wd, lng, lnb, tm=1024):
    T = x1.shape[0]
    n_blocks = -(-(T * TOP_K) // MOE_BLOCK) + N_EXPERTS
    rows = n_blocks * MOE_BLOCK
    counts = cnt[:, 0].astype(I32)
    padded = (counts + MOE_BLOCK - 1) // MOE_BLOCK * MOE_BLOCK
    pad_end = jnp.cumsum(padded)
    pad_start = pad_end - padded
    dest = pos
    for j in range(N_EXPERTS):
        dest = dest + jnp.where(e == j, pad_start[j], 0)
    blk_start = jnp.arange(n_blocks, dtype=I32) * MOE_BLOCK
    blk_exp = jnp.minimum(jnp.sum((pad_end[None, :] <= blk_start[:, None]).astype(I32), axis=1), N_EXPERTS - 1)
    n_used = (pad_end[-1:] // MOE_BLOCK).astype(I32)
    dest_tiles = dest.reshape(TOP_K, T // tm, tm).transpose(1, 0, 2).reshape(T // tm, TOP_K * tm)
    xs = _dispatch(dest_tiles, x1, rows, tm)
    y = _ffn(blk_exp, n_used, xs, wg, wu, wd)
    return _combine(dest_tiles, y, x1, wt.T, lng, lnb, tm)


def _proj_rt_kernel(x_ref, w_ref, c_ref, s_ref, q_ref, k_ref, v_ref, g_ref):
    xb = x_ref[...].astype(BF16)
    cs, sn = c_ref[...], s_ref[...]
    rope = lambda y: y * cs + pltpu.roll(y, RT_QK // 2, 1) * sn
    for c in range(RT_HEADS):
        cols = slice(c * RT_QK, (c + 1) * RT_QK)
        q_ref[:, cols] = rope(jnp.dot(xb, w_ref[:, cols], preferred_element_type=F32)).astype(BF16)
    for c in range(RT_HEADS):
        cols = slice(c * RT_QK, (c + 1) * RT_QK)
        yk = jnp.dot(xb, w_ref[:, RT_QKW + c * RT_QK:RT_QKW + (c + 1) * RT_QK], preferred_element_type=F32)
        k_ref[:, cols] = (rope(yk) * RT_QK ** -0.5).astype(BF16)
    step = 1024
    for c in range(RT_VW // step):
        cols = slice(c * step, (c + 1) * step)
        v_ref[:, cols] = jnp.dot(xb, w_ref[:, 2 * RT_QKW + c * step:2 * RT_QKW + (c + 1) * step],
                                 preferred_element_type=F32).astype(BF16)
        base = 2 * RT_QKW + RT_VW
        g_ref[:, cols] = jax.nn.silu(jnp.dot(xb, w_ref[:, base + c * step:base + (c + 1) * step],
                                             preferred_element_type=F32))


def _proj_rt(xt, w, cs, sn, S, tm=512):
    T = xt.shape[0]
    nseq = S // tm
    row = lambda i: (i, 0)
    tab = lambda i: (i % nseq, 0)
    return pl.pallas_call(
        _proj_rt_kernel,
        grid=(T // tm,),
        in_specs=[pl.BlockSpec((tm, D_MODEL), row), pl.BlockSpec(w.shape, lambda i: (0, 0)),
                  pl.BlockSpec((tm, RT_QK), tab), pl.BlockSpec((tm, RT_QK), tab)],
        out_specs=[pl.BlockSpec((tm, RT_QKW), row), pl.BlockSpec((tm, RT_QKW), row),
                   pl.BlockSpec((tm, RT_VW), row), pl.BlockSpec((tm, RT_VW), row)],
        out_shape=[jax.ShapeDtypeStruct((T, RT_QKW), BF16), jax.ShapeDtypeStruct((T, RT_QKW), BF16),
                   jax.ShapeDtypeStruct((T, RT_VW), BF16), jax.ShapeDtypeStruct((T, RT_VW), F32)],
        compiler_params=_cparams(("parallel",)),
        name="proj_rt",
    )(xt, w, cs, sn)


def _retention_kernel(q_ref, k_ref, v_ref, sg_ref, dec_ref, qd_ref, kd_ref, cd_ref, gng_ref, gnb_ref, o_ref, state):
    j = pl.program_id(1)

    @pl.when(j == 0)
    def _():
        state[...] = jnp.zeros_like(state)

    NB = q_ref.shape[0]
    H = range(NB * RT_HEADS)
    nh = lambda i: (i // RT_HEADS, i % RT_HEADS)
    qk_cols = lambda i: slice(nh(i)[1] * RT_QK, (nh(i)[1] + 1) * RT_QK)
    v_cols = lambda i: slice(nh(i)[1] * RT_V, (nh(i)[1] + 1) * RT_V)
    q = [q_ref[nh(i)[0], :, qk_cols(i)] for i in H]
    k = [k_ref[nh(i)[0], :, qk_cols(i)] for i in H]
    v = [v_ref[nh(i)[0], :, v_cols(i)] for i in H]
    r_old = [state[i] for i in H]
    att = [(_dot_nt(q[i], k[i]) * dec_ref[nh(i)[1]]).astype(BF16) for i in H]
    cross = [jnp.dot(q[i], r_old[i].astype(BF16), preferred_element_type=F32) * qd_ref[nh(i)[1]] for i in H]
    inner = [jnp.dot(att[i], v[i], preferred_element_type=F32) for i in H]
    kdec = [(k[i].astype(F32) * kd_ref[nh(i)[1]]).astype(BF16) for i in H]
    for i in H:
        upd = lax.dot_general(kdec[i], v[i], (((0,), (0,)), ((), ())), preferred_element_type=F32)
        state[i] = r_old[i] * cd_ref[nh(i)[1]][:, 0:1] + upd
    for i in H:
        o = inner[i] + cross[i]
        mu = jnp.mean(o, axis=1, keepdims=True)
        oc = o - mu
        var = jnp.mean(oc * oc, axis=1, keepdims=True)
        on = oc * lax.rsqrt(var + RT_GN_EPS) * gng_ref[:, v_cols(i)] + gnb_ref[:, v_cols(i)]
        o_ref[nh(i)[0], :, v_cols(i)] = (sg_ref[nh(i)[0], :, v_cols(i)] * on).astype(BF16)


def _retention(q3, k3, v3, sg3, dec, qd, kd, cd, gng, gnb, nb=2):
    B, S, _ = q3.shape
    C = RT_CHUNK
    nb = nb if B % nb == 0 else 1
    qk = pl.BlockSpec((nb, C, RT_QKW), lambda b, j: (b, j, 0))
    vv = pl.BlockSpec((nb, C, RT_VW), lambda b, j: (b, j, 0))
    const = lambda a: pl.BlockSpec(a.shape, lambda b, j: (0,) * a.ndim)
    return pl.pallas_call(
        _retention_kernel,
        grid=(B // nb, S // C),
        in_specs=[qk, qk, vv, vv, const(dec), const(qd), const(kd), const(cd), const(gng), const(gnb)],
        out_specs=vv,
        out_shape=jax.ShapeDtypeStruct((B, S, RT_VW), BF16),
        scratch_shapes=[pltpu.VMEM((nb * RT_HEADS, RT_QK, RT_V), F32)],
        compiler_params=_cparams(("parallel", "arbitrary")),
        name="retention",
    )(q3, k3, v3, sg3, dec, qd, kd, cd, gng, gnb)


def _nsa_rope_tables(S):
    half = ROPE_DIM // 2
    inv = ROPE_THETA ** (-jnp.arange(half, dtype=F32) / half)
    ang = jnp.arange(S, dtype=F32)[:, None] * inv[None, :]
    cos, sin = jnp.cos(ang), jnp.sin(ang)
    zeros = lambda n: jnp.zeros((S, n), F32)
    cn = jnp.concatenate([cos, cos, jnp.ones((S, NS_HEAD - ROPE_DIM), F32)], axis=1)
    s1 = jnp.concatenate([-sin, zeros(NS_HEAD - half)], axis=1)
    s2 = jnp.concatenate([zeros(half), sin, zeros(NS_HEAD - ROPE_DIM)], axis=1)
    two = lambda a: jnp.concatenate([a, a], axis=1)
    return two(cn), two(s1), two(s2)


def _rt_rope_tables(S):
    inv = RT_THETA ** (-jnp.linspace(0.0, 1.0, RT_QK // 2, dtype=F32))
    ang = jnp.arange(S, dtype=F32)[:, None] * inv[None, :]
    cos, sin = jnp.cos(ang), jnp.sin(ang)
    return jnp.concatenate([cos, cos], axis=1), jnp.concatenate([-sin, sin], axis=1)


def _rt_decay_tables():
    log_g = jnp.log(1.0 - 2.0 ** (-5.0 - jnp.arange(RT_HEADS, dtype=F32)))
    idx = jnp.arange(RT_CHUNK, dtype=F32)
    diff = idx[:, None] - idx[None, :]
    dec = jnp.where(diff >= 0, jnp.exp(jnp.maximum(diff, 0.0) * log_g[:, None, None]), 0.0)
    qd = jnp.exp((idx + 1.0) * log_g[:, None])[..., None]
    kd = jnp.exp((RT_CHUNK - 1.0 - idx) * log_g[:, None])[..., None]
    cd = jnp.broadcast_to(jnp.exp(RT_CHUNK * log_g)[:, None, None], (RT_HEADS, 1, LANES))
    return dec, qd, kd, cd


def _overlap_table(S, ncp):
    n_cmp = (S - CMP_LEN) // CMP_STRIDE + 1
    n_sel = S // SEL_BLOCK
    cs = jnp.arange(ncp) * CMP_STRIDE
    ss = jnp.arange(LANES) * SEL_BLOCK
    ov = jnp.clip(jnp.minimum(cs[:, None] + CMP_LEN, ss[None, :] + SEL_BLOCK)
                  - jnp.maximum(cs[:, None], ss[None, :]), 0, None).astype(F32) / CMP_LEN
    keep = (jnp.arange(ncp)[:, None] < n_cmp) & (jnp.arange(LANES)[None, :] < n_sel)
    return jnp.where(keep, ov, 0.0)


def kernel(x, ab_w_in, ab_w_out, rk_mu, rk_w0, rk_w1, rk_w2, rk_a0, rk_a1, rk_a2, rk_g1, rk_g2, rk_kk, rk_ka, rk_rk,
           rk_ln, ns_pe, ns_c_w1, ns_c_w2, rt_w_in, rt_w_out, rt_gn, router_w, router_b, moe_w_gate, moe_w_up,
           moe_w_down, ln):
    B, S, D = x.shape
    T = B * S
    assert D == D_MODEL and S % 256 == 0 and S // SEL_BLOCK <= LANES and S >= WINDOW
    xt = x.reshape(T, D)
    rwt = router_w.T
    rb = router_b.reshape(N_EXPERTS, 1)
    vec = lambda a: a.reshape(1, -1)

    w_in = ab_w_in[0]
    n_gate = 3 * NS_HEADS
    w_cat = jnp.concatenate([w_in[:, :-n_gate], jnp.pad(w_in[:, -n_gate:], ((0, 0), (0, LANES - n_gate)))],
                            axis=1).astype(BF16)
    cn, s1, s2 = _nsa_rope_tables(S)
    prk, q, kv, gt, vt = _proj_ab(xt, w_cat, cn, s1, s2, S)

    ones = (jnp.arange(RK_W)[:, None] // RK_HEAD == jnp.arange(RK_W)[None, :] // RK_HEAD).astype(F32)
    b16 = lambda a: a.astype(BF16)
    r, lw, km, v, kk, bb, g, bon = _rwkv_prep(
        prk.reshape(B, S, 4 * RK_W), rk_mu[0], vec(rk_w0[0]), b16(rk_w1[0]), b16(rk_w2[0]), vec(rk_a0[0]),
        b16(rk_a1[0]), b16(rk_a2[0]), b16(rk_g1[0]), b16(rk_g2[0]), vec(rk_kk[0]), vec(rk_ka[0]), vec(rk_rk[0]), ones)
    o_a = _rwkv_chunk(r, lw, km, v, kk, bb, g, bon, rk_ln[0, 0:1], rk_ln[0, 1:2])

    ncp = S // CMP_STRIDE
    kv3 = kv.reshape(B, S, 6 * LANES)
    x16 = (kv3[:, :, :2 * LANES].reshape(B, ncp, CMP_STRIDE, 2, NS_KV, NS_HEAD)
           .transpose(0, 3, 4, 1, 2, 5).reshape(B, 2, NS_KV, ncp, CMP_STRIDE * NS_HEAD))
    kcv, kcvt = _nsa_compress(x16, b16(ns_c_w1[0]), ns_c_w1[0], b16(ns_c_w2[0]),
                              ns_pe[0].reshape(2, 1, CMP_LEN * NS_HEAD))
    o_b = _nsa_attn(q.reshape(B, S, NS_W), kcv, kcvt, kv3, vt.reshape(2, B, S // QT, LANES, QT),
                    gt.reshape(B, S, LANES), _overlap_table(S, ncp).T)

    w_out = b16(ab_w_out[0])
    x1, e, wt, pos, cnt = _outproj_router([o_a.reshape(T, RK_W), o_b.reshape(T, NS_W)], [w_out[:RK_W], w_out[RK_W:]],
                                          xt, ln[0, 0, 0:1], ln[0, 0, 1:2], rwt, rb)
    x2 = _moe(x1, e, wt, pos, cnt, b16(moe_w_gate[0]), b16(moe_w_up[0]), b16(moe_w_down[0]),
              ln[0, 1, 0:1], ln[0, 1, 1:2])

    cs, sn = _rt_rope_tables(S)
    qr, kr, vr, sg = _proj_rt(x2, b16(rt_w_in[0]), cs, sn, S)
    dec, qd, kd, cd = _rt_decay_tables()
    ret = _retention(qr.reshape(B, S, RT_QKW), kr.reshape(B, S, RT_QKW), vr.reshape(B, S, RT_VW),
                     sg.reshape(B, S, RT_VW), dec, qd, kd, cd, rt_gn[0, 0:1], rt_gn[0, 1:2])
    x3, e, wt, pos, cnt = _outproj_router([ret.reshape(T, RT_VW)], [b16(rt_w_out[0])], x2,
                                          ln[1, 0, 0:1], ln[1, 0, 1:2], rwt, rb)
    x4 = _moe(x3, e, wt, pos, cnt, b16(moe_w_gate[1]), b16(moe_w_up[1]), b16(moe_w_down[1]),
              ln[1, 1, 0:1], ln[1, 1, 1:2])
    return x4.reshape(B, S, D)
```

```python
import functools
import math

import jax
import jax.numpy as jnp
from jax import lax
from jax.experimental import pallas as pl
from jax.experimental.pallas import tpu as pltpu

F32 = jnp.float32
BF16 = jnp.bfloat16
I32 = jnp.int32
HI = lax.Precision.HIGHEST

LANES = 128
VMEM_LIMIT = 56 * 1024 * 1024

D_MODEL = 1024
RK_HEADS, RK_HEAD = 8, 64
RK_W = RK_HEADS * RK_HEAD
RK_DECAY_SCALE = 0.606531
RK_LN_EPS = 64e-5
NS_HEADS, NS_KV, NS_HPG, NS_HEAD = 8, 2, 4, 64
NS_W = NS_HEADS * NS_HEAD
CMP_LEN, CMP_STRIDE, SEL_BLOCK, SEL_TOPK, WINDOW = 32, 16, 64, 16, 512
ROPE_THETA = 500000.0
ROPE_DIM = NS_HEAD // 4
Q_SCALE = NS_HEAD ** -0.5 * math.log2(math.e)
QT = 128
KT = 512
RT_HEADS, RT_QK, RT_V = 8, 128, 256
RT_QKW, RT_VW = RT_HEADS * RT_QK, RT_HEADS * RT_V
RT_CHUNK = 128
RT_THETA = 10000.0
RT_GN_EPS = 1e-5
N_EXPERTS, N_GROUPS, EXP_PER_GROUP, TOP_K = 16, 4, 4, 2
D_EXPERT = 1024
MOE_BLOCK = 512
ROW_DMA_UNROLL = 8
DEPTH = 2
ALPHA = (2.0 * DEPTH) ** 0.25
LN_EPS = 1e-5
NEG = -1e30


def _cparams(sem):
    return pltpu.CompilerParams(dimension_semantics=sem, vmem_limit_bytes=VMEM_LIMIT)


def _bdot(a, w):
    return jnp.dot(a.astype(BF16), w, preferred_element_type=F32)


def _dot_nt(a, b):
    return lax.dot_general(a, b, (((1,), (1,)), ((), ())), preferred_element_type=F32)


def _layer_norm_rows(z, g, b):
    mu = jnp.mean(z, axis=1, keepdims=True)
    zc = z - mu
    var = jnp.mean(zc * zc, axis=1, keepdims=True)
    return zc * lax.rsqrt(var + LN_EPS) * g + b


def _proj_ab_kernel(x_ref, w_ref, cn_ref, s1_ref, s2_ref, pick_ref, prk_ref, q_ref, kk_ref, gt_ref, vt_ref, cx_ref):
    xb = x_ref[...].astype(BF16)
    for c in range(4):
        prk_ref[:, c * RK_W:(c + 1) * RK_W] = jnp.dot(xb, w_ref[:, c * RK_W:(c + 1) * RK_W],
                                                      preferred_element_type=F32)
    cn, s1, s2 = cn_ref[...], s1_ref[...], s2_ref[...]

    def rope(y):
        return y * cn + pltpu.roll(y, LANES - ROPE_DIM // 2, 1) * s1 + pltpu.roll(y, ROPE_DIM // 2, 1) * s2

    base = 4 * RK_W
    yq = jnp.dot(xb, w_ref[:, base:base + NS_W], preferred_element_type=F32)
    for c in range(NS_W // LANES):
        q_ref[:, c * LANES:(c + 1) * LANES] = (rope(yq[:, c * LANES:(c + 1) * LANES]) * Q_SCALE).astype(BF16)
    base += NS_W
    ykv = jnp.dot(xb, w_ref[:, base:base + 6 * LANES], preferred_element_type=F32)
    for c in range(6):
        y = ykv[:, c * LANES:(c + 1) * LANES]
        if c % 2 == 0:
            y = rope(y)
        if c < 2:
            yb = y.astype(BF16)
            for l in range(CMP_STRIDE):
                cx_ref[c, :, l * LANES:(l + 1) * LANES] = jnp.dot(pick_ref[l], yb,
                                                                  preferred_element_type=F32).astype(BF16)
        elif c % 2 == 0:
            kk_ref[:, (c // 2 - 1) * LANES:(c // 2) * LANES] = y.astype(BF16)
        else:
            for t in range(y.shape[0] // QT):
                vt_ref[c // 2 - 1, t] = y[t * QT:(t + 1) * QT].T.astype(BF16)
    base += 6 * LANES
    gt_ref[...] = jax.nn.sigmoid(jnp.dot(xb, w_ref[:, base:base + LANES], preferred_element_type=F32))


def _proj_ab(xt, w, cn, s1, s2, S, tm=512):
    T = xt.shape[0]
    ncols = w.shape[1]
    nseq = S // tm
    row = lambda i: (i, 0)
    tab = lambda i: (i % nseq, 0)
    ng = tm // CMP_STRIDE
    pick = (jnp.arange(tm, dtype=I32)[None, None, :]
            == jnp.arange(ng, dtype=I32)[None, :, None] * CMP_STRIDE + jnp.arange(CMP_STRIDE, dtype=I32)[:, None, None])
    pick = pick.astype(BF16)
    return pl.pallas_call(
        _proj_ab_kernel,
        grid=(T // tm,),
        in_specs=[pl.BlockSpec((tm, D_MODEL), row),
                  pl.BlockSpec((D_MODEL, ncols), lambda i: (0, 0)),
                  pl.BlockSpec((tm, LANES), tab), pl.BlockSpec((tm, LANES), tab), pl.BlockSpec((tm, LANES), tab),
                  pl.BlockSpec(pick.shape, lambda i: (0, 0, 0))],
        out_specs=[pl.BlockSpec((tm, 4 * RK_W), row), pl.BlockSpec((tm, NS_W), row),
                   pl.BlockSpec((tm, 2 * LANES), row), pl.BlockSpec((tm, LANES), row),
                   pl.BlockSpec((2, tm // QT, LANES, QT), lambda i: (0, i, 0, 0)),
                   pl.BlockSpec((2, ng, CMP_STRIDE * LANES), lambda i: (0, i, 0))],
        out_shape=[jax.ShapeDtypeStruct((T, 4 * RK_W), F32), jax.ShapeDtypeStruct((T, NS_W), BF16),
                   jax.ShapeDtypeStruct((T, 2 * LANES), BF16), jax.ShapeDtypeStruct((T, LANES), F32),
                   jax.ShapeDtypeStruct((2, T // QT, LANES, QT), BF16),
                   jax.ShapeDtypeStruct((2, T // CMP_STRIDE, CMP_STRIDE * LANES), BF16)],
        compiler_params=_cparams(("parallel",)),
        name="proj_ab",
    )(xt, w, cn, s1, s2, pick)


def _rwkv_prep_kernel(p_ref, mu_ref, w0_ref, w1_ref, w2_ref, a0_ref, a1_ref, a2_ref, g1_ref, g2_ref,
                      kk_ref, ka_ref, rk_ref, ones_ref,
                      r_o, w_o, k_o, v_o, kk_o, b_o, g_o, bon_o, carry):
    j = pl.program_id(1)
    p = p_ref[...]
    tm = p.shape[0]

    @pl.when(j == 0)
    def _():
        carry[...] = jnp.zeros_like(carry)

    rowi = lax.broadcasted_iota(I32, p.shape, 0)
    prev = jnp.where(rowi == 0, carry[...], pltpu.roll(p, 1, 0))
    carry[...] = p[tm - 1:tm, :]
    dp = prev - p
    sl = lambda a, c: a[:, c * RK_W:(c + 1) * RK_W]
    mu = mu_ref[...]
    r = sl(p, 0) + sl(dp, 0) * mu[0:1]
    k = sl(p, 1) + sl(dp, 1) * mu[1:2]
    v = sl(p, 2) + sl(dp, 2) * mu[2:3]
    xw = sl(p, 3) + sl(dp, 3) * mu[3:4]
    xa = sl(p, 3) + sl(dp, 3) * mu[4:5]
    xg = sl(p, 3) + sl(dp, 3) * mu[5:6]
    lw = -RK_DECAY_SCALE * jax.nn.sigmoid(w0_ref[...] + _bdot(jnp.tanh(_bdot(xw, w1_ref[...])), w2_ref[...]))
    a = jax.nn.sigmoid(a0_ref[...] + _bdot(_bdot(xa, a1_ref[...]), a2_ref[...]))
    g = _bdot(jax.nn.sigmoid(_bdot(xg, g1_ref[...])), g2_ref[...])
    ones = ones_ref[...]
    kk = k * kk_ref[...]
    ss = jnp.dot(kk * kk, ones, precision=HI, preferred_element_type=F32)
    kk = kk / jnp.maximum(jnp.sqrt(ss), 1e-12)
    km = k * (1.0 + (a - 1.0) * ka_ref[...])
    bon = jnp.dot(r * km * rk_ref[...], ones, precision=HI, preferred_element_type=F32) * v
    r_o[...] = r
    w_o[...] = lw
    k_o[...] = km
    v_o[...] = v
    kk_o[...] = kk
    b_o[...] = kk * a
    g_o[...] = g
    bon_o[...] = bon


def _rwkv_prep(prk3, mu, w0, w1, w2, a0, a1, a2, g1, g2, k_k, k_a, r_k, ones, tm=512):
    B, S, _ = prk3.shape
    full = lambda a: pl.BlockSpec(a.shape, lambda b, j: (0,) * a.ndim)
    params = [mu, w0, w1, w2, a0, a1, a2, g1, g2, k_k, k_a, r_k, ones]
    ospec = pl.BlockSpec((None, tm, RK_W), lambda b, j: (b, j, 0))
    return pl.pallas_call(
        _rwkv_prep_kernel,
        grid=(B, S // tm),
        in_specs=[pl.BlockSpec((None, tm, 4 * RK_W), lambda b, j: (b, j, 0))] + [full(a) for a in params],
        out_specs=[ospec] * 8,
        out_shape=[jax.ShapeDtypeStruct((B, S, RK_W), F32)] * 8,
        scratch_shapes=[pltpu.VMEM((1, 4 * RK_W), F32)],
        compiler_params=_cparams(("parallel", "arbitrary")),
        name="rwkv_prep",
    )(prk3, *params)


RK_CHUNK = 16


def _pdot(a, b, dims, precise):
    if precise:
        return lax.dot_general(a, b, (dims, ((), ())), precision=HI, preferred_element_type=F32)
    return lax.dot_general(a.astype(BF16), b.astype(BF16), (dims, ((), ())), preferred_element_type=F32)


def _rwkv_chunk_kernel(r_ref, lw_ref, k_ref, v_ref, kk_ref, b_ref, g_ref, bon_ref, lng_ref, lnb_ref, o_ref, ht,
                       *, precise):
    j = pl.program_id(1)

    @pl.when(j == 0)
    def _():
        ht[...] = jnp.zeros_like(ht)

    NB, TT = r_ref.shape[0], r_ref.shape[1]
    C, N = RK_CHUNK, RK_HEAD
    mm = lambda a, b: _pdot(a, b, ((1,), (0,)), precise)
    mm_nt = lambda a, b: _pdot(a, b, ((1,), (1,)), precise)
    mm_tn = lambda a, b: _pdot(a, b, ((0,), (0,)), precise)
    wide = lambda ref: jnp.concatenate([ref[n] for n in range(NB)], axis=1)

    lw = wide(lw_ref)
    rowc = lax.broadcasted_iota(I32, lw.shape, 0) & (C - 1)
    linc, lrev = lw, lw
    sh = 1
    while sh < C:
        linc = linc + jnp.where(rowc >= sh, pltpu.roll(linc, sh, 0), 0.0)
        lrev = lrev + jnp.where(rowc < C - sh, pltpu.roll(lrev, TT - sh, 0), 0.0)
        sh *= 2
    lrev = lrev - lw
    r, k, v, kk, b = wide(r_ref), wide(k_ref), wide(v_ref), wide(kk_ref), wide(b_ref)
    e_in, e_inv, e_rev = jnp.exp(linc), jnp.exp(-linc), jnp.exp(lrev)
    kkd = kk * jnp.exp(linc - lw)
    rd = r * e_in
    binv, kinv = b * e_inv, k * e_inv
    bd, kd = b * e_rev, k * e_rev
    gam = jnp.exp(linc + lrev)

    ti = lax.broadcasted_iota(I32, (TT, TT), 0)
    tj = lax.broadcasted_iota(I32, (TT, TT), 1)
    same = (ti // C) == (tj // C)
    strict = same & (tj < ti)
    incl = same & (tj <= ti)

    H = range(NB * RK_HEADS)
    lo = lambda a: a if precise else a.astype(BF16)
    hs = lambda a: [a[:, h * N:(h + 1) * N] for h in H]
    rows2 = lambda a, b: jnp.concatenate([a, b], axis=0)
    kkd_h, rd_h, v_h = hs(lo(kkd)), hs(lo(rd)), hs(lo(v))
    binv_h, kinv_h, bd_h, kd_h = hs(lo(binv)), hs(lo(kinv)), hs(lo(bd)), hs(lo(kd))
    gam_h = hs(gam)
    gm = [mm_nt(rows2(kkd_h[h], rd_h[h]), rows2(binv_h[h], kinv_h[h])) for h in H]
    a_b = [lo(jnp.where(strict, gm[h][:TT, :TT], 0.0)) for h in H]
    b_rb = [lo(jnp.where(incl, gm[h][TT:, :TT], 0.0)) for h in H]
    akb = [lo(rows2(jnp.where(strict, gm[h][:TT, TT:], 0.0), jnp.where(incl, gm[h][TT:, TT:], 0.0))) for h in H]
    av = [mm(akb[h], v_h[h]) for h in H]
    x = [jnp.concatenate([kkd_h[h].astype(F32), av[h][:TT]], axis=1) for h in H]
    a2 = [lo(mm(a_b[h], a_b[h])) for h in H]
    a4 = [lo(mm(a2[h], a2[h])) for h in H]
    a8 = [lo(mm(a4[h], a4[h])) for h in H]
    x = [x[h] + mm(a8[h], lo(x[h])) for h in H]
    x = [x[h] + mm(a4[h], lo(x[h])) for h in H]
    x = [x[h] + mm(a2[h], lo(x[h])) for h in H]
    x = [x[h] - mm(a_b[h], lo(x[h])) for h in H]
    wt = [lo(x[h][:, :N]) for h in H]
    h_t = [ht[h] for h in H]
    us = [[] for _ in H]
    rhs = [[] for _ in H]
    for c in range(TT // C):
        rs = slice(c * C, (c + 1) * C)
        xh = [mm_nt(rows2(wt[h][rs], rd_h[h][rs]), lo(h_t[h])) for h in H]
        for h in H:
            u_c = -(xh[h][:C] + x[h][rs, N:])
            us[h].append(u_c)
            rhs[h].append(xh[h][C:])
        upd = [mm_tn(rows2(lo(us[h][c]), v_h[h][rs]), rows2(bd_h[h][rs], kd_h[h][rs])) for h in H]
        h_t = [h_t[h] * gam_h[h][c * C:c * C + 1] + upd[h] for h in H]
    outs = []
    for h in H:
        ht[h] = h_t[h]
        o = jnp.concatenate(rhs[h], axis=0) + mm(b_rb[h], lo(jnp.concatenate(us[h], axis=0))) + av[h][TT:]
        mu = jnp.mean(o, axis=1, keepdims=True)
        oc = o - mu
        var = jnp.mean(oc * oc, axis=1, keepdims=True)
        outs.append(oc * lax.rsqrt(var + RK_LN_EPS))
    for n in range(NB):
        on = jnp.concatenate(outs[n * RK_HEADS:(n + 1) * RK_HEADS], axis=1)
        o_ref[n] = (on * lng_ref[...] + lnb_ref[...] + bon_ref[n]) * g_ref[n]


def _rwkv_chunk(r, lw, k, v, kk, bb, g, bon, lng, lnb, tt=128, nb=4, precise=False):
    B, S, _ = r.shape
    nb = nb if B % nb == 0 else 1
    blk = pl.BlockSpec((nb, tt, RK_W), lambda b, j: (b, j, 0))
    vec = pl.BlockSpec((1, RK_W), lambda b, j: (0, 0))
    return pl.pallas_call(
        functools.partial(_rwkv_chunk_kernel, precise=precise),
        grid=(B // nb, S // tt),
        in_specs=[blk] * 8 + [vec, vec],
        out_specs=blk,
        out_shape=jax.ShapeDtypeStruct((B, S, RK_W), F32),
        scratch_shapes=[pltpu.VMEM((nb * RK_HEADS, RK_HEAD, RK_HEAD), F32)],
        compiler_params=_cparams(("parallel", "arbitrary")),
        name="rwkv_chunk",
    )(r, lw, k, v, kk, bb, g, bon, lng, lnb)


def _nsa_compress_kernel(x_ref, w1g_ref, w1f_ref, w2_ref, pe_ref, o_ref, ot_ref):
    bias = jnp.dot(jnp.broadcast_to(pe_ref[...], (8, CMP_LEN * NS_HEAD)), w1f_ref[...], precision=HI,
                   preferred_element_type=F32)[0:1]
    x = x_ref[...]
    n = x.shape[0]
    outs = []
    for g in range(NS_KV):
        ya = jnp.dot(x, w1g_ref[g, 0], preferred_element_type=F32)
        yb = jnp.dot(x, w1g_ref[g, 1], preferred_element_type=F32)
        h = ya + pltpu.roll(yb, n - 1, 0) + bias
        outs.append(_bdot(jax.nn.gelu(h), w2_ref[...]))
    out = jnp.concatenate(outs, axis=1)
    o_ref[...] = out.astype(BF16)
    ot_ref[...] = out.T.astype(BF16)


def _nsa_compress(cx, w1, w2b, pe):
    _, B, ncp, width = cx.shape
    hid = w1.shape[-1]
    w1h = w1.reshape(2, 2, CMP_STRIDE, 1, NS_HEAD, hid)
    zero = jnp.zeros_like(w1h)
    w1g = jnp.stack([jnp.concatenate([w1h, zero], axis=3), jnp.concatenate([zero, w1h], axis=3)], axis=1)
    w1g = w1g.reshape(2, NS_KV, 2, width, hid).astype(BF16)
    return pl.pallas_call(
        _nsa_compress_kernel,
        grid=(B, 2),
        in_specs=[pl.BlockSpec((None, None, ncp, width), lambda b, c: (c, b, 0, 0)),
                  pl.BlockSpec((None, NS_KV, 2, width, hid), lambda b, c: (c, 0, 0, 0, 0)),
                  pl.BlockSpec((None, CMP_LEN * NS_HEAD, hid), lambda b, c: (c, 0, 0)),
                  pl.BlockSpec((None, hid, NS_HEAD), lambda b, c: (c, 0, 0)),
                  pl.BlockSpec((None, 1, CMP_LEN * NS_HEAD), lambda b, c: (c, 0, 0))],
        out_specs=[pl.BlockSpec((None, None, ncp, LANES), lambda b, c: (b, c, 0, 0)),
                   pl.BlockSpec((None, None, LANES, ncp), lambda b, c: (b, c, 0, 0))],
        out_shape=[jax.ShapeDtypeStruct((B, 2, ncp, LANES), BF16), jax.ShapeDtypeStruct((B, 2, LANES, ncp), BF16)],
        compiler_params=_cparams(("parallel", "parallel")),
        name="nsa_compress",
    )(cx, w1g, w1, w2b, pe)


def _nsa_attn_kernel(q_ref, kc_ref, vct_ref, ks_ref, vst_ref, kw_ref, vwt_ref, gt_ref, ovt_ref, ext_ref, o_ref,
                     m_s, l_s, acc_s, m_w, l_w, acc_w, s_buf, *, ncp):
    i = pl.program_id(1)
    s0 = i * QT
    heads = range(NS_HEADS)
    hcols = lambda h: slice(h * QT, (h + 1) * QT)
    gcols = lambda h: slice(h // NS_HPG * QT, (h // NS_HPG + 1) * QT)
    iota = lambda shape, d: lax.broadcasted_iota(I32, shape, d)
    qpos = lambda shape: s0 + (iota(shape, 1) & (QT - 1))

    def col_reduce(x, op, final):
        n = x.shape[0]
        while n > 8:
            n //= 2
            x = op(x[:n], x[n:])
        return final(x, axis=0, keepdims=True)

    q8 = q_ref[...].astype(F32)
    zeros = jnp.zeros((QT, NS_HEAD), F32)
    qt = []
    for h in heads:
        qh = q8[:, h * NS_HEAD:(h + 1) * NS_HEAD]
        qt.append(jnp.concatenate([qh, zeros] if h < NS_HPG else [zeros, qh], axis=1).T)
    qt = jnp.concatenate(qt, axis=1).astype(BF16)

    sc = jnp.dot(kc_ref[...], qt, preferred_element_type=F32)
    n_row = iota((ncp, QT), 0)
    cmask = (n_row * CMP_STRIDE + (CMP_LEN - 1) <= s0 + iota((ncp, QT), 1)) & (n_row < ncp - 1)
    cpen = jnp.where(cmask, 0.0, NEG)
    cok = jnp.where(cmask, 1.0, 0.0)
    sc = jnp.concatenate([sc[:, hcols(h)] + cpen for h in heads], axis=1)
    pc = jnp.exp2(sc - col_reduce(sc, jnp.maximum, jnp.max))
    pc = jnp.concatenate([pc[:, hcols(h)] * cok for h in heads], axis=1)
    lc = col_reduce(pc, jnp.add, jnp.sum)
    pc = pc / jnp.where(lc > 0.0, lc, 1.0)
    pcs = jnp.concatenate([functools.reduce(jnp.add, [pc[:, hcols(h)] for h in range(g * NS_HPG, (g + 1) * NS_HPG)])
                           for g in range(NS_KV)], axis=1)
    o_c = jnp.dot(vct_ref[...], pc.astype(BF16), preferred_element_type=F32)

    imp = jnp.dot(ovt_ref[...], pcs, precision=HI, preferred_element_type=F32)
    blk = iota((LANES, NS_KV * QT), 0)
    cur = qpos((LANES, NS_KV * QT)) // SEL_BLOCK
    valid = blk <= cur
    forced = (blk == 0) | (blk == cur) | (blk == cur - 1)
    pri = jnp.where(valid & ~forced, imp, -jnp.inf)
    picked = forced
    blkf = blk.astype(F32)
    for _ in range(SEL_TOPK - 3):
        mx = col_reduce(pri, jnp.maximum, jnp.max)
        hit = blkf == col_reduce(jnp.where(pri == mx, blkf, float(LANES)), jnp.minimum, jnp.min)
        picked = picked | hit
        pri = jnp.where(hit, -jnp.inf, pri)
    selpen = jnp.where(picked & valid, 0.0, NEG).astype(BF16)
    wq = jnp.concatenate([qt, jnp.concatenate([selpen[:, gcols(h)] for h in heads], axis=1)], axis=0)

    def attend(s, vt, pen, m_ref, l_ref, acc_ref):
        if pen is not None:
            s = jnp.concatenate([s[:, hcols(h)] + pen for h in heads], axis=1)
        m_old = m_ref[...]
        m_new = jnp.maximum(m_old, col_reduce(s, jnp.maximum, jnp.max))
        alpha = jnp.exp2(m_old - m_new)
        p = jnp.exp2(s - m_new)
        m_ref[...] = m_new
        l_ref[...] = alpha * l_ref[...] + col_reduce(p, jnp.add, jnp.sum)
        acc_ref[...] = alpha * acc_ref[...] + jnp.dot(vt, p.astype(BF16), preferred_element_type=F32)

    def reset(m_ref, l_ref, acc_ref):
        m_ref[...] = jnp.full(m_ref.shape, NEG, F32)
        l_ref[...] = jnp.zeros_like(l_ref)
        acc_ref[...] = jnp.zeros_like(acc_ref)

    tiles = lambda ref, t0, n: jnp.concatenate([ref[t0 + c] for c in range(n)], axis=1)

    reset(m_s, l_s, acc_s)
    diag = i // (KT // QT)

    def sel_scores(kt, slot):
        k0 = pl.multiple_of(kt * KT, KT)
        keys = jnp.concatenate([ks_ref[pl.ds(k0, KT), :], ext_ref[pl.ds(k0, KT), :]], axis=1)
        s_buf[slot] = jnp.dot(keys, wq, preferred_element_type=F32)

    def sel_attend(kt, slot, causal):
        pen = None
        if causal:
            pen = jnp.where(kt * KT + iota((KT, QT), 0) <= s0 + iota((KT, QT), 1), 0.0, NEG)
        attend(s_buf[slot], tiles(vst_ref, kt * (KT // QT), KT // QT), pen, m_s, l_s, acc_s)

    def sel_pair(j, c):
        sel_scores(2 * j + 1, 1)
        sel_attend(2 * j, 0, False)
        sel_scores(2 * j + 2, 0)
        sel_attend(2 * j + 1, 1, False)
        return c

    d0 = pl.multiple_of(s0, QT)
    a0 = pl.multiple_of(jnp.maximum(s0 - WINDOW, 0), QT)
    s_d = jnp.dot(kw_ref[pl.ds(d0, QT), :], qt, preferred_element_type=F32)
    s_a = jnp.dot(kw_ref[pl.ds(a0, WINDOW), :], qt, preferred_element_type=F32)
    pen_d = jnp.where(iota((QT, QT), 0) <= iota((QT, QT), 1), 0.0, NEG)
    kpos = a0 + iota((WINDOW, QT), 0)
    pen_a = jnp.where((kpos < s0) & (kpos > s0 + iota((WINDOW, QT), 1) - WINDOW), 0.0, NEG)

    sel_scores(0, 0)
    reset(m_w, l_w, acc_w)
    attend(s_d, vwt_ref[i], pen_d, m_w, l_w, acc_w)
    attend(s_a, tiles(vwt_ref, a0 // QT, WINDOW // QT), pen_a, m_w, l_w, acc_w)
    lax.fori_loop(0, diag // 2, sel_pair, 0)

    @pl.when(diag % 2 == 1)
    def _():
        sel_scores(diag, 1)
        sel_attend(diag - 1, 0, False)
        sel_attend(diag, 1, True)

    @pl.when(diag % 2 == 0)
    def _():
        sel_attend(diag, 0, True)

    gtt = gt_ref[...].T
    o_s = acc_s[...] / l_s[...]
    o_w = acc_w[...] / l_w[...]
    outs = []
    for h in heads:
        gate = lambda br: gtt[br * NS_HEADS + h:br * NS_HEADS + h + 1]
        o = gate(0) * o_c[:, hcols(h)] + gate(1) * o_s[:, hcols(h)] + gate(2) * o_w[:, hcols(h)]
        outs.append(o[:NS_HEAD] if h < NS_HPG else o[NS_HEAD:])
    pairs = [jnp.concatenate(outs[p:p + 2], axis=0).T for p in range(0, NS_HEADS, 2)]
    o_ref[...] = jnp.concatenate(pairs, axis=1).astype(BF16)


def _nsa_attn(q3, kcv, kcvt, kv3, vt, gt3, ovt):
    B, S, _ = q3.shape
    ncp = kcv.shape[2]
    R = NS_HEADS * QT
    ext = (jnp.arange(S, dtype=I32)[:, None] // SEL_BLOCK == jnp.arange(LANES, dtype=I32)[None, :]).astype(BF16)
    seq = lambda c: pl.BlockSpec((None, S, LANES), lambda b, i, c=c: (b, 0, c))
    seqt = lambda c: pl.BlockSpec((None, None, S // QT, LANES, QT), lambda b, i, c=c: (c, b, 0, 0, 0))
    return pl.pallas_call(
        functools.partial(_nsa_attn_kernel, ncp=ncp),
        grid=(B, S // QT),
        in_specs=[pl.BlockSpec((None, QT, NS_W), lambda b, i: (b, i, 0)),
                  pl.BlockSpec((None, None, ncp, LANES), lambda b, i: (b, 0, 0, 0)),
                  pl.BlockSpec((None, None, LANES, ncp), lambda b, i: (b, 1, 0, 0)),
                  seq(0), seqt(0), seq(1), seqt(1),
                  pl.BlockSpec((None, QT, LANES), lambda b, i: (b, i, 0)),
                  pl.BlockSpec(ovt.shape, lambda b, i: (0, 0)),
                  pl.BlockSpec(ext.shape, lambda b, i: (0, 0))],
        out_specs=pl.BlockSpec((None, QT, NS_W), lambda b, i: (b, i, 0)),
        out_shape=jax.ShapeDtypeStruct((B, S, NS_W), BF16),
        scratch_shapes=([pltpu.VMEM((1, R), F32), pltpu.VMEM((1, R), F32), pltpu.VMEM((LANES, R), F32)] * 2
                        + [pltpu.VMEM((2, KT, R), F32)]),
        compiler_params=_cparams(("parallel", "arbitrary")),
        name="nsa_attn",
    )(q3, kcv, kcvt, kv3, vt, kv3, vt, gt3, ovt, ext)


def _first_argmax(vals):
    m = vals[0]
    for v in vals[1:]:
        m = jnp.maximum(m, v)
    idx = jnp.full(m.shape, len(vals) - 1, I32)
    for j in range(len(vals) - 2, -1, -1):
        idx = jnp.where(vals[j] == m, j, idx)
    return m, idx


def _outproj_router_kernel(*refs, n_in):
    acts, ws = refs[:n_in], refs[n_in:2 * n_in]
    x_ref, lng_ref, lnb_ref, rwt_ref, rb_ref, tri_ref = refs[2 * n_in:2 * n_in + 6]
    y_ref, e_ref, wt_ref, pos_ref, cnt_ref, cnt = refs[2 * n_in + 6:]
    i = pl.program_id(0)

    @pl.when(i == 0)
    def _():
        cnt[...] = jnp.zeros_like(cnt)

    mix = _bdot(acts[0][...], ws[0][...])
    for a, w in zip(acts[1:], ws[1:]):
        mix = mix + _bdot(a[...], w[...])
    y = _layer_norm_rows(ALPHA * x_ref[...] + mix, lng_ref[...], lnb_ref[...])
    y_ref[...] = y

    logit = lax.dot_general(rwt_ref[...], y, (((1,), (1,)), ((), ())), precision=HI, preferred_element_type=F32)
    aff = jax.nn.sigmoid(logit)
    biased = aff + rb_ref[...]
    neg_inf = -jnp.inf
    g_score, g_i1, g_i2 = [], [], []
    for gi in range(N_GROUPS):
        vals = [biased[gi * EXP_PER_GROUP + j:gi * EXP_PER_GROUP + j + 1, :] for j in range(EXP_PER_GROUP)]
        m1, i1 = _first_argmax(vals)
        m2, i2 = _first_argmax([jnp.where(i1 == j, neg_inf, vals[j]) for j in range(EXP_PER_GROUP)])
        g_score.append(m1 + m2)
        g_i1.append(i1)
        g_i2.append(i2)
    _, grp = _first_argmax(g_score)
    loc1, loc2 = g_i1[-1], g_i2[-1]
    for gi in range(N_GROUPS - 2, -1, -1):
        loc1 = jnp.where(grp == gi, g_i1[gi], loc1)
        loc2 = jnp.where(grp == gi, g_i2[gi], loc2)
    e1 = grp * EXP_PER_GROUP + loc1
    e2 = grp * EXP_PER_GROUP + loc2
    eio = lax.broadcasted_iota(I32, aff.shape, 0)
    oh1 = eio == e1
    oh2 = eio == e2
    a1 = jnp.sum(jnp.where(oh1, aff, 0.0), axis=0, keepdims=True)
    a2 = jnp.sum(jnp.where(oh2, aff, 0.0), axis=0, keepdims=True)
    tot = a1 + a2
    e_ref[...] = jnp.concatenate([e1, e2], axis=0)
    wt_ref[...] = jnp.concatenate([a1 / tot, a2 / tot], axis=0)

    ohs = oh1.astype(F32) + oh2.astype(F32)
    before = jnp.dot(ohs.astype(BF16), tri_ref[...], preferred_element_type=F32) + cnt[...]
    p1 = jnp.sum(jnp.where(oh1, before, 0.0), axis=0, keepdims=True)
    p2 = jnp.sum(jnp.where(oh2, before, 0.0), axis=0, keepdims=True)
    pos_ref[...] = jnp.concatenate([p1, p2], axis=0).astype(I32)
    cnt[...] = cnt[...] + jnp.sum(ohs, axis=1, keepdims=True)
    cnt_ref[...] = jnp.broadcast_to(cnt[...], cnt_ref.shape)


def _outproj_router(acts, ws, xres, lng, lnb, rwt, rb, tm=1024):
    T = xres.shape[0]
    n_in = len(acts)
    tri = (lax.broadcasted_iota(I32, (tm, tm), 0) < lax.broadcasted_iota(I32, (tm, tm), 1)).astype(BF16)
    row = lambda i: (i, 0)
    const = lambda a: pl.BlockSpec(a.shape, lambda i: (0,) * a.ndim)
    lane_blk = pl.BlockSpec((TOP_K, tm), lambda i: (0, i))
    return pl.pallas_call(
        functools.partial(_outproj_router_kernel, n_in=n_in),
        grid=(T // tm,),
        in_specs=([pl.BlockSpec((tm, a.shape[1]), row) for a in acts] + [const(w) for w in ws]
                  + [pl.BlockSpec((tm, D_MODEL), row), const(lng), const(lnb), const(rwt), const(rb), const(tri)]),
        out_specs=[pl.BlockSpec((tm, D_MODEL), row), lane_blk, lane_blk, lane_blk,
                   pl.BlockSpec((N_EXPERTS, LANES), lambda i: (0, 0))],
        out_shape=[jax.ShapeDtypeStruct((T, D_MODEL), F32), jax.ShapeDtypeStruct((TOP_K, T), I32),
                   jax.ShapeDtypeStruct((TOP_K, T), F32), jax.ShapeDtypeStruct((TOP_K, T), I32),
                   jax.ShapeDtypeStruct((N_EXPERTS, LANES), F32)],
        scratch_shapes=[pltpu.VMEM((N_EXPERTS, 1), F32)],
        compiler_params=_cparams(("arbitrary",)),
        name="outproj_router",
    )(*acts, *ws, xres, lng, lnb, rwt, rb, tri)


def _dispatch_kernel(zrow_ref, dest_hbm, x_ref, xs_hbm, dsm, zbuf, sem_idx, sem, sem_z):
    i = pl.program_id(0)
    tm = x_ref.shape[0]

    @pl.when(i == 0)
    def _():
        zbuf[...] = jnp.zeros_like(zbuf)
        zero_copy = lambda j: pltpu.make_async_copy(
            zbuf, xs_hbm.at[pl.ds(pl.multiple_of(jnp.maximum(zrow_ref[j], 0), MOE_BLOCK), MOE_BLOCK)], sem_z)
        for j in range(zrow_ref.shape[0]):
            @pl.when(zrow_ref[j] >= 0)
            def _():
                zero_copy(j).start()
        for j in range(zrow_ref.shape[0]):
            @pl.when(zrow_ref[j] >= 0)
            def _():
                zero_copy(j).wait()

    idx_copy = pltpu.make_async_copy(dest_hbm.at[i], dsm, sem_idx)
    idx_copy.start()
    idx_copy.wait()

    def row_copy(r, k):
        return pltpu.make_async_copy(x_ref.at[pl.ds(r, 1)], xs_hbm.at[pl.ds(dsm[k * tm + r], 1)], sem)

    def start(r, c):
        for k in range(TOP_K):
            row_copy(r, k).start()
        return c

    lax.fori_loop(0, tm, start, 0, unroll=ROW_DMA_UNROLL)
    for k in range(TOP_K):
        pltpu.make_async_copy(x_ref, xs_hbm.at[pl.ds(0, tm)], sem).wait()


def _dispatch(zero_rows, dest_tiles, x, rows, tm):
    T = x.shape[0]
    return pl.pallas_call(
        _dispatch_kernel,
        grid_spec=pltpu.PrefetchScalarGridSpec(
            num_scalar_prefetch=1,
            grid=(T // tm,),
            in_specs=[pl.BlockSpec(memory_space=pl.ANY), pl.BlockSpec((tm, D_MODEL), lambda i, z: (i, 0))],
            out_specs=pl.BlockSpec(memory_space=pl.ANY),
            scratch_shapes=[pltpu.SMEM((TOP_K * tm,), I32), pltpu.VMEM((MOE_BLOCK, D_MODEL), F32),
                            pltpu.SemaphoreType.DMA(()), pltpu.SemaphoreType.DMA(()), pltpu.SemaphoreType.DMA(())]),
        out_shape=jax.ShapeDtypeStruct((rows, D_MODEL), F32),
        compiler_params=_cparams(("arbitrary",)),
        name="moe_dispatch",
    )(zero_rows, dest_tiles, x)


def _ffn_kernel(be_ref, nu_ref, xs_ref, wg_ref, wu_ref, wd_ref, y_ref, h_ref):
    del be_ref
    i = pl.program_id(0)
    half = D_EXPERT // 2

    @pl.when(i < nu_ref[0])
    def _():
        xb = xs_ref[...].astype(BF16)
        for c in range(2):
            cs = slice(c * half, (c + 1) * half)
            gt = jnp.dot(xb, wg_ref[:, cs], preferred_element_type=F32)
            up = jnp.dot(xb, wu_ref[:, cs], preferred_element_type=F32)
            h_ref[:, cs] = (jax.nn.silu(gt) * up).astype(BF16)
        y_ref[...] = jnp.dot(h_ref[...], wd_ref[...], preferred_element_type=F32)

    @pl.when(i >= nu_ref[0])
    def _():
        y_ref[...] = jnp.zeros_like(y_ref)


def _ffn(blk_exp, n_used, xs, wg, wu, wd):
    rows = xs.shape[0]
    wspec = lambda: pl.BlockSpec((None, D_MODEL, D_EXPERT), lambda i, be, nu: (be[i], 0, 0))
    return pl.pallas_call(
        _ffn_kernel,
        grid_spec=pltpu.PrefetchScalarGridSpec(
            num_scalar_prefetch=2,
            grid=(rows // MOE_BLOCK,),
            in_specs=[pl.BlockSpec((MOE_BLOCK, D_MODEL), lambda i, be, nu: (i, 0)), wspec(), wspec(),
                      pl.BlockSpec((None, D_EXPERT, D_MODEL), lambda i, be, nu: (be[i], 0, 0))],
            out_specs=pl.BlockSpec((MOE_BLOCK, D_MODEL), lambda i, be, nu: (i, 0)),
            scratch_shapes=[pltpu.VMEM((MOE_BLOCK, D_EXPERT), BF16)]),
        out_shape=jax.ShapeDtypeStruct((rows, D_MODEL), F32),
        compiler_params=_cparams(("arbitrary",)),
        name="moe_ffn",
    )(blk_exp, n_used, xs, wg, wu, wd)


def _combine_kernel(dest_hbm, y_hbm, x_ref, wt_ref, lng_ref, lnb_ref, o_ref, dsm, buf, sem_idx, sem):
    i = pl.program_id(0)
    tm = x_ref.shape[0]
    idx_copy = pltpu.make_async_copy(dest_hbm.at[i], dsm, sem_idx)
    idx_copy.start()
    idx_copy.wait()

    def row_copy(r, k):
        return pltpu.make_async_copy(y_hbm.at[pl.ds(dsm[k * tm + r], 1)], buf.at[k, pl.ds(r, 1)], sem)

    def start(r, c):
        for k in range(TOP_K):
            row_copy(r, k).start()
        return c

    lax.fori_loop(0, tm, start, 0, unroll=ROW_DMA_UNROLL)
    for k in range(TOP_K):
        pltpu.make_async_copy(y_hbm.at[pl.ds(0, tm)], buf.at[k], sem).wait()
    wt = wt_ref[...]
    z = ALPHA * x_ref[...] + wt[:, 0:1] * buf[0] + wt[:, 1:2] * buf[1]
    o_ref[...] = _layer_norm_rows(z, lng_ref[...], lnb_ref[...])


def _combine(dest_tiles, y, x, wt, lng, lnb, tm):
    T = x.shape[0]
    row = lambda i: (i, 0)
    vec = pl.BlockSpec((1, D_MODEL), lambda i: (0, 0))
    return pl.pallas_call(
        _combine_kernel,
        grid=(T // tm,),
        in_specs=[pl.BlockSpec(memory_space=pl.ANY), pl.BlockSpec(memory_space=pl.ANY),
                  pl.BlockSpec((tm, D_MODEL), row), pl.BlockSpec((tm, TOP_K), row), vec, vec],
        out_specs=pl.BlockSpec((tm, D_MODEL), row),
        out_shape=jax.ShapeDtypeStruct((T, D_MODEL), F32),
        scratch_shapes=[pltpu.SMEM((TOP_K * tm,), I32), pltpu.VMEM((TOP_K, tm, D_MODEL), F32),
                        pltpu.SemaphoreType.DMA(()), pltpu.SemaphoreType.DMA(())],
        compiler_params=_cparams(("arbitrary",)),
        name="moe_combine",
    )(dest_tiles, y, x, wt, lng, lnb)


def _moe(x1, e, wt, pos, cnt, wg, wu, wd, lng, lnb, tm=1024):
    T = x1.shape[0]
    n_blocks = -(-(T * TOP_K) // MOE_BLOCK) + N_EXPERTS
    rows = n_blocks * MOE_BLOCK
    counts = cnt[:, 0].astype(I32)
    padded = (counts + MOE_BLOCK - 1) // MOE_BLOCK * MOE_BLOCK
    pad_end = jnp.cumsum(padded)
    pad_start = pad_end - padded
    dest = pos
    for j in range(N_EXPERTS):
        dest = dest + jnp.where(e == j, pad_start[j], 0)
    blk_start = jnp.arange(n_blocks, dtype=I32) * MOE_BLOCK
    blk_exp = jnp.minimum(jnp.sum((pad_end[None, :] <= blk_start[:, None]).astype(I32), axis=1), N_EXPERTS - 1)
    n_used = (pad_end[-1:] // MOE_BLOCK).astype(I32)
    dest_tiles = dest.reshape(TOP_K, T // tm, tm).transpose(1, 0, 2).reshape(T // tm, TOP_K * tm)
    tail = jnp.where(padded > 0, pad_end - MOE_BLOCK, -1)
    trailing = jnp.arange(n_blocks - N_EXPERTS, n_blocks, dtype=I32)
    trailing = jnp.where(trailing >= n_used[0], trailing * MOE_BLOCK, -1)
    xs = _dispatch(jnp.concatenate([tail, trailing]).astype(I32), dest_tiles, x1, rows, tm)
    y = _ffn(blk_exp, n_used, xs, wg, wu, wd)
    return _combine(dest_tiles, y, x1, wt.T, lng, lnb, tm)


def _proj_rt_kernel(x_ref, w_ref, c_ref, s_ref, q_ref, k_ref, v_ref, g_ref):
    xb = x_ref[...].astype(BF16)
    cs, sn = c_ref[...], s_ref[...]
    rope = lambda y: y * cs + pltpu.roll(y, RT_QK // 2, 1) * sn
    for c in range(RT_HEADS):
        cols = slice(c * RT_QK, (c + 1) * RT_QK)
        q_ref[:, cols] = rope(jnp.dot(xb, w_ref[:, cols], preferred_element_type=F32)).astype(BF16)
    for c in range(RT_HEADS):
        cols = slice(c * RT_QK, (c + 1) * RT_QK)
        yk = jnp.dot(xb, w_ref[:, RT_QKW + c * RT_QK:RT_QKW + (c + 1) * RT_QK], preferred_element_type=F32)
        k_ref[:, cols] = (rope(yk) * RT_QK ** -0.5).astype(BF16)
    step = 1024
    for c in range(RT_VW // step):
        cols = slice(c * step, (c + 1) * step)
        v_ref[:, cols] = jnp.dot(xb, w_ref[:, 2 * RT_QKW + c * step:2 * RT_QKW + (c + 1) * step],
                                 preferred_element_type=F32).astype(BF16)
        base = 2 * RT_QKW + RT_VW
        g_ref[:, cols] = jax.nn.silu(jnp.dot(xb, w_ref[:, base + c * step:base + (c + 1) * step],
                                             preferred_element_type=F32))


def _proj_rt(xt, w, cs, sn, S, tm=512):
    T = xt.shape[0]
    nseq = S // tm
    row = lambda i: (i, 0)
    tab = lambda i: (i % nseq, 0)
    return pl.pallas_call(
        _proj_rt_kernel,
        grid=(T // tm,),
        in_specs=[pl.BlockSpec((tm, D_MODEL), row), pl.BlockSpec(w.shape, lambda i: (0, 0)),
                  pl.BlockSpec((tm, RT_QK), tab), pl.BlockSpec((tm, RT_QK), tab)],
        out_specs=[pl.BlockSpec((tm, RT_QKW), row), pl.BlockSpec((tm, RT_QKW), row),
                   pl.BlockSpec((tm, RT_VW), row), pl.BlockSpec((tm, RT_VW), row)],
        out_shape=[jax.ShapeDtypeStruct((T, RT_QKW), BF16), jax.ShapeDtypeStruct((T, RT_QKW), BF16),
                   jax.ShapeDtypeStruct((T, RT_VW), BF16), jax.ShapeDtypeStruct((T, RT_VW), F32)],
        compiler_params=_cparams(("parallel",)),
        name="proj_rt",
    )(xt, w, cs, sn)


def _retention_kernel(q_ref, k_ref, v_ref, sg_ref, dec_ref, qd_ref, kd_ref, cd_ref, gng_ref, gnb_ref, o_ref, state):
    j = pl.program_id(1)

    @pl.when(j == 0)
    def _():
        state[...] = jnp.zeros_like(state)

    NB = q_ref.shape[0]
    H = range(NB * RT_HEADS)
    nh = lambda i: (i // RT_HEADS, i % RT_HEADS)
    qk_cols = lambda i: slice(nh(i)[1] * RT_QK, (nh(i)[1] + 1) * RT_QK)
    v_cols = lambda i: slice(nh(i)[1] * RT_V, (nh(i)[1] + 1) * RT_V)
    q = [q_ref[nh(i)[0], :, qk_cols(i)] for i in H]
    k = [k_ref[nh(i)[0], :, qk_cols(i)] for i in H]
    v = [v_ref[nh(i)[0], :, v_cols(i)] for i in H]
    r_old = [state[i] for i in H]
    att = [(_dot_nt(q[i], k[i]) * dec_ref[nh(i)[1]]).astype(BF16) for i in H]
    cross = [jnp.dot(q[i], r_old[i].astype(BF16), preferred_element_type=F32) * qd_ref[nh(i)[1]] for i in H]
    inner = [jnp.dot(att[i], v[i], preferred_element_type=F32) for i in H]
    kdec = [(k[i].astype(F32) * kd_ref[nh(i)[1]]).astype(BF16) for i in H]
    for i in H:
        upd = lax.dot_general(kdec[i], v[i], (((0,), (0,)), ((), ())), preferred_element_type=F32)
        state[i] = r_old[i] * cd_ref[nh(i)[1]][:, 0:1] + upd
    for i in H:
        o = inner[i] + cross[i]
        mu = jnp.mean(o, axis=1, keepdims=True)
        oc = o - mu
        var = jnp.mean(oc * oc, axis=1, keepdims=True)
        on = oc * lax.rsqrt(var + RT_GN_EPS) * gng_ref[:, v_cols(i)] + gnb_ref[:, v_cols(i)]
        o_ref[nh(i)[0], :, v_cols(i)] = (sg_ref[nh(i)[0], :, v_cols(i)] * on).astype(BF16)


def _retention(q3, k3, v3, sg3, dec, qd, kd, cd, gng, gnb, nb=2):
    B, S, _ = q3.shape
    C = RT_CHUNK
    nb = nb if B % nb == 0 else 1
    qk = pl.BlockSpec((nb, C, RT_QKW), lambda b, j: (b, j, 0))
    vv = pl.BlockSpec((nb, C, RT_VW), lambda b, j: (b, j, 0))
    const = lambda a: pl.BlockSpec(a.shape, lambda b, j: (0,) * a.ndim)
    return pl.pallas_call(
        _retention_kernel,
        grid=(B // nb, S // C),
        in_specs=[qk, qk, vv, vv, const(dec), const(qd), const(kd), const(cd), const(gng), const(gnb)],
        out_specs=vv,
        out_shape=jax.ShapeDtypeStruct((B, S, RT_VW), BF16),
        scratch_shapes=[pltpu.VMEM((nb * RT_HEADS, RT_QK, RT_V), F32)],
        compiler_params=_cparams(("parallel", "arbitrary")),
        name="retention",
    )(q3, k3, v3, sg3, dec, qd, kd, cd, gng, gnb)


def _nsa_rope_tables(S):
    half = ROPE_DIM // 2
    inv = ROPE_THETA ** (-jnp.arange(half, dtype=F32) / half)
    ang = jnp.arange(S, dtype=F32)[:, None] * inv[None, :]
    cos, sin = jnp.cos(ang), jnp.sin(ang)
    zeros = lambda n: jnp.zeros((S, n), F32)
    cn = jnp.concatenate([cos, cos, jnp.ones((S, NS_HEAD - ROPE_DIM), F32)], axis=1)
    s1 = jnp.concatenate([-sin, zeros(NS_HEAD - half)], axis=1)
    s2 = jnp.concatenate([zeros(half), sin, zeros(NS_HEAD - ROPE_DIM)], axis=1)
    two = lambda a: jnp.concatenate([a, a], axis=1)
    return two(cn), two(s1), two(s2)


def _rt_rope_tables(S):
    inv = RT_THETA ** (-jnp.linspace(0.0, 1.0, RT_QK // 2, dtype=F32))
    ang = jnp.arange(S, dtype=F32)[:, None] * inv[None, :]
    cos, sin = jnp.cos(ang), jnp.sin(ang)
    return jnp.concatenate([cos, cos], axis=1), jnp.concatenate([-sin, sin], axis=1)


def _rt_decay_tables():
    log_g = jnp.log(1.0 - 2.0 ** (-5.0 - jnp.arange(RT_HEADS, dtype=F32)))
    idx = jnp.arange(RT_CHUNK, dtype=F32)
    diff = idx[:, None] - idx[None, :]
    dec = jnp.where(diff >= 0, jnp.exp(jnp.maximum(diff, 0.0) * log_g[:, None, None]), 0.0)
    qd = jnp.exp((idx + 1.0) * log_g[:, None])[..., None]
    kd = jnp.exp((RT_CHUNK - 1.0 - idx) * log_g[:, None])[..., None]
    cd = jnp.broadcast_to(jnp.exp(RT_CHUNK * log_g)[:, None, None], (RT_HEADS, 1, LANES))
    return dec, qd, kd, cd


def _overlap_table(S, ncp):
    n_cmp = (S - CMP_LEN) // CMP_STRIDE + 1
    n_sel = S // SEL_BLOCK
    cs = jnp.arange(ncp) * CMP_STRIDE
    ss = jnp.arange(LANES) * SEL_BLOCK
    ov = jnp.clip(jnp.minimum(cs[:, None] + CMP_LEN, ss[None, :] + SEL_BLOCK)
                  - jnp.maximum(cs[:, None], ss[None, :]), 0, None).astype(F32) / CMP_LEN
    keep = (jnp.arange(ncp)[:, None] < n_cmp) & (jnp.arange(LANES)[None, :] < n_sel)
    return jnp.where(keep, ov, 0.0)


def kernel(x, ab_w_in, ab_w_out, rk_mu, rk_w0, rk_w1, rk_w2, rk_a0, rk_a1, rk_a2, rk_g1, rk_g2, rk_kk, rk_ka, rk_rk,
           rk_ln, ns_pe, ns_c_w1, ns_c_w2, rt_w_in, rt_w_out, rt_gn, router_w, router_b, moe_w_gate, moe_w_up,
           moe_w_down, ln):
    B, S, D = x.shape
    T = B * S
    assert D == D_MODEL and S % 256 == 0 and S // SEL_BLOCK <= LANES and S >= WINDOW
    xt = x.reshape(T, D)
    rwt = router_w.T
    rb = router_b.reshape(N_EXPERTS, 1)
    vec = lambda a: a.reshape(1, -1)

    w_in = ab_w_in[0]
    n_gate = 3 * NS_HEADS
    w_cat = jnp.concatenate([w_in[:, :-n_gate], jnp.pad(w_in[:, -n_gate:], ((0, 0), (0, LANES - n_gate)))],
                            axis=1).astype(BF16)
    cn, s1, s2 = _nsa_rope_tables(S)
    prk, q, kk2, gt, vt, cx = _proj_ab(xt, w_cat, cn, s1, s2, S)

    ones = (jnp.arange(RK_W)[:, None] // RK_HEAD == jnp.arange(RK_W)[None, :] // RK_HEAD).astype(F32)
    b16 = lambda a: a.astype(BF16)
    r, lw, km, v, kk, bb, g, bon = _rwkv_prep(
        prk.reshape(B, S, 4 * RK_W), rk_mu[0], vec(rk_w0[0]), b16(rk_w1[0]), b16(rk_w2[0]), vec(rk_a0[0]),
        b16(rk_a1[0]), b16(rk_a2[0]), b16(rk_g1[0]), b16(rk_g2[0]), vec(rk_kk[0]), vec(rk_ka[0]), vec(rk_rk[0]), ones)
    o_a = _rwkv_chunk(r, lw, km, v, kk, bb, g, bon, rk_ln[0, 0:1], rk_ln[0, 1:2])

    ncp = S // CMP_STRIDE
    kcv, kcvt = _nsa_compress(cx.reshape(2, B, ncp, CMP_STRIDE * LANES), ns_c_w1[0], b16(ns_c_w2[0]),
                              ns_pe[0].reshape(2, 1, CMP_LEN * NS_HEAD))
    o_b = _nsa_attn(q.reshape(B, S, NS_W), kcv, kcvt, kk2.reshape(B, S, 2 * LANES),
                    vt.reshape(2, B, S // QT, LANES, QT), gt.reshape(B, S, LANES), _overlap_table(S, ncp).T)

    w_out = b16(ab_w_out[0])
    x1, e, wt, pos, cnt = _outproj_router([o_a.reshape(T, RK_W), o_b.reshape(T, NS_W)], [w_out[:RK_W], w_out[RK_W:]],
                                          xt, ln[0, 0, 0:1], ln[0, 0, 1:2], rwt, rb)
    x2 = _moe(x1, e, wt, pos, cnt, b16(moe_w_gate[0]), b16(moe_w_up[0]), b16(moe_w_down[0]),
              ln[0, 1, 0:1], ln[0, 1, 1:2])

    cs, sn = _rt_rope_tables(S)
    qr, kr, vr, sg = _proj_rt(x2, b16(rt_w_in[0]), cs, sn, S)
    dec, qd, kd, cd = _rt_decay_tables()
    ret = _retention(qr.reshape(B, S, RT_QKW), kr.reshape(B, S, RT_QKW), vr.reshape(B, S, RT_VW),
                     sg.reshape(B, S, RT_VW), dec, qd, kd, cd, rt_gn[0, 0:1], rt_gn[0, 1:2])
    x3, e, wt, pos, cnt = _outproj_router([ret.reshape(T, RT_VW)], [b16(rt_w_out[0])], x2,
                                          ln[1, 0, 0:1], ln[1, 0, 1:2], rwt, rb)
    x4 = _moe(x3, e, wt, pos, cnt, b16(moe_w_gate[1]), b16(moe_w_up[1]), b16(moe_w_down[1]),
              ln[1, 1, 0:1], ln[1, 1, 1:2])
    return x4.reshape(B, S, D)
```

```python
import functools
import math

import jax
import jax.numpy as jnp
from jax import lax
from jax.experimental import pallas as pl
from jax.experimental.pallas import tpu as pltpu

F32 = jnp.float32
BF16 = jnp.bfloat16
I32 = jnp.int32
HI = lax.Precision.HIGHEST

LANES = 128
VMEM_LIMIT = 56 * 1024 * 1024

D_MODEL = 1024
RK_HEADS, RK_HEAD = 8, 64
RK_W = RK_HEADS * RK_HEAD
RK_DECAY_SCALE = 0.606531
RK_LN_EPS = 64e-5
NS_HEADS, NS_KV, NS_HPG, NS_HEAD = 8, 2, 4, 64
NS_W = NS_HEADS * NS_HEAD
CMP_LEN, CMP_STRIDE, SEL_BLOCK, SEL_TOPK, WINDOW = 32, 16, 64, 16, 512
ROPE_THETA = 500000.0
ROPE_DIM = NS_HEAD // 4
Q_SCALE = NS_HEAD ** -0.5 * math.log2(math.e)
QT = 128
KT = 512
RT_HEADS, RT_QK, RT_V = 8, 128, 256
RT_QKW, RT_VW = RT_HEADS * RT_QK, RT_HEADS * RT_V
RT_CHUNK = 128
RT_THETA = 10000.0
RT_GN_EPS = 1e-5
N_EXPERTS, N_GROUPS, EXP_PER_GROUP, TOP_K = 16, 4, 4, 2
D_EXPERT = 1024
MOE_BLOCK = 512
ROW_DMA_UNROLL = 8
DEPTH = 2
ALPHA = (2.0 * DEPTH) ** 0.25
LN_EPS = 1e-5
NEG = -1e30


def _cparams(sem):
    return pltpu.CompilerParams(dimension_semantics=sem, vmem_limit_bytes=VMEM_LIMIT)


def _bdot(a, w):
    return jnp.dot(a.astype(BF16), w, preferred_element_type=F32)


def _dot_nt(a, b):
    return lax.dot_general(a, b, (((1,), (1,)), ((), ())), preferred_element_type=F32)


def _layer_norm_rows(z, g, b):
    mu = jnp.mean(z, axis=1, keepdims=True)
    zc = z - mu
    var = jnp.mean(zc * zc, axis=1, keepdims=True)
    return zc * lax.rsqrt(var + LN_EPS) * g + b


def _proj_ab_kernel(x_ref, w_ref, cn_ref, s1_ref, s2_ref, pick_ref, prk_ref, q_ref, kk_ref, gt_ref, vt_ref, cx_ref):
    xb = x_ref[...].astype(BF16)
    for c in range(4):
        prk_ref[:, c * RK_W:(c + 1) * RK_W] = jnp.dot(xb, w_ref[:, c * RK_W:(c + 1) * RK_W],
                                                      preferred_element_type=F32)
    cn, s1, s2 = cn_ref[...], s1_ref[...], s2_ref[...]

    def rope(y):
        return y * cn + pltpu.roll(y, LANES - ROPE_DIM // 2, 1) * s1 + pltpu.roll(y, ROPE_DIM // 2, 1) * s2

    base = 4 * RK_W
    yq = jnp.dot(xb, w_ref[:, base:base + NS_W], preferred_element_type=F32)
    for c in range(NS_W // LANES):
        q_ref[:, c * LANES:(c + 1) * LANES] = (rope(yq[:, c * LANES:(c + 1) * LANES]) * Q_SCALE).astype(BF16)
    base += NS_W
    ykv = jnp.dot(xb, w_ref[:, base:base + 6 * LANES], preferred_element_type=F32)
    for c in range(6):
        y = ykv[:, c * LANES:(c + 1) * LANES]
        if c % 2 == 0:
            y = rope(y)
        if c < 2:
            yb = y.astype(BF16)
            for l in range(CMP_STRIDE):
                cx_ref[c, :, l * LANES:(l + 1) * LANES] = jnp.dot(pick_ref[l], yb,
                                                                  preferred_element_type=F32).astype(BF16)
        elif c % 2 == 0:
            kk_ref[:, (c // 2 - 1) * LANES:(c // 2) * LANES] = y.astype(BF16)
        else:
            for t in range(y.shape[0] // QT):
                vt_ref[c // 2 - 1, t] = y[t * QT:(t + 1) * QT].T.astype(BF16)
    base += 6 * LANES
    gt_ref[...] = jax.nn.sigmoid(jnp.dot(xb, w_ref[:, base:base + LANES], preferred_element_type=F32))


def _proj_ab(xt, w, cn, s1, s2, S, tm=512):
    T = xt.shape[0]
    ncols = w.shape[1]
    nseq = S // tm
    row = lambda i: (i, 0)
    tab = lambda i: (i % nseq, 0)
    ng = tm // CMP_STRIDE
    pick = (jnp.arange(tm, dtype=I32)[None, None, :]
            == jnp.arange(ng, dtype=I32)[None, :, None] * CMP_STRIDE + jnp.arange(CMP_STRIDE, dtype=I32)[:, None, None])
    pick = pick.astype(BF16)
    return pl.pallas_call(
        _proj_ab_kernel,
        grid=(T // tm,),
        in_specs=[pl.BlockSpec((tm, D_MODEL), row),
                  pl.BlockSpec((D_MODEL, ncols), lambda i: (0, 0)),
                  pl.BlockSpec((tm, LANES), tab), pl.BlockSpec((tm, LANES), tab), pl.BlockSpec((tm, LANES), tab),
                  pl.BlockSpec(pick.shape, lambda i: (0, 0, 0))],
        out_specs=[pl.BlockSpec((tm, 4 * RK_W), row), pl.BlockSpec((tm, NS_W), row),
                   pl.BlockSpec((tm, 2 * LANES), row), pl.BlockSpec((tm, LANES), row),
                   pl.BlockSpec((2, tm // QT, LANES, QT), lambda i: (0, i, 0, 0)),
                   pl.BlockSpec((2, ng, CMP_STRIDE * LANES), lambda i: (0, i, 0))],
        out_shape=[jax.ShapeDtypeStruct((T, 4 * RK_W), F32), jax.ShapeDtypeStruct((T, NS_W), BF16),
                   jax.ShapeDtypeStruct((T, 2 * LANES), BF16), jax.ShapeDtypeStruct((T, LANES), F32),
                   jax.ShapeDtypeStruct((2, T // QT, LANES, QT), BF16),
                   jax.ShapeDtypeStruct((2, T // CMP_STRIDE, CMP_STRIDE * LANES), BF16)],
        compiler_params=_cparams(("parallel",)),
        name="proj_ab",
    )(xt, w, cn, s1, s2, pick)


def _rwkv_prep_kernel(p_ref, mu_ref, w0_ref, w1_ref, w2_ref, a0_ref, a1_ref, a2_ref, g1_ref, g2_ref,
                      kk_ref, ka_ref, rk_ref, ones_ref,
                      r_o, w_o, k_o, v_o, kk_o, b_o, g_o, bon_o, carry):
    j = pl.program_id(1)
    p = p_ref[...]
    tm = p.shape[0]

    @pl.when(j == 0)
    def _():
        carry[...] = jnp.zeros_like(carry)

    rowi = lax.broadcasted_iota(I32, p.shape, 0)
    prev = jnp.where(rowi == 0, carry[...], pltpu.roll(p, 1, 0))
    carry[...] = p[tm - 1:tm, :]
    dp = prev - p
    sl = lambda a, c: a[:, c * RK_W:(c + 1) * RK_W]
    mu = mu_ref[...]
    r = sl(p, 0) + sl(dp, 0) * mu[0:1]
    k = sl(p, 1) + sl(dp, 1) * mu[1:2]
    v = sl(p, 2) + sl(dp, 2) * mu[2:3]
    xw = sl(p, 3) + sl(dp, 3) * mu[3:4]
    xa = sl(p, 3) + sl(dp, 3) * mu[4:5]
    xg = sl(p, 3) + sl(dp, 3) * mu[5:6]
    lw = -RK_DECAY_SCALE * jax.nn.sigmoid(w0_ref[...] + _bdot(jnp.tanh(_bdot(xw, w1_ref[...])), w2_ref[...]))
    a = jax.nn.sigmoid(a0_ref[...] + _bdot(_bdot(xa, a1_ref[...]), a2_ref[...]))
    g = _bdot(jax.nn.sigmoid(_bdot(xg, g1_ref[...])), g2_ref[...])
    ones = ones_ref[...]
    kk = k * kk_ref[...]
    ss = jnp.dot(kk * kk, ones, precision=HI, preferred_element_type=F32)
    kk = kk / jnp.maximum(jnp.sqrt(ss), 1e-12)
    km = k * (1.0 + (a - 1.0) * ka_ref[...])
    bon = jnp.dot(r * km * rk_ref[...], ones, precision=HI, preferred_element_type=F32) * v
    w_o[...] = lw
    for ref, val in ((r_o, r), (k_o, km), (v_o, v), (kk_o, kk), (b_o, kk * a), (g_o, g), (bon_o, bon)):
        ref[...] = val.astype(ref.dtype)


def _rwkv_prep(prk3, mu, w0, w1, w2, a0, a1, a2, g1, g2, k_k, k_a, r_k, ones, tm=512):
    B, S, _ = prk3.shape
    full = lambda a: pl.BlockSpec(a.shape, lambda b, j: (0,) * a.ndim)
    params = [mu, w0, w1, w2, a0, a1, a2, g1, g2, k_k, k_a, r_k, ones]
    ospec = pl.BlockSpec((None, tm, RK_W), lambda b, j: (b, j, 0))
    return pl.pallas_call(
        _rwkv_prep_kernel,
        grid=(B, S // tm),
        in_specs=[pl.BlockSpec((None, tm, 4 * RK_W), lambda b, j: (b, j, 0))] + [full(a) for a in params],
        out_specs=[ospec] * 8,
        out_shape=[jax.ShapeDtypeStruct((B, S, RK_W), F32 if i == 1 else BF16) for i in range(8)],
        scratch_shapes=[pltpu.VMEM((1, 4 * RK_W), F32)],
        compiler_params=_cparams(("parallel", "arbitrary")),
        name="rwkv_prep",
    )(prk3, *params)


RK_CHUNK = 16


def _pdot(a, b, dims, precise):
    if precise:
        return lax.dot_general(a, b, (dims, ((), ())), precision=HI, preferred_element_type=F32)
    return lax.dot_general(a.astype(BF16), b.astype(BF16), (dims, ((), ())), preferred_element_type=F32)


def _rwkv_chunk_kernel(r_ref, lw_ref, k_ref, v_ref, kk_ref, b_ref, g_ref, bon_ref, lng_ref, lnb_ref, o_ref, ht,
                       *, precise):
    j = pl.program_id(1)

    @pl.when(j == 0)
    def _():
        ht[...] = jnp.zeros_like(ht)

    NB, TT = r_ref.shape[0], r_ref.shape[1]
    C, N = RK_CHUNK, RK_HEAD
    mm = lambda a, b: _pdot(a, b, ((1,), (0,)), precise)
    mm_nt = lambda a, b: _pdot(a, b, ((1,), (1,)), precise)
    mm_tn = lambda a, b: _pdot(a, b, ((0,), (0,)), precise)
    wide = lambda ref: jnp.concatenate([ref[n] for n in range(NB)], axis=1)

    lw = wide(lw_ref)
    rowc = lax.broadcasted_iota(I32, lw.shape, 0) & (C - 1)
    linc, lrev = lw, lw
    sh = 1
    while sh < C:
        linc = linc + jnp.where(rowc >= sh, pltpu.roll(linc, sh, 0), 0.0)
        lrev = lrev + jnp.where(rowc < C - sh, pltpu.roll(lrev, TT - sh, 0), 0.0)
        sh *= 2
    lrev = lrev - lw
    r, k, v, kk, b = wide(r_ref), wide(k_ref), wide(v_ref), wide(kk_ref), wide(b_ref)
    e_in, e_inv, e_rev = jnp.exp(linc), jnp.exp(-linc), jnp.exp(lrev)
    kkd = kk * jnp.exp(linc - lw)
    rd = r * e_in
    binv, kinv = b * e_inv, k * e_inv
    bd, kd = b * e_rev, k * e_rev
    gam = jnp.exp(linc + lrev)

    ti = lax.broadcasted_iota(I32, (TT, TT), 0)
    tj = lax.broadcasted_iota(I32, (TT, TT), 1)
    same = (ti // C) == (tj // C)
    strict = same & (tj < ti)
    incl = same & (tj <= ti)

    H = range(NB * RK_HEADS)
    lo = lambda a: a if precise else a.astype(BF16)
    hs = lambda a: [a[:, h * N:(h + 1) * N] for h in H]
    rows2 = lambda a, b: jnp.concatenate([a, b], axis=0)
    kkd_h, rd_h, v_h = hs(lo(kkd)), hs(lo(rd)), hs(lo(v))
    binv_h, kinv_h, bd_h, kd_h = hs(lo(binv)), hs(lo(kinv)), hs(lo(bd)), hs(lo(kd))
    gam_h = hs(gam)
    gm = [mm_nt(rows2(kkd_h[h], rd_h[h]), rows2(binv_h[h], kinv_h[h])) for h in H]
    a_b = [lo(jnp.where(strict, gm[h][:TT, :TT], 0.0)) for h in H]
    b_rb = [lo(jnp.where(incl, gm[h][TT:, :TT], 0.0)) for h in H]
    akb = [lo(rows2(jnp.where(strict, gm[h][:TT, TT:], 0.0), jnp.where(incl, gm[h][TT:, TT:], 0.0))) for h in H]
    av = [mm(akb[h], v_h[h]) for h in H]
    x = [jnp.concatenate([kkd_h[h].astype(F32), av[h][:TT]], axis=1) for h in H]
    a2 = [lo(mm(a_b[h], a_b[h])) for h in H]
    a4 = [lo(mm(a2[h], a2[h])) for h in H]
    a8 = [lo(mm(a4[h], a4[h])) for h in H]
    x = [x[h] + mm(a8[h], lo(x[h])) for h in H]
    x = [x[h] + mm(a4[h], lo(x[h])) for h in H]
    x = [x[h] + mm(a2[h], lo(x[h])) for h in H]
    x = [x[h] - mm(a_b[h], lo(x[h])) for h in H]
    wt = [lo(x[h][:, :N]) for h in H]
    h_t = [ht[h] for h in H]
    us = [[] for _ in H]
    rhs = [[] for _ in H]
    for c in range(TT // C):
        rs = slice(c * C, (c + 1) * C)
        xh = [mm_nt(rows2(wt[h][rs], rd_h[h][rs]), lo(h_t[h])) for h in H]
        for h in H:
            u_c = -(xh[h][:C] + x[h][rs, N:])
            us[h].append(u_c)
            rhs[h].append(xh[h][C:])
        upd = [mm_tn(rows2(lo(us[h][c]), v_h[h][rs]), rows2(bd_h[h][rs], kd_h[h][rs])) for h in H]
        h_t = [h_t[h] * gam_h[h][c * C:c * C + 1] + upd[h] for h in H]
    outs = []
    for h in H:
        ht[h] = h_t[h]
        o = jnp.concatenate(rhs[h], axis=0) + mm(b_rb[h], lo(jnp.concatenate(us[h], axis=0))) + av[h][TT:]
        mu = jnp.mean(o, axis=1, keepdims=True)
        oc = o - mu
        var = jnp.mean(oc * oc, axis=1, keepdims=True)
        outs.append(oc * lax.rsqrt(var + RK_LN_EPS))
    for n in range(NB):
        on = jnp.concatenate(outs[n * RK_HEADS:(n + 1) * RK_HEADS], axis=1)
        o_ref[n] = (on * lng_ref[...] + lnb_ref[...] + bon_ref[n]) * g_ref[n]


def _rwkv_chunk(r, lw, k, v, kk, bb, g, bon, lng, lnb, tt=128, nb=4, precise=False):
    B, S, _ = r.shape
    nb = nb if B % nb == 0 else 1
    blk = pl.BlockSpec((nb, tt, RK_W), lambda b, j: (b, j, 0))
    vec = pl.BlockSpec((1, RK_W), lambda b, j: (0, 0))
    return pl.pallas_call(
        functools.partial(_rwkv_chunk_kernel, precise=precise),
        grid=(B // nb, S // tt),
        in_specs=[blk] * 8 + [vec, vec],
        out_specs=blk,
        out_shape=jax.ShapeDtypeStruct((B, S, RK_W), F32),
        scratch_shapes=[pltpu.VMEM((nb * RK_HEADS, RK_HEAD, RK_HEAD), F32)],
        compiler_params=_cparams(("parallel", "arbitrary")),
        name="rwkv_chunk",
    )(r, lw, k, v, kk, bb, g, bon, lng, lnb)


def _nsa_compress_kernel(x_ref, w1g_ref, w1f_ref, w2_ref, pe_ref, o_ref, ot_ref):
    bias = jnp.dot(jnp.broadcast_to(pe_ref[...], (8, CMP_LEN * NS_HEAD)), w1f_ref[...], precision=HI,
                   preferred_element_type=F32)[0:1]
    x = x_ref[...]
    n = x.shape[0]
    outs = []
    for g in range(NS_KV):
        ya = jnp.dot(x, w1g_ref[g, 0], preferred_element_type=F32)
        yb = jnp.dot(x, w1g_ref[g, 1], preferred_element_type=F32)
        h = ya + pltpu.roll(yb, n - 1, 0) + bias
        outs.append(_bdot(jax.nn.gelu(h), w2_ref[...]))
    out = jnp.concatenate(outs, axis=1)
    o_ref[...] = out.astype(BF16)
    ot_ref[...] = out.T.astype(BF16)


def _nsa_compress(cx, w1, w2b, pe):
    _, B, ncp, width = cx.shape
    hid = w1.shape[-1]
    w1h = w1.reshape(2, 2, CMP_STRIDE, 1, NS_HEAD, hid)
    zero = jnp.zeros_like(w1h)
    w1g = jnp.stack([jnp.concatenate([w1h, zero], axis=3), jnp.concatenate([zero, w1h], axis=3)], axis=1)
    w1g = w1g.reshape(2, NS_KV, 2, width, hid).astype(BF16)
    return pl.pallas_call(
        _nsa_compress_kernel,
        grid=(B, 2),
        in_specs=[pl.BlockSpec((None, None, ncp, width), lambda b, c: (c, b, 0, 0)),
                  pl.BlockSpec((None, NS_KV, 2, width, hid), lambda b, c: (c, 0, 0, 0, 0)),
                  pl.BlockSpec((None, CMP_LEN * NS_HEAD, hid), lambda b, c: (c, 0, 0)),
                  pl.BlockSpec((None, hid, NS_HEAD), lambda b, c: (c, 0, 0)),
                  pl.BlockSpec((None, 1, CMP_LEN * NS_HEAD), lambda b, c: (c, 0, 0))],
        out_specs=[pl.BlockSpec((None, None, ncp, LANES), lambda b, c: (b, c, 0, 0)),
                   pl.BlockSpec((None, None, LANES, ncp), lambda b, c: (b, c, 0, 0))],
        out_shape=[jax.ShapeDtypeStruct((B, 2, ncp, LANES), BF16), jax.ShapeDtypeStruct((B, 2, LANES, ncp), BF16)],
        compiler_params=_cparams(("parallel", "parallel")),
        name="nsa_compress",
    )(cx, w1g, w1, w2b, pe)


def _nsa_attn_kernel(q_ref, kc_ref, vct_ref, ks_ref, vst_ref, kw_ref, vwt_ref, gt_ref, ovt_ref, ext_ref, o_ref,
                     m_s, l_s, acc_s, m_w, l_w, acc_w, s_buf, *, ncp):
    i = pl.program_id(1)
    s0 = i * QT
    heads = range(NS_HEADS)
    hcols = lambda h: slice(h * QT, (h + 1) * QT)
    gcols = lambda h: slice(h // NS_HPG * QT, (h // NS_HPG + 1) * QT)
    iota = lambda shape, d: lax.broadcasted_iota(I32, shape, d)
    qpos = lambda shape: s0 + (iota(shape, 1) & (QT - 1))

    def col_reduce(x, op, final):
        n = x.shape[0]
        while n > 8:
            n //= 2
            x = op(x[:n], x[n:])
        return final(x, axis=0, keepdims=True)

    q8 = q_ref[...].astype(F32)
    zeros = jnp.zeros((QT, NS_HEAD), F32)
    qt = []
    for h in heads:
        qh = q8[:, h * NS_HEAD:(h + 1) * NS_HEAD]
        qt.append(jnp.concatenate([qh, zeros] if h < NS_HPG else [zeros, qh], axis=1).T)
    qt = jnp.concatenate(qt, axis=1).astype(BF16)

    sc = jnp.dot(kc_ref[...], qt, preferred_element_type=F32)
    n_row = iota((ncp, QT), 0)
    cmask = (n_row * CMP_STRIDE + (CMP_LEN - 1) <= s0 + iota((ncp, QT), 1)) & (n_row < ncp - 1)
    cpen = jnp.where(cmask, 0.0, NEG)
    cok = jnp.where(cmask, 1.0, 0.0)
    sc = jnp.concatenate([sc[:, hcols(h)] + cpen for h in heads], axis=1)
    pc = jnp.exp2(sc - col_reduce(sc, jnp.maximum, jnp.max))
    pc = jnp.concatenate([pc[:, hcols(h)] * cok for h in heads], axis=1)
    lc = col_reduce(pc, jnp.add, jnp.sum)
    pc = pc / jnp.where(lc > 0.0, lc, 1.0)
    pcs = jnp.concatenate([functools.reduce(jnp.add, [pc[:, hcols(h)] for h in range(g * NS_HPG, (g + 1) * NS_HPG)])
                           for g in range(NS_KV)], axis=1)
    o_c = jnp.dot(vct_ref[...], pc.astype(BF16), preferred_element_type=F32)

    imp = jnp.dot(ovt_ref[...], pcs, precision=HI, preferred_element_type=F32)
    blk = iota((LANES, NS_KV * QT), 0)
    cur = qpos((LANES, NS_KV * QT)) // SEL_BLOCK
    valid = blk <= cur
    forced = (blk == 0) | (blk == cur) | (blk == cur - 1)
    pri = jnp.where(valid & ~forced, imp, -jnp.inf)
    picked = forced
    blkf = blk.astype(F32)
    for _ in range(SEL_TOPK - 3):
        mx = col_reduce(pri, jnp.maximum, jnp.max)
        hit = blkf == col_reduce(jnp.where(pri == mx, blkf, float(LANES)), jnp.minimum, jnp.min)
        picked = picked | hit
        pri = jnp.where(hit, -jnp.inf, pri)
    selpen = jnp.where(picked & valid, 0.0, NEG).astype(BF16)
    wq = jnp.concatenate([qt, jnp.concatenate([selpen[:, gcols(h)] for h in heads], axis=1)], axis=0)

    def attend(s, vt, pen, m_ref, l_ref, acc_ref):
        if pen is not None:
            s = jnp.concatenate([s[:, hcols(h)] + pen for h in heads], axis=1)
        m_old = m_ref[...]
        m_new = jnp.maximum(m_old, col_reduce(s, jnp.maximum, jnp.max))
        alpha = jnp.exp2(m_old - m_new)
        p = jnp.exp2(s - m_new).astype(BF16)
        m_ref[...] = m_new
        ones = jnp.ones((16, vt.shape[1]), BF16)
        pv = jnp.dot(jnp.concatenate([vt, ones], axis=0), p, preferred_element_type=F32)
        l_ref[...] = alpha * l_ref[...] + pv[LANES:LANES + 1]
        acc_ref[...] = alpha * acc_ref[...] + pv[:LANES]

    def reset(m_ref, l_ref, acc_ref):
        m_ref[...] = jnp.full(m_ref.shape, NEG, F32)
        l_ref[...] = jnp.zeros_like(l_ref)
        acc_ref[...] = jnp.zeros_like(acc_ref)

    tiles = lambda ref, t0, n: jnp.concatenate([ref[t0 + c] for c in range(n)], axis=1)

    reset(m_s, l_s, acc_s)
    diag = i // (KT // QT)

    def sel_scores(kt, slot):
        k0 = pl.multiple_of(kt * KT, KT)
        keys = jnp.concatenate([ks_ref[pl.ds(k0, KT), :], ext_ref[pl.ds(k0, KT), :]], axis=1)
        s_buf[slot] = jnp.dot(keys, wq, preferred_element_type=F32)

    def sel_attend(kt, slot, causal):
        pen = None
        if causal:
            pen = jnp.where(kt * KT + iota((KT, QT), 0) <= s0 + iota((KT, QT), 1), 0.0, NEG)
        attend(s_buf[slot], tiles(vst_ref, kt * (KT // QT), KT // QT), pen, m_s, l_s, acc_s)

    def sel_pair(j, c):
        sel_scores(2 * j + 1, 1)
        sel_attend(2 * j, 0, False)
        sel_scores(2 * j + 2, 0)
        sel_attend(2 * j + 1, 1, False)
        return c

    d0 = pl.multiple_of(s0, QT)
    a0 = pl.multiple_of(jnp.maximum(s0 - WINDOW, 0), QT)
    s_d = jnp.dot(kw_ref[pl.ds(d0, QT), :], qt, preferred_element_type=F32)
    s_a = jnp.dot(kw_ref[pl.ds(a0, WINDOW), :], qt, preferred_element_type=F32)
    pen_d = jnp.where(iota((QT, QT), 0) <= iota((QT, QT), 1), 0.0, NEG)
    kpos = a0 + iota((WINDOW, QT), 0)
    pen_a = jnp.where((kpos < s0) & (kpos > s0 + iota((WINDOW, QT), 1) - WINDOW), 0.0, NEG)

    sel_scores(0, 0)
    reset(m_w, l_w, acc_w)
    attend(s_d, vwt_ref[i], pen_d, m_w, l_w, acc_w)
    attend(s_a, tiles(vwt_ref, a0 // QT, WINDOW // QT), pen_a, m_w, l_w, acc_w)
    lax.fori_loop(0, diag // 2, sel_pair, 0)

    @pl.when(diag % 2 == 1)
    def _():
        sel_scores(diag, 1)
        sel_attend(diag - 1, 0, False)
        sel_attend(diag, 1, True)

    @pl.when(diag % 2 == 0)
    def _():
        sel_attend(diag, 0, True)

    gtt = gt_ref[...].T
    o_s = acc_s[...] / l_s[...]
    o_w = acc_w[...] / l_w[...]
    outs = []
    for h in heads:
        gate = lambda br: gtt[br * NS_HEADS + h:br * NS_HEADS + h + 1]
        o = gate(0) * o_c[:, hcols(h)] + gate(1) * o_s[:, hcols(h)] + gate(2) * o_w[:, hcols(h)]
        outs.append(o[:NS_HEAD] if h < NS_HPG else o[NS_HEAD:])
    pairs = [jnp.concatenate(outs[p:p + 2], axis=0).T for p in range(0, NS_HEADS, 2)]
    o_ref[...] = jnp.concatenate(pairs, axis=1).astype(BF16)


def _nsa_attn(q3, kcv, kcvt, kv3, vt, gt3, ovt):
    B, S, _ = q3.shape
    ncp = kcv.shape[2]
    R = NS_HEADS * QT
    ext = (jnp.arange(S, dtype=I32)[:, None] // SEL_BLOCK == jnp.arange(LANES, dtype=I32)[None, :]).astype(BF16)
    seq = lambda c: pl.BlockSpec((None, S, LANES), lambda b, i, c=c: (b, 0, c))
    seqt = lambda c: pl.BlockSpec((None, None, S // QT, LANES, QT), lambda b, i, c=c: (c, b, 0, 0, 0))
    return pl.pallas_call(
        functools.partial(_nsa_attn_kernel, ncp=ncp),
        grid=(B, S // QT),
        in_specs=[pl.BlockSpec((None, QT, NS_W), lambda b, i: (b, i, 0)),
                  pl.BlockSpec((None, None, ncp, LANES), lambda b, i: (b, 0, 0, 0)),
                  pl.BlockSpec((None, None, LANES, ncp), lambda b, i: (b, 1, 0, 0)),
                  seq(0), seqt(0), seq(1), seqt(1),
                  pl.BlockSpec((None, QT, LANES), lambda b, i: (b, i, 0)),
                  pl.BlockSpec(ovt.shape, lambda b, i: (0, 0)),
                  pl.BlockSpec(ext.shape, lambda b, i: (0, 0))],
        out_specs=pl.BlockSpec((None, QT, NS_W), lambda b, i: (b, i, 0)),
        out_shape=jax.ShapeDtypeStruct((B, S, NS_W), BF16),
        scratch_shapes=([pltpu.VMEM((1, R), F32), pltpu.VMEM((1, R), F32), pltpu.VMEM((LANES, R), F32)] * 2
                        + [pltpu.VMEM((2, KT, R), F32)]),
        compiler_params=_cparams(("parallel", "arbitrary")),
        name="nsa_attn",
    )(q3, kcv, kcvt, kv3, vt, kv3, vt, gt3, ovt, ext)


def _first_argmax(vals):
    m = vals[0]
    for v in vals[1:]:
        m = jnp.maximum(m, v)
    idx = jnp.full(m.shape, len(vals) - 1, I32)
    for j in range(len(vals) - 2, -1, -1):
        idx = jnp.where(vals[j] == m, j, idx)
    return m, idx


def _outproj_router_kernel(*refs, n_in):
    acts, ws = refs[:n_in], refs[n_in:2 * n_in]
    x_ref, lng_ref, lnb_ref, rwt_ref, rb_ref, tri_ref = refs[2 * n_in:2 * n_in + 6]
    y_ref, e_ref, wt_ref, pos_ref, cnt_ref, cnt = refs[2 * n_in + 6:]
    i = pl.program_id(0)

    @pl.when(i == 0)
    def _():
        cnt[...] = jnp.zeros_like(cnt)

    mix = _bdot(acts[0][...], ws[0][...])
    for a, w in zip(acts[1:], ws[1:]):
        mix = mix + _bdot(a[...], w[...])
    y = _layer_norm_rows(ALPHA * x_ref[...] + mix, lng_ref[...], lnb_ref[...])
    y_ref[...] = y

    logit = lax.dot_general(rwt_ref[...], y, (((1,), (1,)), ((), ())), precision=HI, preferred_element_type=F32)
    aff = jax.nn.sigmoid(logit)
    biased = aff + rb_ref[...]
    neg_inf = -jnp.inf
    g_score, g_i1, g_i2 = [], [], []
    for gi in range(N_GROUPS):
        vals = [biased[gi * EXP_PER_GROUP + j:gi * EXP_PER_GROUP + j + 1, :] for j in range(EXP_PER_GROUP)]
        m1, i1 = _first_argmax(vals)
        m2, i2 = _first_argmax([jnp.where(i1 == j, neg_inf, vals[j]) for j in range(EXP_PER_GROUP)])
        g_score.append(m1 + m2)
        g_i1.append(i1)
        g_i2.append(i2)
    _, grp = _first_argmax(g_score)
    loc1, loc2 = g_i1[-1], g_i2[-1]
    for gi in range(N_GROUPS - 2, -1, -1):
        loc1 = jnp.where(grp == gi, g_i1[gi], loc1)
        loc2 = jnp.where(grp == gi, g_i2[gi], loc2)
    e1 = grp * EXP_PER_GROUP + loc1
    e2 = grp * EXP_PER_GROUP + loc2
    eio = lax.broadcasted_iota(I32, aff.shape, 0)
    oh1 = eio == e1
    oh2 = eio == e2
    a1 = jnp.sum(jnp.where(oh1, aff, 0.0), axis=0, keepdims=True)
    a2 = jnp.sum(jnp.where(oh2, aff, 0.0), axis=0, keepdims=True)
    tot = a1 + a2
    e_ref[...] = jnp.concatenate([e1, e2], axis=0)
    wt_ref[...] = jnp.concatenate([a1 / tot, a2 / tot], axis=0)

    ohs = oh1.astype(F32) + oh2.astype(F32)
    before = jnp.dot(ohs.astype(BF16), tri_ref[...], preferred_element_type=F32) + cnt[...]
    p1 = jnp.sum(jnp.where(oh1, before, 0.0), axis=0, keepdims=True)
    p2 = jnp.sum(jnp.where(oh2, before, 0.0), axis=0, keepdims=True)
    pos_ref[...] = jnp.concatenate([p1, p2], axis=0).astype(I32)
    cnt[...] = cnt[...] + jnp.sum(ohs, axis=1, keepdims=True)
    cnt_ref[...] = jnp.broadcast_to(cnt[...], cnt_ref.shape)


def _outproj_router(acts, ws, xres, lng, lnb, rwt, rb, tm=1024):
    T = xres.shape[0]
    n_in = len(acts)
    tri = (lax.broadcasted_iota(I32, (tm, tm), 0) < lax.broadcasted_iota(I32, (tm, tm), 1)).astype(BF16)
    row = lambda i: (i, 0)
    const = lambda a: pl.BlockSpec(a.shape, lambda i: (0,) * a.ndim)
    lane_blk = pl.BlockSpec((TOP_K, tm), lambda i: (0, i))
    return pl.pallas_call(
        functools.partial(_outproj_router_kernel, n_in=n_in),
        grid=(T // tm,),
        in_specs=([pl.BlockSpec((tm, a.shape[1]), row) for a in acts] + [const(w) for w in ws]
                  + [pl.BlockSpec((tm, D_MODEL), row), const(lng), const(lnb), const(rwt), const(rb), const(tri)]),
        out_specs=[pl.BlockSpec((tm, D_MODEL), row), lane_blk, lane_blk, lane_blk,
                   pl.BlockSpec((N_EXPERTS, LANES), lambda i: (0, 0))],
        out_shape=[jax.ShapeDtypeStruct((T, D_MODEL), F32), jax.ShapeDtypeStruct((TOP_K, T), I32),
                   jax.ShapeDtypeStruct((TOP_K, T), F32), jax.ShapeDtypeStruct((TOP_K, T), I32),
                   jax.ShapeDtypeStruct((N_EXPERTS, LANES), F32)],
        scratch_shapes=[pltpu.VMEM((N_EXPERTS, 1), F32)],
        compiler_params=_cparams(("arbitrary",)),
        name="outproj_router",
    )(*acts, *ws, xres, lng, lnb, rwt, rb, tri)


def _dispatch_kernel(zrow_ref, dest_hbm, x_ref, xs_hbm, dsm, zbuf, sem_idx, sem, sem_z):
    i = pl.program_id(0)
    tm = x_ref.shape[0]

    @pl.when(i == 0)
    def _():
        zbuf[...] = jnp.zeros_like(zbuf)
        zero_copy = lambda j: pltpu.make_async_copy(
            zbuf, xs_hbm.at[pl.ds(pl.multiple_of(jnp.maximum(zrow_ref[j], 0), MOE_BLOCK), MOE_BLOCK)], sem_z)
        for j in range(zrow_ref.shape[0]):
            @pl.when(zrow_ref[j] >= 0)
            def _():
                zero_copy(j).start()
        for j in range(zrow_ref.shape[0]):
            @pl.when(zrow_ref[j] >= 0)
            def _():
                zero_copy(j).wait()

    idx_copy = pltpu.make_async_copy(dest_hbm.at[i], dsm, sem_idx)
    idx_copy.start()
    idx_copy.wait()

    def row_copy(r, k):
        return pltpu.make_async_copy(x_ref.at[pl.ds(r, 1)], xs_hbm.at[pl.ds(dsm[k * tm + r], 1)], sem)

    def start(r, c):
        for k in range(TOP_K):
            row_copy(r, k).start()
        return c

    lax.fori_loop(0, tm, start, 0, unroll=ROW_DMA_UNROLL)
    for k in range(TOP_K):
        pltpu.make_async_copy(x_ref, xs_hbm.at[pl.ds(0, tm)], sem).wait()


def _dispatch(zero_rows, dest_tiles, x, rows, tm):
    T = x.shape[0]
    return pl.pallas_call(
        _dispatch_kernel,
        grid_spec=pltpu.PrefetchScalarGridSpec(
            num_scalar_prefetch=1,
            grid=(T // tm,),
            in_specs=[pl.BlockSpec(memory_space=pl.ANY), pl.BlockSpec((tm, D_MODEL), lambda i, z: (i, 0))],
            out_specs=pl.BlockSpec(memory_space=pl.ANY),
            scratch_shapes=[pltpu.SMEM((TOP_K * tm,), I32), pltpu.VMEM((MOE_BLOCK, D_MODEL), F32),
                            pltpu.SemaphoreType.DMA(()), pltpu.SemaphoreType.DMA(()), pltpu.SemaphoreType.DMA(())]),
        out_shape=jax.ShapeDtypeStruct((rows, D_MODEL), F32),
        compiler_params=_cparams(("arbitrary",)),
        name="moe_dispatch",
    )(zero_rows, dest_tiles, x)


def _ffn_kernel(be_ref, nu_ref, xs_ref, wg_ref, wu_ref, wd_ref, y_ref, h_ref):
    del be_ref
    i = pl.program_id(0)
    half = D_EXPERT // 2

    @pl.when(i < nu_ref[0])
    def _():
        xb = xs_ref[...].astype(BF16)
        for c in range(2):
            cs = slice(c * half, (c + 1) * half)
            gt = jnp.dot(xb, wg_ref[:, cs], preferred_element_type=F32)
            up = jnp.dot(xb, wu_ref[:, cs], preferred_element_type=F32)
            h_ref[:, cs] = (jax.nn.silu(gt) * up).astype(BF16)
        y_ref[...] = jnp.dot(h_ref[...], wd_ref[...], preferred_element_type=F32)

    @pl.when(i >= nu_ref[0])
    def _():
        y_ref[...] = jnp.zeros_like(y_ref)


def _ffn(blk_exp, n_used, xs, wg, wu, wd):
    rows = xs.shape[0]
    wspec = lambda: pl.BlockSpec((None, D_MODEL, D_EXPERT), lambda i, be, nu: (be[i], 0, 0))
    return pl.pallas_call(
        _ffn_kernel,
        grid_spec=pltpu.PrefetchScalarGridSpec(
            num_scalar_prefetch=2,
            grid=(rows // MOE_BLOCK,),
            in_specs=[pl.BlockSpec((MOE_BLOCK, D_MODEL), lambda i, be, nu: (i, 0)), wspec(), wspec(),
                      pl.BlockSpec((None, D_EXPERT, D_MODEL), lambda i, be, nu: (be[i], 0, 0))],
            out_specs=pl.BlockSpec((MOE_BLOCK, D_MODEL), lambda i, be, nu: (i, 0)),
            scratch_shapes=[pltpu.VMEM((MOE_BLOCK, D_EXPERT), BF16)]),
        out_shape=jax.ShapeDtypeStruct((rows, D_MODEL), F32),
        compiler_params=_cparams(("arbitrary",)),
        name="moe_ffn",
    )(blk_exp, n_used, xs, wg, wu, wd)


def _combine_kernel(dest_hbm, y_hbm, x_ref, wt_ref, lng_ref, lnb_ref, o_ref, dsm, buf, sem_idx, sem):
    i = pl.program_id(0)
    tm = x_ref.shape[0]
    idx_copy = pltpu.make_async_copy(dest_hbm.at[i], dsm, sem_idx)
    idx_copy.start()
    idx_copy.wait()

    def row_copy(r, k):
        return pltpu.make_async_copy(y_hbm.at[pl.ds(dsm[k * tm + r], 1)], buf.at[k, pl.ds(r, 1)], sem)

    def start(r, c):
        for k in range(TOP_K):
            row_copy(r, k).start()
        return c

    lax.fori_loop(0, tm, start, 0, unroll=ROW_DMA_UNROLL)
    for k in range(TOP_K):
        pltpu.make_async_copy(y_hbm.at[pl.ds(0, tm)], buf.at[k], sem).wait()
    wt = wt_ref[...]
    z = ALPHA * x_ref[...] + wt[:, 0:1] * buf[0] + wt[:, 1:2] * buf[1]
    o_ref[...] = _layer_norm_rows(z, lng_ref[...], lnb_ref[...])


def _combine(dest_tiles, y, x, wt, lng, lnb, tm):
    T = x.shape[0]
    row = lambda i: (i, 0)
    vec = pl.BlockSpec((1, D_MODEL), lambda i: (0, 0))
    return pl.pallas_call(
        _combine_kernel,
        grid=(T // tm,),
        in_specs=[pl.BlockSpec(memory_space=pl.ANY), pl.BlockSpec(memory_space=pl.ANY),
                  pl.BlockSpec((tm, D_MODEL), row), pl.BlockSpec((tm, TOP_K), row), vec, vec],
        out_specs=pl.BlockSpec((tm, D_MODEL), row),
        out_shape=jax.ShapeDtypeStruct((T, D_MODEL), F32),
        scratch_shapes=[pltpu.SMEM((TOP_K * tm,), I32), pltpu.VMEM((TOP_K, tm, D_MODEL), F32),
                        pltpu.SemaphoreType.DMA(()), pltpu.SemaphoreType.DMA(())],
        compiler_params=_cparams(("arbitrary",)),
        name="moe_combine",
    )(dest_tiles, y, x, wt, lng, lnb)


def _moe(x1, e, wt, pos, cnt, wg, wu, wd, lng, lnb, tm=1024):
    T = x1.shape[0]
    n_blocks = -(-(T * TOP_K) // MOE_BLOCK) + N_EXPERTS
    rows = n_blocks * MOE_BLOCK
    counts = cnt[:, 0].astype(I32)
    padded = (counts + MOE_BLOCK - 1) // MOE_BLOCK * MOE_BLOCK
    pad_end = jnp.cumsum(padded)
    pad_start = pad_end - padded
    dest = pos
    for j in range(N_EXPERTS):
        dest = dest + jnp.where(e == j, pad_start[j], 0)
    blk_start = jnp.arange(n_blocks, dtype=I32) * MOE_BLOCK
    blk_exp = jnp.minimum(jnp.sum((pad_end[None, :] <= blk_start[:, None]).astype(I32), axis=1), N_EXPERTS - 1)
    n_used = (pad_end[-1:] // MOE_BLOCK).astype(I32)
    dest_tiles = dest.reshape(TOP_K, T // tm, tm).transpose(1, 0, 2).reshape(T // tm, TOP_K * tm)
    tail = jnp.where(padded > 0, pad_end - MOE_BLOCK, -1)
    trailing = jnp.arange(n_blocks - N_EXPERTS, n_blocks, dtype=I32)
    trailing = jnp.where(trailing >= n_used[0], trailing * MOE_BLOCK, -1)
    xs = _dispatch(jnp.concatenate([tail, trailing]).astype(I32), dest_tiles, x1, rows, tm)
    y = _ffn(blk_exp, n_used, xs, wg, wu, wd)
    return _combine(dest_tiles, y, x1, wt.T, lng, lnb, tm)


def _proj_rt_kernel(x_ref, w_ref, c_ref, s_ref, q_ref, k_ref, v_ref, g_ref):
    xb = x_ref[...].astype(BF16)
    cs, sn = c_ref[...], s_ref[...]
    rope = lambda y: y * cs + pltpu.roll(y, RT_QK // 2, 1) * sn
    pair = 2 * RT_QK
    for c in range(RT_QKW // pair):
        yq = jnp.dot(xb, w_ref[:, c * pair:(c + 1) * pair], preferred_element_type=F32)
        yk = jnp.dot(xb, w_ref[:, RT_QKW + c * pair:RT_QKW + (c + 1) * pair], preferred_element_type=F32)
        for h in range(2):
            cols = slice(c * pair + h * RT_QK, c * pair + (h + 1) * RT_QK)
            q_ref[:, cols] = rope(yq[:, h * RT_QK:(h + 1) * RT_QK]).astype(BF16)
            k_ref[:, cols] = (rope(yk[:, h * RT_QK:(h + 1) * RT_QK]) * RT_QK ** -0.5).astype(BF16)
    step = 1024
    for c in range(RT_VW // step):
        cols = slice(c * step, (c + 1) * step)
        v_ref[:, cols] = jnp.dot(xb, w_ref[:, 2 * RT_QKW + c * step:2 * RT_QKW + (c + 1) * step],
                                 preferred_element_type=F32).astype(BF16)
        base = 2 * RT_QKW + RT_VW
        g_ref[:, cols] = jax.nn.silu(jnp.dot(xb, w_ref[:, base + c * step:base + (c + 1) * step],
                                             preferred_element_type=F32))


def _proj_rt(xt, w, cs, sn, S, tm=512):
    T = xt.shape[0]
    nseq = S // tm
    row = lambda i: (i, 0)
    tab = lambda i: (i % nseq, 0)
    return pl.pallas_call(
        _proj_rt_kernel,
        grid=(T // tm,),
        in_specs=[pl.BlockSpec((tm, D_MODEL), row), pl.BlockSpec(w.shape, lambda i: (0, 0)),
                  pl.BlockSpec((tm, RT_QK), tab), pl.BlockSpec((tm, RT_QK), tab)],
        out_specs=[pl.BlockSpec((tm, RT_QKW), row), pl.BlockSpec((tm, RT_QKW), row),
                   pl.BlockSpec((tm, RT_VW), row), pl.BlockSpec((tm, RT_VW), row)],
        out_shape=[jax.ShapeDtypeStruct((T, RT_QKW), BF16), jax.ShapeDtypeStruct((T, RT_QKW), BF16),
                   jax.ShapeDtypeStruct((T, RT_VW), BF16), jax.ShapeDtypeStruct((T, RT_VW), F32)],
        compiler_params=_cparams(("parallel",)),
        name="proj_rt",
    )(xt, w, cs, sn)


def _retention_kernel(q_ref, k_ref, v_ref, sg_ref, dec_ref, qd_ref, kd_ref, cd_ref, gng_ref, gnb_ref, o_ref, state):
    j = pl.program_id(1)

    @pl.when(j == 0)
    def _():
        state[...] = jnp.zeros_like(state)

    NB = q_ref.shape[0]
    H = range(NB * RT_HEADS)
    nh = lambda i: (i // RT_HEADS, i % RT_HEADS)
    qk_cols = lambda i: slice(nh(i)[1] * RT_QK, (nh(i)[1] + 1) * RT_QK)
    v_cols = lambda i: slice(nh(i)[1] * RT_V, (nh(i)[1] + 1) * RT_V)
    q = [q_ref[nh(i)[0], :, qk_cols(i)] for i in H]
    k = [k_ref[nh(i)[0], :, qk_cols(i)] for i in H]
    v = [v_ref[nh(i)[0], :, v_cols(i)] for i in H]
    r_old = [state[i] for i in H]
    att = [(_dot_nt(q[i], k[i]) * dec_ref[nh(i)[1]]).astype(BF16) for i in H]
    cross = [jnp.dot(q[i], r_old[i].astype(BF16), preferred_element_type=F32) * qd_ref[nh(i)[1]] for i in H]
    inner = [jnp.dot(att[i], v[i], preferred_element_type=F32) for i in H]
    kdec = [(k[i].astype(F32) * kd_ref[nh(i)[1]]).astype(BF16) for i in H]
    for i in H:
        upd = lax.dot_general(kdec[i], v[i], (((0,), (0,)), ((), ())), preferred_element_type=F32)
        state[i] = r_old[i] * cd_ref[nh(i)[1]][:, 0:1] + upd
    for i in H:
        o = inner[i] + cross[i]
        mu = jnp.mean(o, axis=1, keepdims=True)
        oc = o - mu
        var = jnp.mean(oc * oc, axis=1, keepdims=True)
        on = oc * lax.rsqrt(var + RT_GN_EPS) * gng_ref[:, v_cols(i)] + gnb_ref[:, v_cols(i)]
        o_ref[nh(i)[0], :, v_cols(i)] = (sg_ref[nh(i)[0], :, v_cols(i)] * on).astype(BF16)


def _retention(q3, k3, v3, sg3, dec, qd, kd, cd, gng, gnb, nb=2):
    B, S, _ = q3.shape
    C = RT_CHUNK
    nb = nb if B % nb == 0 else 1
    qk = pl.BlockSpec((nb, C, RT_QKW), lambda b, j: (b, j, 0))
    vv = pl.BlockSpec((nb, C, RT_VW), lambda b, j: (b, j, 0))
    const = lambda a: pl.BlockSpec(a.shape, lambda b, j: (0,) * a.ndim)
    return pl.pallas_call(
        _retention_kernel,
        grid=(B // nb, S // C),
        in_specs=[qk, qk, vv, vv, const(dec), const(qd), const(kd), const(cd), const(gng), const(gnb)],
        out_specs=vv,
        out_shape=jax.ShapeDtypeStruct((B, S, RT_VW), BF16),
        scratch_shapes=[pltpu.VMEM((nb * RT_HEADS, RT_QK, RT_V), F32)],
        compiler_params=_cparams(("parallel", "arbitrary")),
        name="retention",
    )(q3, k3, v3, sg3, dec, qd, kd, cd, gng, gnb)


def _nsa_rope_tables(S):
    half = ROPE_DIM // 2
    inv = ROPE_THETA ** (-jnp.arange(half, dtype=F32) / half)
    ang = jnp.arange(S, dtype=F32)[:, None] * inv[None, :]
    cos, sin = jnp.cos(ang), jnp.sin(ang)
    zeros = lambda n: jnp.zeros((S, n), F32)
    cn = jnp.concatenate([cos, cos, jnp.ones((S, NS_HEAD - ROPE_DIM), F32)], axis=1)
    s1 = jnp.concatenate([-sin, zeros(NS_HEAD - half)], axis=1)
    s2 = jnp.concatenate([zeros(half), sin, zeros(NS_HEAD - ROPE_DIM)], axis=1)
    two = lambda a: jnp.concatenate([a, a], axis=1)
    return two(cn), two(s1), two(s2)


def _rt_rope_tables(S):
    inv = RT_THETA ** (-jnp.linspace(0.0, 1.0, RT_QK // 2, dtype=F32))
    ang = jnp.arange(S, dtype=F32)[:, None] * inv[None, :]
    cos, sin = jnp.cos(ang), jnp.sin(ang)
    return jnp.concatenate([cos, cos], axis=1), jnp.concatenate([-sin, sin], axis=1)


def _rt_decay_tables():
    log_g = jnp.log(1.0 - 2.0 ** (-5.0 - jnp.arange(RT_HEADS, dtype=F32)))
    idx = jnp.arange(RT_CHUNK, dtype=F32)
    diff = idx[:, None] - idx[None, :]
    dec = jnp.where(diff >= 0, jnp.exp(jnp.maximum(diff, 0.0) * log_g[:, None, None]), 0.0)
    qd = jnp.exp((idx + 1.0) * log_g[:, None])[..., None]
    kd = jnp.exp((RT_CHUNK - 1.0 - idx) * log_g[:, None])[..., None]
    cd = jnp.broadcast_to(jnp.exp(RT_CHUNK * log_g)[:, None, None], (RT_HEADS, 1, LANES))
    return dec, qd, kd, cd


def _overlap_table(S, ncp):
    n_cmp = (S - CMP_LEN) // CMP_STRIDE + 1
    n_sel = S // SEL_BLOCK
    cs = jnp.arange(ncp) * CMP_STRIDE
    ss = jnp.arange(LANES) * SEL_BLOCK
    ov = jnp.clip(jnp.minimum(cs[:, None] + CMP_LEN, ss[None, :] + SEL_BLOCK)
                  - jnp.maximum(cs[:, None], ss[None, :]), 0, None).astype(F32) / CMP_LEN
    keep = (jnp.arange(ncp)[:, None] < n_cmp) & (jnp.arange(LANES)[None, :] < n_sel)
    return jnp.where(keep, ov, 0.0)


def kernel(x, ab_w_in, ab_w_out, rk_mu, rk_w0, rk_w1, rk_w2, rk_a0, rk_a1, rk_a2, rk_g1, rk_g2, rk_kk, rk_ka, rk_rk,
           rk_ln, ns_pe, ns_c_w1, ns_c_w2, rt_w_in, rt_w_out, rt_gn, router_w, router_b, moe_w_gate, moe_w_up,
           moe_w_down, ln):
    B, S, D = x.shape
    T = B * S
    assert D == D_MODEL and S % 256 == 0 and S // SEL_BLOCK <= LANES and S >= WINDOW
    xt = x.reshape(T, D)
    rwt = router_w.T
    rb = router_b.reshape(N_EXPERTS, 1)
    vec = lambda a: a.reshape(1, -1)

    w_in = ab_w_in[0]
    n_gate = 3 * NS_HEADS
    w_cat = jnp.concatenate([w_in[:, :-n_gate], jnp.pad(w_in[:, -n_gate:], ((0, 0), (0, LANES - n_gate)))],
                            axis=1).astype(BF16)
    cn, s1, s2 = _nsa_rope_tables(S)
    prk, q, kk2, gt, vt, cx = _proj_ab(xt, w_cat, cn, s1, s2, S)

    ones = (jnp.arange(RK_W)[:, None] // RK_HEAD == jnp.arange(RK_W)[None, :] // RK_HEAD).astype(F32)
    b16 = lambda a: a.astype(BF16)
    r, lw, km, v, kk, bb, g, bon = _rwkv_prep(
        prk.reshape(B, S, 4 * RK_W), rk_mu[0], vec(rk_w0[0]), b16(rk_w1[0]), b16(rk_w2[0]), vec(rk_a0[0]),
        b16(rk_a1[0]), b16(rk_a2[0]), b16(rk_g1[0]), b16(rk_g2[0]), vec(rk_kk[0]), vec(rk_ka[0]), vec(rk_rk[0]), ones)
    o_a = _rwkv_chunk(r, lw, km, v, kk, bb, g, bon, rk_ln[0, 0:1], rk_ln[0, 1:2])

    ncp = S // CMP_STRIDE
    kcv, kcvt = _nsa_compress(cx.reshape(2, B, ncp, CMP_STRIDE * LANES), ns_c_w1[0], b16(ns_c_w2[0]),
                              ns_pe[0].reshape(2, 1, CMP_LEN * NS_HEAD))
    o_b = _nsa_attn(q.reshape(B, S, NS_W), kcv, kcvt, kk2.reshape(B, S, 2 * LANES),
                    vt.reshape(2, B, S // QT, LANES, QT), gt.reshape(B, S, LANES), _overlap_table(S, ncp).T)

    w_out = b16(ab_w_out[0])
    x1, e, wt, pos, cnt = _outproj_router([o_a.reshape(T, RK_W), o_b.reshape(T, NS_W)], [w_out[:RK_W], w_out[RK_W:]],
                                          xt, ln[0, 0, 0:1], ln[0, 0, 1:2], rwt, rb)
    x2 = _moe(x1, e, wt, pos, cnt, b16(moe_w_gate[0]), b16(moe_w_up[0]), b16(moe_w_down[0]),
              ln[0, 1, 0:1], ln[0, 1, 1:2])

    cs, sn = _rt_rope_tables(S)
    qr, kr, vr, sg = _proj_rt(x2, b16(rt_w_in[0]), cs, sn, S)
    dec, qd, kd, cd = _rt_decay_tables()
    ret = _retention(qr.reshape(B, S, RT_QKW), kr.reshape(B, S, RT_QKW), vr.reshape(B, S, RT_VW),
                     sg.reshape(B, S, RT_VW), dec, qd, kd, cd, rt_gn[0, 0:1], rt_gn[0, 1:2])
    x3, e, wt, pos, cnt = _outproj_router([ret.reshape(T, RT_VW)], [b16(rt_w_out[0])], x2,
                                          ln[1, 0, 0:1], ln[1, 0, 1:2], rwt, rb)
    x4 = _moe(x3, e, wt, pos, cnt, b16(moe_w_gate[1]), b16(moe_w_up[1]), b16(moe_w_down[1]),
              ln[1, 1, 0:1], ln[1, 1, 1:2])
    return x4.reshape(B, S, D)
```

```python
import functools
import math

import jax
import jax.numpy as jnp
from jax import lax
from jax.experimental import pallas as pl
from jax.experimental.pallas import tpu as pltpu

F32 = jnp.float32
BF16 = jnp.bfloat16
I32 = jnp.int32
HI = lax.Precision.HIGHEST

LANES = 128
VMEM_LIMIT = 56 * 1024 * 1024

D_MODEL = 1024
RK_HEADS, RK_HEAD = 8, 64
RK_W = RK_HEADS * RK_HEAD
RK_DECAY_SCALE = 0.606531
RK_LN_EPS = 64e-5
NS_HEADS, NS_KV, NS_HPG, NS_HEAD = 8, 2, 4, 64
NS_W = NS_HEADS * NS_HEAD
CMP_LEN, CMP_STRIDE, SEL_BLOCK, SEL_TOPK, WINDOW = 32, 16, 64, 16, 512
ROPE_THETA = 500000.0
ROPE_DIM = NS_HEAD // 4
Q_SCALE = NS_HEAD ** -0.5 * math.log2(math.e)
QT = 128
KT = 512
RT_HEADS, RT_QK, RT_V = 8, 128, 256
RT_QKW, RT_VW = RT_HEADS * RT_QK, RT_HEADS * RT_V
RT_CHUNK = 128
RT_THETA = 10000.0
RT_GN_EPS = 1e-5
N_EXPERTS, N_GROUPS, EXP_PER_GROUP, TOP_K = 16, 4, 4, 2
D_EXPERT = 1024
MOE_BLOCK = 512
ROW_DMA_UNROLL = 8
DEPTH = 2
ALPHA = (2.0 * DEPTH) ** 0.25
LN_EPS = 1e-5
NEG = -1e30


def _cparams(sem):
    return pltpu.CompilerParams(dimension_semantics=sem, vmem_limit_bytes=VMEM_LIMIT)


def _bdot(a, w):
    return jnp.dot(a.astype(BF16), w, preferred_element_type=F32)


def _dot_nt(a, b):
    return lax.dot_general(a, b, (((1,), (1,)), ((), ())), preferred_element_type=F32)


def _layer_norm_rows(z, g, b):
    mu = jnp.mean(z, axis=1, keepdims=True)
    zc = z - mu
    var = jnp.mean(zc * zc, axis=1, keepdims=True)
    return zc * lax.rsqrt(var + LN_EPS) * g + b


def _proj_ab_kernel(x_ref, w_ref, cn_ref, s1_ref, s2_ref, pick_ref, prk_ref, q_ref, kk_ref, gt_ref, vt_ref, cx_ref):
    xb = x_ref[...].astype(BF16)
    for c in range(4):
        prk_ref[:, c * RK_W:(c + 1) * RK_W] = jnp.dot(xb, w_ref[:, c * RK_W:(c + 1) * RK_W],
                                                      preferred_element_type=F32)
    cn, s1, s2 = cn_ref[...], s1_ref[...], s2_ref[...]

    def rope(y):
        return y * cn + pltpu.roll(y, LANES - ROPE_DIM // 2, 1) * s1 + pltpu.roll(y, ROPE_DIM // 2, 1) * s2

    base = 4 * RK_W
    yq = jnp.dot(xb, w_ref[:, base:base + NS_W], preferred_element_type=F32)
    for c in range(NS_W // LANES):
        q_ref[:, c * LANES:(c + 1) * LANES] = (rope(yq[:, c * LANES:(c + 1) * LANES]) * Q_SCALE).astype(BF16)
    base += NS_W
    ykv = jnp.dot(xb, w_ref[:, base:base + 6 * LANES], preferred_element_type=F32)
    for c in range(6):
        y = ykv[:, c * LANES:(c + 1) * LANES]
        if c % 2 == 0:
            y = rope(y)
        if c < 2:
            yb = y.astype(BF16)
            for l in range(CMP_STRIDE):
                cx_ref[c, :, l * LANES:(l + 1) * LANES] = jnp.dot(pick_ref[l], yb,
                                                                  preferred_element_type=F32).astype(BF16)
        elif c % 2 == 0:
            kk_ref[:, (c // 2 - 1) * LANES:(c // 2) * LANES] = y.astype(BF16)
        else:
            for t in range(y.shape[0] // QT):
                vt_ref[c // 2 - 1, t] = y[t * QT:(t + 1) * QT].T.astype(BF16)
    base += 6 * LANES
    gt_ref[...] = jax.nn.sigmoid(jnp.dot(xb, w_ref[:, base:base + LANES], preferred_element_type=F32))


def _proj_ab(xt, w, cn, s1, s2, S, tm=512):
    T = xt.shape[0]
    ncols = w.shape[1]
    nseq = S // tm
    row = lambda i: (i, 0)
    tab = lambda i: (i % nseq, 0)
    ng = tm // CMP_STRIDE
    pick = (jnp.arange(tm, dtype=I32)[None, None, :]
            == jnp.arange(ng, dtype=I32)[None, :, None] * CMP_STRIDE + jnp.arange(CMP_STRIDE, dtype=I32)[:, None, None])
    pick = pick.astype(BF16)
    return pl.pallas_call(
        _proj_ab_kernel,
        grid=(T // tm,),
        in_specs=[pl.BlockSpec((tm, D_MODEL), row),
                  pl.BlockSpec((D_MODEL, ncols), lambda i: (0, 0)),
                  pl.BlockSpec((tm, LANES), tab), pl.BlockSpec((tm, LANES), tab), pl.BlockSpec((tm, LANES), tab),
                  pl.BlockSpec(pick.shape, lambda i: (0, 0, 0))],
        out_specs=[pl.BlockSpec((tm, 4 * RK_W), row), pl.BlockSpec((tm, NS_W), row),
                   pl.BlockSpec((tm, 2 * LANES), row), pl.BlockSpec((tm, LANES), row),
                   pl.BlockSpec((2, tm // QT, LANES, QT), lambda i: (0, i, 0, 0)),
                   pl.BlockSpec((2, ng, CMP_STRIDE * LANES), lambda i: (0, i, 0))],
        out_shape=[jax.ShapeDtypeStruct((T, 4 * RK_W), F32), jax.ShapeDtypeStruct((T, NS_W), BF16),
                   jax.ShapeDtypeStruct((T, 2 * LANES), BF16), jax.ShapeDtypeStruct((T, LANES), F32),
                   jax.ShapeDtypeStruct((2, T // QT, LANES, QT), BF16),
                   jax.ShapeDtypeStruct((2, T // CMP_STRIDE, CMP_STRIDE * LANES), BF16)],
        compiler_params=_cparams(("parallel",)),
        name="proj_ab",
    )(xt, w, cn, s1, s2, pick)


def _rwkv_prep_kernel(p_ref, mu_ref, w0_ref, w1_ref, w2_ref, a0_ref, a1_ref, a2_ref, g1_ref, g2_ref,
                      kk_ref, ka_ref, rk_ref, ones_ref,
                      r_o, w_o, k_o, v_o, kk_o, b_o, g_o, bon_o, carry):
    j = pl.program_id(1)
    p = p_ref[...]
    tm = p.shape[0]

    @pl.when(j == 0)
    def _():
        carry[...] = jnp.zeros_like(carry)

    rowi = lax.broadcasted_iota(I32, p.shape, 0)
    prev = jnp.where(rowi == 0, carry[...], pltpu.roll(p, 1, 0))
    carry[...] = p[tm - 1:tm, :]
    dp = prev - p
    sl = lambda a, c: a[:, c * RK_W:(c + 1) * RK_W]
    mu = mu_ref[...]
    r = sl(p, 0) + sl(dp, 0) * mu[0:1]
    k = sl(p, 1) + sl(dp, 1) * mu[1:2]
    v = sl(p, 2) + sl(dp, 2) * mu[2:3]
    xw = sl(p, 3) + sl(dp, 3) * mu[3:4]
    xa = sl(p, 3) + sl(dp, 3) * mu[4:5]
    xg = sl(p, 3) + sl(dp, 3) * mu[5:6]
    lw = -RK_DECAY_SCALE * jax.nn.sigmoid(w0_ref[...] + _bdot(jnp.tanh(_bdot(xw, w1_ref[...])), w2_ref[...]))
    a = jax.nn.sigmoid(a0_ref[...] + _bdot(_bdot(xa, a1_ref[...]), a2_ref[...]))
    g = _bdot(jax.nn.sigmoid(_bdot(xg, g1_ref[...])), g2_ref[...])
    ones = ones_ref[...]

    def head_sum(t):
        hi = t.astype(BF16)
        lo = (t - hi.astype(F32)).astype(BF16)
        return (jnp.dot(hi, ones, preferred_element_type=F32) + jnp.dot(lo, ones, preferred_element_type=F32))

    kk = k * kk_ref[...]
    kk = kk / jnp.maximum(jnp.sqrt(head_sum(kk * kk)), 1e-12)
    km = k * (1.0 + (a - 1.0) * ka_ref[...])
    bon = head_sum(r * km * rk_ref[...]) * v
    w_o[...] = lw
    for ref, val in ((r_o, r), (k_o, km), (v_o, v), (kk_o, kk), (b_o, kk * a), (g_o, g), (bon_o, bon)):
        ref[...] = val.astype(ref.dtype)


def _rwkv_prep(prk3, mu, w0, w1, w2, a0, a1, a2, g1, g2, k_k, k_a, r_k, ones, tm=512):
    B, S, _ = prk3.shape
    full = lambda a: pl.BlockSpec(a.shape, lambda b, j: (0,) * a.ndim)
    params = [mu, w0, w1, w2, a0, a1, a2, g1, g2, k_k, k_a, r_k, ones]
    ospec = pl.BlockSpec((None, tm, RK_W), lambda b, j: (b, j, 0))
    return pl.pallas_call(
        _rwkv_prep_kernel,
        grid=(B, S // tm),
        in_specs=[pl.BlockSpec((None, tm, 4 * RK_W), lambda b, j: (b, j, 0))] + [full(a) for a in params],
        out_specs=[ospec] * 8,
        out_shape=[jax.ShapeDtypeStruct((B, S, RK_W), F32 if i == 1 else BF16) for i in range(8)],
        scratch_shapes=[pltpu.VMEM((1, 4 * RK_W), F32)],
        compiler_params=_cparams(("parallel", "arbitrary")),
        name="rwkv_prep",
    )(prk3, *params)


RK_CHUNK = 16


def _pdot(a, b, dims, precise):
    if precise:
        return lax.dot_general(a, b, (dims, ((), ())), precision=HI, preferred_element_type=F32)
    return lax.dot_general(a.astype(BF16), b.astype(BF16), (dims, ((), ())), preferred_element_type=F32)


def _rwkv_chunk_kernel(r_ref, lw_ref, k_ref, v_ref, kk_ref, b_ref, g_ref, bon_ref, lng_ref, lnb_ref, o_ref, ht,
                       *, precise):
    j = pl.program_id(1)

    @pl.when(j == 0)
    def _():
        ht[...] = jnp.zeros_like(ht)

    NB, TT = r_ref.shape[0], r_ref.shape[1]
    C, N = RK_CHUNK, RK_HEAD
    mm = lambda a, b: _pdot(a, b, ((1,), (0,)), precise)
    mm_nt = lambda a, b: _pdot(a, b, ((1,), (1,)), precise)
    mm_tn = lambda a, b: _pdot(a, b, ((0,), (0,)), precise)
    wide = lambda ref: jnp.concatenate([ref[n] for n in range(NB)], axis=1)

    lw = wide(lw_ref)
    rowc = lax.broadcasted_iota(I32, lw.shape, 0) & (C - 1)
    linc, lrev = lw, lw
    sh = 1
    while sh < C:
        linc = linc + jnp.where(rowc >= sh, pltpu.roll(linc, sh, 0), 0.0)
        lrev = lrev + jnp.where(rowc < C - sh, pltpu.roll(lrev, TT - sh, 0), 0.0)
        sh *= 2
    lrev = lrev - lw
    r, k, v, kk, b = wide(r_ref), wide(k_ref), wide(v_ref), wide(kk_ref), wide(b_ref)
    e_in, e_inv, e_rev = jnp.exp(linc), jnp.exp(-linc), jnp.exp(lrev)
    kkd = kk * jnp.exp(linc - lw)
    rd = r * e_in
    binv, kinv = b * e_inv, k * e_inv
    bd, kd = b * e_rev, k * e_rev
    gam = jnp.exp(linc + lrev)

    ti = lax.broadcasted_iota(I32, (TT, TT), 0)
    tj = lax.broadcasted_iota(I32, (TT, TT), 1)
    same = (ti // C) == (tj // C)
    strict = same & (tj < ti)
    incl = same & (tj <= ti)

    H = range(NB * RK_HEADS)
    lo = lambda a: a if precise else a.astype(BF16)
    hs = lambda a: [a[:, h * N:(h + 1) * N] for h in H]
    rows2 = lambda a, b: jnp.concatenate([a, b], axis=0)
    kkd_h, rd_h, v_h = hs(lo(kkd)), hs(lo(rd)), hs(lo(v))
    binv_h, kinv_h, bd_h, kd_h = hs(lo(binv)), hs(lo(kinv)), hs(lo(bd)), hs(lo(kd))
    gam_h = hs(gam)
    gm = [mm_nt(rows2(kkd_h[h], rd_h[h]), rows2(binv_h[h], kinv_h[h])) for h in H]
    a_b = [lo(jnp.where(strict, gm[h][:TT, :TT], 0.0)) for h in H]
    b_rb = [lo(jnp.where(incl, gm[h][TT:, :TT], 0.0)) for h in H]
    akb = [lo(rows2(jnp.where(strict, gm[h][:TT, TT:], 0.0), jnp.where(incl, gm[h][TT:, TT:], 0.0))) for h in H]
    av = [mm(akb[h], v_h[h]) for h in H]
    x = [jnp.concatenate([kkd_h[h].astype(F32), av[h][:TT]], axis=1) for h in H]
    a2 = [lo(mm(a_b[h], a_b[h])) for h in H]
    a4 = [lo(mm(a2[h], a2[h])) for h in H]
    a8 = [lo(mm(a4[h], a4[h])) for h in H]
    x = [x[h] + mm(a8[h], lo(x[h])) for h in H]
    x = [x[h] + mm(a4[h], lo(x[h])) for h in H]
    x = [x[h] + mm(a2[h], lo(x[h])) for h in H]
    x = [x[h] - mm(a_b[h], lo(x[h])) for h in H]
    wt = [lo(x[h][:, :N]) for h in H]
    h_t = [ht[h] for h in H]
    us = [[] for _ in H]
    rhs = [[] for _ in H]
    for c in range(TT // C):
        rs = slice(c * C, (c + 1) * C)
        xh = [mm_nt(rows2(wt[h][rs], rd_h[h][rs]), lo(h_t[h])) for h in H]
        for h in H:
            u_c = -(xh[h][:C] + x[h][rs, N:])
            us[h].append(u_c)
            rhs[h].append(xh[h][C:])
        upd = [mm_tn(rows2(lo(us[h][c]), v_h[h][rs]), rows2(bd_h[h][rs], kd_h[h][rs])) for h in H]
        h_t = [h_t[h] * gam_h[h][c * C:c * C + 1] + upd[h] for h in H]
    outs = []
    for h in H:
        ht[h] = h_t[h]
        o = jnp.concatenate(rhs[h], axis=0) + mm(b_rb[h], lo(jnp.concatenate(us[h], axis=0))) + av[h][TT:]
        mu = jnp.mean(o, axis=1, keepdims=True)
        oc = o - mu
        var = jnp.mean(oc * oc, axis=1, keepdims=True)
        outs.append(oc * lax.rsqrt(var + RK_LN_EPS))
    for n in range(NB):
        on = jnp.concatenate(outs[n * RK_HEADS:(n + 1) * RK_HEADS], axis=1)
        o_ref[n] = (on * lng_ref[...] + lnb_ref[...] + bon_ref[n]) * g_ref[n]


def _rwkv_chunk(r, lw, k, v, kk, bb, g, bon, lng, lnb, tt=128, nb=4, precise=False):
    B, S, _ = r.shape
    nb = nb if B % nb == 0 else 1
    blk = pl.BlockSpec((nb, tt, RK_W), lambda b, j: (b, j, 0))
    vec = pl.BlockSpec((1, RK_W), lambda b, j: (0, 0))
    return pl.pallas_call(
        functools.partial(_rwkv_chunk_kernel, precise=precise),
        grid=(B // nb, S // tt),
        in_specs=[blk] * 8 + [vec, vec],
        out_specs=blk,
        out_shape=jax.ShapeDtypeStruct((B, S, RK_W), F32),
        scratch_shapes=[pltpu.VMEM((nb * RK_HEADS, RK_HEAD, RK_HEAD), F32)],
        compiler_params=_cparams(("parallel", "arbitrary")),
        name="rwkv_chunk",
    )(r, lw, k, v, kk, bb, g, bon, lng, lnb)


def _nsa_compress_kernel(x_ref, w1g_ref, w1f_ref, w2_ref, pe_ref, o_ref, ot_ref):
    bias = jnp.dot(jnp.broadcast_to(pe_ref[...], (8, CMP_LEN * NS_HEAD)), w1f_ref[...], precision=HI,
                   preferred_element_type=F32)[0:1]
    x = x_ref[...]
    n = x.shape[0]
    outs = []
    for g in range(NS_KV):
        ya = jnp.dot(x, w1g_ref[g, 0], preferred_element_type=F32)
        yb = jnp.dot(x, w1g_ref[g, 1], preferred_element_type=F32)
        h = ya + pltpu.roll(yb, n - 1, 0) + bias
        outs.append(_bdot(jax.nn.gelu(h), w2_ref[...]))
    out = jnp.concatenate(outs, axis=1)
    o_ref[...] = out.astype(BF16)
    ot_ref[...] = out.T.astype(BF16)


def _nsa_compress(cx, w1, w2b, pe):
    _, B, ncp, width = cx.shape
    hid = w1.shape[-1]
    w1h = w1.reshape(2, 2, CMP_STRIDE, 1, NS_HEAD, hid)
    zero = jnp.zeros_like(w1h)
    w1g = jnp.stack([jnp.concatenate([w1h, zero], axis=3), jnp.concatenate([zero, w1h], axis=3)], axis=1)
    w1g = w1g.reshape(2, NS_KV, 2, width, hid).astype(BF16)
    return pl.pallas_call(
        _nsa_compress_kernel,
        grid=(B, 2),
        in_specs=[pl.BlockSpec((None, None, ncp, width), lambda b, c: (c, b, 0, 0)),
                  pl.BlockSpec((None, NS_KV, 2, width, hid), lambda b, c: (c, 0, 0, 0, 0)),
                  pl.BlockSpec((None, CMP_LEN * NS_HEAD, hid), lambda b, c: (c, 0, 0)),
                  pl.BlockSpec((None, hid, NS_HEAD), lambda b, c: (c, 0, 0)),
                  pl.BlockSpec((None, 1, CMP_LEN * NS_HEAD), lambda b, c: (c, 0, 0))],
        out_specs=[pl.BlockSpec((None, None, ncp, LANES), lambda b, c: (b, c, 0, 0)),
                   pl.BlockSpec((None, None, LANES, ncp), lambda b, c: (b, c, 0, 0))],
        out_shape=[jax.ShapeDtypeStruct((B, 2, ncp, LANES), BF16), jax.ShapeDtypeStruct((B, 2, LANES, ncp), BF16)],
        compiler_params=_cparams(("parallel", "parallel")),
        name="nsa_compress",
    )(cx, w1g, w1, w2b, pe)


def _nsa_attn_kernel(q_ref, kc_ref, vct_ref, ks_ref, vst_ref, kw_ref, vwt_ref, gt_ref, ovt_ref, ext_ref, o_ref,
                     m_s, l_s, acc_s, m_w, l_w, acc_w, s_buf, *, ncp):
    i = pl.program_id(1)
    s0 = i * QT
    heads = range(NS_HEADS)
    hcols = lambda h: slice(h * QT, (h + 1) * QT)
    gcols = lambda h: slice(h // NS_HPG * QT, (h // NS_HPG + 1) * QT)
    iota = lambda shape, d: lax.broadcasted_iota(I32, shape, d)
    qpos = lambda shape: s0 + (iota(shape, 1) & (QT - 1))

    def col_reduce(x, op, final):
        n = x.shape[0]
        while n > 8:
            n //= 2
            x = op(x[:n], x[n:])
        return final(x, axis=0, keepdims=True)

    q8 = q_ref[...].astype(F32)
    zeros = jnp.zeros((QT, NS_HEAD), F32)
    qt = []
    for h in heads:
        qh = q8[:, h * NS_HEAD:(h + 1) * NS_HEAD]
        qt.append(jnp.concatenate([qh, zeros] if h < NS_HPG else [zeros, qh], axis=1).T)
    qt = jnp.concatenate(qt, axis=1).astype(BF16)

    sc = jnp.dot(kc_ref[...], qt, preferred_element_type=F32)
    n_row = iota((ncp, QT), 0)
    cmask = (n_row * CMP_STRIDE + (CMP_LEN - 1) <= s0 + iota((ncp, QT), 1)) & (n_row < ncp - 1)
    cpen = jnp.where(cmask, 0.0, NEG)
    cok = jnp.where(cmask, 1.0, 0.0)
    sc = jnp.concatenate([sc[:, hcols(h)] + cpen for h in heads], axis=1)
    pc = jnp.exp2(sc - col_reduce(sc, jnp.maximum, jnp.max))
    pc = jnp.concatenate([pc[:, hcols(h)] * cok for h in heads], axis=1)
    lc = col_reduce(pc, jnp.add, jnp.sum)
    pc = pc / jnp.where(lc > 0.0, lc, 1.0)
    pcs = jnp.concatenate([functools.reduce(jnp.add, [pc[:, hcols(h)] for h in range(g * NS_HPG, (g + 1) * NS_HPG)])
                           for g in range(NS_KV)], axis=1)
    o_c = jnp.dot(vct_ref[...], pc.astype(BF16), preferred_element_type=F32)

    imp = jnp.dot(ovt_ref[...], pcs, precision=HI, preferred_element_type=F32)
    blk = iota((LANES, NS_KV * QT), 0)
    cur = qpos((LANES, NS_KV * QT)) // SEL_BLOCK
    valid = blk <= cur
    forced = (blk == 0) | (blk == cur) | (blk == cur - 1)
    pri = jnp.where(valid & ~forced, imp, -jnp.inf)
    picked = forced
    blkf = blk.astype(F32)
    for _ in range(SEL_TOPK - 3):
        mx = col_reduce(pri, jnp.maximum, jnp.max)
        hit = blkf == col_reduce(jnp.where(pri == mx, blkf, float(LANES)), jnp.minimum, jnp.min)
        picked = picked | hit
        pri = jnp.where(hit, -jnp.inf, pri)
    selpen = jnp.where(picked & valid, 0.0, NEG).astype(BF16)
    wq = jnp.concatenate([qt, jnp.concatenate([selpen[:, gcols(h)] for h in heads], axis=1)], axis=0)

    def attend(s, vt, pen, m_ref, l_ref, acc_ref):
        if pen is not None:
            s = jnp.concatenate([s[:, hcols(h)] + pen for h in heads], axis=1)
        m_old = m_ref[...]
        m_new = jnp.maximum(m_old, col_reduce(s, jnp.maximum, jnp.max))
        alpha = jnp.exp2(m_old - m_new)
        p = jnp.exp2(s - m_new).astype(BF16)
        m_ref[...] = m_new
        ones = jnp.ones((16, vt.shape[1]), BF16)
        pv = jnp.dot(jnp.concatenate([vt, ones], axis=0), p, preferred_element_type=F32)
        l_ref[...] = alpha * l_ref[...] + pv[LANES:LANES + 1]
        acc_ref[...] = alpha * acc_ref[...] + pv[:LANES]

    def reset(m_ref, l_ref, acc_ref):
        m_ref[...] = jnp.full(m_ref.shape, NEG, F32)
        l_ref[...] = jnp.zeros_like(l_ref)
        acc_ref[...] = jnp.zeros_like(acc_ref)

    tiles = lambda ref, t0, n: jnp.concatenate([ref[t0 + c] for c in range(n)], axis=1)

    reset(m_s, l_s, acc_s)
    diag = i // (KT // QT)

    def sel_scores(kt, slot):
        k0 = pl.multiple_of(kt * KT, KT)
        keys = jnp.concatenate([ks_ref[pl.ds(k0, KT), :], ext_ref[pl.ds(k0, KT), :]], axis=1)
        s_buf[slot] = jnp.dot(keys, wq, preferred_element_type=F32)

    def sel_attend(kt, slot, causal):
        pen = None
        if causal:
            pen = jnp.where(kt * KT + iota((KT, QT), 0) <= s0 + iota((KT, QT), 1), 0.0, NEG)
        attend(s_buf[slot], tiles(vst_ref, kt * (KT // QT), KT // QT), pen, m_s, l_s, acc_s)

    def sel_pair(j, c):
        sel_scores(2 * j + 1, 1)
        sel_attend(2 * j, 0, False)
        sel_scores(2 * j + 2, 0)
        sel_attend(2 * j + 1, 1, False)
        return c

    d0 = pl.multiple_of(s0, QT)
    a0 = pl.multiple_of(jnp.maximum(s0 - WINDOW, 0), QT)
    s_d = jnp.dot(kw_ref[pl.ds(d0, QT), :], qt, preferred_element_type=F32)
    s_a = jnp.dot(kw_ref[pl.ds(a0, WINDOW), :], qt, preferred_element_type=F32)
    pen_d = jnp.where(iota((QT, QT), 0) <= iota((QT, QT), 1), 0.0, NEG)
    kpos = a0 + iota((WINDOW, QT), 0)
    pen_a = jnp.where((kpos < s0) & (kpos > s0 + iota((WINDOW, QT), 1) - WINDOW), 0.0, NEG)

    sel_scores(0, 0)
    reset(m_w, l_w, acc_w)
    attend(s_d, vwt_ref[i], pen_d, m_w, l_w, acc_w)
    attend(s_a, tiles(vwt_ref, a0 // QT, WINDOW // QT), pen_a, m_w, l_w, acc_w)
    lax.fori_loop(0, diag // 2, sel_pair, 0)

    @pl.when(diag % 2 == 1)
    def _():
        sel_scores(diag, 1)
        sel_attend(diag - 1, 0, False)
        sel_attend(diag, 1, True)

    @pl.when(diag % 2 == 0)
    def _():
        sel_attend(diag, 0, True)

    gtt = gt_ref[...].T
    o_s = acc_s[...] / l_s[...]
    o_w = acc_w[...] / l_w[...]
    outs = []
    for h in heads:
        gate = lambda br: gtt[br * NS_HEADS + h:br * NS_HEADS + h + 1]
        o = gate(0) * o_c[:, hcols(h)] + gate(1) * o_s[:, hcols(h)] + gate(2) * o_w[:, hcols(h)]
        outs.append(o[:NS_HEAD] if h < NS_HPG else o[NS_HEAD:])
    pairs = [jnp.concatenate(outs[p:p + 2], axis=0).T for p in range(0, NS_HEADS, 2)]
    o_ref[...] = jnp.concatenate(pairs, axis=1).astype(BF16)


def _nsa_attn(q3, kcv, kcvt, kv3, vt, gt3, ovt):
    B, S, _ = q3.shape
    ncp = kcv.shape[2]
    R = NS_HEADS * QT
    ext = (jnp.arange(S, dtype=I32)[:, None] // SEL_BLOCK == jnp.arange(LANES, dtype=I32)[None, :]).astype(BF16)
    seq = lambda c: pl.BlockSpec((None, S, LANES), lambda b, i, c=c: (b, 0, c))
    seqt = lambda c: pl.BlockSpec((None, None, S // QT, LANES, QT), lambda b, i, c=c: (c, b, 0, 0, 0))
    return pl.pallas_call(
        functools.partial(_nsa_attn_kernel, ncp=ncp),
        grid=(B, S // QT),
        in_specs=[pl.BlockSpec((None, QT, NS_W), lambda b, i: (b, i, 0)),
                  pl.BlockSpec((None, None, ncp, LANES), lambda b, i: (b, 0, 0, 0)),
                  pl.BlockSpec((None, None, LANES, ncp), lambda b, i: (b, 1, 0, 0)),
                  seq(0), seqt(0), seq(1), seqt(1),
                  pl.BlockSpec((None, QT, LANES), lambda b, i: (b, i, 0)),
                  pl.BlockSpec(ovt.shape, lambda b, i: (0, 0)),
                  pl.BlockSpec(ext.shape, lambda b, i: (0, 0))],
        out_specs=pl.BlockSpec((None, QT, NS_W), lambda b, i: (b, i, 0)),
        out_shape=jax.ShapeDtypeStruct((B, S, NS_W), BF16),
        scratch_shapes=([pltpu.VMEM((1, R), F32), pltpu.VMEM((1, R), F32), pltpu.VMEM((LANES, R), F32)] * 2
                        + [pltpu.VMEM((2, KT, R), F32)]),
        compiler_params=_cparams(("parallel", "arbitrary")),
        name="nsa_attn",
    )(q3, kcv, kcvt, kv3, vt, kv3, vt, gt3, ovt, ext)


def _first_argmax(vals):
    m = vals[0]
    for v in vals[1:]:
        m = jnp.maximum(m, v)
    idx = jnp.full(m.shape, len(vals) - 1, I32)
    for j in range(len(vals) - 2, -1, -1):
        idx = jnp.where(vals[j] == m, j, idx)
    return m, idx


def _outproj_router_kernel(*refs, n_in):
    acts, ws = refs[:n_in], refs[n_in:2 * n_in]
    x_ref, lng_ref, lnb_ref, rwt_ref, rb_ref, tri_ref = refs[2 * n_in:2 * n_in + 6]
    y_ref, e_ref, wt_ref, pos_ref, cnt_ref, cnt = refs[2 * n_in + 6:]
    i = pl.program_id(0)

    @pl.when(i == 0)
    def _():
        cnt[...] = jnp.zeros_like(cnt)

    mix = _bdot(acts[0][...], ws[0][...])
    for a, w in zip(acts[1:], ws[1:]):
        mix = mix + _bdot(a[...], w[...])
    y = _layer_norm_rows(ALPHA * x_ref[...] + mix, lng_ref[...], lnb_ref[...])
    y_ref[...] = y

    logit = lax.dot_general(rwt_ref[...], y, (((1,), (1,)), ((), ())), precision=HI, preferred_element_type=F32)
    aff = jax.nn.sigmoid(logit)
    biased = aff + rb_ref[...]
    neg_inf = -jnp.inf
    g_score, g_i1, g_i2 = [], [], []
    for gi in range(N_GROUPS):
        vals = [biased[gi * EXP_PER_GROUP + j:gi * EXP_PER_GROUP + j + 1, :] for j in range(EXP_PER_GROUP)]
        m1, i1 = _first_argmax(vals)
        m2, i2 = _first_argmax([jnp.where(i1 == j, neg_inf, vals[j]) for j in range(EXP_PER_GROUP)])
        g_score.append(m1 + m2)
        g_i1.append(i1)
        g_i2.append(i2)
    _, grp = _first_argmax(g_score)
    loc1, loc2 = g_i1[-1], g_i2[-1]
    for gi in range(N_GROUPS - 2, -1, -1):
        loc1 = jnp.where(grp == gi, g_i1[gi], loc1)
        loc2 = jnp.where(grp == gi, g_i2[gi], loc2)
    e1 = grp * EXP_PER_GROUP + loc1
    e2 = grp * EXP_PER_GROUP + loc2
    eio = lax.broadcasted_iota(I32, aff.shape, 0)
    oh1 = eio == e1
    oh2 = eio == e2
    a1 = jnp.sum(jnp.where(oh1, aff, 0.0), axis=0, keepdims=True)
    a2 = jnp.sum(jnp.where(oh2, aff, 0.0), axis=0, keepdims=True)
    tot = a1 + a2
    e_ref[...] = jnp.concatenate([e1, e2], axis=0)
    wt_ref[...] = jnp.concatenate([a1 / tot, a2 / tot], axis=0)

    ohs = oh1.astype(F32) + oh2.astype(F32)
    before = jnp.dot(ohs.astype(BF16), tri_ref[...], preferred_element_type=F32) + cnt[...]
    p1 = jnp.sum(jnp.where(oh1, before, 0.0), axis=0, keepdims=True)
    p2 = jnp.sum(jnp.where(oh2, before, 0.0), axis=0, keepdims=True)
    pos_ref[...] = jnp.concatenate([p1, p2], axis=0).astype(I32)
    cnt[...] = cnt[...] + jnp.sum(ohs, axis=1, keepdims=True)
    cnt_ref[...] = jnp.broadcast_to(cnt[...], cnt_ref.shape)


def _outproj_router(acts, ws, xres, lng, lnb, rwt, rb, tm=1024):
    T = xres.shape[0]
    n_in = len(acts)
    tri = (lax.broadcasted_iota(I32, (tm, tm), 0) < lax.broadcasted_iota(I32, (tm, tm), 1)).astype(BF16)
    row = lambda i: (i, 0)
    const = lambda a: pl.BlockSpec(a.shape, lambda i: (0,) * a.ndim)
    lane_blk = pl.BlockSpec((TOP_K, tm), lambda i: (0, i))
    return pl.pallas_call(
        functools.partial(_outproj_router_kernel, n_in=n_in),
        grid=(T // tm,),
        in_specs=([pl.BlockSpec((tm, a.shape[1]), row) for a in acts] + [const(w) for w in ws]
                  + [pl.BlockSpec((tm, D_MODEL), row), const(lng), const(lnb), const(rwt), const(rb), const(tri)]),
        out_specs=[pl.BlockSpec((tm, D_MODEL), row), lane_blk, lane_blk, lane_blk,
                   pl.BlockSpec((N_EXPERTS, LANES), lambda i: (0, 0))],
        out_shape=[jax.ShapeDtypeStruct((T, D_MODEL), F32), jax.ShapeDtypeStruct((TOP_K, T), I32),
                   jax.ShapeDtypeStruct((TOP_K, T), F32), jax.ShapeDtypeStruct((TOP_K, T), I32),
                   jax.ShapeDtypeStruct((N_EXPERTS, LANES), F32)],
        scratch_shapes=[pltpu.VMEM((N_EXPERTS, 1), F32)],
        compiler_params=_cparams(("arbitrary",)),
        name="outproj_router",
    )(*acts, *ws, xres, lng, lnb, rwt, rb, tri)


def _dispatch_kernel(zrow_ref, dest_hbm, x_ref, xs_hbm, dsm, zbuf, sem_idx, sem, sem_z):
    i = pl.program_id(0)
    tm = x_ref.shape[0]

    @pl.when(i == 0)
    def _():
        zbuf[...] = jnp.zeros_like(zbuf)
        zero_copy = lambda j: pltpu.make_async_copy(
            zbuf, xs_hbm.at[pl.ds(pl.multiple_of(jnp.maximum(zrow_ref[j], 0), MOE_BLOCK), MOE_BLOCK)], sem_z)
        for j in range(zrow_ref.shape[0]):
            @pl.when(zrow_ref[j] >= 0)
            def _():
                zero_copy(j).start()
        for j in range(zrow_ref.shape[0]):
            @pl.when(zrow_ref[j] >= 0)
            def _():
                zero_copy(j).wait()

    idx_copy = pltpu.make_async_copy(dest_hbm.at[i], dsm, sem_idx)
    idx_copy.start()
    idx_copy.wait()

    def row_copy(r, k):
        return pltpu.make_async_copy(x_ref.at[pl.ds(r, 1)], xs_hbm.at[pl.ds(dsm[k * tm + r], 1)], sem)

    def start(r, c):
        for k in range(TOP_K):
            row_copy(r, k).start()
        return c

    lax.fori_loop(0, tm, start, 0, unroll=ROW_DMA_UNROLL)
    for k in range(TOP_K):
        pltpu.make_async_copy(x_ref, xs_hbm.at[pl.ds(0, tm)], sem).wait()


def _dispatch(zero_rows, dest_tiles, x, rows, tm):
    T = x.shape[0]
    return pl.pallas_call(
        _dispatch_kernel,
        grid_spec=pltpu.PrefetchScalarGridSpec(
            num_scalar_prefetch=1,
            grid=(T // tm,),
            in_specs=[pl.BlockSpec(memory_space=pl.ANY), pl.BlockSpec((tm, D_MODEL), lambda i, z: (i, 0))],
            out_specs=pl.BlockSpec(memory_space=pl.ANY),
            scratch_shapes=[pltpu.SMEM((TOP_K * tm,), I32), pltpu.VMEM((MOE_BLOCK, D_MODEL), F32),
                            pltpu.SemaphoreType.DMA(()), pltpu.SemaphoreType.DMA(()), pltpu.SemaphoreType.DMA(())]),
        out_shape=jax.ShapeDtypeStruct((rows, D_MODEL), F32),
        compiler_params=_cparams(("arbitrary",)),
        name="moe_dispatch",
    )(zero_rows, dest_tiles, x)


def _ffn_kernel(be_ref, nu_ref, xs_ref, wg_ref, wu_ref, wd_ref, y_ref, h_ref):
    del be_ref
    i = pl.program_id(0)
    half = D_EXPERT // 2

    @pl.when(i < nu_ref[0])
    def _():
        xb = xs_ref[...].astype(BF16)
        for c in range(2):
            cs = slice(c * half, (c + 1) * half)
            gt = jnp.dot(xb, wg_ref[:, cs], preferred_element_type=F32)
            up = jnp.dot(xb, wu_ref[:, cs], preferred_element_type=F32)
            h_ref[:, cs] = (jax.nn.silu(gt) * up).astype(BF16)
        y_ref[...] = jnp.dot(h_ref[...], wd_ref[...], preferred_element_type=F32)

    @pl.when(i >= nu_ref[0])
    def _():
        y_ref[...] = jnp.zeros_like(y_ref)


def _ffn(blk_exp, n_used, xs, wg, wu, wd):
    rows = xs.shape[0]
    wspec = lambda: pl.BlockSpec((None, D_MODEL, D_EXPERT), lambda i, be, nu: (be[i], 0, 0))
    return pl.pallas_call(
        _ffn_kernel,
        grid_spec=pltpu.PrefetchScalarGridSpec(
            num_scalar_prefetch=2,
            grid=(rows // MOE_BLOCK,),
            in_specs=[pl.BlockSpec((MOE_BLOCK, D_MODEL), lambda i, be, nu: (i, 0)), wspec(), wspec(),
                      pl.BlockSpec((None, D_EXPERT, D_MODEL), lambda i, be, nu: (be[i], 0, 0))],
            out_specs=pl.BlockSpec((MOE_BLOCK, D_MODEL), lambda i, be, nu: (i, 0)),
            scratch_shapes=[pltpu.VMEM((MOE_BLOCK, D_EXPERT), BF16)]),
        out_shape=jax.ShapeDtypeStruct((rows, D_MODEL), F32),
        compiler_params=_cparams(("arbitrary",)),
        name="moe_ffn",
    )(blk_exp, n_used, xs, wg, wu, wd)


def _combine_kernel(dest_hbm, y_hbm, x_ref, wt_ref, lng_ref, lnb_ref, o_ref, dsm0, dsm1, buf, sem_idx, sem):
    i = pl.program_id(0)
    tm = x_ref.shape[0]
    dsm = (dsm0, dsm1)

    def gather(step, s):
        idx_copy = pltpu.make_async_copy(dest_hbm.at[step], dsm[s], sem_idx)
        idx_copy.start()
        idx_copy.wait()

        def start(r, c):
            for k in range(TOP_K):
                pltpu.make_async_copy(y_hbm.at[pl.ds(dsm[s][k * tm + r], 1)], buf.at[s, k, pl.ds(r, 1)],
                                      sem.at[s]).start()
            return c

        lax.fori_loop(0, tm, start, 0, unroll=ROW_DMA_UNROLL)

    def reduce(s):
        for k in range(TOP_K):
            pltpu.make_async_copy(y_hbm.at[pl.ds(0, tm)], buf.at[s, k], sem.at[s]).wait()
        wt = wt_ref[...]
        z = ALPHA * x_ref[...] + wt[:, 0:1] * buf[s, 0] + wt[:, 1:2] * buf[s, 1]
        o_ref[...] = _layer_norm_rows(z, lng_ref[...], lnb_ref[...])

    @pl.when(i == 0)
    def _():
        gather(0, 0)

    for s in range(2):
        @pl.when(i % 2 == s)
        def _():
            @pl.when(i + 1 < pl.num_programs(0))
            def _():
                gather(i + 1, 1 - s)
            reduce(s)


def _combine(dest_tiles, y, x, wt, lng, lnb, tm):
    T = x.shape[0]
    row = lambda i: (i, 0)
    vec = pl.BlockSpec((1, D_MODEL), lambda i: (0, 0))
    return pl.pallas_call(
        _combine_kernel,
        grid=(T // tm,),
        in_specs=[pl.BlockSpec(memory_space=pl.ANY), pl.BlockSpec(memory_space=pl.ANY),
                  pl.BlockSpec((tm, D_MODEL), row), pl.BlockSpec((tm, TOP_K), row), vec, vec],
        out_specs=pl.BlockSpec((tm, D_MODEL), row),
        out_shape=jax.ShapeDtypeStruct((T, D_MODEL), F32),
        scratch_shapes=[pltpu.SMEM((TOP_K * tm,), I32), pltpu.SMEM((TOP_K * tm,), I32),
                        pltpu.VMEM((2, TOP_K, tm, D_MODEL), F32),
                        pltpu.SemaphoreType.DMA(()), pltpu.SemaphoreType.DMA((2,))],
        compiler_params=_cparams(("arbitrary",)),
        name="moe_combine",
    )(dest_tiles, y, x, wt, lng, lnb)


def _moe(x1, e, wt, pos, cnt, wg, wu, wd, lng, lnb, tm=1024):
    T = x1.shape[0]
    n_blocks = -(-(T * TOP_K) // MOE_BLOCK) + N_EXPERTS
    rows = n_blocks * MOE_BLOCK
    counts = cnt[:, 0].astype(I32)
    padded = (counts + MOE_BLOCK - 1) // MOE_BLOCK * MOE_BLOCK
    pad_end = jnp.cumsum(padded)
    pad_start = pad_end - padded
    dest = pos
    for j in range(N_EXPERTS):
        dest = dest + jnp.where(e == j, pad_start[j], 0)
    blk_start = jnp.arange(n_blocks, dtype=I32) * MOE_BLOCK
    blk_exp = jnp.minimum(jnp.sum((pad_end[None, :] <= blk_start[:, None]).astype(I32), axis=1), N_EXPERTS - 1)
    n_used = (pad_end[-1:] // MOE_BLOCK).astype(I32)
    dest_tiles = dest.reshape(TOP_K, T // tm, tm).transpose(1, 0, 2).reshape(T // tm, TOP_K * tm)
    tail = jnp.where(padded > 0, pad_end - MOE_BLOCK, -1)
    trailing = jnp.arange(n_blocks - N_EXPERTS, n_blocks, dtype=I32)
    trailing = jnp.where(trailing >= n_used[0], trailing * MOE_BLOCK, -1)
    xs = _dispatch(jnp.concatenate([tail, trailing]).astype(I32), dest_tiles, x1, rows, tm)
    y = _ffn(blk_exp, n_used, xs, wg, wu, wd)
    return _combine(dest_tiles, y, x1, wt.T, lng, lnb, tm)


def _proj_rt_kernel(x_ref, w_ref, c_ref, s_ref, q_ref, k_ref, v_ref, g_ref):
    xb = x_ref[...].astype(BF16)
    cs, sn = c_ref[...], s_ref[...]
    rope = lambda y: y * cs + pltpu.roll(y, RT_QK // 2, 1) * sn
    pair = 2 * RT_QK
    for c in range(RT_QKW // pair):
        yq = jnp.dot(xb, w_ref[:, c * pair:(c + 1) * pair], preferred_element_type=F32)
        yk = jnp.dot(xb, w_ref[:, RT_QKW + c * pair:RT_QKW + (c + 1) * pair], preferred_element_type=F32)
        for h in range(2):
            cols = slice(c * pair + h * RT_QK, c * pair + (h + 1) * RT_QK)
            q_ref[:, cols] = rope(yq[:, h * RT_QK:(h + 1) * RT_QK]).astype(BF16)
            k_ref[:, cols] = (rope(yk[:, h * RT_QK:(h + 1) * RT_QK]) * RT_QK ** -0.5).astype(BF16)
    step = 1024
    for c in range(RT_VW // step):
        cols = slice(c * step, (c + 1) * step)
        v_ref[:, cols] = jnp.dot(xb, w_ref[:, 2 * RT_QKW + c * step:2 * RT_QKW + (c + 1) * step],
                                 preferred_element_type=F32).astype(BF16)
        base = 2 * RT_QKW + RT_VW
        g_ref[:, cols] = jax.nn.silu(jnp.dot(xb, w_ref[:, base + c * step:base + (c + 1) * step],
                                             preferred_element_type=F32))


def _proj_rt(xt, w, cs, sn, S, tm=512):
    T = xt.shape[0]
    nseq = S // tm
    row = lambda i: (i, 0)
    tab = lambda i: (i % nseq, 0)
    return pl.pallas_call(
        _proj_rt_kernel,
        grid=(T // tm,),
        in_specs=[pl.BlockSpec((tm, D_MODEL), row), pl.BlockSpec(w.shape, lambda i: (0, 0)),
                  pl.BlockSpec((tm, RT_QK), tab), pl.BlockSpec((tm, RT_QK), tab)],
        out_specs=[pl.BlockSpec((tm, RT_QKW), row), pl.BlockSpec((tm, RT_QKW), row),
                   pl.BlockSpec((tm, RT_VW), row), pl.BlockSpec((tm, RT_VW), row)],
        out_shape=[jax.ShapeDtypeStruct((T, RT_QKW), BF16), jax.ShapeDtypeStruct((T, RT_QKW), BF16),
                   jax.ShapeDtypeStruct((T, RT_VW), BF16), jax.ShapeDtypeStruct((T, RT_VW), F32)],
        compiler_params=_cparams(("parallel",)),
        name="proj_rt",
    )(xt, w, cs, sn)


def _retention_kernel(q_ref, k_ref, v_ref, sg_ref, dec_ref, qd_ref, kd_ref, cd_ref, gng_ref, gnb_ref, o_ref, state):
    j = pl.program_id(1)

    @pl.when(j == 0)
    def _():
        state[...] = jnp.zeros_like(state)

    NB = q_ref.shape[0]
    H = range(NB * RT_HEADS)
    nh = lambda i: (i // RT_HEADS, i % RT_HEADS)
    qk_cols = lambda i: slice(nh(i)[1] * RT_QK, (nh(i)[1] + 1) * RT_QK)
    v_cols = lambda i: slice(nh(i)[1] * RT_V, (nh(i)[1] + 1) * RT_V)
    q = [q_ref[nh(i)[0], :, qk_cols(i)] for i in H]
    k = [k_ref[nh(i)[0], :, qk_cols(i)] for i in H]
    v = [v_ref[nh(i)[0], :, v_cols(i)] for i in H]
    r_old = [state[i] for i in H]
    att = [(_dot_nt(q[i], k[i]) * dec_ref[nh(i)[1]]).astype(BF16) for i in H]
    cross = [jnp.dot(q[i], r_old[i].astype(BF16), preferred_element_type=F32) * qd_ref[nh(i)[1]] for i in H]
    inner = [jnp.dot(att[i], v[i], preferred_element_type=F32) for i in H]
    kdec = [(k[i].astype(F32) * kd_ref[nh(i)[1]]).astype(BF16) for i in H]
    for i in H:
        upd = lax.dot_general(kdec[i], v[i], (((0,), (0,)), ((), ())), preferred_element_type=F32)
        state[i] = r_old[i] * cd_ref[nh(i)[1]][:, 0:1] + upd
    for i in H:
        o = inner[i] + cross[i]
        mu = jnp.mean(o, axis=1, keepdims=True)
        oc = o - mu
        var = jnp.mean(oc * oc, axis=1, keepdims=True)
        on = oc * lax.rsqrt(var + RT_GN_EPS) * gng_ref[:, v_cols(i)] + gnb_ref[:, v_cols(i)]
        o_ref[nh(i)[0], :, v_cols(i)] = (sg_ref[nh(i)[0], :, v_cols(i)] * on).astype(BF16)


def _retention(q3, k3, v3, sg3, dec, qd, kd, cd, gng, gnb, nb=2):
    B, S, _ = q3.shape
    C = RT_CHUNK
    nb = nb if B % nb == 0 else 1
    qk = pl.BlockSpec((nb, C, RT_QKW), lambda b, j: (b, j, 0))
    vv = pl.BlockSpec((nb, C, RT_VW), lambda b, j: (b, j, 0))
    const = lambda a: pl.BlockSpec(a.shape, lambda b, j: (0,) * a.ndim)
    return pl.pallas_call(
        _retention_kernel,
        grid=(B // nb, S // C),
        in_specs=[qk, qk, vv, vv, const(dec), const(qd), const(kd), const(cd), const(gng), const(gnb)],
        out_specs=vv,
        out_shape=jax.ShapeDtypeStruct((B, S, RT_VW), BF16),
        scratch_shapes=[pltpu.VMEM((nb * RT_HEADS, RT_QK, RT_V), F32)],
        compiler_params=_cparams(("parallel", "arbitrary")),
        name="retention",
    )(q3, k3, v3, sg3, dec, qd, kd, cd, gng, gnb)


def _nsa_rope_tables(S):
    half = ROPE_DIM // 2
    inv = ROPE_THETA ** (-jnp.arange(half, dtype=F32) / half)
    ang = jnp.arange(S, dtype=F32)[:, None] * inv[None, :]
    cos, sin = jnp.cos(ang), jnp.sin(ang)
    zeros = lambda n: jnp.zeros((S, n), F32)
    cn = jnp.concatenate([cos, cos, jnp.ones((S, NS_HEAD - ROPE_DIM), F32)], axis=1)
    s1 = jnp.concatenate([-sin, zeros(NS_HEAD - half)], axis=1)
    s2 = jnp.concatenate([zeros(half), sin, zeros(NS_HEAD - ROPE_DIM)], axis=1)
    two = lambda a: jnp.concatenate([a, a], axis=1)
    return two(cn), two(s1), two(s2)


def _rt_rope_tables(S):
    inv = RT_THETA ** (-jnp.linspace(0.0, 1.0, RT_QK // 2, dtype=F32))
    ang = jnp.arange(S, dtype=F32)[:, None] * inv[None, :]
    cos, sin = jnp.cos(ang), jnp.sin(ang)
    return jnp.concatenate([cos, cos], axis=1), jnp.concatenate([-sin, sin], axis=1)


def _rt_decay_tables():
    log_g = jnp.log(1.0 - 2.0 ** (-5.0 - jnp.arange(RT_HEADS, dtype=F32)))
    idx = jnp.arange(RT_CHUNK, dtype=F32)
    diff = idx[:, None] - idx[None, :]
    dec = jnp.where(diff >= 0, jnp.exp(jnp.maximum(diff, 0.0) * log_g[:, None, None]), 0.0)
    qd = jnp.exp((idx + 1.0) * log_g[:, None])[..., None]
    kd = jnp.exp((RT_CHUNK - 1.0 - idx) * log_g[:, None])[..., None]
    cd = jnp.broadcast_to(jnp.exp(RT_CHUNK * log_g)[:, None, None], (RT_HEADS, 1, LANES))
    return dec, qd, kd, cd


def _overlap_table(S, ncp):
    n_cmp = (S - CMP_LEN) // CMP_STRIDE + 1
    n_sel = S // SEL_BLOCK
    cs = jnp.arange(ncp) * CMP_STRIDE
    ss = jnp.arange(LANES) * SEL_BLOCK
    ov = jnp.clip(jnp.minimum(cs[:, None] + CMP_LEN, ss[None, :] + SEL_BLOCK)
                  - jnp.maximum(cs[:, None], ss[None, :]), 0, None).astype(F32) / CMP_LEN
    keep = (jnp.arange(ncp)[:, None] < n_cmp) & (jnp.arange(LANES)[None, :] < n_sel)
    return jnp.where(keep, ov, 0.0)


def kernel(x, ab_w_in, ab_w_out, rk_mu, rk_w0, rk_w1, rk_w2, rk_a0, rk_a1, rk_a2, rk_g1, rk_g2, rk_kk, rk_ka, rk_rk,
           rk_ln, ns_pe, ns_c_w1, ns_c_w2, rt_w_in, rt_w_out, rt_gn, router_w, router_b, moe_w_gate, moe_w_up,
           moe_w_down, ln):
    B, S, D = x.shape
    T = B * S
    assert D == D_MODEL and S % 256 == 0 and S // SEL_BLOCK <= LANES and S >= WINDOW
    xt = x.reshape(T, D)
    rwt = router_w.T
    rb = router_b.reshape(N_EXPERTS, 1)
    vec = lambda a: a.reshape(1, -1)

    w_in = ab_w_in[0]
    n_gate = 3 * NS_HEADS
    w_cat = jnp.concatenate([w_in[:, :-n_gate], jnp.pad(w_in[:, -n_gate:], ((0, 0), (0, LANES - n_gate)))],
                            axis=1).astype(BF16)
    cn, s1, s2 = _nsa_rope_tables(S)
    prk, q, kk2, gt, vt, cx = _proj_ab(xt, w_cat, cn, s1, s2, S)

    ones = (jnp.arange(RK_W)[:, None] // RK_HEAD == jnp.arange(RK_W)[None, :] // RK_HEAD).astype(BF16)
    b16 = lambda a: a.astype(BF16)
    r, lw, km, v, kk, bb, g, bon = _rwkv_prep(
        prk.reshape(B, S, 4 * RK_W), rk_mu[0], vec(rk_w0[0]), b16(rk_w1[0]), b16(rk_w2[0]), vec(rk_a0[0]),
        b16(rk_a1[0]), b16(rk_a2[0]), b16(rk_g1[0]), b16(rk_g2[0]), vec(rk_kk[0]), vec(rk_ka[0]), vec(rk_rk[0]), ones)
    o_a = _rwkv_chunk(r, lw, km, v, kk, bb, g, bon, rk_ln[0, 0:1], rk_ln[0, 1:2])

    ncp = S // CMP_STRIDE
    kcv, kcvt = _nsa_compress(cx.reshape(2, B, ncp, CMP_STRIDE * LANES), ns_c_w1[0], b16(ns_c_w2[0]),
                              ns_pe[0].reshape(2, 1, CMP_LEN * NS_HEAD))
    o_b = _nsa_attn(q.reshape(B, S, NS_W), kcv, kcvt, kk2.reshape(B, S, 2 * LANES),
                    vt.reshape(2, B, S // QT, LANES, QT), gt.reshape(B, S, LANES), _overlap_table(S, ncp).T)

    w_out = b16(ab_w_out[0])
    x1, e, wt, pos, cnt = _outproj_router([o_a.reshape(T, RK_W), o_b.reshape(T, NS_W)], [w_out[:RK_W], w_out[RK_W:]],
                                          xt, ln[0, 0, 0:1], ln[0, 0, 1:2], rwt, rb)
    x2 = _moe(x1, e, wt, pos, cnt, b16(moe_w_gate[0]), b16(moe_w_up[0]), b16(moe_w_down[0]),
              ln[0, 1, 0:1], ln[0, 1, 1:2])

    cs, sn = _rt_rope_tables(S)
    qr, kr, vr, sg = _proj_rt(x2, b16(rt_w_in[0]), cs, sn, S)
    dec, qd, kd, cd = _rt_decay_tables()
    ret = _retention(qr.reshape(B, S, RT_QKW), kr.reshape(B, S, RT_QKW), vr.reshape(B, S, RT_VW),
                     sg.reshape(B, S, RT_VW), dec, qd, kd, cd, rt_gn[0, 0:1], rt_gn[0, 1:2])
    x3, e, wt, pos, cnt = _outproj_router([ret.reshape(T, RT_VW)], [b16(rt_w_out[0])], x2,
                                          ln[1, 0, 0:1], ln[1, 0, 1:2], rwt, rb)
    x4 = _moe(x3, e, wt, pos, cnt, b16(moe_w_gate[1]), b16(moe_w_up[1]), b16(moe_w_down[1]),
              ln[1, 1, 0:1], ln[1, 1, 1:2])
    return x4.reshape(B, S, D)
```

```python
import functools
import math

import jax
import jax.numpy as jnp
from jax import lax
from jax.experimental import pallas as pl
from jax.experimental.pallas import tpu as pltpu

F32 = jnp.float32
BF16 = jnp.bfloat16
I32 = jnp.int32
HI = lax.Precision.HIGHEST

LANES = 128
VMEM_LIMIT = 56 * 1024 * 1024

D_MODEL = 1024
RK_HEADS, RK_HEAD = 8, 64
RK_W = RK_HEADS * RK_HEAD
RK_DECAY_SCALE = 0.606531
RK_LN_EPS = 64e-5
NS_HEADS, NS_KV, NS_HPG, NS_HEAD = 8, 2, 4, 64
NS_W = NS_HEADS * NS_HEAD
CMP_LEN, CMP_STRIDE, SEL_BLOCK, SEL_TOPK, WINDOW = 32, 16, 64, 16, 512
ROPE_THETA = 500000.0
ROPE_DIM = NS_HEAD // 4
Q_SCALE = NS_HEAD ** -0.5 * math.log2(math.e)
QT = 256
KT = 512
RT_HEADS, RT_QK, RT_V = 8, 128, 256
RT_QKW, RT_VW = RT_HEADS * RT_QK, RT_HEADS * RT_V
RT_CHUNK = 128
RT_THETA = 10000.0
RT_GN_EPS = 1e-5
N_EXPERTS, N_GROUPS, EXP_PER_GROUP, TOP_K = 16, 4, 4, 2
D_EXPERT = 1024
MOE_BLOCK = 512
ROW_DMA_UNROLL = 8
DEPTH = 2
ALPHA = (2.0 * DEPTH) ** 0.25
LN_EPS = 1e-5
NEG = -1e30


def _cparams(sem):
    return pltpu.CompilerParams(dimension_semantics=sem, vmem_limit_bytes=VMEM_LIMIT)


def _bdot(a, w):
    return jnp.dot(a.astype(BF16), w, preferred_element_type=F32)


def _dot_nt(a, b):
    return lax.dot_general(a, b, (((1,), (1,)), ((), ())), preferred_element_type=F32)


def _layer_norm_rows(z, g, b):
    mu = jnp.mean(z, axis=1, keepdims=True)
    zc = z - mu
    var = jnp.mean(zc * zc, axis=1, keepdims=True)
    return zc * lax.rsqrt(var + LN_EPS) * g + b


def _proj_ab_kernel(x_ref, w_ref, cn_ref, s1_ref, s2_ref, pick_ref, prk_ref, q_ref, kk_ref, gt_ref, vt_ref, cx_ref):
    xb = x_ref[...].astype(BF16)
    for c in range(4):
        prk_ref[:, c * RK_W:(c + 1) * RK_W] = jnp.dot(xb, w_ref[:, c * RK_W:(c + 1) * RK_W],
                                                      preferred_element_type=F32)
    cn, s1, s2 = cn_ref[...], s1_ref[...], s2_ref[...]

    def rope(y):
        return y * cn + pltpu.roll(y, LANES - ROPE_DIM // 2, 1) * s1 + pltpu.roll(y, ROPE_DIM // 2, 1) * s2

    base = 4 * RK_W
    yq = jnp.dot(xb, w_ref[:, base:base + NS_W], preferred_element_type=F32)
    for c in range(NS_W // LANES):
        q_ref[:, c * LANES:(c + 1) * LANES] = (rope(yq[:, c * LANES:(c + 1) * LANES]) * Q_SCALE).astype(BF16)
    base += NS_W
    ykv = jnp.dot(xb, w_ref[:, base:base + 6 * LANES], preferred_element_type=F32)
    for c in range(6):
        y = ykv[:, c * LANES:(c + 1) * LANES]
        if c % 2 == 0:
            y = rope(y)
        if c < 2:
            yb = y.astype(BF16)
            for l in range(CMP_STRIDE):
                cx_ref[c, :, l * LANES:(l + 1) * LANES] = jnp.dot(pick_ref[l], yb,
                                                                  preferred_element_type=F32).astype(BF16)
        elif c % 2 == 0:
            kk_ref[:, (c // 2 - 1) * LANES:(c // 2) * LANES] = y.astype(BF16)
        else:
            for t in range(y.shape[0] // QT):
                vt_ref[c // 2 - 1, t] = y[t * QT:(t + 1) * QT].T.astype(BF16)
    base += 6 * LANES
    gt_ref[...] = jax.nn.sigmoid(jnp.dot(xb, w_ref[:, base:base + LANES], preferred_element_type=F32))


def _proj_ab(xt, w, cn, s1, s2, S, tm=512):
    T = xt.shape[0]
    ncols = w.shape[1]
    nseq = S // tm
    row = lambda i: (i, 0)
    tab = lambda i: (i % nseq, 0)
    ng = tm // CMP_STRIDE
    pick = (jnp.arange(tm, dtype=I32)[None, None, :]
            == jnp.arange(ng, dtype=I32)[None, :, None] * CMP_STRIDE + jnp.arange(CMP_STRIDE, dtype=I32)[:, None, None])
    pick = pick.astype(BF16)
    return pl.pallas_call(
        _proj_ab_kernel,
        grid=(T // tm,),
        in_specs=[pl.BlockSpec((tm, D_MODEL), row),
                  pl.BlockSpec((D_MODEL, ncols), lambda i: (0, 0)),
                  pl.BlockSpec((tm, LANES), tab), pl.BlockSpec((tm, LANES), tab), pl.BlockSpec((tm, LANES), tab),
                  pl.BlockSpec(pick.shape, lambda i: (0, 0, 0))],
        out_specs=[pl.BlockSpec((tm, 4 * RK_W), row), pl.BlockSpec((tm, NS_W), row),
                   pl.BlockSpec((tm, 2 * LANES), row), pl.BlockSpec((tm, LANES), row),
                   pl.BlockSpec((2, tm // QT, LANES, QT), lambda i: (0, i, 0, 0)),
                   pl.BlockSpec((2, ng, CMP_STRIDE * LANES), lambda i: (0, i, 0))],
        out_shape=[jax.ShapeDtypeStruct((T, 4 * RK_W), F32), jax.ShapeDtypeStruct((T, NS_W), BF16),
                   jax.ShapeDtypeStruct((T, 2 * LANES), BF16), jax.ShapeDtypeStruct((T, LANES), F32),
                   jax.ShapeDtypeStruct((2, T // QT, LANES, QT), BF16),
                   jax.ShapeDtypeStruct((2, T // CMP_STRIDE, CMP_STRIDE * LANES), BF16)],
        compiler_params=_cparams(("parallel",)),
        name="proj_ab",
    )(xt, w, cn, s1, s2, pick)


def _rwkv_prep_kernel(p_ref, mu_ref, w0_ref, w1_ref, w2_ref, a0_ref, a1_ref, a2_ref, g1_ref, g2_ref,
                      kk_ref, ka_ref, rk_ref, ones_ref,
                      r_o, w_o, k_o, v_o, kk_o, b_o, g_o, bon_o, carry):
    j = pl.program_id(1)
    p = p_ref[...]
    tm = p.shape[0]

    @pl.when(j == 0)
    def _():
        carry[...] = jnp.zeros_like(carry)

    rowi = lax.broadcasted_iota(I32, p.shape, 0)
    prev = jnp.where(rowi == 0, carry[...], pltpu.roll(p, 1, 0))
    carry[...] = p[tm - 1:tm, :]
    dp = prev - p
    sl = lambda a, c: a[:, c * RK_W:(c + 1) * RK_W]
    mu = mu_ref[...]
    r = sl(p, 0) + sl(dp, 0) * mu[0:1]
    k = sl(p, 1) + sl(dp, 1) * mu[1:2]
    v = sl(p, 2) + sl(dp, 2) * mu[2:3]
    xw = sl(p, 3) + sl(dp, 3) * mu[3:4]
    xa = sl(p, 3) + sl(dp, 3) * mu[4:5]
    xg = sl(p, 3) + sl(dp, 3) * mu[5:6]
    lw = -RK_DECAY_SCALE * jax.nn.sigmoid(w0_ref[...] + _bdot(jnp.tanh(_bdot(xw, w1_ref[...])), w2_ref[...]))
    a = jax.nn.sigmoid(a0_ref[...] + _bdot(_bdot(xa, a1_ref[...]), a2_ref[...]))
    g = _bdot(jax.nn.sigmoid(_bdot(xg, g1_ref[...])), g2_ref[...])
    ones = ones_ref[...]

    def head_sum(t):
        hi = t.astype(BF16)
        lo = (t - hi.astype(F32)).astype(BF16)
        return (jnp.dot(hi, ones, preferred_element_type=F32) + jnp.dot(lo, ones, preferred_element_type=F32))

    kk = k * kk_ref[...]
    kk = kk / jnp.maximum(jnp.sqrt(head_sum(kk * kk)), 1e-12)
    km = k * (1.0 + (a - 1.0) * ka_ref[...])
    bon = head_sum(r * km * rk_ref[...]) * v
    w_o[...] = lw
    for ref, val in ((r_o, r), (k_o, km), (v_o, v), (kk_o, kk), (b_o, kk * a), (g_o, g), (bon_o, bon)):
        ref[...] = val.astype(ref.dtype)


def _rwkv_prep(prk3, mu, w0, w1, w2, a0, a1, a2, g1, g2, k_k, k_a, r_k, ones, tm=512):
    B, S, _ = prk3.shape
    full = lambda a: pl.BlockSpec(a.shape, lambda b, j: (0,) * a.ndim)
    params = [mu, w0, w1, w2, a0, a1, a2, g1, g2, k_k, k_a, r_k, ones]
    ospec = pl.BlockSpec((None, tm, RK_W), lambda b, j: (b, j, 0))
    return pl.pallas_call(
        _rwkv_prep_kernel,
        grid=(B, S // tm),
        in_specs=[pl.BlockSpec((None, tm, 4 * RK_W), lambda b, j: (b, j, 0))] + [full(a) for a in params],
        out_specs=[ospec] * 8,
        out_shape=[jax.ShapeDtypeStruct((B, S, RK_W), F32 if i == 1 else BF16) for i in range(8)],
        scratch_shapes=[pltpu.VMEM((1, 4 * RK_W), F32)],
        compiler_params=_cparams(("parallel", "arbitrary")),
        name="rwkv_prep",
    )(prk3, *params)


RK_CHUNK = 16


def _pdot(a, b, dims, precise):
    if precise:
        return lax.dot_general(a, b, (dims, ((), ())), precision=HI, preferred_element_type=F32)
    return lax.dot_general(a.astype(BF16), b.astype(BF16), (dims, ((), ())), preferred_element_type=F32)


def _rwkv_chunk_kernel(r_ref, lw_ref, k_ref, v_ref, kk_ref, b_ref, g_ref, bon_ref, lng_ref, lnb_ref, o_ref, ht,
                       *, precise):
    j = pl.program_id(1)

    @pl.when(j == 0)
    def _():
        ht[...] = jnp.zeros_like(ht)

    NB, TT = r_ref.shape[0], r_ref.shape[1]
    C, N = RK_CHUNK, RK_HEAD
    mm = lambda a, b: _pdot(a, b, ((1,), (0,)), precise)
    mm_nt = lambda a, b: _pdot(a, b, ((1,), (1,)), precise)
    mm_tn = lambda a, b: _pdot(a, b, ((0,), (0,)), precise)
    wide = lambda ref: jnp.concatenate([ref[n] for n in range(NB)], axis=1)

    lw = wide(lw_ref)
    rowc = lax.broadcasted_iota(I32, lw.shape, 0) & (C - 1)
    linc, lrev = lw, lw
    sh = 1
    while sh < C:
        linc = linc + jnp.where(rowc >= sh, pltpu.roll(linc, sh, 0), 0.0)
        lrev = lrev + jnp.where(rowc < C - sh, pltpu.roll(lrev, TT - sh, 0), 0.0)
        sh *= 2
    lrev = lrev - lw
    r, k, v, kk, b = wide(r_ref), wide(k_ref), wide(v_ref), wide(kk_ref), wide(b_ref)
    e_in, e_inv, e_rev = jnp.exp(linc), jnp.exp(-linc), jnp.exp(lrev)
    kkd = kk * jnp.exp(linc - lw)
    rd = r * e_in
    binv, kinv = b * e_inv, k * e_inv
    bd, kd = b * e_rev, k * e_rev
    gam = jnp.exp(linc + lrev)

    ti = lax.broadcasted_iota(I32, (TT, TT), 0)
    tj = lax.broadcasted_iota(I32, (TT, TT), 1)
    same = (ti // C) == (tj // C)
    strict = same & (tj < ti)
    incl = same & (tj <= ti)

    H = range(NB * RK_HEADS)
    lo = lambda a: a if precise else a.astype(BF16)
    hs = lambda a: [a[:, h * N:(h + 1) * N] for h in H]
    rows2 = lambda a, b: jnp.concatenate([a, b], axis=0)
    kkd_h, rd_h, v_h = hs(lo(kkd)), hs(lo(rd)), hs(lo(v))
    binv_h, kinv_h, bd_h, kd_h = hs(lo(binv)), hs(lo(kinv)), hs(lo(bd)), hs(lo(kd))
    gam_h = hs(gam)
    gm = [mm_nt(rows2(kkd_h[h], rd_h[h]), rows2(binv_h[h], kinv_h[h])) for h in H]
    a_b = [lo(jnp.where(strict, gm[h][:TT, :TT], 0.0)) for h in H]
    b_rb = [lo(jnp.where(incl, gm[h][TT:, :TT], 0.0)) for h in H]
    akb = [lo(rows2(jnp.where(strict, gm[h][:TT, TT:], 0.0), jnp.where(incl, gm[h][TT:, TT:], 0.0))) for h in H]
    av = [mm(akb[h], v_h[h]) for h in H]
    x = [jnp.concatenate([kkd_h[h].astype(F32), av[h][:TT]], axis=1) for h in H]
    a2 = [lo(mm(a_b[h], a_b[h])) for h in H]
    a4 = [lo(mm(a2[h], a2[h])) for h in H]
    a8 = [lo(mm(a4[h], a4[h])) for h in H]
    x = [x[h] + mm(a8[h], lo(x[h])) for h in H]
    x = [x[h] + mm(a4[h], lo(x[h])) for h in H]
    x = [x[h] + mm(a2[h], lo(x[h])) for h in H]
    x = [x[h] - mm(a_b[h], lo(x[h])) for h in H]
    wt = [lo(x[h][:, :N]) for h in H]
    h_t = [ht[h] for h in H]
    us = [[] for _ in H]
    rhs = [[] for _ in H]
    for c in range(TT // C):
        rs = slice(c * C, (c + 1) * C)
        xh = [mm_nt(rows2(wt[h][rs], rd_h[h][rs]), lo(h_t[h])) for h in H]
        for h in H:
            u_c = -(xh[h][:C] + x[h][rs, N:])
            us[h].append(u_c)
            rhs[h].append(xh[h][C:])
        upd = [mm_tn(rows2(lo(us[h][c]), v_h[h][rs]), rows2(bd_h[h][rs], kd_h[h][rs])) for h in H]
        h_t = [h_t[h] * gam_h[h][c * C:c * C + 1] + upd[h] for h in H]
    outs = []
    for h in H:
        ht[h] = h_t[h]
        o = jnp.concatenate(rhs[h], axis=0) + mm(b_rb[h], lo(jnp.concatenate(us[h], axis=0))) + av[h][TT:]
        mu = jnp.mean(o, axis=1, keepdims=True)
        oc = o - mu
        var = jnp.mean(oc * oc, axis=1, keepdims=True)
        outs.append(oc * lax.rsqrt(var + RK_LN_EPS))
    for n in range(NB):
        on = jnp.concatenate(outs[n * RK_HEADS:(n + 1) * RK_HEADS], axis=1)
        o_ref[n] = (on * lng_ref[...] + lnb_ref[...] + bon_ref[n]) * g_ref[n]


def _rwkv_chunk(r, lw, k, v, kk, bb, g, bon, lng, lnb, tt=128, nb=4, precise=False):
    B, S, _ = r.shape
    nb = nb if B % nb == 0 else 1
    blk = pl.BlockSpec((nb, tt, RK_W), lambda b, j: (b, j, 0))
    vec = pl.BlockSpec((1, RK_W), lambda b, j: (0, 0))
    return pl.pallas_call(
        functools.partial(_rwkv_chunk_kernel, precise=precise),
        grid=(B // nb, S // tt),
        in_specs=[blk] * 8 + [vec, vec],
        out_specs=blk,
        out_shape=jax.ShapeDtypeStruct((B, S, RK_W), F32),
        scratch_shapes=[pltpu.VMEM((nb * RK_HEADS, RK_HEAD, RK_HEAD), F32)],
        compiler_params=_cparams(("parallel", "arbitrary")),
        name="rwkv_chunk",
    )(r, lw, k, v, kk, bb, g, bon, lng, lnb)


def _nsa_compress_kernel(x_ref, w1g_ref, w1f_ref, w2_ref, pe_ref, o_ref, ot_ref):
    bias = jnp.dot(jnp.broadcast_to(pe_ref[...], (8, CMP_LEN * NS_HEAD)), w1f_ref[...], precision=HI,
                   preferred_element_type=F32)[0:1]
    x = x_ref[...]
    n = x.shape[0]
    outs = []
    for g in range(NS_KV):
        ya = jnp.dot(x, w1g_ref[g, 0], preferred_element_type=F32)
        yb = jnp.dot(x, w1g_ref[g, 1], preferred_element_type=F32)
        h = ya + pltpu.roll(yb, n - 1, 0) + bias
        outs.append(_bdot(jax.nn.gelu(h), w2_ref[...]))
    out = jnp.concatenate(outs, axis=1)
    o_ref[...] = out.astype(BF16)
    ot_ref[...] = out.T.astype(BF16)


def _nsa_compress(cx, w1, w2b, pe):
    _, B, ncp, width = cx.shape
    hid = w1.shape[-1]
    w1h = w1.reshape(2, 2, CMP_STRIDE, 1, NS_HEAD, hid)
    zero = jnp.zeros_like(w1h)
    w1g = jnp.stack([jnp.concatenate([w1h, zero], axis=3), jnp.concatenate([zero, w1h], axis=3)], axis=1)
    w1g = w1g.reshape(2, NS_KV, 2, width, hid).astype(BF16)
    return pl.pallas_call(
        _nsa_compress_kernel,
        grid=(B, 2),
        in_specs=[pl.BlockSpec((None, None, ncp, width), lambda b, c: (c, b, 0, 0)),
                  pl.BlockSpec((None, NS_KV, 2, width, hid), lambda b, c: (c, 0, 0, 0, 0)),
                  pl.BlockSpec((None, CMP_LEN * NS_HEAD, hid), lambda b, c: (c, 0, 0)),
                  pl.BlockSpec((None, hid, NS_HEAD), lambda b, c: (c, 0, 0)),
                  pl.BlockSpec((None, 1, CMP_LEN * NS_HEAD), lambda b, c: (c, 0, 0))],
        out_specs=[pl.BlockSpec((None, None, ncp, LANES), lambda b, c: (b, c, 0, 0)),
                   pl.BlockSpec((None, None, LANES, ncp), lambda b, c: (b, c, 0, 0))],
        out_shape=[jax.ShapeDtypeStruct((B, 2, ncp, LANES), BF16), jax.ShapeDtypeStruct((B, 2, LANES, ncp), BF16)],
        compiler_params=_cparams(("parallel", "parallel")),
        name="nsa_compress",
    )(cx, w1g, w1, w2b, pe)


def _nsa_attn_kernel(q_ref, kc_ref, vct_ref, ks_ref, vst_ref, kw_ref, vwt_ref, gt_ref, ovt_ref, ext_ref, o_ref,
                     m_s, l_s, acc_s, m_w, l_w, acc_w, s_buf, *, ncp):
    i = pl.program_id(1)
    s0 = i * QT
    heads = range(NS_HEADS)
    hcols = lambda h: slice(h * QT, (h + 1) * QT)
    gcols = lambda h: slice(h // NS_HPG * QT, (h // NS_HPG + 1) * QT)
    iota = lambda shape, d: lax.broadcasted_iota(I32, shape, d)
    qpos = lambda shape: s0 + (iota(shape, 1) & (QT - 1))

    def col_reduce(x, op, final):
        n = x.shape[0]
        while n > 8:
            n //= 2
            x = op(x[:n], x[n:])
        return final(x, axis=0, keepdims=True)

    q8 = q_ref[...].astype(F32)
    zeros = jnp.zeros((QT, NS_HEAD), F32)
    qt = []
    for h in heads:
        qh = q8[:, h * NS_HEAD:(h + 1) * NS_HEAD]
        qt.append(jnp.concatenate([qh, zeros] if h < NS_HPG else [zeros, qh], axis=1).T)
    qt = jnp.concatenate(qt, axis=1).astype(BF16)

    sc = jnp.dot(kc_ref[...], qt, preferred_element_type=F32)
    n_row = iota((ncp, QT), 0)
    cmask = (n_row * CMP_STRIDE + (CMP_LEN - 1) <= s0 + iota((ncp, QT), 1)) & (n_row < ncp - 1)
    cpen = jnp.where(cmask, 0.0, NEG)
    sc = jnp.concatenate([sc[:, hcols(h)] + cpen for h in heads], axis=1)
    mc = col_reduce(sc, jnp.maximum, jnp.max)
    pc = jnp.exp2(sc - mc)
    lc = col_reduce(pc, jnp.add, jnp.sum)
    pc = pc * jnp.where(mc > 0.5 * NEG, 1.0 / lc, 0.0)
    pcs = jnp.concatenate([functools.reduce(jnp.add, [pc[:, hcols(h)] for h in range(g * NS_HPG, (g + 1) * NS_HPG)])
                           for g in range(NS_KV)], axis=1)

    def group_dot(vt, p, extra=None):
        half = NS_HPG * QT
        outs = []
        for g in range(NS_KV):
            vg = vt[g * NS_HEAD:(g + 1) * NS_HEAD]
            if extra is not None:
                vg = jnp.concatenate([vg, extra], axis=0)
            outs.append(jnp.dot(vg, p[:, g * half:(g + 1) * half], preferred_element_type=F32))
        return jnp.concatenate(outs, axis=1)

    o_c = group_dot(vct_ref[...], pc.astype(BF16))

    imp = jnp.dot(ovt_ref[...], pcs, precision=HI, preferred_element_type=F32)
    blk = iota((LANES, NS_KV * QT), 0)
    cur = qpos((LANES, NS_KV * QT)) // SEL_BLOCK
    valid = blk <= cur
    forced = (blk == 0) | (blk == cur) | (blk == cur - 1)
    pri = jnp.where(valid & ~forced, imp, -jnp.inf)
    picked = forced
    blkf = blk.astype(F32)
    for _ in range(SEL_TOPK - 3):
        mx = col_reduce(pri, jnp.maximum, jnp.max)
        hit = blkf == col_reduce(jnp.where(pri == mx, blkf, float(LANES)), jnp.minimum, jnp.min)
        picked = picked | hit
        pri = jnp.where(hit, -jnp.inf, pri)
    selpen = jnp.where(picked & valid, 0.0, NEG).astype(BF16)
    wq = jnp.concatenate([qt, jnp.concatenate([selpen[:, gcols(h)] for h in heads], axis=1)], axis=0)

    def attend(s, vt, pen, m_ref, l_ref, acc_ref):
        if pen is not None:
            s = jnp.concatenate([s[:, hcols(h)] + pen for h in heads], axis=1)
        m_old = m_ref[...]
        m_new = jnp.maximum(m_old, col_reduce(s, jnp.maximum, jnp.max))
        alpha = jnp.exp2(m_old - m_new)
        p = jnp.exp2(s - m_new).astype(BF16)
        m_ref[...] = m_new
        pv = group_dot(vt, p, extra=jnp.ones((16, vt.shape[1]), BF16))
        l_ref[...] = alpha * l_ref[...] + pv[NS_HEAD:NS_HEAD + 1]
        acc_ref[...] = alpha * acc_ref[...] + pv[:NS_HEAD]

    def reset(m_ref, l_ref, acc_ref):
        m_ref[...] = jnp.full(m_ref.shape, NEG, F32)
        l_ref[...] = jnp.zeros_like(l_ref)
        acc_ref[...] = jnp.zeros_like(acc_ref)

    tiles = lambda ref, t0, n: jnp.concatenate([ref[t0 + c] for c in range(n)], axis=1)

    reset(m_s, l_s, acc_s)
    diag = i // (KT // QT)

    def sel_scores(kt, slot):
        k0 = pl.multiple_of(kt * KT, KT)
        keys = jnp.concatenate([ks_ref[pl.ds(k0, KT), :], ext_ref[pl.ds(k0, KT), :]], axis=1)
        s_buf[slot] = jnp.dot(keys, wq, preferred_element_type=F32)

    def sel_attend(kt, slot, causal):
        pen = None
        if causal:
            pen = jnp.where(kt * KT + iota((KT, QT), 0) <= s0 + iota((KT, QT), 1), 0.0, NEG)
        attend(s_buf[slot], tiles(vst_ref, kt * (KT // QT), KT // QT), pen, m_s, l_s, acc_s)

    def sel_pair(j, c):
        sel_scores(2 * j + 1, 1)
        sel_attend(2 * j, 0, False)
        sel_scores(2 * j + 2, 0)
        sel_attend(2 * j + 1, 1, False)
        return c

    d0 = pl.multiple_of(s0, QT)
    a0 = pl.multiple_of(jnp.maximum(s0 - WINDOW, 0), QT)
    s_d = jnp.dot(kw_ref[pl.ds(d0, QT), :], qt, preferred_element_type=F32)
    s_a = jnp.dot(kw_ref[pl.ds(a0, WINDOW), :], qt, preferred_element_type=F32)
    pen_d = jnp.where(iota((QT, QT), 0) <= iota((QT, QT), 1), 0.0, NEG)
    kpos = a0 + iota((WINDOW, QT), 0)
    pen_a = jnp.where((kpos < s0) & (kpos > s0 + iota((WINDOW, QT), 1) - WINDOW), 0.0, NEG)

    sel_scores(0, 0)
    reset(m_w, l_w, acc_w)
    attend(s_d, vwt_ref[i], pen_d, m_w, l_w, acc_w)
    attend(s_a, tiles(vwt_ref, a0 // QT, WINDOW // QT), pen_a, m_w, l_w, acc_w)
    lax.fori_loop(0, diag // 2, sel_pair, 0)

    @pl.when(diag % 2 == 1)
    def _():
        sel_scores(diag, 1)
        sel_attend(diag - 1, 0, False)
        sel_attend(diag, 1, True)

    @pl.when(diag % 2 == 0)
    def _():
        sel_attend(diag, 0, True)

    gtt = gt_ref[...].T
    o_s = acc_s[...] / l_s[...]
    o_w = acc_w[...] / l_w[...]
    outs = []
    for h in heads:
        gate = lambda br: gtt[br * NS_HEADS + h:br * NS_HEADS + h + 1]
        outs.append(gate(0) * o_c[:, hcols(h)] + gate(1) * o_s[:, hcols(h)] + gate(2) * o_w[:, hcols(h)])
    pairs = [jnp.concatenate(outs[p:p + 2], axis=0).T for p in range(0, NS_HEADS, 2)]
    o_ref[...] = jnp.concatenate(pairs, axis=1).astype(BF16)


def _nsa_attn(q3, kcv, kcvt, kv3, vt, gt3, ovt):
    B, S, _ = q3.shape
    ncp = kcv.shape[2]
    R = NS_HEADS * QT
    ext = (jnp.arange(S, dtype=I32)[:, None] // SEL_BLOCK == jnp.arange(LANES, dtype=I32)[None, :]).astype(BF16)
    seq = lambda c: pl.BlockSpec((None, S, LANES), lambda b, i, c=c: (b, 0, c))
    seqt = lambda c: pl.BlockSpec((None, None, S // QT, LANES, QT), lambda b, i, c=c: (c, b, 0, 0, 0))
    return pl.pallas_call(
        functools.partial(_nsa_attn_kernel, ncp=ncp),
        grid=(B, S // QT),
        in_specs=[pl.BlockSpec((None, QT, NS_W), lambda b, i: (b, i, 0)),
                  pl.BlockSpec((None, None, ncp, LANES), lambda b, i: (b, 0, 0, 0)),
                  pl.BlockSpec((None, None, LANES, ncp), lambda b, i: (b, 1, 0, 0)),
                  seq(0), seqt(0), seq(1), seqt(1),
                  pl.BlockSpec((None, QT, LANES), lambda b, i: (b, i, 0)),
                  pl.BlockSpec(ovt.shape, lambda b, i: (0, 0)),
                  pl.BlockSpec(ext.shape, lambda b, i: (0, 0))],
        out_specs=pl.BlockSpec((None, QT, NS_W), lambda b, i: (b, i, 0)),
        out_shape=jax.ShapeDtypeStruct((B, S, NS_W), BF16),
        scratch_shapes=([pltpu.VMEM((1, R), F32), pltpu.VMEM((1, R), F32), pltpu.VMEM((NS_HEAD, R), F32)] * 2
                        + [pltpu.VMEM((2, KT, R), F32)]),
        compiler_params=_cparams(("parallel", "arbitrary")),
        name="nsa_attn",
    )(q3, kcv, kcvt, kv3, vt, kv3, vt, gt3, ovt, ext)


def _first_argmax(vals):
    m = vals[0]
    for v in vals[1:]:
        m = jnp.maximum(m, v)
    idx = jnp.full(m.shape, len(vals) - 1, I32)
    for j in range(len(vals) - 2, -1, -1):
        idx = jnp.where(vals[j] == m, j, idx)
    return m, idx


def _outproj_router_kernel(*refs, n_in):
    acts, ws = refs[:n_in], refs[n_in:2 * n_in]
    x_ref, lng_ref, lnb_ref, rwt_ref, rb_ref, tri_ref = refs[2 * n_in:2 * n_in + 6]
    y_ref, e_ref, wt_ref, pos_ref, cnt_ref, cnt = refs[2 * n_in + 6:]
    i = pl.program_id(0)

    @pl.when(i == 0)
    def _():
        cnt[...] = jnp.zeros_like(cnt)

    mix = _bdot(acts[0][...], ws[0][...])
    for a, w in zip(acts[1:], ws[1:]):
        mix = mix + _bdot(a[...], w[...])
    y = _layer_norm_rows(ALPHA * x_ref[...] + mix, lng_ref[...], lnb_ref[...])
    y_ref[...] = y

    logit = lax.dot_general(rwt_ref[...], y, (((1,), (1,)), ((), ())), precision=HI, preferred_element_type=F32)
    aff = jax.nn.sigmoid(logit)
    biased = aff + rb_ref[...]
    neg_inf = -jnp.inf
    g_score, g_i1, g_i2 = [], [], []
    for gi in range(N_GROUPS):
        vals = [biased[gi * EXP_PER_GROUP + j:gi * EXP_PER_GROUP + j + 1, :] for j in range(EXP_PER_GROUP)]
        m1, i1 = _first_argmax(vals)
        m2, i2 = _first_argmax([jnp.where(i1 == j, neg_inf, vals[j]) for j in range(EXP_PER_GROUP)])
        g_score.append(m1 + m2)
        g_i1.append(i1)
        g_i2.append(i2)
    _, grp = _first_argmax(g_score)
    loc1, loc2 = g_i1[-1], g_i2[-1]
    for gi in range(N_GROUPS - 2, -1, -1):
        loc1 = jnp.where(grp == gi, g_i1[gi], loc1)
        loc2 = jnp.where(grp == gi, g_i2[gi], loc2)
    e1 = grp * EXP_PER_GROUP + loc1
    e2 = grp * EXP_PER_GROUP + loc2
    eio = lax.broadcasted_iota(I32, aff.shape, 0)
    oh1 = eio == e1
    oh2 = eio == e2
    a1 = jnp.sum(jnp.where(oh1, aff, 0.0), axis=0, keepdims=True)
    a2 = jnp.sum(jnp.where(oh2, aff, 0.0), axis=0, keepdims=True)
    tot = a1 + a2
    e_ref[...] = jnp.concatenate([e1, e2], axis=0)
    wt_ref[...] = jnp.concatenate([a1 / tot, a2 / tot], axis=0)

    ohs = oh1.astype(F32) + oh2.astype(F32)
    before = jnp.dot(ohs.astype(BF16), tri_ref[...], preferred_element_type=F32) + cnt[...]
    p1 = jnp.sum(jnp.where(oh1, before, 0.0), axis=0, keepdims=True)
    p2 = jnp.sum(jnp.where(oh2, before, 0.0), axis=0, keepdims=True)
    pos_ref[...] = jnp.concatenate([p1, p2], axis=0).astype(I32)
    cnt[...] = cnt[...] + jnp.sum(ohs, axis=1, keepdims=True)
    cnt_ref[...] = jnp.broadcast_to(cnt[...], cnt_ref.shape)


def _outproj_router(acts, ws, xres, lng, lnb, rwt, rb, tm=1024):
    T = xres.shape[0]
    n_in = len(acts)
    tri = (lax.broadcasted_iota(I32, (tm, tm), 0) < lax.broadcasted_iota(I32, (tm, tm), 1)).astype(BF16)
    row = lambda i: (i, 0)
    const = lambda a: pl.BlockSpec(a.shape, lambda i: (0,) * a.ndim)
    lane_blk = pl.BlockSpec((TOP_K, tm), lambda i: (0, i))
    return pl.pallas_call(
        functools.partial(_outproj_router_kernel, n_in=n_in),
        grid=(T // tm,),
        in_specs=([pl.BlockSpec((tm, a.shape[1]), row) for a in acts] + [const(w) for w in ws]
                  + [pl.BlockSpec((tm, D_MODEL), row), const(lng), const(lnb), const(rwt), const(rb), const(tri)]),
        out_specs=[pl.BlockSpec((tm, D_MODEL), row), lane_blk, lane_blk, lane_blk,
                   pl.BlockSpec((N_EXPERTS, LANES), lambda i: (0, 0))],
        out_shape=[jax.ShapeDtypeStruct((T, D_MODEL), F32), jax.ShapeDtypeStruct((TOP_K, T), I32),
                   jax.ShapeDtypeStruct((TOP_K, T), F32), jax.ShapeDtypeStruct((TOP_K, T), I32),
                   jax.ShapeDtypeStruct((N_EXPERTS, LANES), F32)],
        scratch_shapes=[pltpu.VMEM((N_EXPERTS, 1), F32)],
        compiler_params=_cparams(("arbitrary",)),
        name="outproj_router",
    )(*acts, *ws, xres, lng, lnb, rwt, rb, tri)


def _dispatch_kernel(zrow_ref, dest_hbm, x_ref, xs_hbm, dsm, zbuf, sem_idx, sem, sem_z):
    i = pl.program_id(0)
    tm = x_ref.shape[0]

    @pl.when(i == 0)
    def _():
        zbuf[...] = jnp.zeros_like(zbuf)
        zero_copy = lambda j: pltpu.make_async_copy(
            zbuf, xs_hbm.at[pl.ds(pl.multiple_of(jnp.maximum(zrow_ref[j], 0), MOE_BLOCK), MOE_BLOCK)], sem_z)
        for j in range(zrow_ref.shape[0]):
            @pl.when(zrow_ref[j] >= 0)
            def _():
                zero_copy(j).start()
        for j in range(zrow_ref.shape[0]):
            @pl.when(zrow_ref[j] >= 0)
            def _():
                zero_copy(j).wait()

    idx_copy = pltpu.make_async_copy(dest_hbm.at[i], dsm, sem_idx)
    idx_copy.start()
    idx_copy.wait()

    def row_copy(r, k):
        return pltpu.make_async_copy(x_ref.at[pl.ds(r, 1)], xs_hbm.at[pl.ds(dsm[k * tm + r], 1)], sem)

    def start(r, c):
        for k in range(TOP_K):
            row_copy(r, k).start()
        return c

    lax.fori_loop(0, tm, start, 0, unroll=ROW_DMA_UNROLL)
    for k in range(TOP_K):
        pltpu.make_async_copy(x_ref, xs_hbm.at[pl.ds(0, tm)], sem).wait()


def _dispatch(zero_rows, dest_tiles, x, rows, tm):
    T = x.shape[0]
    return pl.pallas_call(
        _dispatch_kernel,
        grid_spec=pltpu.PrefetchScalarGridSpec(
            num_scalar_prefetch=1,
            grid=(T // tm,),
            in_specs=[pl.BlockSpec(memory_space=pl.ANY), pl.BlockSpec((tm, D_MODEL), lambda i, z: (i, 0))],
            out_specs=pl.BlockSpec(memory_space=pl.ANY),
            scratch_shapes=[pltpu.SMEM((TOP_K * tm,), I32), pltpu.VMEM((MOE_BLOCK, D_MODEL), F32),
                            pltpu.SemaphoreType.DMA(()), pltpu.SemaphoreType.DMA(()), pltpu.SemaphoreType.DMA(())]),
        out_shape=jax.ShapeDtypeStruct((rows, D_MODEL), F32),
        compiler_params=_cparams(("arbitrary",)),
        name="moe_dispatch",
    )(zero_rows, dest_tiles, x)


def _ffn_kernel(be_ref, nu_ref, xs_ref, wg_ref, wu_ref, wd_ref, y_ref, h_ref):
    del be_ref
    i = pl.program_id(0)
    half = D_EXPERT // 2

    @pl.when(i < nu_ref[0])
    def _():
        xb = xs_ref[...].astype(BF16)
        for c in range(2):
            cs = slice(c * half, (c + 1) * half)
            gt = jnp.dot(xb, wg_ref[:, cs], preferred_element_type=F32)
            up = jnp.dot(xb, wu_ref[:, cs], preferred_element_type=F32)
            h_ref[:, cs] = (jax.nn.silu(gt) * up).astype(BF16)
        y_ref[...] = jnp.dot(h_ref[...], wd_ref[...], preferred_element_type=F32)

    @pl.when(i >= nu_ref[0])
    def _():
        y_ref[...] = jnp.zeros_like(y_ref)


def _ffn(blk_exp, n_used, xs, wg, wu, wd):
    rows = xs.shape[0]
    wspec = lambda: pl.BlockSpec((None, D_MODEL, D_EXPERT), lambda i, be, nu: (be[i], 0, 0))
    return pl.pallas_call(
        _ffn_kernel,
        grid_spec=pltpu.PrefetchScalarGridSpec(
            num_scalar_prefetch=2,
            grid=(rows // MOE_BLOCK,),
            in_specs=[pl.BlockSpec((MOE_BLOCK, D_MODEL), lambda i, be, nu: (i, 0)), wspec(), wspec(),
                      pl.BlockSpec((None, D_EXPERT, D_MODEL), lambda i, be, nu: (be[i], 0, 0))],
            out_specs=pl.BlockSpec((MOE_BLOCK, D_MODEL), lambda i, be, nu: (i, 0)),
            scratch_shapes=[pltpu.VMEM((MOE_BLOCK, D_EXPERT), BF16)]),
        out_shape=jax.ShapeDtypeStruct((rows, D_MODEL), F32),
        compiler_params=_cparams(("arbitrary",)),
        name="moe_ffn",
    )(blk_exp, n_used, xs, wg, wu, wd)


def _combine_kernel(dest_hbm, y_hbm, x_ref, wt_ref, lng_ref, lnb_ref, o_ref, dsm0, dsm1, buf, sem_idx, sem):
    i = pl.program_id(0)
    tm = x_ref.shape[0]
    dsm = (dsm0, dsm1)

    def gather(step, s):
        idx_copy = pltpu.make_async_copy(dest_hbm.at[step], dsm[s], sem_idx)
        idx_copy.start()
        idx_copy.wait()

        def start(r, c):
            for k in range(TOP_K):
                pltpu.make_async_copy(y_hbm.at[pl.ds(dsm[s][k * tm + r], 1)], buf.at[s, k, pl.ds(r, 1)],
                                      sem.at[s]).start()
            return c

        lax.fori_loop(0, tm, start, 0, unroll=ROW_DMA_UNROLL)

    def reduce(s):
        for k in range(TOP_K):
            pltpu.make_async_copy(y_hbm.at[pl.ds(0, tm)], buf.at[s, k], sem.at[s]).wait()
        wt = wt_ref[...]
        z = ALPHA * x_ref[...] + wt[:, 0:1] * buf[s, 0] + wt[:, 1:2] * buf[s, 1]
        o_ref[...] = _layer_norm_rows(z, lng_ref[...], lnb_ref[...])

    @pl.when(i == 0)
    def _():
        gather(0, 0)

    for s in range(2):
        @pl.when(i % 2 == s)
        def _():
            @pl.when(i + 1 < pl.num_programs(0))
            def _():
                gather(i + 1, 1 - s)
            reduce(s)


def _combine(dest_tiles, y, x, wt, lng, lnb, tm):
    T = x.shape[0]
    row = lambda i: (i, 0)
    vec = pl.BlockSpec((1, D_MODEL), lambda i: (0, 0))
    return pl.pallas_call(
        _combine_kernel,
        grid=(T // tm,),
        in_specs=[pl.BlockSpec(memory_space=pl.ANY), pl.BlockSpec(memory_space=pl.ANY),
                  pl.BlockSpec((tm, D_MODEL), row), pl.BlockSpec((tm, TOP_K), row), vec, vec],
        out_specs=pl.BlockSpec((tm, D_MODEL), row),
        out_shape=jax.ShapeDtypeStruct((T, D_MODEL), F32),
        scratch_shapes=[pltpu.SMEM((TOP_K * tm,), I32), pltpu.SMEM((TOP_K * tm,), I32),
                        pltpu.VMEM((2, TOP_K, tm, D_MODEL), F32),
                        pltpu.SemaphoreType.DMA(()), pltpu.SemaphoreType.DMA((2,))],
        compiler_params=_cparams(("arbitrary",)),
        name="moe_combine",
    )(dest_tiles, y, x, wt, lng, lnb)


def _moe(x1, e, wt, pos, cnt, wg, wu, wd, lng, lnb, tm=1024):
    T = x1.shape[0]
    n_blocks = -(-(T * TOP_K) // MOE_BLOCK) + N_EXPERTS
    rows = n_blocks * MOE_BLOCK
    counts = cnt[:, 0].astype(I32)
    padded = (counts + MOE_BLOCK - 1) // MOE_BLOCK * MOE_BLOCK
    pad_end = jnp.cumsum(padded)
    pad_start = pad_end - padded
    dest = pos
    for j in range(N_EXPERTS):
        dest = dest + jnp.where(e == j, pad_start[j], 0)
    blk_start = jnp.arange(n_blocks, dtype=I32) * MOE_BLOCK
    blk_exp = jnp.minimum(jnp.sum((pad_end[None, :] <= blk_start[:, None]).astype(I32), axis=1), N_EXPERTS - 1)
    n_used = (pad_end[-1:] // MOE_BLOCK).astype(I32)
    dest_tiles = dest.reshape(TOP_K, T // tm, tm).transpose(1, 0, 2).reshape(T // tm, TOP_K * tm)
    tail = jnp.where(padded > 0, pad_end - MOE_BLOCK, -1)
    trailing = jnp.arange(n_blocks - N_EXPERTS, n_blocks, dtype=I32)
    trailing = jnp.where(trailing >= n_used[0], trailing * MOE_BLOCK, -1)
    xs = _dispatch(jnp.concatenate([tail, trailing]).astype(I32), dest_tiles, x1, rows, tm)
    y = _ffn(blk_exp, n_used, xs, wg, wu, wd)
    return _combine(dest_tiles, y, x1, wt.T, lng, lnb, tm)


def _proj_rt_kernel(x_ref, w_ref, c_ref, s_ref, q_ref, k_ref, v_ref, g_ref):
    xb = x_ref[...].astype(BF16)
    cs, sn = c_ref[...], s_ref[...]
    rope = lambda y: y * cs + pltpu.roll(y, RT_QK // 2, 1) * sn
    pair = 2 * RT_QK
    for c in range(RT_QKW // pair):
        yq = jnp.dot(xb, w_ref[:, c * pair:(c + 1) * pair], preferred_element_type=F32)
        yk = jnp.dot(xb, w_ref[:, RT_QKW + c * pair:RT_QKW + (c + 1) * pair], preferred_element_type=F32)
        for h in range(2):
            cols = slice(c * pair + h * RT_QK, c * pair + (h + 1) * RT_QK)
            q_ref[:, cols] = rope(yq[:, h * RT_QK:(h + 1) * RT_QK]).astype(BF16)
            k_ref[:, cols] = (rope(yk[:, h * RT_QK:(h + 1) * RT_QK]) * RT_QK ** -0.5).astype(BF16)
    step = 1024
    for c in range(RT_VW // step):
        cols = slice(c * step, (c + 1) * step)
        v_ref[:, cols] = jnp.dot(xb, w_ref[:, 2 * RT_QKW + c * step:2 * RT_QKW + (c + 1) * step],
                                 preferred_element_type=F32).astype(BF16)
        base = 2 * RT_QKW + RT_VW
        g_ref[:, cols] = jax.nn.silu(jnp.dot(xb, w_ref[:, base + c * step:base + (c + 1) * step],
                                             preferred_element_type=F32))


def _proj_rt(xt, w, cs, sn, S, tm=512):
    T = xt.shape[0]
    nseq = S // tm
    row = lambda i: (i, 0)
    tab = lambda i: (i % nseq, 0)
    return pl.pallas_call(
        _proj_rt_kernel,
        grid=(T // tm,),
        in_specs=[pl.BlockSpec((tm, D_MODEL), row), pl.BlockSpec(w.shape, lambda i: (0, 0)),
                  pl.BlockSpec((tm, RT_QK), tab), pl.BlockSpec((tm, RT_QK), tab)],
        out_specs=[pl.BlockSpec((tm, RT_QKW), row), pl.BlockSpec((tm, RT_QKW), row),
                   pl.BlockSpec((tm, RT_VW), row), pl.BlockSpec((tm, RT_VW), row)],
        out_shape=[jax.ShapeDtypeStruct((T, RT_QKW), BF16), jax.ShapeDtypeStruct((T, RT_QKW), BF16),
                   jax.ShapeDtypeStruct((T, RT_VW), BF16), jax.ShapeDtypeStruct((T, RT_VW), F32)],
        compiler_params=_cparams(("parallel",)),
        name="proj_rt",
    )(xt, w, cs, sn)


def _retention_kernel(q_ref, k_ref, v_ref, sg_ref, dec_ref, qd_ref, kd_ref, cd_ref, gng_ref, gnb_ref, o_ref, state):
    j = pl.program_id(1)

    @pl.when(j == 0)
    def _():
        state[...] = jnp.zeros_like(state)

    NB = q_ref.shape[0]
    H = range(NB * RT_HEADS)
    nh = lambda i: (i // RT_HEADS, i % RT_HEADS)
    qk_cols = lambda i: slice(nh(i)[1] * RT_QK, (nh(i)[1] + 1) * RT_QK)
    v_cols = lambda i: slice(nh(i)[1] * RT_V, (nh(i)[1] + 1) * RT_V)
    q = [q_ref[nh(i)[0], :, qk_cols(i)] for i in H]
    k = [k_ref[nh(i)[0], :, qk_cols(i)] for i in H]
    v = [v_ref[nh(i)[0], :, v_cols(i)] for i in H]
    r_old = [state[i] for i in H]
    att = [(_dot_nt(q[i], k[i]) * dec_ref[nh(i)[1]]).astype(BF16) for i in H]
    cross = [jnp.dot(q[i], r_old[i].astype(BF16), preferred_element_type=F32) * qd_ref[nh(i)[1]] for i in H]
    inner = [jnp.dot(att[i], v[i], preferred_element_type=F32) for i in H]
    kdec = [(k[i].astype(F32) * kd_ref[nh(i)[1]]).astype(BF16) for i in H]
    for i in H:
        upd = lax.dot_general(kdec[i], v[i], (((0,), (0,)), ((), ())), preferred_element_type=F32)
        state[i] = r_old[i] * cd_ref[nh(i)[1]][:, 0:1] + upd
    for i in H:
        o = inner[i] + cross[i]
        mu = jnp.mean(o, axis=1, keepdims=True)
        oc = o - mu
        var = jnp.mean(oc * oc, axis=1, keepdims=True)
        on = oc * lax.rsqrt(var + RT_GN_EPS) * gng_ref[:, v_cols(i)] + gnb_ref[:, v_cols(i)]
        o_ref[nh(i)[0], :, v_cols(i)] = (sg_ref[nh(i)[0], :, v_cols(i)] * on).astype(BF16)


def _retention(q3, k3, v3, sg3, dec, qd, kd, cd, gng, gnb, nb=2):
    B, S, _ = q3.shape
    C = RT_CHUNK
    nb = nb if B % nb == 0 else 1
    qk = pl.BlockSpec((nb, C, RT_QKW), lambda b, j: (b, j, 0))
    vv = pl.BlockSpec((nb, C, RT_VW), lambda b, j: (b, j, 0))
    const = lambda a: pl.BlockSpec(a.shape, lambda b, j: (0,) * a.ndim)
    return pl.pallas_call(
        _retention_kernel,
        grid=(B // nb, S // C),
        in_specs=[qk, qk, vv, vv, const(dec), const(qd), const(kd), const(cd), const(gng), const(gnb)],
        out_specs=vv,
        out_shape=jax.ShapeDtypeStruct((B, S, RT_VW), BF16),
        scratch_shapes=[pltpu.VMEM((nb * RT_HEADS, RT_QK, RT_V), F32)],
        compiler_params=_cparams(("parallel", "arbitrary")),
        name="retention",
    )(q3, k3, v3, sg3, dec, qd, kd, cd, gng, gnb)


def _nsa_rope_tables(S):
    half = ROPE_DIM // 2
    inv = ROPE_THETA ** (-jnp.arange(half, dtype=F32) / half)
    ang = jnp.arange(S, dtype=F32)[:, None] * inv[None, :]
    cos, sin = jnp.cos(ang), jnp.sin(ang)
    zeros = lambda n: jnp.zeros((S, n), F32)
    cn = jnp.concatenate([cos, cos, jnp.ones((S, NS_HEAD - ROPE_DIM), F32)], axis=1)
    s1 = jnp.concatenate([-sin, zeros(NS_HEAD - half)], axis=1)
    s2 = jnp.concatenate([zeros(half), sin, zeros(NS_HEAD - ROPE_DIM)], axis=1)
    two = lambda a: jnp.concatenate([a, a], axis=1)
    return two(cn), two(s1), two(s2)


def _rt_rope_tables(S):
    inv = RT_THETA ** (-jnp.linspace(0.0, 1.0, RT_QK // 2, dtype=F32))
    ang = jnp.arange(S, dtype=F32)[:, None] * inv[None, :]
    cos, sin = jnp.cos(ang), jnp.sin(ang)
    return jnp.concatenate([cos, cos], axis=1), jnp.concatenate([-sin, sin], axis=1)


def _rt_decay_tables():
    log_g = jnp.log(1.0 - 2.0 ** (-5.0 - jnp.arange(RT_HEADS, dtype=F32)))
    idx = jnp.arange(RT_CHUNK, dtype=F32)
    diff = idx[:, None] - idx[None, :]
    dec = jnp.where(diff >= 0, jnp.exp(jnp.maximum(diff, 0.0) * log_g[:, None, None]), 0.0)
    qd = jnp.exp((idx + 1.0) * log_g[:, None])[..., None]
    kd = jnp.exp((RT_CHUNK - 1.0 - idx) * log_g[:, None])[..., None]
    cd = jnp.broadcast_to(jnp.exp(RT_CHUNK * log_g)[:, None, None], (RT_HEADS, 1, LANES))
    return dec, qd, kd, cd


def _overlap_table(S, ncp):
    n_cmp = (S - CMP_LEN) // CMP_STRIDE + 1
    n_sel = S // SEL_BLOCK
    cs = jnp.arange(ncp) * CMP_STRIDE
    ss = jnp.arange(LANES) * SEL_BLOCK
    ov = jnp.clip(jnp.minimum(cs[:, None] + CMP_LEN, ss[None, :] + SEL_BLOCK)
                  - jnp.maximum(cs[:, None], ss[None, :]), 0, None).astype(F32) / CMP_LEN
    keep = (jnp.arange(ncp)[:, None] < n_cmp) & (jnp.arange(LANES)[None, :] < n_sel)
    return jnp.where(keep, ov, 0.0)


def kernel(x, ab_w_in, ab_w_out, rk_mu, rk_w0, rk_w1, rk_w2, rk_a0, rk_a1, rk_a2, rk_g1, rk_g2, rk_kk, rk_ka, rk_rk,
           rk_ln, ns_pe, ns_c_w1, ns_c_w2, rt_w_in, rt_w_out, rt_gn, router_w, router_b, moe_w_gate, moe_w_up,
           moe_w_down, ln):
    B, S, D = x.shape
    T = B * S
    assert D == D_MODEL and S % 256 == 0 and S // SEL_BLOCK <= LANES and S >= WINDOW
    xt = x.reshape(T, D)
    rwt = router_w.T
    rb = router_b.reshape(N_EXPERTS, 1)
    vec = lambda a: a.reshape(1, -1)

    w_in = ab_w_in[0]
    n_gate = 3 * NS_HEADS
    w_cat = jnp.concatenate([w_in[:, :-n_gate], jnp.pad(w_in[:, -n_gate:], ((0, 0), (0, LANES - n_gate)))],
                            axis=1).astype(BF16)
    cn, s1, s2 = _nsa_rope_tables(S)
    prk, q, kk2, gt, vt, cx = _proj_ab(xt, w_cat, cn, s1, s2, S)

    ones = (jnp.arange(RK_W)[:, None] // RK_HEAD == jnp.arange(RK_W)[None, :] // RK_HEAD).astype(BF16)
    b16 = lambda a: a.astype(BF16)
    r, lw, km, v, kk, bb, g, bon = _rwkv_prep(
        prk.reshape(B, S, 4 * RK_W), rk_mu[0], vec(rk_w0[0]), b16(rk_w1[0]), b16(rk_w2[0]), vec(rk_a0[0]),
        b16(rk_a1[0]), b16(rk_a2[0]), b16(rk_g1[0]), b16(rk_g2[0]), vec(rk_kk[0]), vec(rk_ka[0]), vec(rk_rk[0]), ones)
    o_a = _rwkv_chunk(r, lw, km, v, kk, bb, g, bon, rk_ln[0, 0:1], rk_ln[0, 1:2])

    ncp = S // CMP_STRIDE
    kcv, kcvt = _nsa_compress(cx.reshape(2, B, ncp, CMP_STRIDE * LANES), ns_c_w1[0], b16(ns_c_w2[0]),
                              ns_pe[0].reshape(2, 1, CMP_LEN * NS_HEAD))
    o_b = _nsa_attn(q.reshape(B, S, NS_W), kcv, kcvt, kk2.reshape(B, S, 2 * LANES),
                    vt.reshape(2, B, S // QT, LANES, QT), gt.reshape(B, S, LANES), _overlap_table(S, ncp).T)

    w_out = b16(ab_w_out[0])
    x1, e, wt, pos, cnt = _outproj_router([o_a.reshape(T, RK_W), o_b.reshape(T, NS_W)], [w_out[:RK_W], w_out[RK_W:]],
                                          xt, ln[0, 0, 0:1], ln[0, 0, 1:2], rwt, rb)
    x2 = _moe(x1, e, wt, pos, cnt, b16(moe_w_gate[0]), b16(moe_w_up[0]), b16(moe_w_down[0]),
              ln[0, 1, 0:1], ln[0, 1, 1:2])

    cs, sn = _rt_rope_tables(S)
    qr, kr, vr, sg = _proj_rt(x2, b16(rt_w_in[0]), cs, sn, S)
    dec, qd, kd, cd = _rt_decay_tables()
    ret = _retention(qr.reshape(B, S, RT_QKW), kr.reshape(B, S, RT_QKW), vr.reshape(B, S, RT_VW),
                     sg.reshape(B, S, RT_VW), dec, qd, kd, cd, rt_gn[0, 0:1], rt_gn[0, 1:2])
    x3, e, wt, pos, cnt = _outproj_router([ret.reshape(T, RT_VW)], [b16(rt_w_out[0])], x2,
                                          ln[1, 0, 0:1], ln[1, 0, 1:2], rwt, rb)
    x4 = _moe(x3, e, wt, pos, cnt, b16(moe_w_gate[1]), b16(moe_w_up[1]), b16(moe_w_down[1]),
              ln[1, 1, 0:1], ln[1, 1, 1:2])
    return x4.reshape(B, S, D)
```

```python
import functools
import math

import jax
import jax.numpy as jnp
from jax import lax
from jax.experimental import pallas as pl
from jax.experimental.pallas import tpu as pltpu

F32 = jnp.float32
BF16 = jnp.bfloat16
I32 = jnp.int32
HI = lax.Precision.HIGHEST

LANES = 128
VMEM_LIMIT = 56 * 1024 * 1024

D_MODEL = 1024
RK_HEADS, RK_HEAD = 8, 64
RK_W = RK_HEADS * RK_HEAD
RK_DECAY_SCALE = 0.606531
RK_LN_EPS = 64e-5
NS_HEADS, NS_KV, NS_HPG, NS_HEAD = 8, 2, 4, 64
NS_W = NS_HEADS * NS_HEAD
CMP_LEN, CMP_STRIDE, SEL_BLOCK, SEL_TOPK, WINDOW = 32, 16, 64, 16, 512
ROPE_THETA = 500000.0
ROPE_DIM = NS_HEAD // 4
Q_SCALE = NS_HEAD ** -0.5 * math.log2(math.e)
QT = 256
KT = 512
RT_HEADS, RT_QK, RT_V = 8, 128, 256
RT_QKW, RT_VW = RT_HEADS * RT_QK, RT_HEADS * RT_V
RT_CHUNK = 128
RT_THETA = 10000.0
RT_GN_EPS = 1e-5
N_EXPERTS, N_GROUPS, EXP_PER_GROUP, TOP_K = 16, 4, 4, 2
D_EXPERT = 1024
MOE_BLOCK = 512
ROW_DMA_UNROLL = 8
DEPTH = 2
ALPHA = (2.0 * DEPTH) ** 0.25
LN_EPS = 1e-5
NEG = -1e30


def _cparams(sem):
    return pltpu.CompilerParams(dimension_semantics=sem, vmem_limit_bytes=VMEM_LIMIT)


def _bdot(a, w):
    return jnp.dot(a.astype(BF16), w, preferred_element_type=F32)


def _dot_nt(a, b):
    return lax.dot_general(a, b, (((1,), (1,)), ((), ())), preferred_element_type=F32)


def _layer_norm_rows(z, g, b):
    mu = jnp.mean(z, axis=1, keepdims=True)
    zc = z - mu
    var = jnp.mean(zc * zc, axis=1, keepdims=True)
    return zc * lax.rsqrt(var + LN_EPS) * g + b


def _proj_ab_kernel(x_ref, w_ref, cn_ref, s1_ref, s2_ref, pick_ref, prk_ref, q_ref, kk_ref, gt_ref, vt_ref, cx_ref):
    xb = x_ref[...].astype(BF16)
    for c in range(4):
        prk_ref[:, c * RK_W:(c + 1) * RK_W] = jnp.dot(xb, w_ref[:, c * RK_W:(c + 1) * RK_W],
                                                      preferred_element_type=F32)
    cn, s1, s2 = cn_ref[...], s1_ref[...], s2_ref[...]

    def rope(y):
        return y * cn + pltpu.roll(y, LANES - ROPE_DIM // 2, 1) * s1 + pltpu.roll(y, ROPE_DIM // 2, 1) * s2

    base = 4 * RK_W
    yq = jnp.dot(xb, w_ref[:, base:base + NS_W], preferred_element_type=F32)
    for c in range(NS_W // LANES):
        q_ref[:, c * LANES:(c + 1) * LANES] = (rope(yq[:, c * LANES:(c + 1) * LANES]) * Q_SCALE).astype(BF16)
    base += NS_W
    ykv = jnp.dot(xb, w_ref[:, base:base + 6 * LANES], preferred_element_type=F32)
    for c in range(6):
        y = ykv[:, c * LANES:(c + 1) * LANES]
        if c % 2 == 0:
            y = rope(y)
        if c < 2:
            yb = y.astype(BF16)
            for l in range(CMP_STRIDE):
                cx_ref[c, :, l * LANES:(l + 1) * LANES] = jnp.dot(pick_ref[l], yb,
                                                                  preferred_element_type=F32).astype(BF16)
        elif c % 2 == 0:
            kk_ref[:, (c // 2 - 1) * LANES:(c // 2) * LANES] = y.astype(BF16)
        else:
            for t in range(y.shape[0] // QT):
                vt_ref[c // 2 - 1, t] = y[t * QT:(t + 1) * QT].T.astype(BF16)
    base += 6 * LANES
    gt_ref[...] = jax.nn.sigmoid(jnp.dot(xb, w_ref[:, base:base + LANES], preferred_element_type=F32))


def _proj_ab(xt, w, cn, s1, s2, S, tm=512):
    T = xt.shape[0]
    ncols = w.shape[1]
    nseq = S // tm
    row = lambda i: (i, 0)
    tab = lambda i: (i % nseq, 0)
    ng = tm // CMP_STRIDE
    pick = (jnp.arange(tm, dtype=I32)[None, None, :]
            == jnp.arange(ng, dtype=I32)[None, :, None] * CMP_STRIDE + jnp.arange(CMP_STRIDE, dtype=I32)[:, None, None])
    pick = pick.astype(BF16)
    return pl.pallas_call(
        _proj_ab_kernel,
        grid=(T // tm,),
        in_specs=[pl.BlockSpec((tm, D_MODEL), row),
                  pl.BlockSpec((D_MODEL, ncols), lambda i: (0, 0)),
                  pl.BlockSpec((tm, LANES), tab), pl.BlockSpec((tm, LANES), tab), pl.BlockSpec((tm, LANES), tab),
                  pl.BlockSpec(pick.shape, lambda i: (0, 0, 0))],
        out_specs=[pl.BlockSpec((tm, 4 * RK_W), row), pl.BlockSpec((tm, NS_W), row),
                   pl.BlockSpec((tm, 2 * LANES), row), pl.BlockSpec((tm, LANES), row),
                   pl.BlockSpec((2, tm // QT, LANES, QT), lambda i: (0, i, 0, 0)),
                   pl.BlockSpec((2, ng, CMP_STRIDE * LANES), lambda i: (0, i, 0))],
        out_shape=[jax.ShapeDtypeStruct((T, 4 * RK_W), F32), jax.ShapeDtypeStruct((T, NS_W), BF16),
                   jax.ShapeDtypeStruct((T, 2 * LANES), BF16), jax.ShapeDtypeStruct((T, LANES), F32),
                   jax.ShapeDtypeStruct((2, T // QT, LANES, QT), BF16),
                   jax.ShapeDtypeStruct((2, T // CMP_STRIDE, CMP_STRIDE * LANES), BF16)],
        compiler_params=_cparams(("parallel",)),
        name="proj_ab",
    )(xt, w, cn, s1, s2, pick)


def _rwkv_prep_kernel(p_ref, mu_ref, w0_ref, w1_ref, w2_ref, a0_ref, a1_ref, a2_ref, g1_ref, g2_ref,
                      kk_ref, ka_ref, rk_ref, ones_ref,
                      r_o, w_o, k_o, v_o, kk_o, b_o, g_o, bon_o, carry):
    j = pl.program_id(1)
    p = p_ref[...]
    tm = p.shape[0]

    @pl.when(j == 0)
    def _():
        carry[...] = jnp.zeros_like(carry)

    rowi = lax.broadcasted_iota(I32, p.shape, 0)
    prev = jnp.where(rowi == 0, carry[...], pltpu.roll(p, 1, 0))
    carry[...] = p[tm - 1:tm, :]
    dp = prev - p
    sl = lambda a, c: a[:, c * RK_W:(c + 1) * RK_W]
    mu = mu_ref[...]
    r = sl(p, 0) + sl(dp, 0) * mu[0:1]
    k = sl(p, 1) + sl(dp, 1) * mu[1:2]
    v = sl(p, 2) + sl(dp, 2) * mu[2:3]
    xw = sl(p, 3) + sl(dp, 3) * mu[3:4]
    xa = sl(p, 3) + sl(dp, 3) * mu[4:5]
    xg = sl(p, 3) + sl(dp, 3) * mu[5:6]
    lw = -RK_DECAY_SCALE * jax.nn.sigmoid(w0_ref[...] + _bdot(jnp.tanh(_bdot(xw, w1_ref[...])), w2_ref[...]))
    a = jax.nn.sigmoid(a0_ref[...] + _bdot(_bdot(xa, a1_ref[...]), a2_ref[...]))
    g = _bdot(jax.nn.sigmoid(_bdot(xg, g1_ref[...])), g2_ref[...])
    ones = ones_ref[...]

    def head_sum(t):
        hi = t.astype(BF16)
        lo = (t - hi.astype(F32)).astype(BF16)
        return (jnp.dot(hi, ones, preferred_element_type=F32) + jnp.dot(lo, ones, preferred_element_type=F32))

    kk = k * kk_ref[...]
    kk = kk / jnp.maximum(jnp.sqrt(head_sum(kk * kk)), 1e-12)
    km = k * (1.0 + (a - 1.0) * ka_ref[...])
    bon = head_sum(r * km * rk_ref[...]) * v
    w_o[...] = lw
    for ref, val in ((r_o, r), (k_o, km), (v_o, v), (kk_o, kk), (b_o, kk * a), (g_o, g), (bon_o, bon)):
        ref[...] = val.astype(ref.dtype)


def _rwkv_prep(prk3, mu, w0, w1, w2, a0, a1, a2, g1, g2, k_k, k_a, r_k, ones, tm=512):
    B, S, _ = prk3.shape
    full = lambda a: pl.BlockSpec(a.shape, lambda b, j: (0,) * a.ndim)
    params = [mu, w0, w1, w2, a0, a1, a2, g1, g2, k_k, k_a, r_k, ones]
    ospec = pl.BlockSpec((None, tm, RK_W), lambda b, j: (b, j, 0))
    return pl.pallas_call(
        _rwkv_prep_kernel,
        grid=(B, S // tm),
        in_specs=[pl.BlockSpec((None, tm, 4 * RK_W), lambda b, j: (b, j, 0))] + [full(a) for a in params],
        out_specs=[ospec] * 8,
        out_shape=[jax.ShapeDtypeStruct((B, S, RK_W), F32 if i == 1 else BF16) for i in range(8)],
        scratch_shapes=[pltpu.VMEM((1, 4 * RK_W), F32)],
        compiler_params=_cparams(("parallel", "arbitrary")),
        name="rwkv_prep",
    )(prk3, *params)


RK_CHUNK = 16


def _pdot(a, b, dims, precise):
    if precise:
        return lax.dot_general(a, b, (dims, ((), ())), precision=HI, preferred_element_type=F32)
    return lax.dot_general(a.astype(BF16), b.astype(BF16), (dims, ((), ())), preferred_element_type=F32)


def _rwkv_chunk_kernel(r_ref, lw_ref, k_ref, v_ref, kk_ref, b_ref, g_ref, bon_ref, lng_ref, lnb_ref, o_ref, ht,
                       *, precise):
    j = pl.program_id(1)

    @pl.when(j == 0)
    def _():
        ht[...] = jnp.zeros_like(ht)

    NB, TT = r_ref.shape[0], r_ref.shape[1]
    C, N = RK_CHUNK, RK_HEAD
    mm = lambda a, b: _pdot(a, b, ((1,), (0,)), precise)
    mm_nt = lambda a, b: _pdot(a, b, ((1,), (1,)), precise)
    mm_tn = lambda a, b: _pdot(a, b, ((0,), (0,)), precise)
    wide = lambda ref: jnp.concatenate([ref[n] for n in range(NB)], axis=1)

    lw = wide(lw_ref)
    rowc = lax.broadcasted_iota(I32, lw.shape, 0) & (C - 1)
    linc, lrev = lw, lw
    sh = 1
    while sh < C:
        linc = linc + jnp.where(rowc >= sh, pltpu.roll(linc, sh, 0), 0.0)
        lrev = lrev + jnp.where(rowc < C - sh, pltpu.roll(lrev, TT - sh, 0), 0.0)
        sh *= 2
    lrev = lrev - lw
    r, k, v, kk, b = wide(r_ref), wide(k_ref), wide(v_ref), wide(kk_ref), wide(b_ref)
    e_in, e_inv, e_rev = jnp.exp(linc), jnp.exp(-linc), jnp.exp(lrev)
    kkd = kk * jnp.exp(linc - lw)
    rd = r * e_in
    binv, kinv = b * e_inv, k * e_inv
    bd, kd = b * e_rev, k * e_rev
    gam = jnp.exp(linc + lrev)

    ti = lax.broadcasted_iota(I32, (TT, TT), 0)
    tj = lax.broadcasted_iota(I32, (TT, TT), 1)
    same = (ti // C) == (tj // C)
    strict = same & (tj < ti)
    incl = same & (tj <= ti)

    H = range(NB * RK_HEADS)
    lo = lambda a: a if precise else a.astype(BF16)
    hs = lambda a: [a[:, h * N:(h + 1) * N] for h in H]
    rows2 = lambda a, b: jnp.concatenate([a, b], axis=0)
    kkd_h, rd_h, v_h = hs(lo(kkd)), hs(lo(rd)), hs(lo(v))
    binv_h, kinv_h, bd_h, kd_h = hs(lo(binv)), hs(lo(kinv)), hs(lo(bd)), hs(lo(kd))
    gam_h = hs(gam)
    gm = [mm_nt(rows2(kkd_h[h], rd_h[h]), rows2(binv_h[h], kinv_h[h])) for h in H]
    a_b = [lo(jnp.where(strict, gm[h][:TT, :TT], 0.0)) for h in H]
    b_rb = [lo(jnp.where(incl, gm[h][TT:, :TT], 0.0)) for h in H]
    akb = [lo(rows2(jnp.where(strict, gm[h][:TT, TT:], 0.0), jnp.where(incl, gm[h][TT:, TT:], 0.0))) for h in H]
    av = [mm(akb[h], v_h[h]) for h in H]
    x = [jnp.concatenate([kkd_h[h].astype(F32), av[h][:TT]], axis=1) for h in H]
    a2 = [lo(mm(a_b[h], a_b[h])) for h in H]
    a4 = [lo(mm(a2[h], a2[h])) for h in H]
    a8 = [lo(mm(a4[h], a4[h])) for h in H]
    x = [x[h] + mm(a8[h], lo(x[h])) for h in H]
    x = [x[h] + mm(a4[h], lo(x[h])) for h in H]
    x = [x[h] + mm(a2[h], lo(x[h])) for h in H]
    x = [x[h] - mm(a_b[h], lo(x[h])) for h in H]
    wt = [lo(x[h][:, :N]) for h in H]
    h_t = [ht[h] for h in H]
    us = [[] for _ in H]
    rhs = [[] for _ in H]
    for c in range(TT // C):
        rs = slice(c * C, (c + 1) * C)
        xh = [mm_nt(rows2(wt[h][rs], rd_h[h][rs]), lo(h_t[h])) for h in H]
        for h in H:
            u_c = -(xh[h][:C] + x[h][rs, N:])
            us[h].append(u_c)
            rhs[h].append(xh[h][C:])
        upd = [mm_tn(rows2(lo(us[h][c]), v_h[h][rs]), rows2(bd_h[h][rs], kd_h[h][rs])) for h in H]
        h_t = [h_t[h] * gam_h[h][c * C:c * C + 1] + upd[h] for h in H]
    outs = []
    for h in H:
        ht[h] = h_t[h]
        o = jnp.concatenate(rhs[h], axis=0) + mm(b_rb[h], lo(jnp.concatenate(us[h], axis=0))) + av[h][TT:]
        mu = jnp.mean(o, axis=1, keepdims=True)
        oc = o - mu
        var = jnp.mean(oc * oc, axis=1, keepdims=True)
        outs.append(oc * lax.rsqrt(var + RK_LN_EPS))
    for n in range(NB):
        on = jnp.concatenate(outs[n * RK_HEADS:(n + 1) * RK_HEADS], axis=1)
        o_ref[n] = (on * lng_ref[...] + lnb_ref[...] + bon_ref[n]) * g_ref[n]


def _rwkv_chunk(r, lw, k, v, kk, bb, g, bon, lng, lnb, tt=128, nb=4, precise=False):
    B, S, _ = r.shape
    nb = nb if B % nb == 0 else 1
    blk = pl.BlockSpec((nb, tt, RK_W), lambda b, j: (b, j, 0))
    vec = pl.BlockSpec((1, RK_W), lambda b, j: (0, 0))
    return pl.pallas_call(
        functools.partial(_rwkv_chunk_kernel, precise=precise),
        grid=(B // nb, S // tt),
        in_specs=[blk] * 8 + [vec, vec],
        out_specs=blk,
        out_shape=jax.ShapeDtypeStruct((B, S, RK_W), F32),
        scratch_shapes=[pltpu.VMEM((nb * RK_HEADS, RK_HEAD, RK_HEAD), F32)],
        compiler_params=_cparams(("parallel", "arbitrary")),
        name="rwkv_chunk",
    )(r, lw, k, v, kk, bb, g, bon, lng, lnb)


def _nsa_compress_kernel(x_ref, w1g_ref, w1f_ref, w2_ref, pe_ref, o_ref, ot_ref):
    bias = jnp.dot(jnp.broadcast_to(pe_ref[...], (8, CMP_LEN * NS_HEAD)), w1f_ref[...], precision=HI,
                   preferred_element_type=F32)[0:1]
    x = x_ref[...]
    n = x.shape[0]
    outs = []
    for g in range(NS_KV):
        ya = jnp.dot(x, w1g_ref[g, 0], preferred_element_type=F32)
        yb = jnp.dot(x, w1g_ref[g, 1], preferred_element_type=F32)
        h = ya + pltpu.roll(yb, n - 1, 0) + bias
        outs.append(_bdot(jax.nn.gelu(h), w2_ref[...]))
    out = jnp.concatenate(outs, axis=1)
    o_ref[...] = out.astype(BF16)
    ot_ref[...] = out.T.astype(BF16)


def _nsa_compress(cx, w1, w2b, pe):
    _, B, ncp, width = cx.shape
    hid = w1.shape[-1]
    w1h = w1.reshape(2, 2, CMP_STRIDE, 1, NS_HEAD, hid)
    zero = jnp.zeros_like(w1h)
    w1g = jnp.stack([jnp.concatenate([w1h, zero], axis=3), jnp.concatenate([zero, w1h], axis=3)], axis=1)
    w1g = w1g.reshape(2, NS_KV, 2, width, hid).astype(BF16)
    return pl.pallas_call(
        _nsa_compress_kernel,
        grid=(B, 2),
        in_specs=[pl.BlockSpec((None, None, ncp, width), lambda b, c: (c, b, 0, 0)),
                  pl.BlockSpec((None, NS_KV, 2, width, hid), lambda b, c: (c, 0, 0, 0, 0)),
                  pl.BlockSpec((None, CMP_LEN * NS_HEAD, hid), lambda b, c: (c, 0, 0)),
                  pl.BlockSpec((None, hid, NS_HEAD), lambda b, c: (c, 0, 0)),
                  pl.BlockSpec((None, 1, CMP_LEN * NS_HEAD), lambda b, c: (c, 0, 0))],
        out_specs=[pl.BlockSpec((None, None, ncp, LANES), lambda b, c: (b, c, 0, 0)),
                   pl.BlockSpec((None, None, LANES, ncp), lambda b, c: (b, c, 0, 0))],
        out_shape=[jax.ShapeDtypeStruct((B, 2, ncp, LANES), BF16), jax.ShapeDtypeStruct((B, 2, LANES, ncp), BF16)],
        compiler_params=_cparams(("parallel", "parallel")),
        name="nsa_compress",
    )(cx, w1g, w1, w2b, pe)


def _nsa_attn_kernel(q_ref, kc_ref, vct_ref, ks_ref, vst_ref, kw_ref, vwt_ref, gt_ref, ovt_ref, ext_ref, o_ref,
                     m_s, l_s, acc_s, m_w, l_w, acc_w, s_buf, *, ncp):
    i = pl.program_id(1)
    s0 = i * QT
    heads = range(NS_HEADS)
    hcols = lambda h: slice(h * QT, (h + 1) * QT)
    gcols = lambda h: slice(h // NS_HPG * QT, (h // NS_HPG + 1) * QT)
    iota = lambda shape, d: lax.broadcasted_iota(I32, shape, d)
    qpos = lambda shape: s0 + (iota(shape, 1) & (QT - 1))

    def col_reduce(x, op, final):
        n = x.shape[0]
        while n > 8:
            n //= 2
            x = op(x[:n], x[n:])
        return final(x, axis=0, keepdims=True)

    q8 = q_ref[...].astype(F32)
    zeros = jnp.zeros((QT, NS_HEAD), F32)
    qt = []
    for h in heads:
        qh = q8[:, h * NS_HEAD:(h + 1) * NS_HEAD]
        qt.append(jnp.concatenate([qh, zeros] if h < NS_HPG else [zeros, qh], axis=1).T)
    qt = jnp.concatenate(qt, axis=1).astype(BF16)

    sc = jnp.dot(kc_ref[...], qt, preferred_element_type=F32)
    n_row = iota((ncp, QT), 0)
    cmask = (n_row * CMP_STRIDE + (CMP_LEN - 1) <= s0 + iota((ncp, QT), 1)) & (n_row < ncp - 1)
    cpen = jnp.where(cmask, 0.0, NEG)
    sc = jnp.concatenate([sc[:, hcols(h)] + cpen for h in heads], axis=1)
    mc = col_reduce(sc, jnp.maximum, jnp.max)
    pc = jnp.exp2(sc - mc)
    lc = col_reduce(pc, jnp.add, jnp.sum)
    pc = pc * jnp.where(mc > 0.5 * NEG, 1.0 / lc, 0.0)
    pcs = jnp.concatenate([functools.reduce(jnp.add, [pc[:, hcols(h)] for h in range(g * NS_HPG, (g + 1) * NS_HPG)])
                           for g in range(NS_KV)], axis=1)

    def group_dot(vt, p, extra=None):
        half = NS_HPG * QT
        outs = []
        for g in range(NS_KV):
            vg = vt[g * NS_HEAD:(g + 1) * NS_HEAD]
            if extra is not None:
                vg = jnp.concatenate([vg, extra], axis=0)
            outs.append(jnp.dot(vg, p[:, g * half:(g + 1) * half], preferred_element_type=F32))
        return jnp.concatenate(outs, axis=1)

    o_c = group_dot(vct_ref[...], pc.astype(BF16))

    imp = jnp.dot(ovt_ref[...], pcs, precision=HI, preferred_element_type=F32)
    blk = iota((LANES, NS_KV * QT), 0)
    cur = qpos((LANES, NS_KV * QT)) // SEL_BLOCK
    valid = blk <= cur
    forced = (blk == 0) | (blk == cur) | (blk == cur - 1)
    pri = jnp.where(valid & ~forced, imp, -jnp.inf)
    picked = forced
    blkf = blk.astype(F32)
    for _ in range(SEL_TOPK - 3):
        mx = col_reduce(pri, jnp.maximum, jnp.max)
        hit = blkf == col_reduce(jnp.where(pri == mx, blkf, float(LANES)), jnp.minimum, jnp.min)
        picked = picked | hit
        pri = jnp.where(hit, -jnp.inf, pri)
    selpen = jnp.where(picked & valid, 0.0, NEG).astype(BF16)
    wq = jnp.concatenate([qt, jnp.concatenate([selpen[:, gcols(h)] for h in heads], axis=1)], axis=0)

    def attend(s, vt, pen, m_ref, l_ref, acc_ref):
        if pen is not None:
            s = jnp.concatenate([s[:, hcols(h)] + pen for h in heads], axis=1)
        m_old = m_ref[...]
        m_new = jnp.maximum(m_old, col_reduce(s, jnp.maximum, jnp.max))
        alpha = jnp.exp2(m_old - m_new)
        p = jnp.exp2(s - m_new).astype(BF16)
        m_ref[...] = m_new
        pv = group_dot(vt, p, extra=jnp.ones((16, vt.shape[1]), BF16))
        l_ref[...] = alpha * l_ref[...] + pv[NS_HEAD:NS_HEAD + 1]
        acc_ref[...] = alpha * acc_ref[...] + pv[:NS_HEAD]

    def reset(m_ref, l_ref, acc_ref):
        m_ref[...] = jnp.full(m_ref.shape, NEG, F32)
        l_ref[...] = jnp.zeros_like(l_ref)
        acc_ref[...] = jnp.zeros_like(acc_ref)

    tiles = lambda ref, t0, n: jnp.concatenate([ref[t0 + c] for c in range(n)], axis=1)

    reset(m_s, l_s, acc_s)
    diag = i // (KT // QT)

    def sel_scores(kt, slot):
        k0 = pl.multiple_of(kt * KT, KT)
        keys = jnp.concatenate([ks_ref[pl.ds(k0, KT), :], ext_ref[pl.ds(k0, KT), :]], axis=1)
        s_buf[slot] = jnp.dot(keys, wq, preferred_element_type=F32)

    def sel_attend(kt, slot, causal):
        pen = None
        if causal:
            pen = jnp.where(kt * KT + iota((KT, QT), 0) <= s0 + iota((KT, QT), 1), 0.0, NEG)
        attend(s_buf[slot], tiles(vst_ref, kt * (KT // QT), KT // QT), pen, m_s, l_s, acc_s)

    def sel_pair(j, c):
        sel_scores(2 * j + 1, 1)
        sel_attend(2 * j, 0, False)
        sel_scores(2 * j + 2, 0)
        sel_attend(2 * j + 1, 1, False)
        return c

    d0 = pl.multiple_of(s0, QT)
    a0 = pl.multiple_of(jnp.maximum(s0 - WINDOW, 0), QT)
    s_d = jnp.dot(kw_ref[pl.ds(d0, QT), :], qt, preferred_element_type=F32)
    s_a = jnp.dot(kw_ref[pl.ds(a0, WINDOW), :], qt, preferred_element_type=F32)
    pen_d = jnp.where(iota((QT, QT), 0) <= iota((QT, QT), 1), 0.0, NEG)
    kpos = a0 + iota((WINDOW, QT), 0)
    pen_a = jnp.where((kpos < s0) & (kpos > s0 + iota((WINDOW, QT), 1) - WINDOW), 0.0, NEG)

    sel_scores(0, 0)
    reset(m_w, l_w, acc_w)
    attend(s_d, vwt_ref[i], pen_d, m_w, l_w, acc_w)
    attend(s_a, tiles(vwt_ref, a0 // QT, WINDOW // QT), pen_a, m_w, l_w, acc_w)
    lax.fori_loop(0, diag // 2, sel_pair, 0)

    @pl.when(diag % 2 == 1)
    def _():
        sel_scores(diag, 1)
        sel_attend(diag - 1, 0, False)
        sel_attend(diag, 1, True)

    @pl.when(diag % 2 == 0)
    def _():
        sel_attend(diag, 0, True)

    gtt = gt_ref[...].T
    o_s = acc_s[...] / l_s[...]
    o_w = acc_w[...] / l_w[...]
    outs = []
    for h in heads:
        gate = lambda br: gtt[br * NS_HEADS + h:br * NS_HEADS + h + 1]
        outs.append(gate(0) * o_c[:, hcols(h)] + gate(1) * o_s[:, hcols(h)] + gate(2) * o_w[:, hcols(h)])
    pairs = [jnp.concatenate(outs[p:p + 2], axis=0).T for p in range(0, NS_HEADS, 2)]
    o_ref[...] = jnp.concatenate(pairs, axis=1).astype(BF16)


def _nsa_attn(q3, kcv, kcvt, kv3, vt, gt3, ovt):
    B, S, _ = q3.shape
    ncp = kcv.shape[2]
    R = NS_HEADS * QT
    ext = (jnp.arange(S, dtype=I32)[:, None] // SEL_BLOCK == jnp.arange(LANES, dtype=I32)[None, :]).astype(BF16)
    seq = lambda c: pl.BlockSpec((None, S, LANES), lambda b, i, c=c: (b, 0, c))
    seqt = lambda c: pl.BlockSpec((None, None, S // QT, LANES, QT), lambda b, i, c=c: (c, b, 0, 0, 0))
    return pl.pallas_call(
        functools.partial(_nsa_attn_kernel, ncp=ncp),
        grid=(B, S // QT),
        in_specs=[pl.BlockSpec((None, QT, NS_W), lambda b, i: (b, i, 0)),
                  pl.BlockSpec((None, None, ncp, LANES), lambda b, i: (b, 0, 0, 0)),
                  pl.BlockSpec((None, None, LANES, ncp), lambda b, i: (b, 1, 0, 0)),
                  seq(0), seqt(0), seq(1), seqt(1),
                  pl.BlockSpec((None, QT, LANES), lambda b, i: (b, i, 0)),
                  pl.BlockSpec(ovt.shape, lambda b, i: (0, 0)),
                  pl.BlockSpec(ext.shape, lambda b, i: (0, 0))],
        out_specs=pl.BlockSpec((None, QT, NS_W), lambda b, i: (b, i, 0)),
        out_shape=jax.ShapeDtypeStruct((B, S, NS_W), BF16),
        scratch_shapes=([pltpu.VMEM((1, R), F32), pltpu.VMEM((1, R), F32), pltpu.VMEM((NS_HEAD, R), F32)] * 2
                        + [pltpu.VMEM((2, KT, R), F32)]),
        compiler_params=_cparams(("parallel", "arbitrary")),
        name="nsa_attn",
    )(q3, kcv, kcvt, kv3, vt, kv3, vt, gt3, ovt, ext)


def _first_argmax(vals):
    m = vals[0]
    for v in vals[1:]:
        m = jnp.maximum(m, v)
    idx = jnp.full(m.shape, len(vals) - 1, I32)
    for j in range(len(vals) - 2, -1, -1):
        idx = jnp.where(vals[j] == m, j, idx)
    return m, idx


def _outproj_router_kernel(*refs, n_in):
    acts, ws = refs[:n_in], refs[n_in:2 * n_in]
    x_ref, lng_ref, lnb_ref, rwt_ref, rb_ref, tri_ref = refs[2 * n_in:2 * n_in + 6]
    y_ref, e_ref, wt_ref, pos_ref, cnt_ref, cnt = refs[2 * n_in + 6:]
    i = pl.program_id(0)

    @pl.when(i == 0)
    def _():
        cnt[...] = jnp.zeros_like(cnt)

    mix = _bdot(acts[0][...], ws[0][...])
    for a, w in zip(acts[1:], ws[1:]):
        mix = mix + _bdot(a[...], w[...])
    y = _layer_norm_rows(ALPHA * x_ref[...] + mix, lng_ref[...], lnb_ref[...])
    y_ref[...] = y

    logit = lax.dot_general(rwt_ref[...], y, (((1,), (1,)), ((), ())), precision=HI, preferred_element_type=F32)
    aff = jax.nn.sigmoid(logit)
    biased = aff + rb_ref[...]
    neg_inf = -jnp.inf
    g_score, g_i1, g_i2 = [], [], []
    for gi in range(N_GROUPS):
        vals = [biased[gi * EXP_PER_GROUP + j:gi * EXP_PER_GROUP + j + 1, :] for j in range(EXP_PER_GROUP)]
        m1, i1 = _first_argmax(vals)
        m2, i2 = _first_argmax([jnp.where(i1 == j, neg_inf, vals[j]) for j in range(EXP_PER_GROUP)])
        g_score.append(m1 + m2)
        g_i1.append(i1)
        g_i2.append(i2)
    _, grp = _first_argmax(g_score)
    loc1, loc2 = g_i1[-1], g_i2[-1]
    for gi in range(N_GROUPS - 2, -1, -1):
        loc1 = jnp.where(grp == gi, g_i1[gi], loc1)
        loc2 = jnp.where(grp == gi, g_i2[gi], loc2)
    e1 = grp * EXP_PER_GROUP + loc1
    e2 = grp * EXP_PER_GROUP + loc2
    eio = lax.broadcasted_iota(I32, aff.shape, 0)
    oh1 = eio == e1
    oh2 = eio == e2
    a1 = jnp.sum(jnp.where(oh1, aff, 0.0), axis=0, keepdims=True)
    a2 = jnp.sum(jnp.where(oh2, aff, 0.0), axis=0, keepdims=True)
    tot = a1 + a2
    e_ref[...] = jnp.concatenate([e1, e2], axis=0)
    wt_ref[...] = jnp.concatenate([a1 / tot, a2 / tot], axis=0)

    ohs = oh1.astype(F32) + oh2.astype(F32)
    before = jnp.dot(ohs.astype(BF16), tri_ref[...], preferred_element_type=F32) + cnt[...]
    p1 = jnp.sum(jnp.where(oh1, before, 0.0), axis=0, keepdims=True)
    p2 = jnp.sum(jnp.where(oh2, before, 0.0), axis=0, keepdims=True)
    pos_ref[...] = jnp.concatenate([p1, p2], axis=0).astype(I32)
    cnt[...] = cnt[...] + jnp.sum(ohs, axis=1, keepdims=True)
    cnt_ref[...] = jnp.broadcast_to(cnt[...], cnt_ref.shape)


def _outproj_router(acts, ws, xres, lng, lnb, rwt, rb, tm=1024):
    T = xres.shape[0]
    n_in = len(acts)
    tri = (lax.broadcasted_iota(I32, (tm, tm), 0) < lax.broadcasted_iota(I32, (tm, tm), 1)).astype(BF16)
    row = lambda i: (i, 0)
    const = lambda a: pl.BlockSpec(a.shape, lambda i: (0,) * a.ndim)
    lane_blk = pl.BlockSpec((TOP_K, tm), lambda i: (0, i))
    return pl.pallas_call(
        functools.partial(_outproj_router_kernel, n_in=n_in),
        grid=(T // tm,),
        in_specs=([pl.BlockSpec((tm, a.shape[1]), row) for a in acts] + [const(w) for w in ws]
                  + [pl.BlockSpec((tm, D_MODEL), row), const(lng), const(lnb), const(rwt), const(rb), const(tri)]),
        out_specs=[pl.BlockSpec((tm, D_MODEL), row), lane_blk, lane_blk, lane_blk,
                   pl.BlockSpec((N_EXPERTS, LANES), lambda i: (0, 0))],
        out_shape=[jax.ShapeDtypeStruct((T, D_MODEL), F32), jax.ShapeDtypeStruct((TOP_K, T), I32),
                   jax.ShapeDtypeStruct((TOP_K, T), F32), jax.ShapeDtypeStruct((TOP_K, T), I32),
                   jax.ShapeDtypeStruct((N_EXPERTS, LANES), F32)],
        scratch_shapes=[pltpu.VMEM((N_EXPERTS, 1), F32)],
        compiler_params=_cparams(("arbitrary",)),
        name="outproj_router",
    )(*acts, *ws, xres, lng, lnb, rwt, rb, tri)


def _dispatch_kernel(zrow_ref, dest_hbm, x_ref, xs_hbm, dsm, zbuf, sem_idx, sem, sem_z):
    i = pl.program_id(0)
    tm = x_ref.shape[0]

    @pl.when(i == 0)
    def _():
        zbuf[...] = jnp.zeros_like(zbuf)
        zero_copy = lambda j: pltpu.make_async_copy(
            zbuf, xs_hbm.at[pl.ds(pl.multiple_of(jnp.maximum(zrow_ref[j], 0), MOE_BLOCK), MOE_BLOCK)], sem_z)
        for j in range(zrow_ref.shape[0]):
            @pl.when(zrow_ref[j] >= 0)
            def _():
                zero_copy(j).start()
        for j in range(zrow_ref.shape[0]):
            @pl.when(zrow_ref[j] >= 0)
            def _():
                zero_copy(j).wait()

    idx_copy = pltpu.make_async_copy(dest_hbm.at[i], dsm, sem_idx)
    idx_copy.start()
    idx_copy.wait()

    def row_copy(r, k):
        return pltpu.make_async_copy(x_ref.at[pl.ds(r, 1)], xs_hbm.at[pl.ds(dsm[k * tm + r], 1)], sem)

    def start(r, c):
        for k in range(TOP_K):
            row_copy(r, k).start()
        return c

    lax.fori_loop(0, tm, start, 0, unroll=ROW_DMA_UNROLL)
    for k in range(TOP_K):
        pltpu.make_async_copy(x_ref, xs_hbm.at[pl.ds(0, tm)], sem).wait()


def _dispatch(zero_rows, dest_tiles, x, rows, tm):
    T = x.shape[0]
    return pl.pallas_call(
        _dispatch_kernel,
        grid_spec=pltpu.PrefetchScalarGridSpec(
            num_scalar_prefetch=1,
            grid=(T // tm,),
            in_specs=[pl.BlockSpec(memory_space=pl.ANY), pl.BlockSpec((tm, D_MODEL), lambda i, z: (i, 0))],
            out_specs=pl.BlockSpec(memory_space=pl.ANY),
            scratch_shapes=[pltpu.SMEM((TOP_K * tm,), I32), pltpu.VMEM((MOE_BLOCK, D_MODEL), F32),
                            pltpu.SemaphoreType.DMA(()), pltpu.SemaphoreType.DMA(()), pltpu.SemaphoreType.DMA(())]),
        out_shape=jax.ShapeDtypeStruct((rows, D_MODEL), F32),
        compiler_params=_cparams(("arbitrary",)),
        name="moe_dispatch",
    )(zero_rows, dest_tiles, x)


def _ffn_kernel(be_ref, nu_ref, xs_ref, wg_ref, wu_ref, wd_ref, y_ref, h_ref, wg_b, wu_b, wd_b):
    i = pl.program_id(0)
    half = D_EXPERT // 2
    used = i < nu_ref[0]

    @pl.when(used & ((i == 0) | (be_ref[i] != be_ref[jnp.maximum(i - 1, 0)])))
    def _():
        wg_b[...] = wg_ref[...].astype(BF16)
        wu_b[...] = wu_ref[...].astype(BF16)
        wd_b[...] = wd_ref[...].astype(BF16)

    @pl.when(used)
    def _():
        xb = xs_ref[...].astype(BF16)
        for c in range(2):
            cs = slice(c * half, (c + 1) * half)
            gt = jnp.dot(xb, wg_b[:, cs], preferred_element_type=F32)
            up = jnp.dot(xb, wu_b[:, cs], preferred_element_type=F32)
            h_ref[:, cs] = (jax.nn.silu(gt) * up).astype(BF16)
        y_ref[...] = jnp.dot(h_ref[...], wd_b[...], preferred_element_type=F32)

    @pl.when(i >= nu_ref[0])
    def _():
        y_ref[...] = jnp.zeros_like(y_ref)


def _ffn(blk_exp, n_used, xs, wg, wu, wd, layer):
    rows = xs.shape[0]
    wspec = lambda a: pl.BlockSpec((None, None) + a.shape[2:], lambda i, be, nu: (layer, be[i], 0, 0))
    wbuf = lambda a: pltpu.VMEM(a.shape[2:], BF16)
    return pl.pallas_call(
        _ffn_kernel,
        grid_spec=pltpu.PrefetchScalarGridSpec(
            num_scalar_prefetch=2,
            grid=(rows // MOE_BLOCK,),
            in_specs=[pl.BlockSpec((MOE_BLOCK, D_MODEL), lambda i, be, nu: (i, 0)), wspec(wg), wspec(wu), wspec(wd)],
            out_specs=pl.BlockSpec((MOE_BLOCK, D_MODEL), lambda i, be, nu: (i, 0)),
            scratch_shapes=[pltpu.VMEM((MOE_BLOCK, D_EXPERT), BF16), wbuf(wg), wbuf(wu), wbuf(wd)]),
        out_shape=jax.ShapeDtypeStruct((rows, D_MODEL), F32),
        compiler_params=_cparams(("arbitrary",)),
        name="moe_ffn",
    )(blk_exp, n_used, xs, wg, wu, wd)


def _combine_kernel(dest_hbm, y_hbm, x_ref, wt_ref, lng_ref, lnb_ref, o_ref, dsm0, dsm1, buf, sem_idx, sem):
    i = pl.program_id(0)
    tm = x_ref.shape[0]
    dsm = (dsm0, dsm1)

    def gather(step, s):
        idx_copy = pltpu.make_async_copy(dest_hbm.at[step], dsm[s], sem_idx)
        idx_copy.start()
        idx_copy.wait()

        def start(r, c):
            for k in range(TOP_K):
                pltpu.make_async_copy(y_hbm.at[pl.ds(dsm[s][k * tm + r], 1)], buf.at[s, k, pl.ds(r, 1)],
                                      sem.at[s]).start()
            return c

        lax.fori_loop(0, tm, start, 0, unroll=ROW_DMA_UNROLL)

    def reduce(s):
        for k in range(TOP_K):
            pltpu.make_async_copy(y_hbm.at[pl.ds(0, tm)], buf.at[s, k], sem.at[s]).wait()
        wt = wt_ref[...]
        z = ALPHA * x_ref[...] + wt[:, 0:1] * buf[s, 0] + wt[:, 1:2] * buf[s, 1]
        o_ref[...] = _layer_norm_rows(z, lng_ref[...], lnb_ref[...])

    @pl.when(i == 0)
    def _():
        gather(0, 0)

    for s in range(2):
        @pl.when(i % 2 == s)
        def _():
            @pl.when(i + 1 < pl.num_programs(0))
            def _():
                gather(i + 1, 1 - s)
            reduce(s)


def _combine(dest_tiles, y, x, wt, lng, lnb, tm):
    T = x.shape[0]
    row = lambda i: (i, 0)
    vec = pl.BlockSpec((1, D_MODEL), lambda i: (0, 0))
    return pl.pallas_call(
        _combine_kernel,
        grid=(T // tm,),
        in_specs=[pl.BlockSpec(memory_space=pl.ANY), pl.BlockSpec(memory_space=pl.ANY),
                  pl.BlockSpec((tm, D_MODEL), row), pl.BlockSpec((tm, TOP_K), row), vec, vec],
        out_specs=pl.BlockSpec((tm, D_MODEL), row),
        out_shape=jax.ShapeDtypeStruct((T, D_MODEL), F32),
        scratch_shapes=[pltpu.SMEM((TOP_K * tm,), I32), pltpu.SMEM((TOP_K * tm,), I32),
                        pltpu.VMEM((2, TOP_K, tm, D_MODEL), F32),
                        pltpu.SemaphoreType.DMA(()), pltpu.SemaphoreType.DMA((2,))],
        compiler_params=_cparams(("arbitrary",)),
        name="moe_combine",
    )(dest_tiles, y, x, wt, lng, lnb)


def _moe(x1, e, wt, pos, cnt, wg, wu, wd, layer, lng, lnb, tm=1024):
    T = x1.shape[0]
    n_blocks = -(-(T * TOP_K) // MOE_BLOCK) + N_EXPERTS
    rows = n_blocks * MOE_BLOCK
    counts = cnt[:, 0].astype(I32)
    padded = (counts + MOE_BLOCK - 1) // MOE_BLOCK * MOE_BLOCK
    pad_end = jnp.cumsum(padded)
    pad_start = pad_end - padded
    dest = pos
    for j in range(N_EXPERTS):
        dest = dest + jnp.where(e == j, pad_start[j], 0)
    blk_start = jnp.arange(n_blocks, dtype=I32) * MOE_BLOCK
    blk_exp = jnp.minimum(jnp.sum((pad_end[None, :] <= blk_start[:, None]).astype(I32), axis=1), N_EXPERTS - 1)
    n_used = (pad_end[-1:] // MOE_BLOCK).astype(I32)
    dest_tiles = dest.reshape(TOP_K, T // tm, tm).transpose(1, 0, 2).reshape(T // tm, TOP_K * tm)
    tail = jnp.where(padded > 0, pad_end - MOE_BLOCK, -1)
    trailing = jnp.arange(n_blocks - N_EXPERTS, n_blocks, dtype=I32)
    trailing = jnp.where(trailing >= n_used[0], trailing * MOE_BLOCK, -1)
    xs = _dispatch(jnp.concatenate([tail, trailing]).astype(I32), dest_tiles, x1, rows, tm)
    y = _ffn(blk_exp, n_used, xs, wg, wu, wd, layer)
    return _combine(dest_tiles, y, x1, wt.T, lng, lnb, tm)


def _proj_rt_kernel(x_ref, w_ref, c_ref, s_ref, q_ref, k_ref, v_ref, g_ref):
    xb = x_ref[...].astype(BF16)
    cs, sn = c_ref[...], s_ref[...]
    rope = lambda y: y * cs + pltpu.roll(y, RT_QK // 2, 1) * sn
    pair = 2 * RT_QK
    for c in range(RT_QKW // pair):
        yq = jnp.dot(xb, w_ref[:, c * pair:(c + 1) * pair], preferred_element_type=F32)
        yk = jnp.dot(xb, w_ref[:, RT_QKW + c * pair:RT_QKW + (c + 1) * pair], preferred_element_type=F32)
        for h in range(2):
            cols = slice(c * pair + h * RT_QK, c * pair + (h + 1) * RT_QK)
            q_ref[:, cols] = rope(yq[:, h * RT_QK:(h + 1) * RT_QK]).astype(BF16)
            k_ref[:, cols] = (rope(yk[:, h * RT_QK:(h + 1) * RT_QK]) * RT_QK ** -0.5).astype(BF16)
    step = 1024
    for c in range(RT_VW // step):
        cols = slice(c * step, (c + 1) * step)
        v_ref[:, cols] = jnp.dot(xb, w_ref[:, 2 * RT_QKW + c * step:2 * RT_QKW + (c + 1) * step],
                                 preferred_element_type=F32).astype(BF16)
        base = 2 * RT_QKW + RT_VW
        g_ref[:, cols] = jax.nn.silu(jnp.dot(xb, w_ref[:, base + c * step:base + (c + 1) * step],
                                             preferred_element_type=F32))


def _proj_rt(xt, w, cs, sn, S, tm=512):
    T = xt.shape[0]
    nseq = S // tm
    row = lambda i: (i, 0)
    tab = lambda i: (i % nseq, 0)
    return pl.pallas_call(
        _proj_rt_kernel,
        grid=(T // tm,),
        in_specs=[pl.BlockSpec((tm, D_MODEL), row), pl.BlockSpec(w.shape, lambda i: (0, 0)),
                  pl.BlockSpec((tm, RT_QK), tab), pl.BlockSpec((tm, RT_QK), tab)],
        out_specs=[pl.BlockSpec((tm, RT_QKW), row), pl.BlockSpec((tm, RT_QKW), row),
                   pl.BlockSpec((tm, RT_VW), row), pl.BlockSpec((tm, RT_VW), row)],
        out_shape=[jax.ShapeDtypeStruct((T, RT_QKW), BF16), jax.ShapeDtypeStruct((T, RT_QKW), BF16),
                   jax.ShapeDtypeStruct((T, RT_VW), BF16), jax.ShapeDtypeStruct((T, RT_VW), F32)],
        compiler_params=_cparams(("parallel",)),
        name="proj_rt",
    )(xt, w, cs, sn)


def _retention_kernel(q_ref, k_ref, v_ref, sg_ref, dec_ref, qd_ref, kd_ref, cd_ref, gng_ref, gnb_ref, o_ref, state):
    j = pl.program_id(1)

    @pl.when(j == 0)
    def _():
        state[...] = jnp.zeros_like(state)

    NB = q_ref.shape[0]
    H = range(NB * RT_HEADS)
    nh = lambda i: (i // RT_HEADS, i % RT_HEADS)
    qk_cols = lambda i: slice(nh(i)[1] * RT_QK, (nh(i)[1] + 1) * RT_QK)
    v_cols = lambda i: slice(nh(i)[1] * RT_V, (nh(i)[1] + 1) * RT_V)
    q = [q_ref[nh(i)[0], :, qk_cols(i)] for i in H]
    k = [k_ref[nh(i)[0], :, qk_cols(i)] for i in H]
    v = [v_ref[nh(i)[0], :, v_cols(i)] for i in H]
    r_old = [state[i] for i in H]
    att = [(_dot_nt(q[i], k[i]) * dec_ref[nh(i)[1]]).astype(BF16) for i in H]
    cross = [jnp.dot(q[i], r_old[i].astype(BF16), preferred_element_type=F32) * qd_ref[nh(i)[1]] for i in H]
    inner = [jnp.dot(att[i], v[i], preferred_element_type=F32) for i in H]
    kdec = [(k[i].astype(F32) * kd_ref[nh(i)[1]]).astype(BF16) for i in H]
    for i in H:
        upd = lax.dot_general(kdec[i], v[i], (((0,), (0,)), ((), ())), preferred_element_type=F32)
        state[i] = r_old[i] * cd_ref[nh(i)[1]][:, 0:1] + upd
    for i in H:
        o = inner[i] + cross[i]
        mu = jnp.mean(o, axis=1, keepdims=True)
        oc = o - mu
        var = jnp.mean(oc * oc, axis=1, keepdims=True)
        on = oc * lax.rsqrt(var + RT_GN_EPS) * gng_ref[:, v_cols(i)] + gnb_ref[:, v_cols(i)]
        o_ref[nh(i)[0], :, v_cols(i)] = (sg_ref[nh(i)[0], :, v_cols(i)] * on).astype(BF16)


def _retention(q3, k3, v3, sg3, dec, qd, kd, cd, gng, gnb, nb=2):
    B, S, _ = q3.shape
    C = RT_CHUNK
    nb = nb if B % nb == 0 else 1
    qk = pl.BlockSpec((nb, C, RT_QKW), lambda b, j: (b, j, 0))
    vv = pl.BlockSpec((nb, C, RT_VW), lambda b, j: (b, j, 0))
    const = lambda a: pl.BlockSpec(a.shape, lambda b, j: (0,) * a.ndim)
    return pl.pallas_call(
        _retention_kernel,
        grid=(B // nb, S // C),
        in_specs=[qk, qk, vv, vv, const(dec), const(qd), const(kd), const(cd), const(gng), const(gnb)],
        out_specs=vv,
        out_shape=jax.ShapeDtypeStruct((B, S, RT_VW), BF16),
        scratch_shapes=[pltpu.VMEM((nb * RT_HEADS, RT_QK, RT_V), F32)],
        compiler_params=_cparams(("parallel", "arbitrary")),
        name="retention",
    )(q3, k3, v3, sg3, dec, qd, kd, cd, gng, gnb)


def _nsa_rope_tables(S):
    half = ROPE_DIM // 2
    inv = ROPE_THETA ** (-jnp.arange(half, dtype=F32) / half)
    ang = jnp.arange(S, dtype=F32)[:, None] * inv[None, :]
    cos, sin = jnp.cos(ang), jnp.sin(ang)
    zeros = lambda n: jnp.zeros((S, n), F32)
    cn = jnp.concatenate([cos, cos, jnp.ones((S, NS_HEAD - ROPE_DIM), F32)], axis=1)
    s1 = jnp.concatenate([-sin, zeros(NS_HEAD - half)], axis=1)
    s2 = jnp.concatenate([zeros(half), sin, zeros(NS_HEAD - ROPE_DIM)], axis=1)
    two = lambda a: jnp.concatenate([a, a], axis=1)
    return two(cn), two(s1), two(s2)


def _rt_rope_tables(S):
    inv = RT_THETA ** (-jnp.linspace(0.0, 1.0, RT_QK // 2, dtype=F32))
    ang = jnp.arange(S, dtype=F32)[:, None] * inv[None, :]
    cos, sin = jnp.cos(ang), jnp.sin(ang)
    return jnp.concatenate([cos, cos], axis=1), jnp.concatenate([-sin, sin], axis=1)


def _rt_decay_tables():
    log_g = jnp.log(1.0 - 2.0 ** (-5.0 - jnp.arange(RT_HEADS, dtype=F32)))
    idx = jnp.arange(RT_CHUNK, dtype=F32)
    diff = idx[:, None] - idx[None, :]
    dec = jnp.where(diff >= 0, jnp.exp(jnp.maximum(diff, 0.0) * log_g[:, None, None]), 0.0)
    qd = jnp.exp((idx + 1.0) * log_g[:, None])[..., None]
    kd = jnp.exp((RT_CHUNK - 1.0 - idx) * log_g[:, None])[..., None]
    cd = jnp.broadcast_to(jnp.exp(RT_CHUNK * log_g)[:, None, None], (RT_HEADS, 1, LANES))
    return dec, qd, kd, cd


def _overlap_table(S, ncp):
    n_cmp = (S - CMP_LEN) // CMP_STRIDE + 1
    n_sel = S // SEL_BLOCK
    cs = jnp.arange(ncp) * CMP_STRIDE
    ss = jnp.arange(LANES) * SEL_BLOCK
    ov = jnp.clip(jnp.minimum(cs[:, None] + CMP_LEN, ss[None, :] + SEL_BLOCK)
                  - jnp.maximum(cs[:, None], ss[None, :]), 0, None).astype(F32) / CMP_LEN
    keep = (jnp.arange(ncp)[:, None] < n_cmp) & (jnp.arange(LANES)[None, :] < n_sel)
    return jnp.where(keep, ov, 0.0)


def kernel(x, ab_w_in, ab_w_out, rk_mu, rk_w0, rk_w1, rk_w2, rk_a0, rk_a1, rk_a2, rk_g1, rk_g2, rk_kk, rk_ka, rk_rk,
           rk_ln, ns_pe, ns_c_w1, ns_c_w2, rt_w_in, rt_w_out, rt_gn, router_w, router_b, moe_w_gate, moe_w_up,
           moe_w_down, ln):
    B, S, D = x.shape
    T = B * S
    assert D == D_MODEL and S % 256 == 0 and S // SEL_BLOCK <= LANES and S >= WINDOW
    xt = x.reshape(T, D)
    rwt = router_w.T
    rb = router_b.reshape(N_EXPERTS, 1)
    vec = lambda a: a.reshape(1, -1)

    w_in = ab_w_in[0]
    n_gate = 3 * NS_HEADS
    w_cat = jnp.concatenate([w_in[:, :-n_gate], jnp.pad(w_in[:, -n_gate:], ((0, 0), (0, LANES - n_gate)))],
                            axis=1).astype(BF16)
    cn, s1, s2 = _nsa_rope_tables(S)
    prk, q, kk2, gt, vt, cx = _proj_ab(xt, w_cat, cn, s1, s2, S)

    ones = (jnp.arange(RK_W)[:, None] // RK_HEAD == jnp.arange(RK_W)[None, :] // RK_HEAD).astype(BF16)
    b16 = lambda a: a.astype(BF16)
    r, lw, km, v, kk, bb, g, bon = _rwkv_prep(
        prk.reshape(B, S, 4 * RK_W), rk_mu[0], vec(rk_w0[0]), b16(rk_w1[0]), b16(rk_w2[0]), vec(rk_a0[0]),
        b16(rk_a1[0]), b16(rk_a2[0]), b16(rk_g1[0]), b16(rk_g2[0]), vec(rk_kk[0]), vec(rk_ka[0]), vec(rk_rk[0]), ones)
    o_a = _rwkv_chunk(r, lw, km, v, kk, bb, g, bon, rk_ln[0, 0:1], rk_ln[0, 1:2])

    ncp = S // CMP_STRIDE
    kcv, kcvt = _nsa_compress(cx.reshape(2, B, ncp, CMP_STRIDE * LANES), ns_c_w1[0], b16(ns_c_w2[0]),
                              ns_pe[0].reshape(2, 1, CMP_LEN * NS_HEAD))
    o_b = _nsa_attn(q.reshape(B, S, NS_W), kcv, kcvt, kk2.reshape(B, S, 2 * LANES),
                    vt.reshape(2, B, S // QT, LANES, QT), gt.reshape(B, S, LANES), _overlap_table(S, ncp).T)

    w_out = b16(ab_w_out[0])
    x1, e, wt, pos, cnt = _outproj_router([o_a.reshape(T, RK_W), o_b.reshape(T, NS_W)], [w_out[:RK_W], w_out[RK_W:]],
                                          xt, ln[0, 0, 0:1], ln[0, 0, 1:2], rwt, rb)
    x2 = _moe(x1, e, wt, pos, cnt, moe_w_gate, moe_w_up, moe_w_down, 0, ln[0, 1, 0:1], ln[0, 1, 1:2])

    cs, sn = _rt_rope_tables(S)
    qr, kr, vr, sg = _proj_rt(x2, b16(rt_w_in[0]), cs, sn, S)
    dec, qd, kd, cd = _rt_decay_tables()
    ret = _retention(qr.reshape(B, S, RT_QKW), kr.reshape(B, S, RT_QKW), vr.reshape(B, S, RT_VW),
                     sg.reshape(B, S, RT_VW), dec, qd, kd, cd, rt_gn[0, 0:1], rt_gn[0, 1:2])
    x3, e, wt, pos, cnt = _outproj_router([ret.reshape(T, RT_VW)], [b16(rt_w_out[0])], x2,
                                          ln[1, 0, 0:1], ln[1, 0, 1:2], rwt, rb)
    x4 = _moe(x3, e, wt, pos, cnt, moe_w_gate, moe_w_up, moe_w_down, 1, ln[1, 1, 0:1], ln[1, 1, 1:2])
    return x4.reshape(B, S, D)
```

```python
import functools
import math

import jax
import jax.numpy as jnp
from jax import lax
from jax.experimental import pallas as pl
from jax.experimental.pallas import tpu as pltpu

F32 = jnp.float32
BF16 = jnp.bfloat16
I32 = jnp.int32
HI = lax.Precision.HIGHEST

LANES = 128
VMEM_LIMIT = 56 * 1024 * 1024

D_MODEL = 1024
RK_HEADS, RK_HEAD = 8, 64
RK_W = RK_HEADS * RK_HEAD
RK_DECAY_SCALE = 0.606531
RK_LN_EPS = 64e-5
NS_HEADS, NS_KV, NS_HPG, NS_HEAD = 8, 2, 4, 64
NS_W = NS_HEADS * NS_HEAD
CMP_LEN, CMP_STRIDE, SEL_BLOCK, SEL_TOPK, WINDOW = 32, 16, 64, 16, 512
ROPE_THETA = 500000.0
ROPE_DIM = NS_HEAD // 4
Q_SCALE = NS_HEAD ** -0.5 * math.log2(math.e)
QT = 256
KT = 512
RT_HEADS, RT_QK, RT_V = 8, 128, 256
RT_QKW, RT_VW = RT_HEADS * RT_QK, RT_HEADS * RT_V
RT_CHUNK = 256
RT_THETA = 10000.0
RT_GN_EPS = 1e-5
N_EXPERTS, N_GROUPS, EXP_PER_GROUP, TOP_K = 16, 4, 4, 2
D_EXPERT = 1024
MOE_BLOCK = 512
ROW_DMA_UNROLL = 8
DEPTH = 2
ALPHA = (2.0 * DEPTH) ** 0.25
LN_EPS = 1e-5
NEG = -1e30


def _cparams(sem):
    return pltpu.CompilerParams(dimension_semantics=sem, vmem_limit_bytes=VMEM_LIMIT)


def _bdot(a, w):
    return jnp.dot(a.astype(BF16), w, preferred_element_type=F32)


def _dot_nt(a, b):
    return lax.dot_general(a, b, (((1,), (1,)), ((), ())), preferred_element_type=F32)


def _layer_norm_rows(z, g, b):
    mu = jnp.mean(z, axis=1, keepdims=True)
    zc = z - mu
    var = jnp.mean(zc * zc, axis=1, keepdims=True)
    return zc * lax.rsqrt(var + LN_EPS) * g + b


def _proj_ab_kernel(x_ref, w_ref, cn_ref, s1_ref, s2_ref, pick_ref, prk_ref, q_ref, kk_ref, gt_ref, vt_ref, cx_ref):
    xb = x_ref[...].astype(BF16)
    for c in range(4):
        prk_ref[:, c * RK_W:(c + 1) * RK_W] = jnp.dot(xb, w_ref[:, c * RK_W:(c + 1) * RK_W],
                                                      preferred_element_type=F32)
    cn, s1, s2 = cn_ref[...], s1_ref[...], s2_ref[...]

    def rope(y):
        return y * cn + pltpu.roll(y, LANES - ROPE_DIM // 2, 1) * s1 + pltpu.roll(y, ROPE_DIM // 2, 1) * s2

    base = 4 * RK_W
    yq = jnp.dot(xb, w_ref[:, base:base + NS_W], preferred_element_type=F32)
    for c in range(NS_W // LANES):
        q_ref[:, c * LANES:(c + 1) * LANES] = (rope(yq[:, c * LANES:(c + 1) * LANES]) * Q_SCALE).astype(BF16)
    base += NS_W
    ykv = jnp.dot(xb, w_ref[:, base:base + 6 * LANES], preferred_element_type=F32)
    for c in range(6):
        y = ykv[:, c * LANES:(c + 1) * LANES]
        if c % 2 == 0:
            y = rope(y)
        if c < 2:
            yb = y.astype(BF16)
            for l in range(CMP_STRIDE):
                cx_ref[c, :, l * LANES:(l + 1) * LANES] = jnp.dot(pick_ref[l], yb,
                                                                  preferred_element_type=F32).astype(BF16)
        elif c % 2 == 0:
            kk_ref[:, (c // 2 - 1) * LANES:(c // 2) * LANES] = y.astype(BF16)
        else:
            for t in range(y.shape[0] // QT):
                vt_ref[c // 2 - 1, t] = y[t * QT:(t + 1) * QT].T.astype(BF16)
    base += 6 * LANES
    gt_ref[...] = jax.nn.sigmoid(jnp.dot(xb, w_ref[:, base:base + LANES], preferred_element_type=F32))


def _proj_ab(xt, w, cn, s1, s2, S, tm=512):
    T = xt.shape[0]
    ncols = w.shape[1]
    nseq = S // tm
    row = lambda i: (i, 0)
    tab = lambda i: (i % nseq, 0)
    ng = tm // CMP_STRIDE
    pick = (jnp.arange(tm, dtype=I32)[None, None, :]
            == jnp.arange(ng, dtype=I32)[None, :, None] * CMP_STRIDE + jnp.arange(CMP_STRIDE, dtype=I32)[:, None, None])
    pick = pick.astype(BF16)
    return pl.pallas_call(
        _proj_ab_kernel,
        grid=(T // tm,),
        in_specs=[pl.BlockSpec((tm, D_MODEL), row),
                  pl.BlockSpec((D_MODEL, ncols), lambda i: (0, 0)),
                  pl.BlockSpec((tm, LANES), tab), pl.BlockSpec((tm, LANES), tab), pl.BlockSpec((tm, LANES), tab),
                  pl.BlockSpec(pick.shape, lambda i: (0, 0, 0))],
        out_specs=[pl.BlockSpec((tm, 4 * RK_W), row), pl.BlockSpec((tm, NS_W), row),
                   pl.BlockSpec((tm, 2 * LANES), row), pl.BlockSpec((tm, LANES), row),
                   pl.BlockSpec((2, tm // QT, LANES, QT), lambda i: (0, i, 0, 0)),
                   pl.BlockSpec((2, ng, CMP_STRIDE * LANES), lambda i: (0, i, 0))],
        out_shape=[jax.ShapeDtypeStruct((T, 4 * RK_W), F32), jax.ShapeDtypeStruct((T, NS_W), BF16),
                   jax.ShapeDtypeStruct((T, 2 * LANES), BF16), jax.ShapeDtypeStruct((T, LANES), F32),
                   jax.ShapeDtypeStruct((2, T // QT, LANES, QT), BF16),
                   jax.ShapeDtypeStruct((2, T // CMP_STRIDE, CMP_STRIDE * LANES), BF16)],
        compiler_params=_cparams(("parallel",)),
        name="proj_ab",
    )(xt, w, cn, s1, s2, pick)


def _rwkv_prep_kernel(p_ref, mu_ref, w0_ref, w1_ref, w2_ref, a0_ref, a1_ref, a2_ref, g1_ref, g2_ref,
                      kk_ref, ka_ref, rk_ref, ones_ref,
                      r_o, w_o, k_o, v_o, kk_o, b_o, g_o, bon_o, carry):
    j = pl.program_id(1)
    p = p_ref[...]
    tm = p.shape[0]

    @pl.when(j == 0)
    def _():
        carry[...] = jnp.zeros_like(carry)

    rowi = lax.broadcasted_iota(I32, p.shape, 0)
    prev = jnp.where(rowi == 0, carry[...], pltpu.roll(p, 1, 0))
    carry[...] = p[tm - 1:tm, :]
    dp = prev - p
    sl = lambda a, c: a[:, c * RK_W:(c + 1) * RK_W]
    mu = mu_ref[...]
    r = sl(p, 0) + sl(dp, 0) * mu[0:1]
    k = sl(p, 1) + sl(dp, 1) * mu[1:2]
    v = sl(p, 2) + sl(dp, 2) * mu[2:3]
    xw = sl(p, 3) + sl(dp, 3) * mu[3:4]
    xa = sl(p, 3) + sl(dp, 3) * mu[4:5]
    xg = sl(p, 3) + sl(dp, 3) * mu[5:6]
    lw = -RK_DECAY_SCALE * jax.nn.sigmoid(w0_ref[...] + _bdot(jnp.tanh(_bdot(xw, w1_ref[...])), w2_ref[...]))
    a = jax.nn.sigmoid(a0_ref[...] + _bdot(_bdot(xa, a1_ref[...]), a2_ref[...]))
    g = _bdot(jax.nn.sigmoid(_bdot(xg, g1_ref[...])), g2_ref[...])
    ones = ones_ref[...]

    def head_sum(t):
        hi = t.astype(BF16)
        lo = (t - hi.astype(F32)).astype(BF16)
        return (jnp.dot(hi, ones, preferred_element_type=F32) + jnp.dot(lo, ones, preferred_element_type=F32))

    kk = k * kk_ref[...]
    kk = kk / jnp.maximum(jnp.sqrt(head_sum(kk * kk)), 1e-12)
    km = k * (1.0 + (a - 1.0) * ka_ref[...])
    bon = head_sum(r * km * rk_ref[...]) * v
    w_o[...] = lw
    for ref, val in ((r_o, r), (k_o, km), (v_o, v), (kk_o, kk), (b_o, kk * a), (g_o, g), (bon_o, bon)):
        ref[...] = val.astype(ref.dtype)


def _rwkv_prep(prk3, mu, w0, w1, w2, a0, a1, a2, g1, g2, k_k, k_a, r_k, ones, tm=512):
    B, S, _ = prk3.shape
    full = lambda a: pl.BlockSpec(a.shape, lambda b, j: (0,) * a.ndim)
    params = [mu, w0, w1, w2, a0, a1, a2, g1, g2, k_k, k_a, r_k, ones]
    ospec = pl.BlockSpec((None, tm, RK_W), lambda b, j: (b, j, 0))
    return pl.pallas_call(
        _rwkv_prep_kernel,
        grid=(B, S // tm),
        in_specs=[pl.BlockSpec((None, tm, 4 * RK_W), lambda b, j: (b, j, 0))] + [full(a) for a in params],
        out_specs=[ospec] * 8,
        out_shape=[jax.ShapeDtypeStruct((B, S, RK_W), F32 if i == 1 else BF16) for i in range(8)],
        scratch_shapes=[pltpu.VMEM((1, 4 * RK_W), F32)],
        compiler_params=_cparams(("parallel", "arbitrary")),
        name="rwkv_prep",
    )(prk3, *params)


RK_CHUNK = 16


def _pdot(a, b, dims, precise):
    if precise:
        return lax.dot_general(a, b, (dims, ((), ())), precision=HI, preferred_element_type=F32)
    return lax.dot_general(a.astype(BF16), b.astype(BF16), (dims, ((), ())), preferred_element_type=F32)


def _rwkv_chunk_kernel(r_ref, lw_ref, k_ref, v_ref, kk_ref, b_ref, g_ref, bon_ref, lng_ref, lnb_ref, o_ref, ht,
                       *, precise):
    j = pl.program_id(1)

    @pl.when(j == 0)
    def _():
        ht[...] = jnp.zeros_like(ht)

    NB, TT = r_ref.shape[0], r_ref.shape[1]
    C, N = RK_CHUNK, RK_HEAD
    mm = lambda a, b: _pdot(a, b, ((1,), (0,)), precise)
    mm_nt = lambda a, b: _pdot(a, b, ((1,), (1,)), precise)
    mm_tn = lambda a, b: _pdot(a, b, ((0,), (0,)), precise)
    wide = lambda ref: jnp.concatenate([ref[n] for n in range(NB)], axis=1)

    lw = wide(lw_ref)
    rowc = lax.broadcasted_iota(I32, lw.shape, 0) & (C - 1)
    linc, lrev = lw, lw
    sh = 1
    while sh < C:
        linc = linc + jnp.where(rowc >= sh, pltpu.roll(linc, sh, 0), 0.0)
        lrev = lrev + jnp.where(rowc < C - sh, pltpu.roll(lrev, TT - sh, 0), 0.0)
        sh *= 2
    lrev = lrev - lw
    r, k, v, kk, b = wide(r_ref), wide(k_ref), wide(v_ref), wide(kk_ref), wide(b_ref)
    e_in, e_inv, e_rev = jnp.exp(linc), jnp.exp(-linc), jnp.exp(lrev)
    kkd = kk * jnp.exp(linc - lw)
    rd = r * e_in
    binv, kinv = b * e_inv, k * e_inv
    bd, kd = b * e_rev, k * e_rev
    gam = jnp.exp(linc + lrev)

    ti = lax.broadcasted_iota(I32, (TT, TT), 0)
    tj = lax.broadcasted_iota(I32, (TT, TT), 1)
    same = (ti // C) == (tj // C)
    strict = same & (tj < ti)
    incl = same & (tj <= ti)

    H = range(NB * RK_HEADS)
    lo = lambda a: a if precise else a.astype(BF16)
    hs = lambda a: [a[:, h * N:(h + 1) * N] for h in H]
    rows2 = lambda a, b: jnp.concatenate([a, b], axis=0)
    kkd_h, rd_h, v_h = hs(lo(kkd)), hs(lo(rd)), hs(lo(v))
    binv_h, kinv_h, bd_h, kd_h = hs(lo(binv)), hs(lo(kinv)), hs(lo(bd)), hs(lo(kd))
    gam_h = hs(gam)
    gm = [mm_nt(rows2(kkd_h[h], rd_h[h]), rows2(binv_h[h], kinv_h[h])) for h in H]
    a_b = [lo(jnp.where(strict, gm[h][:TT, :TT], 0.0)) for h in H]
    b_rb = [lo(jnp.where(incl, gm[h][TT:, :TT], 0.0)) for h in H]
    akb = [lo(rows2(jnp.where(strict, gm[h][:TT, TT:], 0.0), jnp.where(incl, gm[h][TT:, TT:], 0.0))) for h in H]
    av = [mm(akb[h], v_h[h]) for h in H]
    x = [jnp.concatenate([kkd_h[h].astype(F32), av[h][:TT]], axis=1) for h in H]
    a2 = [lo(mm(a_b[h], a_b[h])) for h in H]
    a4 = [lo(mm(a2[h], a2[h])) for h in H]
    a8 = [lo(mm(a4[h], a4[h])) for h in H]
    x = [x[h] + mm(a8[h], lo(x[h])) for h in H]
    x = [x[h] + mm(a4[h], lo(x[h])) for h in H]
    x = [x[h] + mm(a2[h], lo(x[h])) for h in H]
    x = [x[h] - mm(a_b[h], lo(x[h])) for h in H]
    wt = [lo(x[h][:, :N]) for h in H]
    h_t = [ht[h] for h in H]
    us = [[] for _ in H]
    rhs = [[] for _ in H]
    for c in range(TT // C):
        rs = slice(c * C, (c + 1) * C)
        xh = [mm_nt(rows2(wt[h][rs], rd_h[h][rs]), lo(h_t[h])) for h in H]
        for h in H:
            u_c = -(xh[h][:C] + x[h][rs, N:])
            us[h].append(u_c)
            rhs[h].append(xh[h][C:])
        upd = [mm_tn(rows2(lo(us[h][c]), v_h[h][rs]), rows2(bd_h[h][rs], kd_h[h][rs])) for h in H]
        h_t = [h_t[h] * gam_h[h][c * C:c * C + 1] + upd[h] for h in H]
    outs = []
    for h in H:
        ht[h] = h_t[h]
        o = jnp.concatenate(rhs[h], axis=0) + mm(b_rb[h], lo(jnp.concatenate(us[h], axis=0))) + av[h][TT:]
        mu = jnp.mean(o, axis=1, keepdims=True)
        oc = o - mu
        var = jnp.mean(oc * oc, axis=1, keepdims=True)
        outs.append(oc * lax.rsqrt(var + RK_LN_EPS))
    for n in range(NB):
        on = jnp.concatenate(outs[n * RK_HEADS:(n + 1) * RK_HEADS], axis=1)
        o_ref[n] = (on * lng_ref[...] + lnb_ref[...] + bon_ref[n]) * g_ref[n]


def _rwkv_chunk(r, lw, k, v, kk, bb, g, bon, lng, lnb, tt=128, nb=4, precise=False):
    B, S, _ = r.shape
    nb = nb if B % nb == 0 else 1
    blk = pl.BlockSpec((nb, tt, RK_W), lambda b, j: (b, j, 0))
    vec = pl.BlockSpec((1, RK_W), lambda b, j: (0, 0))
    return pl.pallas_call(
        functools.partial(_rwkv_chunk_kernel, precise=precise),
        grid=(B // nb, S // tt),
        in_specs=[blk] * 8 + [vec, vec],
        out_specs=blk,
        out_shape=jax.ShapeDtypeStruct((B, S, RK_W), F32),
        scratch_shapes=[pltpu.VMEM((nb * RK_HEADS, RK_HEAD, RK_HEAD), F32)],
        compiler_params=_cparams(("parallel", "arbitrary")),
        name="rwkv_chunk",
    )(r, lw, k, v, kk, bb, g, bon, lng, lnb)


def _nsa_compress_kernel(x_ref, w1g_ref, w1f_ref, w2_ref, pe_ref, o_ref, ot_ref):
    bias = jnp.dot(jnp.broadcast_to(pe_ref[...], (8, CMP_LEN * NS_HEAD)), w1f_ref[...], precision=HI,
                   preferred_element_type=F32)[0:1]
    x = x_ref[...]
    n = x.shape[0]
    outs = []
    for g in range(NS_KV):
        ya = jnp.dot(x, w1g_ref[g, 0], preferred_element_type=F32)
        yb = jnp.dot(x, w1g_ref[g, 1], preferred_element_type=F32)
        h = ya + pltpu.roll(yb, n - 1, 0) + bias
        outs.append(_bdot(jax.nn.gelu(h), w2_ref[...]))
    out = jnp.concatenate(outs, axis=1)
    o_ref[...] = out.astype(BF16)
    ot_ref[...] = out.T.astype(BF16)


def _nsa_compress(cx, w1, w2b, pe):
    _, B, ncp, width = cx.shape
    hid = w1.shape[-1]
    w1h = w1.reshape(2, 2, CMP_STRIDE, 1, NS_HEAD, hid)
    zero = jnp.zeros_like(w1h)
    w1g = jnp.stack([jnp.concatenate([w1h, zero], axis=3), jnp.concatenate([zero, w1h], axis=3)], axis=1)
    w1g = w1g.reshape(2, NS_KV, 2, width, hid).astype(BF16)
    return pl.pallas_call(
        _nsa_compress_kernel,
        grid=(B, 2),
        in_specs=[pl.BlockSpec((None, None, ncp, width), lambda b, c: (c, b, 0, 0)),
                  pl.BlockSpec((None, NS_KV, 2, width, hid), lambda b, c: (c, 0, 0, 0, 0)),
                  pl.BlockSpec((None, CMP_LEN * NS_HEAD, hid), lambda b, c: (c, 0, 0)),
                  pl.BlockSpec((None, hid, NS_HEAD), lambda b, c: (c, 0, 0)),
                  pl.BlockSpec((None, 1, CMP_LEN * NS_HEAD), lambda b, c: (c, 0, 0))],
        out_specs=[pl.BlockSpec((None, None, ncp, LANES), lambda b, c: (b, c, 0, 0)),
                   pl.BlockSpec((None, None, LANES, ncp), lambda b, c: (b, c, 0, 0))],
        out_shape=[jax.ShapeDtypeStruct((B, 2, ncp, LANES), BF16), jax.ShapeDtypeStruct((B, 2, LANES, ncp), BF16)],
        compiler_params=_cparams(("parallel", "parallel")),
        name="nsa_compress",
    )(cx, w1g, w1, w2b, pe)


def _nsa_attn_kernel(q_ref, kc_ref, vct_ref, ks_ref, vst_ref, kw_ref, vwt_ref, gt_ref, ovt_ref, ext_ref, o_ref,
                     m_s, l_s, acc_s, m_w, l_w, acc_w, s_buf, *, ncp):
    i = pl.program_id(1)
    s0 = i * QT
    heads = range(NS_HEADS)
    hcols = lambda h: slice(h * QT, (h + 1) * QT)
    gcols = lambda h: slice(h // NS_HPG * QT, (h // NS_HPG + 1) * QT)
    iota = lambda shape, d: lax.broadcasted_iota(I32, shape, d)
    qpos = lambda shape: s0 + (iota(shape, 1) & (QT - 1))

    def col_reduce(x, op, final):
        n = x.shape[0]
        while n > 8:
            n //= 2
            x = op(x[:n], x[n:])
        return final(x, axis=0, keepdims=True)

    q8 = q_ref[...].astype(F32)
    zeros = jnp.zeros((QT, NS_HEAD), F32)
    qt = []
    for h in heads:
        qh = q8[:, h * NS_HEAD:(h + 1) * NS_HEAD]
        qt.append(jnp.concatenate([qh, zeros] if h < NS_HPG else [zeros, qh], axis=1).T)
    qt = jnp.concatenate(qt, axis=1).astype(BF16)

    sc = jnp.dot(kc_ref[...], qt, preferred_element_type=F32)
    n_row = iota((ncp, QT), 0)
    cmask = (n_row * CMP_STRIDE + (CMP_LEN - 1) <= s0 + iota((ncp, QT), 1)) & (n_row < ncp - 1)
    cpen = jnp.where(cmask, 0.0, NEG)
    sc = jnp.concatenate([sc[:, hcols(h)] + cpen for h in heads], axis=1)
    mc = col_reduce(sc, jnp.maximum, jnp.max)
    pc = jnp.exp2(sc - mc)
    lc = col_reduce(pc, jnp.add, jnp.sum)
    pc = pc * jnp.where(mc > 0.5 * NEG, 1.0 / lc, 0.0)
    pcs = jnp.concatenate([functools.reduce(jnp.add, [pc[:, hcols(h)] for h in range(g * NS_HPG, (g + 1) * NS_HPG)])
                           for g in range(NS_KV)], axis=1)

    def group_dot(vt, p, extra=None):
        half = NS_HPG * QT
        outs = []
        for g in range(NS_KV):
            vg = vt[g * NS_HEAD:(g + 1) * NS_HEAD]
            if extra is not None:
                vg = jnp.concatenate([vg, extra], axis=0)
            outs.append(jnp.dot(vg, p[:, g * half:(g + 1) * half], preferred_element_type=F32))
        return jnp.concatenate(outs, axis=1)

    o_c = group_dot(vct_ref[...], pc.astype(BF16))

    imp = jnp.dot(ovt_ref[...], pcs, precision=HI, preferred_element_type=F32)
    blk = iota((LANES, NS_KV * QT), 0)
    cur = qpos((LANES, NS_KV * QT)) // SEL_BLOCK
    valid = blk <= cur
    forced = (blk == 0) | (blk == cur) | (blk == cur - 1)
    pri = jnp.where(valid & ~forced, imp, -jnp.inf)
    picked = forced
    blkf = blk.astype(F32)
    for _ in range(SEL_TOPK - 3):
        mx = col_reduce(pri, jnp.maximum, jnp.max)
        hit = blkf == col_reduce(jnp.where(pri == mx, blkf, float(LANES)), jnp.minimum, jnp.min)
        picked = picked | hit
        pri = jnp.where(hit, -jnp.inf, pri)
    selpen = jnp.where(picked & valid, 0.0, NEG).astype(BF16)
    wq = jnp.concatenate([qt, jnp.concatenate([selpen[:, gcols(h)] for h in heads], axis=1)], axis=0)

    def attend(s, vt, pen, m_ref, l_ref, acc_ref):
        if pen is not None:
            s = jnp.concatenate([s[:, hcols(h)] + pen for h in heads], axis=1)
        m_old = m_ref[...]
        m_new = jnp.maximum(m_old, col_reduce(s, jnp.maximum, jnp.max))
        alpha = jnp.exp2(m_old - m_new)
        p = jnp.exp2(s - m_new).astype(BF16)
        m_ref[...] = m_new
        pv = group_dot(vt, p, extra=jnp.ones((16, vt.shape[1]), BF16))
        l_ref[...] = alpha * l_ref[...] + pv[NS_HEAD:NS_HEAD + 1]
        acc_ref[...] = alpha * acc_ref[...] + pv[:NS_HEAD]

    def reset(m_ref, l_ref, acc_ref):
        m_ref[...] = jnp.full(m_ref.shape, NEG, F32)
        l_ref[...] = jnp.zeros_like(l_ref)
        acc_ref[...] = jnp.zeros_like(acc_ref)

    tiles = lambda ref, t0, n: jnp.concatenate([ref[t0 + c] for c in range(n)], axis=1)

    reset(m_s, l_s, acc_s)
    diag = i // (KT // QT)

    def sel_scores(kt, slot):
        k0 = pl.multiple_of(kt * KT, KT)
        keys = jnp.concatenate([ks_ref[pl.ds(k0, KT), :], ext_ref[pl.ds(k0, KT), :]], axis=1)
        s_buf[slot] = jnp.dot(keys, wq, preferred_element_type=F32)

    def sel_attend(kt, slot, causal):
        pen = None
        if causal:
            pen = jnp.where(kt * KT + iota((KT, QT), 0) <= s0 + iota((KT, QT), 1), 0.0, NEG)
        attend(s_buf[slot], tiles(vst_ref, kt * (KT // QT), KT // QT), pen, m_s, l_s, acc_s)

    def sel_pair(j, c):
        sel_scores(2 * j + 1, 1)
        sel_attend(2 * j, 0, False)
        sel_scores(2 * j + 2, 0)
        sel_attend(2 * j + 1, 1, False)
        return c

    d0 = pl.multiple_of(s0, QT)
    a0 = pl.multiple_of(jnp.maximum(s0 - WINDOW, 0), QT)
    s_d = jnp.dot(kw_ref[pl.ds(d0, QT), :], qt, preferred_element_type=F32)
    s_a = jnp.dot(kw_ref[pl.ds(a0, WINDOW), :], qt, preferred_element_type=F32)
    pen_d = jnp.where(iota((QT, QT), 0) <= iota((QT, QT), 1), 0.0, NEG)
    kpos = a0 + iota((WINDOW, QT), 0)
    pen_a = jnp.where((kpos < s0) & (kpos > s0 + iota((WINDOW, QT), 1) - WINDOW), 0.0, NEG)

    sel_scores(0, 0)
    reset(m_w, l_w, acc_w)
    attend(s_d, vwt_ref[i], pen_d, m_w, l_w, acc_w)
    attend(s_a, tiles(vwt_ref, a0 // QT, WINDOW // QT), pen_a, m_w, l_w, acc_w)
    lax.fori_loop(0, diag // 2, sel_pair, 0)

    @pl.when(diag % 2 == 1)
    def _():
        sel_scores(diag, 1)
        sel_attend(diag - 1, 0, False)
        sel_attend(diag, 1, True)

    @pl.when(diag % 2 == 0)
    def _():
        sel_attend(diag, 0, True)

    gtt = gt_ref[...].T
    o_s = acc_s[...] / l_s[...]
    o_w = acc_w[...] / l_w[...]
    outs = []
    for h in heads:
        gate = lambda br: gtt[br * NS_HEADS + h:br * NS_HEADS + h + 1]
        outs.append(gate(0) * o_c[:, hcols(h)] + gate(1) * o_s[:, hcols(h)] + gate(2) * o_w[:, hcols(h)])
    pairs = [jnp.concatenate(outs[p:p + 2], axis=0).T for p in range(0, NS_HEADS, 2)]
    o_ref[...] = jnp.concatenate(pairs, axis=1).astype(BF16)


def _nsa_attn(q3, kcv, kcvt, kv3, vt, gt3, ovt):
    B, S, _ = q3.shape
    ncp = kcv.shape[2]
    R = NS_HEADS * QT
    ext = (jnp.arange(S, dtype=I32)[:, None] // SEL_BLOCK == jnp.arange(LANES, dtype=I32)[None, :]).astype(BF16)
    seq = lambda c: pl.BlockSpec((None, S, LANES), lambda b, i, c=c: (b, 0, c))
    seqt = lambda c: pl.BlockSpec((None, None, S // QT, LANES, QT), lambda b, i, c=c: (c, b, 0, 0, 0))
    return pl.pallas_call(
        functools.partial(_nsa_attn_kernel, ncp=ncp),
        grid=(B, S // QT),
        in_specs=[pl.BlockSpec((None, QT, NS_W), lambda b, i: (b, i, 0)),
                  pl.BlockSpec((None, None, ncp, LANES), lambda b, i: (b, 0, 0, 0)),
                  pl.BlockSpec((None, None, LANES, ncp), lambda b, i: (b, 1, 0, 0)),
                  seq(0), seqt(0), seq(1), seqt(1),
                  pl.BlockSpec((None, QT, LANES), lambda b, i: (b, i, 0)),
                  pl.BlockSpec(ovt.shape, lambda b, i: (0, 0)),
                  pl.BlockSpec(ext.shape, lambda b, i: (0, 0))],
        out_specs=pl.BlockSpec((None, QT, NS_W), lambda b, i: (b, i, 0)),
        out_shape=jax.ShapeDtypeStruct((B, S, NS_W), BF16),
        scratch_shapes=([pltpu.VMEM((1, R), F32), pltpu.VMEM((1, R), F32), pltpu.VMEM((NS_HEAD, R), F32)] * 2
                        + [pltpu.VMEM((2, KT, R), F32)]),
        compiler_params=_cparams(("parallel", "arbitrary")),
        name="nsa_attn",
    )(q3, kcv, kcvt, kv3, vt, kv3, vt, gt3, ovt, ext)


def _first_argmax(vals):
    m = vals[0]
    for v in vals[1:]:
        m = jnp.maximum(m, v)
    idx = jnp.full(m.shape, len(vals) - 1, I32)
    for j in range(len(vals) - 2, -1, -1):
        idx = jnp.where(vals[j] == m, j, idx)
    return m, idx


def _outproj_router_kernel(*refs, n_in):
    acts, ws = refs[:n_in], refs[n_in:2 * n_in]
    x_ref, lng_ref, lnb_ref, rwt_ref, rb_ref, tri_ref = refs[2 * n_in:2 * n_in + 6]
    y_ref, e_ref, wt_ref, pos_ref, cnt_ref, cnt = refs[2 * n_in + 6:]
    i = pl.program_id(0)

    @pl.when(i == 0)
    def _():
        cnt[...] = jnp.zeros_like(cnt)

    mix = _bdot(acts[0][...], ws[0][...])
    for a, w in zip(acts[1:], ws[1:]):
        mix = mix + _bdot(a[...], w[...])
    y = _layer_norm_rows(ALPHA * x_ref[...] + mix, lng_ref[...], lnb_ref[...])
    y_ref[...] = y

    split = lambda t: (t.astype(BF16), (t - t.astype(BF16).astype(F32)).astype(BF16))
    (w_hi, w_lo), (y_hi, y_lo) = split(rwt_ref[...]), split(y)
    logit = _dot_nt(w_hi, y_hi) + (_dot_nt(w_hi, y_lo) + _dot_nt(w_lo, y_hi))
    aff = jax.nn.sigmoid(logit)
    biased = aff + rb_ref[...]
    neg_inf = -jnp.inf
    g_score, g_i1, g_i2 = [], [], []
    for gi in range(N_GROUPS):
        vals = [biased[gi * EXP_PER_GROUP + j:gi * EXP_PER_GROUP + j + 1, :] for j in range(EXP_PER_GROUP)]
        m1, i1 = _first_argmax(vals)
        m2, i2 = _first_argmax([jnp.where(i1 == j, neg_inf, vals[j]) for j in range(EXP_PER_GROUP)])
        g_score.append(m1 + m2)
        g_i1.append(i1)
        g_i2.append(i2)
    _, grp = _first_argmax(g_score)
    loc1, loc2 = g_i1[-1], g_i2[-1]
    for gi in range(N_GROUPS - 2, -1, -1):
        loc1 = jnp.where(grp == gi, g_i1[gi], loc1)
        loc2 = jnp.where(grp == gi, g_i2[gi], loc2)
    e1 = grp * EXP_PER_GROUP + loc1
    e2 = grp * EXP_PER_GROUP + loc2
    eio = lax.broadcasted_iota(I32, aff.shape, 0)
    oh1 = eio == e1
    oh2 = eio == e2
    a1 = jnp.sum(jnp.where(oh1, aff, 0.0), axis=0, keepdims=True)
    a2 = jnp.sum(jnp.where(oh2, aff, 0.0), axis=0, keepdims=True)
    tot = a1 + a2
    e_ref[...] = jnp.concatenate([e1, e2], axis=0)
    wt_ref[...] = jnp.concatenate([a1 / tot, a2 / tot], axis=0)

    ohs = oh1.astype(F32) + oh2.astype(F32)
    before = jnp.dot(ohs.astype(BF16), tri_ref[...], preferred_element_type=F32) + cnt[...]
    p1 = jnp.sum(jnp.where(oh1, before, 0.0), axis=0, keepdims=True)
    p2 = jnp.sum(jnp.where(oh2, before, 0.0), axis=0, keepdims=True)
    pos_ref[...] = jnp.concatenate([p1, p2], axis=0).astype(I32)
    cnt[...] = cnt[...] + jnp.sum(ohs, axis=1, keepdims=True)
    cnt_ref[...] = jnp.broadcast_to(cnt[...], cnt_ref.shape)


def _outproj_router(acts, ws, xres, lng, lnb, rwt, rb, tm=1024):
    T = xres.shape[0]
    n_in = len(acts)
    tri = (lax.broadcasted_iota(I32, (tm, tm), 0) < lax.broadcasted_iota(I32, (tm, tm), 1)).astype(BF16)
    row = lambda i: (i, 0)
    const = lambda a: pl.BlockSpec(a.shape, lambda i: (0,) * a.ndim)
    lane_blk = pl.BlockSpec((TOP_K, tm), lambda i: (0, i))
    return pl.pallas_call(
        functools.partial(_outproj_router_kernel, n_in=n_in),
        grid=(T // tm,),
        in_specs=([pl.BlockSpec((tm, a.shape[1]), row) for a in acts] + [const(w) for w in ws]
                  + [pl.BlockSpec((tm, D_MODEL), row), const(lng), const(lnb), const(rwt), const(rb), const(tri)]),
        out_specs=[pl.BlockSpec((tm, D_MODEL), row), lane_blk, lane_blk, lane_blk,
                   pl.BlockSpec((N_EXPERTS, LANES), lambda i: (0, 0))],
        out_shape=[jax.ShapeDtypeStruct((T, D_MODEL), F32), jax.ShapeDtypeStruct((TOP_K, T), I32),
                   jax.ShapeDtypeStruct((TOP_K, T), F32), jax.ShapeDtypeStruct((TOP_K, T), I32),
                   jax.ShapeDtypeStruct((N_EXPERTS, LANES), F32)],
        scratch_shapes=[pltpu.VMEM((N_EXPERTS, 1), F32)],
        compiler_params=_cparams(("arbitrary",)),
        name="outproj_router",
    )(*acts, *ws, xres, lng, lnb, rwt, rb, tri)


def _dispatch_kernel(zrow_ref, dest_hbm, x_ref, xs_hbm, dsm, zbuf, sem_idx, sem, sem_z):
    i = pl.program_id(0)
    tm = x_ref.shape[0]

    @pl.when(i == 0)
    def _():
        zbuf[...] = jnp.zeros_like(zbuf)
        zero_copy = lambda j: pltpu.make_async_copy(
            zbuf, xs_hbm.at[pl.ds(pl.multiple_of(jnp.maximum(zrow_ref[j], 0), MOE_BLOCK), MOE_BLOCK)], sem_z)
        for j in range(zrow_ref.shape[0]):
            @pl.when(zrow_ref[j] >= 0)
            def _():
                zero_copy(j).start()
        for j in range(zrow_ref.shape[0]):
            @pl.when(zrow_ref[j] >= 0)
            def _():
                zero_copy(j).wait()

    idx_copy = pltpu.make_async_copy(dest_hbm.at[i], dsm, sem_idx)
    idx_copy.start()
    idx_copy.wait()

    def row_copy(r, k):
        return pltpu.make_async_copy(x_ref.at[pl.ds(r, 1)], xs_hbm.at[pl.ds(dsm[k * tm + r], 1)], sem)

    def start(r, c):
        for k in range(TOP_K):
            row_copy(r, k).start()
        return c

    lax.fori_loop(0, tm, start, 0, unroll=ROW_DMA_UNROLL)
    for k in range(TOP_K):
        pltpu.make_async_copy(x_ref, xs_hbm.at[pl.ds(0, tm)], sem).wait()


def _dispatch(zero_rows, dest_tiles, x, rows, tm):
    T = x.shape[0]
    return pl.pallas_call(
        _dispatch_kernel,
        grid_spec=pltpu.PrefetchScalarGridSpec(
            num_scalar_prefetch=1,
            grid=(T // tm,),
            in_specs=[pl.BlockSpec(memory_space=pl.ANY), pl.BlockSpec((tm, D_MODEL), lambda i, z: (i, 0))],
            out_specs=pl.BlockSpec(memory_space=pl.ANY),
            scratch_shapes=[pltpu.SMEM((TOP_K * tm,), I32), pltpu.VMEM((MOE_BLOCK, D_MODEL), F32),
                            pltpu.SemaphoreType.DMA(()), pltpu.SemaphoreType.DMA(()), pltpu.SemaphoreType.DMA(())]),
        out_shape=jax.ShapeDtypeStruct((rows, D_MODEL), F32),
        compiler_params=_cparams(("arbitrary",)),
        name="moe_dispatch",
    )(zero_rows, dest_tiles, x)


def _ffn_kernel(be_ref, nu_ref, xs_ref, wg_ref, wu_ref, wd_ref, y_ref, h_ref, wg_b, wu_b, wd_b):
    i = pl.program_id(0)
    half = D_EXPERT // 2
    used = i < nu_ref[0]

    @pl.when(used & ((i == 0) | (be_ref[i] != be_ref[jnp.maximum(i - 1, 0)])))
    def _():
        wg_b[...] = wg_ref[...].astype(BF16)
        wu_b[...] = wu_ref[...].astype(BF16)
        wd_b[...] = wd_ref[...].astype(BF16)

    @pl.when(used)
    def _():
        xb = xs_ref[...].astype(BF16)
        for c in range(2):
            cs = slice(c * half, (c + 1) * half)
            gt = jnp.dot(xb, wg_b[:, cs], preferred_element_type=F32)
            up = jnp.dot(xb, wu_b[:, cs], preferred_element_type=F32)
            h_ref[:, cs] = (jax.nn.silu(gt) * up).astype(BF16)
        y_ref[...] = jnp.dot(h_ref[...], wd_b[...], preferred_element_type=F32)

    @pl.when(i >= nu_ref[0])
    def _():
        y_ref[...] = jnp.zeros_like(y_ref)


def _ffn(blk_exp, n_used, xs, wg, wu, wd, layer):
    rows = xs.shape[0]
    wspec = lambda a: pl.BlockSpec((None, None) + a.shape[2:], lambda i, be, nu: (layer, be[i], 0, 0))
    wbuf = lambda a: pltpu.VMEM(a.shape[2:], BF16)
    return pl.pallas_call(
        _ffn_kernel,
        grid_spec=pltpu.PrefetchScalarGridSpec(
            num_scalar_prefetch=2,
            grid=(rows // MOE_BLOCK,),
            in_specs=[pl.BlockSpec((MOE_BLOCK, D_MODEL), lambda i, be, nu: (i, 0)), wspec(wg), wspec(wu), wspec(wd)],
            out_specs=pl.BlockSpec((MOE_BLOCK, D_MODEL), lambda i, be, nu: (i, 0)),
            scratch_shapes=[pltpu.VMEM((MOE_BLOCK, D_EXPERT), BF16), wbuf(wg), wbuf(wu), wbuf(wd)]),
        out_shape=jax.ShapeDtypeStruct((rows, D_MODEL), F32),
        compiler_params=_cparams(("arbitrary",)),
        name="moe_ffn",
    )(blk_exp, n_used, xs, wg, wu, wd)


def _combine_kernel(dest_hbm, y_hbm, x_ref, wt_ref, lng_ref, lnb_ref, o_ref, dsm0, dsm1, buf, sem_idx, sem):
    i = pl.program_id(0)
    tm = x_ref.shape[0]
    dsm = (dsm0, dsm1)

    def gather(step, s):
        idx_copy = pltpu.make_async_copy(dest_hbm.at[step], dsm[s], sem_idx)
        idx_copy.start()
        idx_copy.wait()

        def start(r, c):
            for k in range(TOP_K):
                pltpu.make_async_copy(y_hbm.at[pl.ds(dsm[s][k * tm + r], 1)], buf.at[s, k, pl.ds(r, 1)],
                                      sem.at[s]).start()
            return c

        lax.fori_loop(0, tm, start, 0, unroll=ROW_DMA_UNROLL)

    def reduce(s):
        for k in range(TOP_K):
            pltpu.make_async_copy(y_hbm.at[pl.ds(0, tm)], buf.at[s, k], sem.at[s]).wait()
        wt = wt_ref[...]
        z = ALPHA * x_ref[...] + wt[:, 0:1] * buf[s, 0] + wt[:, 1:2] * buf[s, 1]
        o_ref[...] = _layer_norm_rows(z, lng_ref[...], lnb_ref[...])

    @pl.when(i == 0)
    def _():
        gather(0, 0)

    for s in range(2):
        @pl.when(i % 2 == s)
        def _():
            @pl.when(i + 1 < pl.num_programs(0))
            def _():
                gather(i + 1, 1 - s)
            reduce(s)


def _combine(dest_tiles, y, x, wt, lng, lnb, tm):
    T = x.shape[0]
    row = lambda i: (i, 0)
    vec = pl.BlockSpec((1, D_MODEL), lambda i: (0, 0))
    return pl.pallas_call(
        _combine_kernel,
        grid=(T // tm,),
        in_specs=[pl.BlockSpec(memory_space=pl.ANY), pl.BlockSpec(memory_space=pl.ANY),
                  pl.BlockSpec((tm, D_MODEL), row), pl.BlockSpec((tm, TOP_K), row), vec, vec],
        out_specs=pl.BlockSpec((tm, D_MODEL), row),
        out_shape=jax.ShapeDtypeStruct((T, D_MODEL), F32),
        scratch_shapes=[pltpu.SMEM((TOP_K * tm,), I32), pltpu.SMEM((TOP_K * tm,), I32),
                        pltpu.VMEM((2, TOP_K, tm, D_MODEL), F32),
                        pltpu.SemaphoreType.DMA(()), pltpu.SemaphoreType.DMA((2,))],
        compiler_params=_cparams(("arbitrary",)),
        name="moe_combine",
    )(dest_tiles, y, x, wt, lng, lnb)


def _moe(x1, e, wt, pos, cnt, wg, wu, wd, layer, lng, lnb, tm=1024):
    T = x1.shape[0]
    n_blocks = -(-(T * TOP_K) // MOE_BLOCK) + N_EXPERTS
    rows = n_blocks * MOE_BLOCK
    counts = cnt[:, 0].astype(I32)
    padded = (counts + MOE_BLOCK - 1) // MOE_BLOCK * MOE_BLOCK
    pad_end = jnp.cumsum(padded)
    pad_start = pad_end - padded
    dest = pos
    for j in range(N_EXPERTS):
        dest = dest + jnp.where(e == j, pad_start[j], 0)
    blk_start = jnp.arange(n_blocks, dtype=I32) * MOE_BLOCK
    blk_exp = jnp.minimum(jnp.sum((pad_end[None, :] <= blk_start[:, None]).astype(I32), axis=1), N_EXPERTS - 1)
    n_used = (pad_end[-1:] // MOE_BLOCK).astype(I32)
    dest_tiles = dest.reshape(TOP_K, T // tm, tm).transpose(1, 0, 2).reshape(T // tm, TOP_K * tm)
    tail = jnp.where(padded > 0, pad_end - MOE_BLOCK, -1)
    trailing = jnp.arange(n_blocks - N_EXPERTS, n_blocks, dtype=I32)
    trailing = jnp.where(trailing >= n_used[0], trailing * MOE_BLOCK, -1)
    xs = _dispatch(jnp.concatenate([tail, trailing]).astype(I32), dest_tiles, x1, rows, tm)
    y = _ffn(blk_exp, n_used, xs, wg, wu, wd, layer)
    return _combine(dest_tiles, y, x1, wt.T, lng, lnb, tm)


def _proj_rt_kernel(x_ref, w_ref, c_ref, s_ref, q_ref, k_ref, v_ref, g_ref):
    xb = x_ref[...].astype(BF16)
    cs, sn = c_ref[...], s_ref[...]
    rope = lambda y: y * cs + pltpu.roll(y, RT_QK // 2, 1) * sn
    pair = 2 * RT_QK
    for c in range(RT_QKW // pair):
        yq = jnp.dot(xb, w_ref[:, c * pair:(c + 1) * pair], preferred_element_type=F32)
        yk = jnp.dot(xb, w_ref[:, RT_QKW + c * pair:RT_QKW + (c + 1) * pair], preferred_element_type=F32)
        for h in range(2):
            cols = slice(c * pair + h * RT_QK, c * pair + (h + 1) * RT_QK)
            q_ref[:, cols] = rope(yq[:, h * RT_QK:(h + 1) * RT_QK]).astype(BF16)
            k_ref[:, cols] = (rope(yk[:, h * RT_QK:(h + 1) * RT_QK]) * RT_QK ** -0.5).astype(BF16)
    step = 1024
    for c in range(RT_VW // step):
        cols = slice(c * step, (c + 1) * step)
        v_ref[:, cols] = jnp.dot(xb, w_ref[:, 2 * RT_QKW + c * step:2 * RT_QKW + (c + 1) * step],
                                 preferred_element_type=F32).astype(BF16)
        base = 2 * RT_QKW + RT_VW
        g_ref[:, cols] = jax.nn.silu(jnp.dot(xb, w_ref[:, base + c * step:base + (c + 1) * step],
                                             preferred_element_type=F32))


def _proj_rt(xt, w, cs, sn, S, tm=512):
    T = xt.shape[0]
    nseq = S // tm
    row = lambda i: (i, 0)
    tab = lambda i: (i % nseq, 0)
    return pl.pallas_call(
        _proj_rt_kernel,
        grid=(T // tm,),
        in_specs=[pl.BlockSpec((tm, D_MODEL), row), pl.BlockSpec(w.shape, lambda i: (0, 0)),
                  pl.BlockSpec((tm, RT_QK), tab), pl.BlockSpec((tm, RT_QK), tab)],
        out_specs=[pl.BlockSpec((tm, RT_QKW), row), pl.BlockSpec((tm, RT_QKW), row),
                   pl.BlockSpec((tm, RT_VW), row), pl.BlockSpec((tm, RT_VW), row)],
        out_shape=[jax.ShapeDtypeStruct((T, RT_QKW), BF16), jax.ShapeDtypeStruct((T, RT_QKW), BF16),
                   jax.ShapeDtypeStruct((T, RT_VW), BF16), jax.ShapeDtypeStruct((T, RT_VW), F32)],
        compiler_params=_cparams(("parallel",)),
        name="proj_rt",
    )(xt, w, cs, sn)


def _retention_kernel(q_ref, k_ref, v_ref, sg_ref, dec_ref, qd_ref, kd_ref, cd_ref, gng_ref, gnb_ref, o_ref, state):
    j = pl.program_id(1)

    @pl.when(j == 0)
    def _():
        state[...] = jnp.zeros_like(state)

    NB = q_ref.shape[0]
    H = range(NB * RT_HEADS)
    nh = lambda i: (i // RT_HEADS, i % RT_HEADS)
    qk_cols = lambda i: slice(nh(i)[1] * RT_QK, (nh(i)[1] + 1) * RT_QK)
    v_cols = lambda i: slice(nh(i)[1] * RT_V, (nh(i)[1] + 1) * RT_V)
    q = [q_ref[nh(i)[0], :, qk_cols(i)] for i in H]
    k = [k_ref[nh(i)[0], :, qk_cols(i)] for i in H]
    v = [v_ref[nh(i)[0], :, v_cols(i)] for i in H]
    r_old = [state[i] for i in H]
    att = [(_dot_nt(q[i], k[i]) * dec_ref[nh(i)[1]]).astype(BF16) for i in H]
    cross = [jnp.dot(q[i], r_old[i].astype(BF16), preferred_element_type=F32) * qd_ref[nh(i)[1]] for i in H]
    inner = [jnp.dot(att[i], v[i], preferred_element_type=F32) for i in H]
    kdec = [(k[i].astype(F32) * kd_ref[nh(i)[1]]).astype(BF16) for i in H]
    for i in H:
        upd = lax.dot_general(kdec[i], v[i], (((0,), (0,)), ((), ())), preferred_element_type=F32)
        state[i] = r_old[i] * cd_ref[nh(i)[1]][:, 0:1] + upd
    for i in H:
        o = inner[i] + cross[i]
        mu = jnp.mean(o, axis=1, keepdims=True)
        oc = o - mu
        var = jnp.mean(oc * oc, axis=1, keepdims=True)
        on = oc * lax.rsqrt(var + RT_GN_EPS) * gng_ref[:, v_cols(i)] + gnb_ref[:, v_cols(i)]
        o_ref[nh(i)[0], :, v_cols(i)] = (sg_ref[nh(i)[0], :, v_cols(i)] * on).astype(BF16)


def _retention(q3, k3, v3, sg3, dec, qd, kd, cd, gng, gnb, nb=2):
    B, S, _ = q3.shape
    C = RT_CHUNK
    nb = nb if B % nb == 0 else 1
    qk = pl.BlockSpec((nb, C, RT_QKW), lambda b, j: (b, j, 0))
    vv = pl.BlockSpec((nb, C, RT_VW), lambda b, j: (b, j, 0))
    const = lambda a: pl.BlockSpec(a.shape, lambda b, j: (0,) * a.ndim)
    return pl.pallas_call(
        _retention_kernel,
        grid=(B // nb, S // C),
        in_specs=[qk, qk, vv, vv, const(dec), const(qd), const(kd), const(cd), const(gng), const(gnb)],
        out_specs=vv,
        out_shape=jax.ShapeDtypeStruct((B, S, RT_VW), BF16),
        scratch_shapes=[pltpu.VMEM((nb * RT_HEADS, RT_QK, RT_V), F32)],
        compiler_params=_cparams(("parallel", "arbitrary")),
        name="retention",
    )(q3, k3, v3, sg3, dec, qd, kd, cd, gng, gnb)


def _nsa_rope_tables(S):
    half = ROPE_DIM // 2
    inv = ROPE_THETA ** (-jnp.arange(half, dtype=F32) / half)
    ang = jnp.arange(S, dtype=F32)[:, None] * inv[None, :]
    cos, sin = jnp.cos(ang), jnp.sin(ang)
    zeros = lambda n: jnp.zeros((S, n), F32)
    cn = jnp.concatenate([cos, cos, jnp.ones((S, NS_HEAD - ROPE_DIM), F32)], axis=1)
    s1 = jnp.concatenate([-sin, zeros(NS_HEAD - half)], axis=1)
    s2 = jnp.concatenate([zeros(half), sin, zeros(NS_HEAD - ROPE_DIM)], axis=1)
    two = lambda a: jnp.concatenate([a, a], axis=1)
    return two(cn), two(s1), two(s2)


def _rt_rope_tables(S):
    inv = RT_THETA ** (-jnp.linspace(0.0, 1.0, RT_QK // 2, dtype=F32))
    ang = jnp.arange(S, dtype=F32)[:, None] * inv[None, :]
    cos, sin = jnp.cos(ang), jnp.sin(ang)
    return jnp.concatenate([cos, cos], axis=1), jnp.concatenate([-sin, sin], axis=1)


def _rt_decay_tables():
    log_g = jnp.log(1.0 - 2.0 ** (-5.0 - jnp.arange(RT_HEADS, dtype=F32)))
    idx = jnp.arange(RT_CHUNK, dtype=F32)
    diff = idx[:, None] - idx[None, :]
    dec = jnp.where(diff >= 0, jnp.exp(jnp.maximum(diff, 0.0) * log_g[:, None, None]), 0.0)
    qd = jnp.exp((idx + 1.0) * log_g[:, None])[..., None]
    kd = jnp.exp((RT_CHUNK - 1.0 - idx) * log_g[:, None])[..., None]
    cd = jnp.broadcast_to(jnp.exp(RT_CHUNK * log_g)[:, None, None], (RT_HEADS, 1, LANES))
    return dec, qd, kd, cd


def _overlap_table(S, ncp):
    n_cmp = (S - CMP_LEN) // CMP_STRIDE + 1
    n_sel = S // SEL_BLOCK
    cs = jnp.arange(ncp) * CMP_STRIDE
    ss = jnp.arange(LANES) * SEL_BLOCK
    ov = jnp.clip(jnp.minimum(cs[:, None] + CMP_LEN, ss[None, :] + SEL_BLOCK)
                  - jnp.maximum(cs[:, None], ss[None, :]), 0, None).astype(F32) / CMP_LEN
    keep = (jnp.arange(ncp)[:, None] < n_cmp) & (jnp.arange(LANES)[None, :] < n_sel)
    return jnp.where(keep, ov, 0.0)


def kernel(x, ab_w_in, ab_w_out, rk_mu, rk_w0, rk_w1, rk_w2, rk_a0, rk_a1, rk_a2, rk_g1, rk_g2, rk_kk, rk_ka, rk_rk,
           rk_ln, ns_pe, ns_c_w1, ns_c_w2, rt_w_in, rt_w_out, rt_gn, router_w, router_b, moe_w_gate, moe_w_up,
           moe_w_down, ln):
    B, S, D = x.shape
    T = B * S
    assert D == D_MODEL and S % 256 == 0 and S // SEL_BLOCK <= LANES and S >= WINDOW
    xt = x.reshape(T, D)
    rwt = router_w.T
    rb = router_b.reshape(N_EXPERTS, 1)
    vec = lambda a: a.reshape(1, -1)

    w_in = ab_w_in[0]
    n_gate = 3 * NS_HEADS
    w_cat = jnp.concatenate([w_in[:, :-n_gate], jnp.pad(w_in[:, -n_gate:], ((0, 0), (0, LANES - n_gate)))],
                            axis=1).astype(BF16)
    cn, s1, s2 = _nsa_rope_tables(S)
    prk, q, kk2, gt, vt, cx = _proj_ab(xt, w_cat, cn, s1, s2, S)

    ones = (jnp.arange(RK_W)[:, None] // RK_HEAD == jnp.arange(RK_W)[None, :] // RK_HEAD).astype(BF16)
    b16 = lambda a: a.astype(BF16)
    r, lw, km, v, kk, bb, g, bon = _rwkv_prep(
        prk.reshape(B, S, 4 * RK_W), rk_mu[0], vec(rk_w0[0]), b16(rk_w1[0]), b16(rk_w2[0]), vec(rk_a0[0]),
        b16(rk_a1[0]), b16(rk_a2[0]), b16(rk_g1[0]), b16(rk_g2[0]), vec(rk_kk[0]), vec(rk_ka[0]), vec(rk_rk[0]), ones)
    o_a = _rwkv_chunk(r, lw, km, v, kk, bb, g, bon, rk_ln[0, 0:1], rk_ln[0, 1:2])

    ncp = S // CMP_STRIDE
    kcv, kcvt = _nsa_compress(cx.reshape(2, B, ncp, CMP_STRIDE * LANES), ns_c_w1[0], b16(ns_c_w2[0]),
                              ns_pe[0].reshape(2, 1, CMP_LEN * NS_HEAD))
    o_b = _nsa_attn(q.reshape(B, S, NS_W), kcv, kcvt, kk2.reshape(B, S, 2 * LANES),
                    vt.reshape(2, B, S // QT, LANES, QT), gt.reshape(B, S, LANES), _overlap_table(S, ncp).T)

    w_out = b16(ab_w_out[0])
    x1, e, wt, pos, cnt = _outproj_router([o_a.reshape(T, RK_W), o_b.reshape(T, NS_W)], [w_out[:RK_W], w_out[RK_W:]],
                                          xt, ln[0, 0, 0:1], ln[0, 0, 1:2], rwt, rb)
    x2 = _moe(x1, e, wt, pos, cnt, moe_w_gate, moe_w_up, moe_w_down, 0, ln[0, 1, 0:1], ln[0, 1, 1:2])

    cs, sn = _rt_rope_tables(S)
    qr, kr, vr, sg = _proj_rt(x2, b16(rt_w_in[0]), cs, sn, S)
    dec, qd, kd, cd = _rt_decay_tables()
    ret = _retention(qr.reshape(B, S, RT_QKW), kr.reshape(B, S, RT_QKW), vr.reshape(B, S, RT_VW),
                     sg.reshape(B, S, RT_VW), dec, qd, kd, cd, rt_gn[0, 0:1], rt_gn[0, 1:2])
    x3, e, wt, pos, cnt = _outproj_router([ret.reshape(T, RT_VW)], [b16(rt_w_out[0])], x2,
                                          ln[1, 0, 0:1], ln[1, 0, 1:2], rwt, rb)
    x4 = _moe(x3, e, wt, pos, cnt, moe_w_gate, moe_w_up, moe_w_down, 1, ln[1, 1, 0:1], ln[1, 1, 1:2])
    return x4.reshape(B, S, D)
```

```python
import functools
import math

import jax
import jax.numpy as jnp
from jax import lax
from jax.experimental import pallas as pl
from jax.experimental.pallas import tpu as pltpu

F32 = jnp.float32
BF16 = jnp.bfloat16
I32 = jnp.int32
HI = lax.Precision.HIGHEST

LANES = 128
SUBLANES = 8
BF16_ROWS = 16
VMEM_LIMIT = 56 * 1024 * 1024

D_MODEL = 1024
RK_HEADS, RK_HEAD = 8, 64
RK_W = RK_HEADS * RK_HEAD
RK_DECAY_SCALE = 0.606531
RK_LN_EPS = 64e-5
NS_HEADS, NS_KV, NS_HPG, NS_HEAD = 8, 2, 4, 64
NS_W = NS_HEADS * NS_HEAD
CMP_LEN, CMP_STRIDE, SEL_BLOCK, SEL_TOPK, WINDOW = 32, 16, 64, 16, 512
ROPE_THETA = 500000.0
ROPE_DIM = NS_HEAD // 4
Q_SCALE = NS_HEAD ** -0.5 * math.log2(math.e)
QT = 256
KT = 512
RT_HEADS, RT_QK, RT_V = 8, 128, 256
RT_QKW, RT_VW = RT_HEADS * RT_QK, RT_HEADS * RT_V
RT_CHUNK = 256
RT_THETA = 10000.0
RT_GN_EPS = 1e-5
N_EXPERTS, N_GROUPS, EXP_PER_GROUP, TOP_K = 16, 4, 4, 2
D_EXPERT = 1024
MOE_BLOCK = 512
ROW_DMA_UNROLL = 8
DEPTH = 2
ALPHA = (2.0 * DEPTH) ** 0.25
LN_EPS = 1e-5
NEG = -1e30


def _cparams(sem):
    return pltpu.CompilerParams(dimension_semantics=sem, vmem_limit_bytes=VMEM_LIMIT)


def _bdot(a, w):
    return jnp.dot(a.astype(BF16), w, preferred_element_type=F32)


def _dot_nt(a, b):
    return lax.dot_general(a, b, (((1,), (1,)), ((), ())), preferred_element_type=F32)


def _layer_norm_rows(z, g, b):
    mu = jnp.mean(z, axis=1, keepdims=True)
    zc = z - mu
    var = jnp.mean(zc * zc, axis=1, keepdims=True)
    return zc * lax.rsqrt(var + LN_EPS) * g + b


def _proj_ab_kernel(x_ref, w_ref, cn_ref, s1_ref, s2_ref, pick_ref, prk_ref, q_ref, kk_ref, gt_ref, vt_ref, cx_ref):
    xb = x_ref[...].astype(BF16)
    for c in range(4):
        prk_ref[:, c * RK_W:(c + 1) * RK_W] = jnp.dot(xb, w_ref[:, c * RK_W:(c + 1) * RK_W],
                                                      preferred_element_type=F32)
    cn, s1, s2 = cn_ref[...], s1_ref[...], s2_ref[...]

    def rope(y):
        return y * cn + pltpu.roll(y, LANES - ROPE_DIM // 2, 1) * s1 + pltpu.roll(y, ROPE_DIM // 2, 1) * s2

    base = 4 * RK_W
    yq = jnp.dot(xb, w_ref[:, base:base + NS_W], preferred_element_type=F32)
    for c in range(NS_W // LANES):
        q_ref[:, c * LANES:(c + 1) * LANES] = (rope(yq[:, c * LANES:(c + 1) * LANES]) * Q_SCALE).astype(BF16)
    base += NS_W
    ykv = jnp.dot(xb, w_ref[:, base:base + 6 * LANES], preferred_element_type=F32)
    for c in range(6):
        y = ykv[:, c * LANES:(c + 1) * LANES]
        if c % 2 == 0:
            y = rope(y)
        if c < 2:
            yb = y.astype(BF16)
            for l in range(CMP_STRIDE):
                cx_ref[c, :, l * LANES:(l + 1) * LANES] = jnp.dot(pick_ref[l], yb,
                                                                  preferred_element_type=F32).astype(BF16)
        elif c % 2 == 0:
            kk_ref[:, (c // 2 - 1) * LANES:(c // 2) * LANES] = y.astype(BF16)
        else:
            for t in range(y.shape[0] // QT):
                vt_ref[c // 2 - 1, t] = y[t * QT:(t + 1) * QT].T.astype(BF16)
    base += 6 * LANES
    gt_ref[...] = jax.nn.sigmoid(jnp.dot(xb, w_ref[:, base:base + LANES], preferred_element_type=F32))


def _proj_ab(xt, w, cn, s1, s2, S, tm=512):
    T = xt.shape[0]
    ncols = w.shape[1]
    nseq = S // tm
    row = lambda i: (i, 0)
    tab = lambda i: (i % nseq, 0)
    ng = tm // CMP_STRIDE
    pick = (jnp.arange(tm, dtype=I32)[None, None, :]
            == jnp.arange(ng, dtype=I32)[None, :, None] * CMP_STRIDE + jnp.arange(CMP_STRIDE, dtype=I32)[:, None, None])
    pick = pick.astype(BF16)
    return pl.pallas_call(
        _proj_ab_kernel,
        grid=(T // tm,),
        in_specs=[pl.BlockSpec((tm, D_MODEL), row),
                  pl.BlockSpec((D_MODEL, ncols), lambda i: (0, 0)),
                  pl.BlockSpec((tm, LANES), tab), pl.BlockSpec((tm, LANES), tab), pl.BlockSpec((tm, LANES), tab),
                  pl.BlockSpec(pick.shape, lambda i: (0, 0, 0))],
        out_specs=[pl.BlockSpec((tm, 4 * RK_W), row), pl.BlockSpec((tm, NS_W), row),
                   pl.BlockSpec((tm, 2 * LANES), row), pl.BlockSpec((tm, LANES), row),
                   pl.BlockSpec((2, tm // QT, LANES, QT), lambda i: (0, i, 0, 0)),
                   pl.BlockSpec((2, ng, CMP_STRIDE * LANES), lambda i: (0, i, 0))],
        out_shape=[jax.ShapeDtypeStruct((T, 4 * RK_W), F32), jax.ShapeDtypeStruct((T, NS_W), BF16),
                   jax.ShapeDtypeStruct((T, 2 * LANES), BF16), jax.ShapeDtypeStruct((T, LANES), F32),
                   jax.ShapeDtypeStruct((2, T // QT, LANES, QT), BF16),
                   jax.ShapeDtypeStruct((2, T // CMP_STRIDE, CMP_STRIDE * LANES), BF16)],
        compiler_params=_cparams(("parallel",)),
        name="proj_ab",
    )(xt, w, cn, s1, s2, pick)


def _rwkv_prep_kernel(p_ref, mu_ref, w0_ref, w1_ref, w2_ref, a0_ref, a1_ref, a2_ref, g1_ref, g2_ref,
                      kk_ref, ka_ref, rk_ref, ones_ref,
                      r_o, w_o, k_o, v_o, kk_o, b_o, g_o, bon_o, carry):
    j = pl.program_id(1)
    p = p_ref[...]
    tm = p.shape[0]

    @pl.when(j == 0)
    def _():
        carry[...] = jnp.zeros_like(carry)

    rowi = lax.broadcasted_iota(I32, p.shape, 0)
    prev = jnp.where(rowi == 0, carry[...], pltpu.roll(p, 1, 0))
    carry[...] = p[tm - 1:tm, :]
    dp = prev - p
    sl = lambda a, c: a[:, c * RK_W:(c + 1) * RK_W]
    mu = mu_ref[...]
    r = sl(p, 0) + sl(dp, 0) * mu[0:1]
    k = sl(p, 1) + sl(dp, 1) * mu[1:2]
    v = sl(p, 2) + sl(dp, 2) * mu[2:3]
    xw = sl(p, 3) + sl(dp, 3) * mu[3:4]
    xa = sl(p, 3) + sl(dp, 3) * mu[4:5]
    xg = sl(p, 3) + sl(dp, 3) * mu[5:6]
    lw = -RK_DECAY_SCALE * jax.nn.sigmoid(w0_ref[...] + _bdot(jnp.tanh(_bdot(xw, w1_ref[...])), w2_ref[...]))
    a = jax.nn.sigmoid(a0_ref[...] + _bdot(_bdot(xa, a1_ref[...]), a2_ref[...]))
    g = _bdot(jax.nn.sigmoid(_bdot(xg, g1_ref[...])), g2_ref[...])
    ones = ones_ref[...]

    def head_sum(t):
        hi = t.astype(BF16)
        lo = (t - hi.astype(F32)).astype(BF16)
        return (jnp.dot(hi, ones, preferred_element_type=F32) + jnp.dot(lo, ones, preferred_element_type=F32))

    kk = k * kk_ref[...]
    kk = kk / jnp.maximum(jnp.sqrt(head_sum(kk * kk)), 1e-12)
    km = k * (1.0 + (a - 1.0) * ka_ref[...])
    bon = head_sum(r * km * rk_ref[...]) * v
    w_o[...] = lw
    for ref, val in ((r_o, r), (k_o, km), (v_o, v), (kk_o, kk), (b_o, kk * a), (g_o, g), (bon_o, bon)):
        ref[...] = val.astype(ref.dtype)


def _rwkv_prep(prk3, mu, w0, w1, w2, a0, a1, a2, g1, g2, k_k, k_a, r_k, ones, tm=512):
    B, S, _ = prk3.shape
    full = lambda a: pl.BlockSpec(a.shape, lambda b, j: (0,) * a.ndim)
    params = [mu, w0, w1, w2, a0, a1, a2, g1, g2, k_k, k_a, r_k, ones]
    ospec = pl.BlockSpec((None, tm, RK_W), lambda b, j: (b, j, 0))
    return pl.pallas_call(
        _rwkv_prep_kernel,
        grid=(B, S // tm),
        in_specs=[pl.BlockSpec((None, tm, 4 * RK_W), lambda b, j: (b, j, 0))] + [full(a) for a in params],
        out_specs=[ospec] * 8,
        out_shape=[jax.ShapeDtypeStruct((B, S, RK_W), F32 if i == 1 else BF16) for i in range(8)],
        scratch_shapes=[pltpu.VMEM((1, 4 * RK_W), F32)],
        compiler_params=_cparams(("parallel", "arbitrary")),
        name="rwkv_prep",
    )(prk3, *params)


RK_CHUNK = 16


def _pdot(a, b, dims, precise):
    if precise:
        return lax.dot_general(a, b, (dims, ((), ())), precision=HI, preferred_element_type=F32)
    return lax.dot_general(a.astype(BF16), b.astype(BF16), (dims, ((), ())), preferred_element_type=F32)


def _rwkv_chunk_kernel(r_ref, lw_ref, k_ref, v_ref, kk_ref, b_ref, g_ref, bon_ref, lng_ref, lnb_ref, o_ref, ht,
                       *, precise):
    j = pl.program_id(1)

    @pl.when(j == 0)
    def _():
        ht[...] = jnp.zeros_like(ht)

    NB, TT = r_ref.shape[0], r_ref.shape[1]
    C, N = RK_CHUNK, RK_HEAD
    mm = lambda a, b: _pdot(a, b, ((1,), (0,)), precise)
    mm_nt = lambda a, b: _pdot(a, b, ((1,), (1,)), precise)
    mm_tn = lambda a, b: _pdot(a, b, ((0,), (0,)), precise)
    wide = lambda ref: jnp.concatenate([ref[n] for n in range(NB)], axis=1)

    lw = wide(lw_ref)
    rowc = lax.broadcasted_iota(I32, lw.shape, 0) & (C - 1)
    linc, lrev = lw, lw
    sh = 1
    while sh < C:
        linc = linc + jnp.where(rowc >= sh, pltpu.roll(linc, sh, 0), 0.0)
        lrev = lrev + jnp.where(rowc < C - sh, pltpu.roll(lrev, TT - sh, 0), 0.0)
        sh *= 2
    lrev = lrev - lw
    r, k, v, kk, b = wide(r_ref), wide(k_ref), wide(v_ref), wide(kk_ref), wide(b_ref)
    e_in, e_inv, e_rev = jnp.exp(linc), jnp.exp(-linc), jnp.exp(lrev)
    kkd = kk * jnp.exp(linc - lw)
    rd = r * e_in
    binv, kinv = b * e_inv, k * e_inv
    bd, kd = b * e_rev, k * e_rev
    gam = jnp.exp(linc + lrev)

    ti = lax.broadcasted_iota(I32, (TT, TT), 0)
    tj = lax.broadcasted_iota(I32, (TT, TT), 1)
    same = (ti // C) == (tj // C)
    strict = same & (tj < ti)
    incl = same & (tj <= ti)

    H = range(NB * RK_HEADS)
    lo = lambda a: a if precise else a.astype(BF16)
    hs = lambda a: [a[:, h * N:(h + 1) * N] for h in H]
    rows2 = lambda a, b: jnp.concatenate([a, b], axis=0)
    kkd_h, rd_h, v_h = hs(lo(kkd)), hs(lo(rd)), hs(lo(v))
    binv_h, kinv_h, bd_h, kd_h = hs(lo(binv)), hs(lo(kinv)), hs(lo(bd)), hs(lo(kd))
    gam_h = hs(gam)
    gm = [mm_nt(rows2(kkd_h[h], rd_h[h]), rows2(binv_h[h], kinv_h[h])) for h in H]
    a_b = [lo(jnp.where(strict, gm[h][:TT, :TT], 0.0)) for h in H]
    b_rb = [lo(jnp.where(incl, gm[h][TT:, :TT], 0.0)) for h in H]
    akb = [lo(rows2(jnp.where(strict, gm[h][:TT, TT:], 0.0), jnp.where(incl, gm[h][TT:, TT:], 0.0))) for h in H]
    av = [mm(akb[h], v_h[h]) for h in H]
    x = [jnp.concatenate([kkd_h[h].astype(F32), av[h][:TT]], axis=1) for h in H]
    a2 = [lo(mm(a_b[h], a_b[h])) for h in H]
    a4 = [lo(mm(a2[h], a2[h])) for h in H]
    a8 = [lo(mm(a4[h], a4[h])) for h in H]
    x = [x[h] + mm(a8[h], lo(x[h])) for h in H]
    x = [x[h] + mm(a4[h], lo(x[h])) for h in H]
    x = [x[h] + mm(a2[h], lo(x[h])) for h in H]
    x = [x[h] - mm(a_b[h], lo(x[h])) for h in H]
    wt = [lo(x[h][:, :N]) for h in H]
    h_t = [ht[h] for h in H]
    us = [[] for _ in H]
    rhs = [[] for _ in H]
    for c in range(TT // C):
        rs = slice(c * C, (c + 1) * C)
        xh = [mm_nt(rows2(wt[h][rs], rd_h[h][rs]), lo(h_t[h])) for h in H]
        for h in H:
            u_c = -(xh[h][:C] + x[h][rs, N:])
            us[h].append(u_c)
            rhs[h].append(xh[h][C:])
        upd = [mm_tn(rows2(lo(us[h][c]), v_h[h][rs]), rows2(bd_h[h][rs], kd_h[h][rs])) for h in H]
        h_t = [h_t[h] * gam_h[h][c * C:c * C + 1] + upd[h] for h in H]
    outs = []
    for h in H:
        ht[h] = h_t[h]
        o = jnp.concatenate(rhs[h], axis=0) + mm(b_rb[h], lo(jnp.concatenate(us[h], axis=0))) + av[h][TT:]
        mu = jnp.mean(o, axis=1, keepdims=True)
        oc = o - mu
        var = jnp.mean(oc * oc, axis=1, keepdims=True)
        outs.append(oc * lax.rsqrt(var + RK_LN_EPS))
    for n in range(NB):
        on = jnp.concatenate(outs[n * RK_HEADS:(n + 1) * RK_HEADS], axis=1)
        o_ref[n] = (on * lng_ref[...] + lnb_ref[...] + bon_ref[n]) * g_ref[n]


def _rwkv_chunk(r, lw, k, v, kk, bb, g, bon, lng, lnb, tt=128, nb=4, precise=False):
    B, S, _ = r.shape
    nb = nb if B % nb == 0 else 1
    blk = pl.BlockSpec((nb, tt, RK_W), lambda b, j: (b, j, 0))
    vec = pl.BlockSpec((1, RK_W), lambda b, j: (0, 0))
    return pl.pallas_call(
        functools.partial(_rwkv_chunk_kernel, precise=precise),
        grid=(B // nb, S // tt),
        in_specs=[blk] * 8 + [vec, vec],
        out_specs=blk,
        out_shape=jax.ShapeDtypeStruct((B, S, RK_W), F32),
        scratch_shapes=[pltpu.VMEM((nb * RK_HEADS, RK_HEAD, RK_HEAD), F32)],
        compiler_params=_cparams(("parallel", "arbitrary")),
        name="rwkv_chunk",
    )(r, lw, k, v, kk, bb, g, bon, lng, lnb)


def _nsa_compress_kernel(x_ref, w1g_ref, w1f_ref, w2_ref, pe_ref, o_ref, ot_ref):
    bias = jnp.dot(jnp.broadcast_to(pe_ref[...], (SUBLANES, CMP_LEN * NS_HEAD)), w1f_ref[...], precision=HI,
                   preferred_element_type=F32)[0:1]
    x = x_ref[...]
    n = x.shape[0]
    outs = []
    for g in range(NS_KV):
        ya = jnp.dot(x, w1g_ref[g, 0], preferred_element_type=F32)
        yb = jnp.dot(x, w1g_ref[g, 1], preferred_element_type=F32)
        h = ya + pltpu.roll(yb, n - 1, 0) + bias
        outs.append(_bdot(jax.nn.gelu(h), w2_ref[...]))
    out = jnp.concatenate(outs, axis=1)
    o_ref[...] = out.astype(BF16)
    ot_ref[...] = out.T.astype(BF16)


def _nsa_compress(cx, w1, w2b, pe):
    _, B, ncp, width = cx.shape
    hid = w1.shape[-1]
    w1h = w1.reshape(2, 2, CMP_STRIDE, 1, NS_HEAD, hid)
    zero = jnp.zeros_like(w1h)
    w1g = jnp.stack([jnp.concatenate([w1h, zero], axis=3), jnp.concatenate([zero, w1h], axis=3)], axis=1)
    w1g = w1g.reshape(2, NS_KV, 2, width, hid).astype(BF16)
    return pl.pallas_call(
        _nsa_compress_kernel,
        grid=(B, 2),
        in_specs=[pl.BlockSpec((None, None, ncp, width), lambda b, c: (c, b, 0, 0)),
                  pl.BlockSpec((None, NS_KV, 2, width, hid), lambda b, c: (c, 0, 0, 0, 0)),
                  pl.BlockSpec((None, CMP_LEN * NS_HEAD, hid), lambda b, c: (c, 0, 0)),
                  pl.BlockSpec((None, hid, NS_HEAD), lambda b, c: (c, 0, 0)),
                  pl.BlockSpec((None, 1, CMP_LEN * NS_HEAD), lambda b, c: (c, 0, 0))],
        out_specs=[pl.BlockSpec((None, None, ncp, LANES), lambda b, c: (b, c, 0, 0)),
                   pl.BlockSpec((None, None, LANES, ncp), lambda b, c: (b, c, 0, 0))],
        out_shape=[jax.ShapeDtypeStruct((B, 2, ncp, LANES), BF16), jax.ShapeDtypeStruct((B, 2, LANES, ncp), BF16)],
        compiler_params=_cparams(("parallel", "parallel")),
        name="nsa_compress",
    )(cx, w1g, w1, w2b, pe)


def _nsa_attn_kernel(q_ref, kc_ref, vct_ref, ks_ref, vst_ref, kw_ref, vwt_ref, gt_ref, ovt_ref, ext_ref, o_ref,
                     m_s, l_s, acc_s, m_w, l_w, acc_w, s_buf, *, ncp):
    i = pl.program_id(1)
    s0 = i * QT
    heads = range(NS_HEADS)
    hcols = lambda h: slice(h * QT, (h + 1) * QT)
    gcols = lambda h: slice(h // NS_HPG * QT, (h // NS_HPG + 1) * QT)
    iota = lambda shape, d: lax.broadcasted_iota(I32, shape, d)
    qpos = lambda shape: s0 + (iota(shape, 1) & (QT - 1))

    def col_reduce(x, op, final):
        n = x.shape[0]
        while n > SUBLANES:
            n //= 2
            x = op(x[:n], x[n:])
        return final(x, axis=0, keepdims=True)

    q8 = q_ref[...].astype(F32)
    zeros = jnp.zeros((QT, NS_HEAD), F32)
    qt = []
    for h in heads:
        qh = q8[:, h * NS_HEAD:(h + 1) * NS_HEAD]
        qt.append(jnp.concatenate([qh, zeros] if h < NS_HPG else [zeros, qh], axis=1).T)
    qt = jnp.concatenate(qt, axis=1).astype(BF16)

    sc = jnp.dot(kc_ref[...], qt, preferred_element_type=F32)
    n_row = iota((ncp, QT), 0)
    cmask = (n_row * CMP_STRIDE + (CMP_LEN - 1) <= s0 + iota((ncp, QT), 1)) & (n_row < ncp - 1)
    cpen = jnp.where(cmask, 0.0, NEG)
    sc = jnp.concatenate([sc[:, hcols(h)] + cpen for h in heads], axis=1)
    mc = col_reduce(sc, jnp.maximum, jnp.max)
    pc = jnp.exp2(sc - mc)
    lc = col_reduce(pc, jnp.add, jnp.sum)
    pc = pc * jnp.where(mc > 0.5 * NEG, 1.0 / lc, 0.0)
    pcs = jnp.concatenate([functools.reduce(jnp.add, [pc[:, hcols(h)] for h in range(g * NS_HPG, (g + 1) * NS_HPG)])
                           for g in range(NS_KV)], axis=1)

    def group_dot(vt, p, extra=None):
        half = NS_HPG * QT
        outs = []
        for g in range(NS_KV):
            vg = vt[g * NS_HEAD:(g + 1) * NS_HEAD]
            if extra is not None:
                vg = jnp.concatenate([vg, extra], axis=0)
            outs.append(jnp.dot(vg, p[:, g * half:(g + 1) * half], preferred_element_type=F32))
        return jnp.concatenate(outs, axis=1)

    o_c = group_dot(vct_ref[...], pc.astype(BF16))

    imp = jnp.dot(ovt_ref[...], pcs, precision=HI, preferred_element_type=F32)
    blk = iota((LANES, NS_KV * QT), 0)
    cur = qpos((LANES, NS_KV * QT)) // SEL_BLOCK
    valid = blk <= cur
    forced = (blk == 0) | (blk == cur) | (blk == cur - 1)
    pri = jnp.where(valid & ~forced, imp, -jnp.inf)
    picked = forced
    blkf = blk.astype(F32)
    for _ in range(SEL_TOPK - 3):
        mx = col_reduce(pri, jnp.maximum, jnp.max)
        hit = blkf == col_reduce(jnp.where(pri == mx, blkf, float(LANES)), jnp.minimum, jnp.min)
        picked = picked | hit
        pri = jnp.where(hit, -jnp.inf, pri)
    selpen = jnp.where(picked & valid, 0.0, NEG).astype(BF16)
    wq = jnp.concatenate([qt, jnp.concatenate([selpen[:, gcols(h)] for h in heads], axis=1)], axis=0)

    def attend(s, vt, pen, m_ref, l_ref, acc_ref):
        if pen is not None:
            s = jnp.concatenate([s[:, hcols(h)] + pen for h in heads], axis=1)
        m_old = m_ref[...]
        m_new = jnp.maximum(m_old, col_reduce(s, jnp.maximum, jnp.max))
        alpha = jnp.exp2(m_old - m_new)
        p = jnp.exp2(s - m_new).astype(BF16)
        m_ref[...] = m_new
        pv = group_dot(vt, p, extra=jnp.ones((BF16_ROWS, vt.shape[1]), BF16))
        l_ref[...] = alpha * l_ref[...] + pv[NS_HEAD:NS_HEAD + 1]
        acc_ref[...] = alpha * acc_ref[...] + pv[:NS_HEAD]

    def reset(m_ref, l_ref, acc_ref):
        m_ref[...] = jnp.full(m_ref.shape, NEG, F32)
        l_ref[...] = jnp.zeros_like(l_ref)
        acc_ref[...] = jnp.zeros_like(acc_ref)

    tiles = lambda ref, t0, n: jnp.concatenate([ref[t0 + c] for c in range(n)], axis=1)

    reset(m_s, l_s, acc_s)
    diag = i // (KT // QT)

    def sel_scores(kt, slot):
        k0 = pl.multiple_of(kt * KT, KT)
        keys = jnp.concatenate([ks_ref[pl.ds(k0, KT), :], ext_ref[pl.ds(k0, KT), :]], axis=1)
        s_buf[slot] = jnp.dot(keys, wq, preferred_element_type=F32)

    def sel_attend(kt, slot, causal):
        pen = None
        if causal:
            pen = jnp.where(kt * KT + iota((KT, QT), 0) <= s0 + iota((KT, QT), 1), 0.0, NEG)
        attend(s_buf[slot], tiles(vst_ref, kt * (KT // QT), KT // QT), pen, m_s, l_s, acc_s)

    def sel_pair(j, c):
        sel_scores(2 * j + 1, 1)
        sel_attend(2 * j, 0, False)
        sel_scores(2 * j + 2, 0)
        sel_attend(2 * j + 1, 1, False)
        return c

    d0 = pl.multiple_of(s0, QT)
    a0 = pl.multiple_of(jnp.maximum(s0 - WINDOW, 0), QT)
    s_d = jnp.dot(kw_ref[pl.ds(d0, QT), :], qt, preferred_element_type=F32)
    s_a = jnp.dot(kw_ref[pl.ds(a0, WINDOW), :], qt, preferred_element_type=F32)
    pen_d = jnp.where(iota((QT, QT), 0) <= iota((QT, QT), 1), 0.0, NEG)
    kpos = a0 + iota((WINDOW, QT), 0)
    pen_a = jnp.where((kpos < s0) & (kpos > s0 + iota((WINDOW, QT), 1) - WINDOW), 0.0, NEG)

    sel_scores(0, 0)
    reset(m_w, l_w, acc_w)
    attend(s_d, vwt_ref[i], pen_d, m_w, l_w, acc_w)
    attend(s_a, tiles(vwt_ref, a0 // QT, WINDOW // QT), pen_a, m_w, l_w, acc_w)
    lax.fori_loop(0, diag // 2, sel_pair, 0)

    @pl.when(diag % 2 == 1)
    def _():
        sel_scores(diag, 1)
        sel_attend(diag - 1, 0, False)
        sel_attend(diag, 1, True)

    @pl.when(diag % 2 == 0)
    def _():
        sel_attend(diag, 0, True)

    gtt = gt_ref[...].T
    o_s = acc_s[...] / l_s[...]
    o_w = acc_w[...] / l_w[...]
    outs = []
    for h in heads:
        gate = lambda br: gtt[br * NS_HEADS + h:br * NS_HEADS + h + 1]
        outs.append(gate(0) * o_c[:, hcols(h)] + gate(1) * o_s[:, hcols(h)] + gate(2) * o_w[:, hcols(h)])
    pairs = [jnp.concatenate(outs[p:p + 2], axis=0).T for p in range(0, NS_HEADS, 2)]
    o_ref[...] = jnp.concatenate(pairs, axis=1).astype(BF16)


def _nsa_attn(q3, kcv, kcvt, kv3, vt, gt3, ovt):
    B, S, _ = q3.shape
    ncp = kcv.shape[2]
    R = NS_HEADS * QT
    ext = (jnp.arange(S, dtype=I32)[:, None] // SEL_BLOCK == jnp.arange(LANES, dtype=I32)[None, :]).astype(BF16)
    seq = lambda c: pl.BlockSpec((None, S, LANES), lambda b, i, c=c: (b, 0, c))
    seqt = lambda c: pl.BlockSpec((None, None, S // QT, LANES, QT), lambda b, i, c=c: (c, b, 0, 0, 0))
    return pl.pallas_call(
        functools.partial(_nsa_attn_kernel, ncp=ncp),
        grid=(B, S // QT),
        in_specs=[pl.BlockSpec((None, QT, NS_W), lambda b, i: (b, i, 0)),
                  pl.BlockSpec((None, None, ncp, LANES), lambda b, i: (b, 0, 0, 0)),
                  pl.BlockSpec((None, None, LANES, ncp), lambda b, i: (b, 1, 0, 0)),
                  seq(0), seqt(0), seq(1), seqt(1),
                  pl.BlockSpec((None, QT, LANES), lambda b, i: (b, i, 0)),
                  pl.BlockSpec(ovt.shape, lambda b, i: (0, 0)),
                  pl.BlockSpec(ext.shape, lambda b, i: (0, 0))],
        out_specs=pl.BlockSpec((None, QT, NS_W), lambda b, i: (b, i, 0)),
        out_shape=jax.ShapeDtypeStruct((B, S, NS_W), BF16),
        scratch_shapes=([pltpu.VMEM((1, R), F32), pltpu.VMEM((1, R), F32), pltpu.VMEM((NS_HEAD, R), F32)] * 2
                        + [pltpu.VMEM((2, KT, R), F32)]),
        compiler_params=_cparams(("parallel", "arbitrary")),
        name="nsa_attn",
    )(q3, kcv, kcvt, kv3, vt, kv3, vt, gt3, ovt, ext)


def _first_argmax(vals):
    m = vals[0]
    for v in vals[1:]:
        m = jnp.maximum(m, v)
    idx = jnp.full(m.shape, len(vals) - 1, I32)
    for j in range(len(vals) - 2, -1, -1):
        idx = jnp.where(vals[j] == m, j, idx)
    return m, idx


def _outproj_router_kernel(*refs, n_in):
    acts, ws = refs[:n_in], refs[n_in:2 * n_in]
    x_ref, lng_ref, lnb_ref, rwt_ref, rb_ref, tri_ref = refs[2 * n_in:2 * n_in + 6]
    y_ref, e_ref, wt_ref, pos_ref, cnt_ref, cnt = refs[2 * n_in + 6:]
    i = pl.program_id(0)

    @pl.when(i == 0)
    def _():
        cnt[...] = jnp.zeros_like(cnt)

    mix = _bdot(acts[0][...], ws[0][...])
    for a, w in zip(acts[1:], ws[1:]):
        mix = mix + _bdot(a[...], w[...])
    y = _layer_norm_rows(ALPHA * x_ref[...] + mix, lng_ref[...], lnb_ref[...])
    y_ref[...] = y

    split = lambda t: (t.astype(BF16), (t - t.astype(BF16).astype(F32)).astype(BF16))
    (w_hi, w_lo), (y_hi, y_lo) = split(rwt_ref[...]), split(y)
    logit = _dot_nt(w_hi, y_hi) + (_dot_nt(w_hi, y_lo) + _dot_nt(w_lo, y_hi))
    aff = jax.nn.sigmoid(logit)
    biased = aff + rb_ref[...]
    neg_inf = -jnp.inf
    g_score, g_i1, g_i2 = [], [], []
    for gi in range(N_GROUPS):
        vals = [biased[gi * EXP_PER_GROUP + j:gi * EXP_PER_GROUP + j + 1, :] for j in range(EXP_PER_GROUP)]
        m1, i1 = _first_argmax(vals)
        m2, i2 = _first_argmax([jnp.where(i1 == j, neg_inf, vals[j]) for j in range(EXP_PER_GROUP)])
        g_score.append(m1 + m2)
        g_i1.append(i1)
        g_i2.append(i2)
    _, grp = _first_argmax(g_score)
    loc1, loc2 = g_i1[-1], g_i2[-1]
    for gi in range(N_GROUPS - 2, -1, -1):
        loc1 = jnp.where(grp == gi, g_i1[gi], loc1)
        loc2 = jnp.where(grp == gi, g_i2[gi], loc2)
    e1 = grp * EXP_PER_GROUP + loc1
    e2 = grp * EXP_PER_GROUP + loc2
    eio = lax.broadcasted_iota(I32, aff.shape, 0)
    oh1 = eio == e1
    oh2 = eio == e2
    a1 = jnp.sum(jnp.where(oh1, aff, 0.0), axis=0, keepdims=True)
    a2 = jnp.sum(jnp.where(oh2, aff, 0.0), axis=0, keepdims=True)
    tot = a1 + a2
    e_ref[...] = jnp.concatenate([e1, e2], axis=0)
    wt_ref[...] = jnp.concatenate([a1 / tot, a2 / tot], axis=0)

    ohs = oh1.astype(F32) + oh2.astype(F32)
    before = jnp.dot(ohs.astype(BF16), tri_ref[...], preferred_element_type=F32) + cnt[...]
    p1 = jnp.sum(jnp.where(oh1, before, 0.0), axis=0, keepdims=True)
    p2 = jnp.sum(jnp.where(oh2, before, 0.0), axis=0, keepdims=True)
    pos_ref[...] = jnp.concatenate([p1, p2], axis=0).astype(I32)
    cnt[...] = cnt[...] + jnp.sum(ohs, axis=1, keepdims=True)
    cnt_ref[...] = jnp.broadcast_to(cnt[...], cnt_ref.shape)


def _outproj_router(acts, ws, xres, lng, lnb, rwt, rb, tm=1024):
    T = xres.shape[0]
    n_in = len(acts)
    tri = (lax.broadcasted_iota(I32, (tm, tm), 0) < lax.broadcasted_iota(I32, (tm, tm), 1)).astype(BF16)
    row = lambda i: (i, 0)
    const = lambda a: pl.BlockSpec(a.shape, lambda i: (0,) * a.ndim)
    lane_blk = pl.BlockSpec((TOP_K, tm), lambda i: (0, i))
    return pl.pallas_call(
        functools.partial(_outproj_router_kernel, n_in=n_in),
        grid=(T // tm,),
        in_specs=([pl.BlockSpec((tm, a.shape[1]), row) for a in acts] + [const(w) for w in ws]
                  + [pl.BlockSpec((tm, D_MODEL), row), const(lng), const(lnb), const(rwt), const(rb), const(tri)]),
        out_specs=[pl.BlockSpec((tm, D_MODEL), row), lane_blk, lane_blk, lane_blk,
                   pl.BlockSpec((N_EXPERTS, LANES), lambda i: (0, 0))],
        out_shape=[jax.ShapeDtypeStruct((T, D_MODEL), F32), jax.ShapeDtypeStruct((TOP_K, T), I32),
                   jax.ShapeDtypeStruct((TOP_K, T), F32), jax.ShapeDtypeStruct((TOP_K, T), I32),
                   jax.ShapeDtypeStruct((N_EXPERTS, LANES), F32)],
        scratch_shapes=[pltpu.VMEM((N_EXPERTS, 1), F32)],
        compiler_params=_cparams(("arbitrary",)),
        name="outproj_router",
    )(*acts, *ws, xres, lng, lnb, rwt, rb, tri)


def _dispatch_kernel(zrow_ref, dest_hbm, x_ref, xs_hbm, dsm, zbuf, sem_idx, sem, sem_z):
    i = pl.program_id(0)
    tm = x_ref.shape[0]

    @pl.when(i == 0)
    def _():
        zbuf[...] = jnp.zeros_like(zbuf)
        zero_copy = lambda j: pltpu.make_async_copy(
            zbuf, xs_hbm.at[pl.ds(pl.multiple_of(jnp.maximum(zrow_ref[j], 0), MOE_BLOCK), MOE_BLOCK)], sem_z)
        for j in range(zrow_ref.shape[0]):
            @pl.when(zrow_ref[j] >= 0)
            def _():
                zero_copy(j).start()
        for j in range(zrow_ref.shape[0]):
            @pl.when(zrow_ref[j] >= 0)
            def _():
                zero_copy(j).wait()

    idx_copy = pltpu.make_async_copy(dest_hbm.at[i], dsm, sem_idx)
    idx_copy.start()
    idx_copy.wait()

    def row_copy(r, k):
        return pltpu.make_async_copy(x_ref.at[pl.ds(r, 1)], xs_hbm.at[pl.ds(dsm[k * tm + r], 1)], sem)

    def start(r, c):
        for k in range(TOP_K):
            row_copy(r, k).start()
        return c

    lax.fori_loop(0, tm, start, 0, unroll=ROW_DMA_UNROLL)
    for k in range(TOP_K):
        pltpu.make_async_copy(x_ref, xs_hbm.at[pl.ds(0, tm)], sem).wait()


def _dispatch(zero_rows, dest_tiles, x, rows, tm):
    T = x.shape[0]
    return pl.pallas_call(
        _dispatch_kernel,
        grid_spec=pltpu.PrefetchScalarGridSpec(
            num_scalar_prefetch=1,
            grid=(T // tm,),
            in_specs=[pl.BlockSpec(memory_space=pl.ANY), pl.BlockSpec((tm, D_MODEL), lambda i, z: (i, 0))],
            out_specs=pl.BlockSpec(memory_space=pl.ANY),
            scratch_shapes=[pltpu.SMEM((TOP_K * tm,), I32), pltpu.VMEM((MOE_BLOCK, D_MODEL), F32),
                            pltpu.SemaphoreType.DMA(()), pltpu.SemaphoreType.DMA(()), pltpu.SemaphoreType.DMA(())]),
        out_shape=jax.ShapeDtypeStruct((rows, D_MODEL), F32),
        compiler_params=_cparams(("arbitrary",)),
        name="moe_dispatch",
    )(zero_rows, dest_tiles, x)


def _ffn_kernel(be_ref, nu_ref, xs_ref, wg_ref, wu_ref, wd_ref, y_ref, h_ref, wg_b, wu_b, wd_b):
    i = pl.program_id(0)
    half = D_EXPERT // 2
    used = i < nu_ref[0]

    @pl.when(used & ((i == 0) | (be_ref[i] != be_ref[jnp.maximum(i - 1, 0)])))
    def _():
        wg_b[...] = wg_ref[...].astype(BF16)
        wu_b[...] = wu_ref[...].astype(BF16)
        wd_b[...] = wd_ref[...].astype(BF16)

    @pl.when(used)
    def _():
        xb = xs_ref[...].astype(BF16)
        for c in range(2):
            cs = slice(c * half, (c + 1) * half)
            gt = jnp.dot(xb, wg_b[:, cs], preferred_element_type=F32)
            up = jnp.dot(xb, wu_b[:, cs], preferred_element_type=F32)
            h_ref[:, cs] = (jax.nn.silu(gt) * up).astype(BF16)
        y_ref[...] = jnp.dot(h_ref[...], wd_b[...], preferred_element_type=F32)

    @pl.when(i >= nu_ref[0])
    def _():
        y_ref[...] = jnp.zeros_like(y_ref)


def _ffn(blk_exp, n_used, xs, wg, wu, wd, layer):
    rows = xs.shape[0]
    wspec = lambda a: pl.BlockSpec((None, None) + a.shape[2:], lambda i, be, nu: (layer, be[i], 0, 0))
    wbuf = lambda a: pltpu.VMEM(a.shape[2:], BF16)
    return pl.pallas_call(
        _ffn_kernel,
        grid_spec=pltpu.PrefetchScalarGridSpec(
            num_scalar_prefetch=2,
            grid=(rows // MOE_BLOCK,),
            in_specs=[pl.BlockSpec((MOE_BLOCK, D_MODEL), lambda i, be, nu: (i, 0)), wspec(wg), wspec(wu), wspec(wd)],
            out_specs=pl.BlockSpec((MOE_BLOCK, D_MODEL), lambda i, be, nu: (i, 0)),
            scratch_shapes=[pltpu.VMEM((MOE_BLOCK, D_EXPERT), BF16), wbuf(wg), wbuf(wu), wbuf(wd)]),
        out_shape=jax.ShapeDtypeStruct((rows, D_MODEL), F32),
        compiler_params=_cparams(("arbitrary",)),
        name="moe_ffn",
    )(blk_exp, n_used, xs, wg, wu, wd)


def _combine_kernel(dest_hbm, y_hbm, x_ref, wt_ref, lng_ref, lnb_ref, o_ref, dsm0, dsm1, buf, sem_idx, sem):
    i = pl.program_id(0)
    tm = x_ref.shape[0]
    dsm = (dsm0, dsm1)

    def gather(step, s):
        idx_copy = pltpu.make_async_copy(dest_hbm.at[step], dsm[s], sem_idx)
        idx_copy.start()
        idx_copy.wait()

        def start(r, c):
            for k in range(TOP_K):
                pltpu.make_async_copy(y_hbm.at[pl.ds(dsm[s][k * tm + r], 1)], buf.at[s, k, pl.ds(r, 1)],
                                      sem.at[s]).start()
            return c

        lax.fori_loop(0, tm, start, 0, unroll=ROW_DMA_UNROLL)

    def reduce(s):
        for k in range(TOP_K):
            pltpu.make_async_copy(y_hbm.at[pl.ds(0, tm)], buf.at[s, k], sem.at[s]).wait()
        wt = wt_ref[...]
        z = ALPHA * x_ref[...] + wt[:, 0:1] * buf[s, 0] + wt[:, 1:2] * buf[s, 1]
        o_ref[...] = _layer_norm_rows(z, lng_ref[...], lnb_ref[...])

    @pl.when(i == 0)
    def _():
        gather(0, 0)

    for s in range(2):
        @pl.when(i % 2 == s)
        def _():
            @pl.when(i + 1 < pl.num_programs(0))
            def _():
                gather(i + 1, 1 - s)
            reduce(s)


def _combine(dest_tiles, y, x, wt, lng, lnb, tm):
    T = x.shape[0]
    row = lambda i: (i, 0)
    vec = pl.BlockSpec((1, D_MODEL), lambda i: (0, 0))
    return pl.pallas_call(
        _combine_kernel,
        grid=(T // tm,),
        in_specs=[pl.BlockSpec(memory_space=pl.ANY), pl.BlockSpec(memory_space=pl.ANY),
                  pl.BlockSpec((tm, D_MODEL), row), pl.BlockSpec((tm, TOP_K), row), vec, vec],
        out_specs=pl.BlockSpec((tm, D_MODEL), row),
        out_shape=jax.ShapeDtypeStruct((T, D_MODEL), F32),
        scratch_shapes=[pltpu.SMEM((TOP_K * tm,), I32), pltpu.SMEM((TOP_K * tm,), I32),
                        pltpu.VMEM((2, TOP_K, tm, D_MODEL), F32),
                        pltpu.SemaphoreType.DMA(()), pltpu.SemaphoreType.DMA((2,))],
        compiler_params=_cparams(("arbitrary",)),
        name="moe_combine",
    )(dest_tiles, y, x, wt, lng, lnb)


def _moe(x1, e, wt, pos, cnt, wg, wu, wd, layer, lng, lnb, tm_dispatch=2048, tm_combine=1024):
    T = x1.shape[0]
    tm_dispatch = tm_dispatch if T % tm_dispatch == 0 else tm_combine
    n_blocks = -(-(T * TOP_K) // MOE_BLOCK) + N_EXPERTS
    rows = n_blocks * MOE_BLOCK
    counts = cnt[:, 0].astype(I32)
    padded = (counts + MOE_BLOCK - 1) // MOE_BLOCK * MOE_BLOCK
    pad_end = jnp.cumsum(padded)
    pad_start = pad_end - padded
    dest = pos
    for j in range(N_EXPERTS):
        dest = dest + jnp.where(e == j, pad_start[j], 0)
    blk_start = jnp.arange(n_blocks, dtype=I32) * MOE_BLOCK
    blk_exp = jnp.minimum(jnp.sum((pad_end[None, :] <= blk_start[:, None]).astype(I32), axis=1), N_EXPERTS - 1)
    n_used = (pad_end[-1:] // MOE_BLOCK).astype(I32)
    tiles = lambda tm: dest.reshape(TOP_K, T // tm, tm).transpose(1, 0, 2).reshape(T // tm, TOP_K * tm)
    tail = jnp.where(padded > 0, pad_end - MOE_BLOCK, -1)
    trailing = jnp.arange(n_blocks - N_EXPERTS, n_blocks, dtype=I32)
    trailing = jnp.where(trailing >= n_used[0], trailing * MOE_BLOCK, -1)
    xs = _dispatch(jnp.concatenate([tail, trailing]).astype(I32), tiles(tm_dispatch), x1, rows, tm_dispatch)
    y = _ffn(blk_exp, n_used, xs, wg, wu, wd, layer)
    return _combine(tiles(tm_combine), y, x1, wt.T, lng, lnb, tm_combine)


def _proj_rt_kernel(x_ref, w_ref, c_ref, s_ref, q_ref, k_ref, v_ref, g_ref):
    xb = x_ref[...].astype(BF16)
    cs, sn = c_ref[...], s_ref[...]
    rope = lambda y: y * cs + pltpu.roll(y, RT_QK // 2, 1) * sn
    pair = 2 * RT_QK
    for c in range(RT_QKW // pair):
        yq = jnp.dot(xb, w_ref[:, c * pair:(c + 1) * pair], preferred_element_type=F32)
        yk = jnp.dot(xb, w_ref[:, RT_QKW + c * pair:RT_QKW + (c + 1) * pair], preferred_element_type=F32)
        for h in range(2):
            cols = slice(c * pair + h * RT_QK, c * pair + (h + 1) * RT_QK)
            q_ref[:, cols] = rope(yq[:, h * RT_QK:(h + 1) * RT_QK]).astype(BF16)
            k_ref[:, cols] = (rope(yk[:, h * RT_QK:(h + 1) * RT_QK]) * RT_QK ** -0.5).astype(BF16)
    step = 1024
    for c in range(RT_VW // step):
        cols = slice(c * step, (c + 1) * step)
        v_ref[:, cols] = jnp.dot(xb, w_ref[:, 2 * RT_QKW + c * step:2 * RT_QKW + (c + 1) * step],
                                 preferred_element_type=F32).astype(BF16)
        base = 2 * RT_QKW + RT_VW
        g_ref[:, cols] = jax.nn.silu(jnp.dot(xb, w_ref[:, base + c * step:base + (c + 1) * step],
                                             preferred_element_type=F32))


def _proj_rt(xt, w, cs, sn, S, tm=512):
    T = xt.shape[0]
    nseq = S // tm
    row = lambda i: (i, 0)
    tab = lambda i: (i % nseq, 0)
    return pl.pallas_call(
        _proj_rt_kernel,
        grid=(T // tm,),
        in_specs=[pl.BlockSpec((tm, D_MODEL), row), pl.BlockSpec(w.shape, lambda i: (0, 0)),
                  pl.BlockSpec((tm, RT_QK), tab), pl.BlockSpec((tm, RT_QK), tab)],
        out_specs=[pl.BlockSpec((tm, RT_QKW), row), pl.BlockSpec((tm, RT_QKW), row),
                   pl.BlockSpec((tm, RT_VW), row), pl.BlockSpec((tm, RT_VW), row)],
        out_shape=[jax.ShapeDtypeStruct((T, RT_QKW), BF16), jax.ShapeDtypeStruct((T, RT_QKW), BF16),
                   jax.ShapeDtypeStruct((T, RT_VW), BF16), jax.ShapeDtypeStruct((T, RT_VW), F32)],
        compiler_params=_cparams(("parallel",)),
        name="proj_rt",
    )(xt, w, cs, sn)


def _retention_kernel(q_ref, k_ref, v_ref, sg_ref, dec_ref, qd_ref, kd_ref, cd_ref, gng_ref, gnb_ref, o_ref, state):
    j = pl.program_id(1)

    @pl.when(j == 0)
    def _():
        state[...] = jnp.zeros_like(state)

    NB = q_ref.shape[0]
    H = range(NB * RT_HEADS)
    nh = lambda i: (i // RT_HEADS, i % RT_HEADS)
    qk_cols = lambda i: slice(nh(i)[1] * RT_QK, (nh(i)[1] + 1) * RT_QK)
    v_cols = lambda i: slice(nh(i)[1] * RT_V, (nh(i)[1] + 1) * RT_V)
    q = [q_ref[nh(i)[0], :, qk_cols(i)] for i in H]
    k = [k_ref[nh(i)[0], :, qk_cols(i)] for i in H]
    v = [v_ref[nh(i)[0], :, v_cols(i)] for i in H]
    r_old = [state[i] for i in H]
    att = [(_dot_nt(q[i], k[i]) * dec_ref[nh(i)[1]]).astype(BF16) for i in H]
    cross = [jnp.dot(q[i], r_old[i].astype(BF16), preferred_element_type=F32) * qd_ref[nh(i)[1]] for i in H]
    inner = [jnp.dot(att[i], v[i], preferred_element_type=F32) for i in H]
    kdec = [(k[i].astype(F32) * kd_ref[nh(i)[1]]).astype(BF16) for i in H]
    for i in H:
        upd = lax.dot_general(kdec[i], v[i], (((0,), (0,)), ((), ())), preferred_element_type=F32)
        state[i] = r_old[i] * cd_ref[nh(i)[1]][:, 0:1] + upd
    for i in H:
        o = inner[i] + cross[i]
        mu = jnp.mean(o, axis=1, keepdims=True)
        oc = o - mu
        var = jnp.mean(oc * oc, axis=1, keepdims=True)
        on = oc * lax.rsqrt(var + RT_GN_EPS) * gng_ref[:, v_cols(i)] + gnb_ref[:, v_cols(i)]
        o_ref[nh(i)[0], :, v_cols(i)] = (sg_ref[nh(i)[0], :, v_cols(i)] * on).astype(BF16)


def _retention(q3, k3, v3, sg3, dec, qd, kd, cd, gng, gnb, nb=2):
    B, S, _ = q3.shape
    C = RT_CHUNK
    nb = nb if B % nb == 0 else 1
    qk = pl.BlockSpec((nb, C, RT_QKW), lambda b, j: (b, j, 0))
    vv = pl.BlockSpec((nb, C, RT_VW), lambda b, j: (b, j, 0))
    const = lambda a: pl.BlockSpec(a.shape, lambda b, j: (0,) * a.ndim)
    return pl.pallas_call(
        _retention_kernel,
        grid=(B // nb, S // C),
        in_specs=[qk, qk, vv, vv, const(dec), const(qd), const(kd), const(cd), const(gng), const(gnb)],
        out_specs=vv,
        out_shape=jax.ShapeDtypeStruct((B, S, RT_VW), BF16),
        scratch_shapes=[pltpu.VMEM((nb * RT_HEADS, RT_QK, RT_V), F32)],
        compiler_params=_cparams(("parallel", "arbitrary")),
        name="retention",
    )(q3, k3, v3, sg3, dec, qd, kd, cd, gng, gnb)


def _nsa_rope_tables(S):
    half = ROPE_DIM // 2
    inv = ROPE_THETA ** (-jnp.arange(half, dtype=F32) / half)
    ang = jnp.arange(S, dtype=F32)[:, None] * inv[None, :]
    cos, sin = jnp.cos(ang), jnp.sin(ang)
    zeros = lambda n: jnp.zeros((S, n), F32)
    cn = jnp.concatenate([cos, cos, jnp.ones((S, NS_HEAD - ROPE_DIM), F32)], axis=1)
    s1 = jnp.concatenate([-sin, zeros(NS_HEAD - half)], axis=1)
    s2 = jnp.concatenate([zeros(half), sin, zeros(NS_HEAD - ROPE_DIM)], axis=1)
    two = lambda a: jnp.concatenate([a, a], axis=1)
    return two(cn), two(s1), two(s2)


def _rt_rope_tables(S):
    inv = RT_THETA ** (-jnp.linspace(0.0, 1.0, RT_QK // 2, dtype=F32))
    ang = jnp.arange(S, dtype=F32)[:, None] * inv[None, :]
    cos, sin = jnp.cos(ang), jnp.sin(ang)
    return jnp.concatenate([cos, cos], axis=1), jnp.concatenate([-sin, sin], axis=1)


def _rt_decay_tables():
    log_g = jnp.log(1.0 - 2.0 ** (-5.0 - jnp.arange(RT_HEADS, dtype=F32)))
    idx = jnp.arange(RT_CHUNK, dtype=F32)
    diff = idx[:, None] - idx[None, :]
    dec = jnp.where(diff >= 0, jnp.exp(jnp.maximum(diff, 0.0) * log_g[:, None, None]), 0.0)
    qd = jnp.exp((idx + 1.0) * log_g[:, None])[..., None]
    kd = jnp.exp((RT_CHUNK - 1.0 - idx) * log_g[:, None])[..., None]
    cd = jnp.broadcast_to(jnp.exp(RT_CHUNK * log_g)[:, None, None], (RT_HEADS, 1, LANES))
    return dec, qd, kd, cd


def _overlap_table(S, ncp):
    n_cmp = (S - CMP_LEN) // CMP_STRIDE + 1
    n_sel = S // SEL_BLOCK
    cs = jnp.arange(ncp) * CMP_STRIDE
    ss = jnp.arange(LANES) * SEL_BLOCK
    ov = jnp.clip(jnp.minimum(cs[:, None] + CMP_LEN, ss[None, :] + SEL_BLOCK)
                  - jnp.maximum(cs[:, None], ss[None, :]), 0, None).astype(F32) / CMP_LEN
    keep = (jnp.arange(ncp)[:, None] < n_cmp) & (jnp.arange(LANES)[None, :] < n_sel)
    return jnp.where(keep, ov, 0.0)


def kernel(x, ab_w_in, ab_w_out, rk_mu, rk_w0, rk_w1, rk_w2, rk_a0, rk_a1, rk_a2, rk_g1, rk_g2, rk_kk, rk_ka, rk_rk,
           rk_ln, ns_pe, ns_c_w1, ns_c_w2, rt_w_in, rt_w_out, rt_gn, router_w, router_b, moe_w_gate, moe_w_up,
           moe_w_down, ln):
    B, S, D = x.shape
    T = B * S
    assert D == D_MODEL and S % 256 == 0 and S // SEL_BLOCK <= LANES and S >= WINDOW
    xt = x.reshape(T, D)
    rwt = router_w.T
    rb = router_b.reshape(N_EXPERTS, 1)
    vec = lambda a: a.reshape(1, -1)

    w_in = ab_w_in[0]
    n_gate = 3 * NS_HEADS
    w_cat = jnp.concatenate([w_in[:, :-n_gate], jnp.pad(w_in[:, -n_gate:], ((0, 0), (0, LANES - n_gate)))],
                            axis=1).astype(BF16)
    cn, s1, s2 = _nsa_rope_tables(S)
    prk, q, kk2, gt, vt, cx = _proj_ab(xt, w_cat, cn, s1, s2, S)

    ones = (jnp.arange(RK_W)[:, None] // RK_HEAD == jnp.arange(RK_W)[None, :] // RK_HEAD).astype(BF16)
    b16 = lambda a: a.astype(BF16)
    r, lw, km, v, kk, bb, g, bon = _rwkv_prep(
        prk.reshape(B, S, 4 * RK_W), rk_mu[0], vec(rk_w0[0]), b16(rk_w1[0]), b16(rk_w2[0]), vec(rk_a0[0]),
        b16(rk_a1[0]), b16(rk_a2[0]), b16(rk_g1[0]), b16(rk_g2[0]), vec(rk_kk[0]), vec(rk_ka[0]), vec(rk_rk[0]), ones)
    o_a = _rwkv_chunk(r, lw, km, v, kk, bb, g, bon, rk_ln[0, 0:1], rk_ln[0, 1:2])

    ncp = S // CMP_STRIDE
    kcv, kcvt = _nsa_compress(cx.reshape(2, B, ncp, CMP_STRIDE * LANES), ns_c_w1[0], b16(ns_c_w2[0]),
                              ns_pe[0].reshape(2, 1, CMP_LEN * NS_HEAD))
    o_b = _nsa_attn(q.reshape(B, S, NS_W), kcv, kcvt, kk2.reshape(B, S, 2 * LANES),
                    vt.reshape(2, B, S // QT, LANES, QT), gt.reshape(B, S, LANES), _overlap_table(S, ncp).T)

    w_out = b16(ab_w_out[0])
    x1, e, wt, pos, cnt = _outproj_router([o_a.reshape(T, RK_W), o_b.reshape(T, NS_W)], [w_out[:RK_W], w_out[RK_W:]],
                                          xt, ln[0, 0, 0:1], ln[0, 0, 1:2], rwt, rb)
    x2 = _moe(x1, e, wt, pos, cnt, moe_w_gate, moe_w_up, moe_w_down, 0, ln[0, 1, 0:1], ln[0, 1, 1:2])

    cs, sn = _rt_rope_tables(S)
    qr, kr, vr, sg = _proj_rt(x2, b16(rt_w_in[0]), cs, sn, S)
    dec, qd, kd, cd = _rt_decay_tables()
    ret = _retention(qr.reshape(B, S, RT_QKW), kr.reshape(B, S, RT_QKW), vr.reshape(B, S, RT_VW),
                     sg.reshape(B, S, RT_VW), dec, qd, kd, cd, rt_gn[0, 0:1], rt_gn[0, 1:2])
    x3, e, wt, pos, cnt = _outproj_router([ret.reshape(T, RT_VW)], [b16(rt_w_out[0])], x2,
                                          ln[1, 0, 0:1], ln[1, 0, 1:2], rwt, rb)
    x4 = _moe(x3, e, wt, pos, cnt, moe_w_gate, moe_w_up, moe_w_down, 1, ln[1, 1, 0:1], ln[1, 1, 1:2])
    return x4.reshape(B, S, D)
```

```python
import functools
import math

import jax
import jax.numpy as jnp
from jax import lax
from jax.experimental import pallas as pl
from jax.experimental.pallas import tpu as pltpu

F32 = jnp.float32
BF16 = jnp.bfloat16
I32 = jnp.int32
HI = lax.Precision.HIGHEST

LANES = 128
SUBLANES = 8
BF16_ROWS = 16
VMEM_LIMIT = 56 * 1024 * 1024

D_MODEL = 1024
RK_HEADS, RK_HEAD = 8, 64
RK_W = RK_HEADS * RK_HEAD
RK_DECAY_SCALE = 0.606531
RK_LN_EPS = 64e-5
NS_HEADS, NS_KV, NS_HPG, NS_HEAD = 8, 2, 4, 64
NS_W = NS_HEADS * NS_HEAD
CMP_LEN, CMP_STRIDE, SEL_BLOCK, SEL_TOPK, WINDOW = 32, 16, 64, 16, 512
ROPE_THETA = 500000.0
ROPE_DIM = NS_HEAD // 4
Q_SCALE = NS_HEAD ** -0.5 * math.log2(math.e)
QT = 256
KT = 512
RT_HEADS, RT_QK, RT_V = 8, 128, 256
RT_QKW, RT_VW = RT_HEADS * RT_QK, RT_HEADS * RT_V
RT_CHUNK = 256
RT_THETA = 10000.0
RT_GN_EPS = 1e-5
N_EXPERTS, N_GROUPS, EXP_PER_GROUP, TOP_K = 16, 4, 4, 2
D_EXPERT = 1024
MOE_BLOCK = 512
ROW_DMA_UNROLL = 8
DEPTH = 2
ALPHA = (2.0 * DEPTH) ** 0.25
LN_EPS = 1e-5
NEG = -1e30


def _cparams(sem):
    return pltpu.CompilerParams(dimension_semantics=sem, vmem_limit_bytes=VMEM_LIMIT)


def _bdot(a, w):
    return jnp.dot(a.astype(BF16), w, preferred_element_type=F32)


def _dot_nt(a, b):
    return lax.dot_general(a, b, (((1,), (1,)), ((), ())), preferred_element_type=F32)


def _layer_norm_rows(z, g, b):
    mu = jnp.mean(z, axis=1, keepdims=True)
    zc = z - mu
    var = jnp.mean(zc * zc, axis=1, keepdims=True)
    return zc * lax.rsqrt(var + LN_EPS) * g + b


def _proj_ab_kernel(x_ref, w_ref, cn_ref, s1_ref, s2_ref, pick_ref, prk_ref, q_ref, kk_ref, gt_ref, vt_ref, cx_ref):
    xb = x_ref[...].astype(BF16)
    for c in range(4):
        prk_ref[:, c * RK_W:(c + 1) * RK_W] = jnp.dot(xb, w_ref[:, c * RK_W:(c + 1) * RK_W],
                                                      preferred_element_type=F32)
    cn, s1, s2 = cn_ref[...], s1_ref[...], s2_ref[...]

    def rope(y):
        return y * cn + pltpu.roll(y, LANES - ROPE_DIM // 2, 1) * s1 + pltpu.roll(y, ROPE_DIM // 2, 1) * s2

    base = 4 * RK_W
    yq = jnp.dot(xb, w_ref[:, base:base + NS_W], preferred_element_type=F32)
    for c in range(NS_W // LANES):
        q_ref[:, c * LANES:(c + 1) * LANES] = (rope(yq[:, c * LANES:(c + 1) * LANES]) * Q_SCALE).astype(BF16)
    base += NS_W
    ykv = jnp.dot(xb, w_ref[:, base:base + 6 * LANES], preferred_element_type=F32)
    for c in range(6):
        y = ykv[:, c * LANES:(c + 1) * LANES]
        if c % 2 == 0:
            y = rope(y)
        if c < 2:
            yb = y.astype(BF16)
            for l in range(CMP_STRIDE):
                cx_ref[c, :, l * LANES:(l + 1) * LANES] = jnp.dot(pick_ref[l], yb,
                                                                  preferred_element_type=F32).astype(BF16)
        elif c % 2 == 0:
            kk_ref[:, (c // 2 - 1) * LANES:(c // 2) * LANES] = y.astype(BF16)
        else:
            for t in range(y.shape[0] // QT):
                vt_ref[c // 2 - 1, t] = y[t * QT:(t + 1) * QT].T.astype(BF16)
    base += 6 * LANES
    gt_ref[...] = jax.nn.sigmoid(jnp.dot(xb, w_ref[:, base:base + LANES], preferred_element_type=F32))


def _proj_ab(xt, w, cn, s1, s2, S, tm=512):
    T = xt.shape[0]
    ncols = w.shape[1]
    nseq = S // tm
    row = lambda i: (i, 0)
    tab = lambda i: (i % nseq, 0)
    ng = tm // CMP_STRIDE
    pick = (jnp.arange(tm, dtype=I32)[None, None, :]
            == jnp.arange(ng, dtype=I32)[None, :, None] * CMP_STRIDE + jnp.arange(CMP_STRIDE, dtype=I32)[:, None, None])
    pick = pick.astype(BF16)
    return pl.pallas_call(
        _proj_ab_kernel,
        grid=(T // tm,),
        in_specs=[pl.BlockSpec((tm, D_MODEL), row),
                  pl.BlockSpec((D_MODEL, ncols), lambda i: (0, 0)),
                  pl.BlockSpec((tm, LANES), tab), pl.BlockSpec((tm, LANES), tab), pl.BlockSpec((tm, LANES), tab),
                  pl.BlockSpec(pick.shape, lambda i: (0, 0, 0))],
        out_specs=[pl.BlockSpec((tm, 4 * RK_W), row), pl.BlockSpec((tm, NS_W), row),
                   pl.BlockSpec((tm, 2 * LANES), row), pl.BlockSpec((tm, LANES), row),
                   pl.BlockSpec((2, tm // QT, LANES, QT), lambda i: (0, i, 0, 0)),
                   pl.BlockSpec((2, ng, CMP_STRIDE * LANES), lambda i: (0, i, 0))],
        out_shape=[jax.ShapeDtypeStruct((T, 4 * RK_W), F32), jax.ShapeDtypeStruct((T, NS_W), BF16),
                   jax.ShapeDtypeStruct((T, 2 * LANES), BF16), jax.ShapeDtypeStruct((T, LANES), F32),
                   jax.ShapeDtypeStruct((2, T // QT, LANES, QT), BF16),
                   jax.ShapeDtypeStruct((2, T // CMP_STRIDE, CMP_STRIDE * LANES), BF16)],
        compiler_params=_cparams(("parallel",)),
        name="proj_ab",
    )(xt, w, cn, s1, s2, pick)


def _rwkv_prep_kernel(p_ref, mu_ref, w0_ref, w1_ref, w2_ref, a0_ref, a1_ref, a2_ref, g1_ref, g2_ref,
                      kk_ref, ka_ref, rk_ref, ones_ref,
                      r_o, w_o, k_o, v_o, kk_o, b_o, g_o, bon_o, carry):
    j = pl.program_id(1)
    p = p_ref[...]
    tm = p.shape[0]

    @pl.when(j == 0)
    def _():
        carry[...] = jnp.zeros_like(carry)

    rowi = lax.broadcasted_iota(I32, p.shape, 0)
    prev = jnp.where(rowi == 0, carry[...], pltpu.roll(p, 1, 0))
    carry[...] = p[tm - 1:tm, :]
    dp = prev - p
    sl = lambda a, c: a[:, c * RK_W:(c + 1) * RK_W]
    mu = mu_ref[...]
    r = sl(p, 0) + sl(dp, 0) * mu[0:1]
    k = sl(p, 1) + sl(dp, 1) * mu[1:2]
    v = sl(p, 2) + sl(dp, 2) * mu[2:3]
    xw = sl(p, 3) + sl(dp, 3) * mu[3:4]
    xa = sl(p, 3) + sl(dp, 3) * mu[4:5]
    xg = sl(p, 3) + sl(dp, 3) * mu[5:6]
    lw = -RK_DECAY_SCALE * jax.nn.sigmoid(w0_ref[...] + _bdot(jnp.tanh(_bdot(xw, w1_ref[...])), w2_ref[...]))
    a = jax.nn.sigmoid(a0_ref[...] + _bdot(_bdot(xa, a1_ref[...]), a2_ref[...]))
    g = _bdot(jax.nn.sigmoid(_bdot(xg, g1_ref[...])), g2_ref[...])
    ones = ones_ref[...]

    def head_sum(t):
        hi = t.astype(BF16)
        lo = (t - hi.astype(F32)).astype(BF16)
        return (jnp.dot(hi, ones, preferred_element_type=F32) + jnp.dot(lo, ones, preferred_element_type=F32))

    kk = k * kk_ref[...]
    kk = kk / jnp.maximum(jnp.sqrt(head_sum(kk * kk)), 1e-12)
    km = k * (1.0 + (a - 1.0) * ka_ref[...])
    bon = head_sum(r * km * rk_ref[...]) * v
    w_o[...] = lw
    for ref, val in ((r_o, r), (k_o, km), (v_o, v), (kk_o, kk), (b_o, kk * a), (g_o, g), (bon_o, bon)):
        ref[...] = val.astype(ref.dtype)


def _rwkv_prep(prk3, mu, w0, w1, w2, a0, a1, a2, g1, g2, k_k, k_a, r_k, ones, tm=512):
    B, S, _ = prk3.shape
    full = lambda a: pl.BlockSpec(a.shape, lambda b, j: (0,) * a.ndim)
    params = [mu, w0, w1, w2, a0, a1, a2, g1, g2, k_k, k_a, r_k, ones]
    ospec = pl.BlockSpec((None, tm, RK_W), lambda b, j: (b, j, 0))
    return pl.pallas_call(
        _rwkv_prep_kernel,
        grid=(B, S // tm),
        in_specs=[pl.BlockSpec((None, tm, 4 * RK_W), lambda b, j: (b, j, 0))] + [full(a) for a in params],
        out_specs=[ospec] * 8,
        out_shape=[jax.ShapeDtypeStruct((B, S, RK_W), F32 if i == 1 else BF16) for i in range(8)],
        scratch_shapes=[pltpu.VMEM((1, 4 * RK_W), F32)],
        compiler_params=_cparams(("parallel", "arbitrary")),
        name="rwkv_prep",
    )(prk3, *params)


RK_CHUNK = 16


def _pdot(a, b, dims, precise):
    if precise:
        return lax.dot_general(a, b, (dims, ((), ())), precision=HI, preferred_element_type=F32)
    return lax.dot_general(a.astype(BF16), b.astype(BF16), (dims, ((), ())), preferred_element_type=F32)


def _rwkv_chunk_kernel(r_ref, lw_ref, k_ref, v_ref, kk_ref, b_ref, g_ref, bon_ref, lng_ref, lnb_ref, o_ref, ht,
                       *, precise):
    j = pl.program_id(1)

    @pl.when(j == 0)
    def _():
        ht[...] = jnp.zeros_like(ht)

    NB, TT = r_ref.shape[0], r_ref.shape[1]
    C, N = RK_CHUNK, RK_HEAD
    mm = lambda a, b: _pdot(a, b, ((1,), (0,)), precise)
    mm_nt = lambda a, b: _pdot(a, b, ((1,), (1,)), precise)
    mm_tn = lambda a, b: _pdot(a, b, ((0,), (0,)), precise)
    wide = lambda ref: jnp.concatenate([ref[n] for n in range(NB)], axis=1)

    lw = wide(lw_ref)
    rowc = lax.broadcasted_iota(I32, lw.shape, 0) & (C - 1)
    linc, lrev = lw, lw
    sh = 1
    while sh < C:
        linc = linc + jnp.where(rowc >= sh, pltpu.roll(linc, sh, 0), 0.0)
        lrev = lrev + jnp.where(rowc < C - sh, pltpu.roll(lrev, TT - sh, 0), 0.0)
        sh *= 2
    lrev = lrev - lw
    r, k, v, kk, b = wide(r_ref), wide(k_ref), wide(v_ref), wide(kk_ref), wide(b_ref)
    e_in, e_inv, e_rev = jnp.exp(linc), jnp.exp(-linc), jnp.exp(lrev)
    kkd = kk * jnp.exp(linc - lw)
    rd = r * e_in
    binv, kinv = b * e_inv, k * e_inv
    bd, kd = b * e_rev, k * e_rev
    gam = jnp.exp(linc + lrev)

    ti = lax.broadcasted_iota(I32, (TT, TT), 0)
    tj = lax.broadcasted_iota(I32, (TT, TT), 1)
    same = (ti // C) == (tj // C)
    strict = same & (tj < ti)
    incl = same & (tj <= ti)

    H = range(NB * RK_HEADS)
    lo = lambda a: a if precise else a.astype(BF16)
    hs = lambda a: [a[:, h * N:(h + 1) * N] for h in H]
    rows2 = lambda a, b: jnp.concatenate([a, b], axis=0)
    kkd_h, rd_h, v_h = hs(lo(kkd)), hs(lo(rd)), hs(lo(v))
    binv_h, kinv_h, bd_h, kd_h = hs(lo(binv)), hs(lo(kinv)), hs(lo(bd)), hs(lo(kd))
    gam_h = hs(gam)
    gm = [mm_nt(rows2(kkd_h[h], rd_h[h]), rows2(binv_h[h], kinv_h[h])) for h in H]
    a_b = [lo(jnp.where(strict, gm[h][:TT, :TT], 0.0)) for h in H]
    b_rb = [lo(jnp.where(incl, gm[h][TT:, :TT], 0.0)) for h in H]
    akb = [lo(rows2(jnp.where(strict, gm[h][:TT, TT:], 0.0), jnp.where(incl, gm[h][TT:, TT:], 0.0))) for h in H]
    av = [mm(akb[h], v_h[h]) for h in H]
    x = [jnp.concatenate([kkd_h[h].astype(F32), av[h][:TT]], axis=1) for h in H]
    a2 = [lo(mm(a_b[h], a_b[h])) for h in H]
    a4 = [lo(mm(a2[h], a2[h])) for h in H]
    a8 = [lo(mm(a4[h], a4[h])) for h in H]
    x = [x[h] + mm(a8[h], lo(x[h])) for h in H]
    x = [x[h] + mm(a4[h], lo(x[h])) for h in H]
    x = [x[h] + mm(a2[h], lo(x[h])) for h in H]
    x = [x[h] - mm(a_b[h], lo(x[h])) for h in H]
    wt = [lo(x[h][:, :N]) for h in H]
    h_t = [ht[h] for h in H]
    us = [[] for _ in H]
    rhs = [[] for _ in H]
    for c in range(TT // C):
        rs = slice(c * C, (c + 1) * C)
        xh = [mm_nt(rows2(wt[h][rs], rd_h[h][rs]), lo(h_t[h])) for h in H]
        for h in H:
            u_c = -(xh[h][:C] + x[h][rs, N:])
            us[h].append(u_c)
            rhs[h].append(xh[h][C:])
        upd = [mm_tn(rows2(lo(us[h][c]), v_h[h][rs]), rows2(bd_h[h][rs], kd_h[h][rs])) for h in H]
        h_t = [h_t[h] * gam_h[h][c * C:c * C + 1] + upd[h] for h in H]
    outs = []
    for h in H:
        ht[h] = h_t[h]
        o = jnp.concatenate(rhs[h], axis=0) + mm(b_rb[h], lo(jnp.concatenate(us[h], axis=0))) + av[h][TT:]
        mu = jnp.mean(o, axis=1, keepdims=True)
        oc = o - mu
        var = jnp.mean(oc * oc, axis=1, keepdims=True)
        outs.append(oc * lax.rsqrt(var + RK_LN_EPS))
    for n in range(NB):
        on = jnp.concatenate(outs[n * RK_HEADS:(n + 1) * RK_HEADS], axis=1)
        o_ref[n] = (on * lng_ref[...] + lnb_ref[...] + bon_ref[n]) * g_ref[n]


def _rwkv_chunk(r, lw, k, v, kk, bb, g, bon, lng, lnb, tt=128, nb=4, precise=False):
    B, S, _ = r.shape
    nb = nb if B % nb == 0 else 1
    blk = pl.BlockSpec((nb, tt, RK_W), lambda b, j: (b, j, 0))
    vec = pl.BlockSpec((1, RK_W), lambda b, j: (0, 0))
    return pl.pallas_call(
        functools.partial(_rwkv_chunk_kernel, precise=precise),
        grid=(B // nb, S // tt),
        in_specs=[blk] * 8 + [vec, vec],
        out_specs=blk,
        out_shape=jax.ShapeDtypeStruct((B, S, RK_W), F32),
        scratch_shapes=[pltpu.VMEM((nb * RK_HEADS, RK_HEAD, RK_HEAD), F32)],
        compiler_params=_cparams(("parallel", "arbitrary")),
        name="rwkv_chunk",
    )(r, lw, k, v, kk, bb, g, bon, lng, lnb)


def _nsa_compress_kernel(x_ref, w1g_ref, w1f_ref, w2_ref, pe_ref, o_ref, ot_ref):
    bias = jnp.dot(jnp.broadcast_to(pe_ref[...], (SUBLANES, CMP_LEN * NS_HEAD)), w1f_ref[...], precision=HI,
                   preferred_element_type=F32)[0:1]
    x = x_ref[...]
    n = x.shape[0]
    outs = []
    for g in range(NS_KV):
        ya = jnp.dot(x, w1g_ref[g, 0], preferred_element_type=F32)
        yb = jnp.dot(x, w1g_ref[g, 1], preferred_element_type=F32)
        h = ya + pltpu.roll(yb, n - 1, 0) + bias
        outs.append(_bdot(jax.nn.gelu(h), w2_ref[...]))
    out = jnp.concatenate(outs, axis=1)
    o_ref[...] = out.astype(BF16)
    ot_ref[...] = out.T.astype(BF16)


def _nsa_compress(cx, w1, w2b, pe):
    _, B, ncp, width = cx.shape
    hid = w1.shape[-1]
    w1h = w1.reshape(2, 2, CMP_STRIDE, 1, NS_HEAD, hid)
    zero = jnp.zeros_like(w1h)
    w1g = jnp.stack([jnp.concatenate([w1h, zero], axis=3), jnp.concatenate([zero, w1h], axis=3)], axis=1)
    w1g = w1g.reshape(2, NS_KV, 2, width, hid).astype(BF16)
    return pl.pallas_call(
        _nsa_compress_kernel,
        grid=(B, 2),
        in_specs=[pl.BlockSpec((None, None, ncp, width), lambda b, c: (c, b, 0, 0)),
                  pl.BlockSpec((None, NS_KV, 2, width, hid), lambda b, c: (c, 0, 0, 0, 0)),
                  pl.BlockSpec((None, CMP_LEN * NS_HEAD, hid), lambda b, c: (c, 0, 0)),
                  pl.BlockSpec((None, hid, NS_HEAD), lambda b, c: (c, 0, 0)),
                  pl.BlockSpec((None, 1, CMP_LEN * NS_HEAD), lambda b, c: (c, 0, 0))],
        out_specs=[pl.BlockSpec((None, None, ncp, LANES), lambda b, c: (b, c, 0, 0)),
                   pl.BlockSpec((None, None, LANES, ncp), lambda b, c: (b, c, 0, 0))],
        out_shape=[jax.ShapeDtypeStruct((B, 2, ncp, LANES), BF16), jax.ShapeDtypeStruct((B, 2, LANES, ncp), BF16)],
        compiler_params=_cparams(("parallel", "parallel")),
        name="nsa_compress",
    )(cx, w1g, w1, w2b, pe)


def _nsa_attn_kernel(q_ref, kc_ref, vct_ref, ks_ref, vst_ref, kw_ref, vwt_ref, gt_ref, ovt_ref, ext_ref, o_ref,
                     m_s, l_s, acc_s, m_w, l_w, acc_w, s_buf, *, ncp):
    i = pl.program_id(1)
    s0 = i * QT
    heads = range(NS_HEADS)
    hcols = lambda h: slice(h * QT, (h + 1) * QT)
    gcols = lambda h: slice(h // NS_HPG * QT, (h // NS_HPG + 1) * QT)
    iota = lambda shape, d: lax.broadcasted_iota(I32, shape, d)
    qpos = lambda shape: s0 + (iota(shape, 1) & (QT - 1))

    def col_reduce(x, op, final):
        n = x.shape[0]
        while n > SUBLANES:
            n //= 2
            x = op(x[:n], x[n:])
        return final(x, axis=0, keepdims=True)

    q8 = q_ref[...].astype(F32)
    zeros = jnp.zeros((QT, NS_HEAD), F32)
    qt = []
    for h in heads:
        qh = q8[:, h * NS_HEAD:(h + 1) * NS_HEAD]
        qt.append(jnp.concatenate([qh, zeros] if h < NS_HPG else [zeros, qh], axis=1).T)
    qt = jnp.concatenate(qt, axis=1).astype(BF16)

    sc = jnp.dot(kc_ref[...], qt, preferred_element_type=F32)
    n_row = iota((ncp, QT), 0)
    cmask = (n_row * CMP_STRIDE + (CMP_LEN - 1) <= s0 + iota((ncp, QT), 1)) & (n_row < ncp - 1)
    cpen = jnp.where(cmask, 0.0, NEG)
    sc = jnp.concatenate([sc[:, hcols(h)] + cpen for h in heads], axis=1)
    mc = col_reduce(sc, jnp.maximum, jnp.max)
    pc = jnp.exp2(sc - mc)
    lc = col_reduce(pc, jnp.add, jnp.sum)
    pc = pc * jnp.where(mc > 0.5 * NEG, 1.0 / lc, 0.0)
    pcs = jnp.concatenate([functools.reduce(jnp.add, [pc[:, hcols(h)] for h in range(g * NS_HPG, (g + 1) * NS_HPG)])
                           for g in range(NS_KV)], axis=1)

    def group_dot(vt, p, extra=None):
        half = NS_HPG * QT
        outs = []
        for g in range(NS_KV):
            vg = vt[g * NS_HEAD:(g + 1) * NS_HEAD]
            if extra is not None:
                vg = jnp.concatenate([vg, extra], axis=0)
            outs.append(jnp.dot(vg, p[:, g * half:(g + 1) * half], preferred_element_type=F32))
        return jnp.concatenate(outs, axis=1)

    o_c = group_dot(vct_ref[...], pc.astype(BF16))

    imp = jnp.dot(ovt_ref[...], pcs, precision=HI, preferred_element_type=F32)
    blk = iota((LANES, NS_KV * QT), 0)
    cur = qpos((LANES, NS_KV * QT)) // SEL_BLOCK
    valid = blk <= cur
    forced = (blk == 0) | (blk == cur) | (blk == cur - 1)
    pri = jnp.where(valid & ~forced, imp, -jnp.inf)
    picked = forced
    blkf = blk.astype(F32)
    for _ in range(SEL_TOPK - 3):
        mx = col_reduce(pri, jnp.maximum, jnp.max)
        hit = blkf == col_reduce(jnp.where(pri == mx, blkf, float(LANES)), jnp.minimum, jnp.min)
        picked = picked | hit
        pri = jnp.where(hit, -jnp.inf, pri)
    selpen = jnp.where(picked & valid, 0.0, NEG).astype(BF16)
    wq = jnp.concatenate([qt, jnp.concatenate([selpen[:, gcols(h)] for h in heads], axis=1)], axis=0)

    def attend(s, vt, pen, m_ref, l_ref, acc_ref):
        if pen is not None:
            s = jnp.concatenate([s[:, hcols(h)] + pen for h in heads], axis=1)
        m_old = m_ref[...]
        m_new = jnp.maximum(m_old, col_reduce(s, jnp.maximum, jnp.max))
        alpha = jnp.exp2(m_old - m_new)
        p = jnp.exp2(s - m_new).astype(BF16)
        m_ref[...] = m_new
        pv = group_dot(vt, p, extra=jnp.ones((BF16_ROWS, vt.shape[1]), BF16))
        l_ref[...] = alpha * l_ref[...] + pv[NS_HEAD:NS_HEAD + 1]
        acc_ref[...] = alpha * acc_ref[...] + pv[:NS_HEAD]

    def reset(m_ref, l_ref, acc_ref):
        m_ref[...] = jnp.full(m_ref.shape, NEG, F32)
        l_ref[...] = jnp.zeros_like(l_ref)
        acc_ref[...] = jnp.zeros_like(acc_ref)

    tiles = lambda ref, t0, n: jnp.concatenate([ref[t0 + c] for c in range(n)], axis=1)

    reset(m_s, l_s, acc_s)
    diag = i // (KT // QT)

    def sel_scores(kt, slot):
        k0 = pl.multiple_of(kt * KT, KT)
        keys = jnp.concatenate([ks_ref[pl.ds(k0, KT), :], ext_ref[pl.ds(k0, KT), :]], axis=1)
        s_buf[slot] = jnp.dot(keys, wq, preferred_element_type=F32)

    def sel_attend(kt, slot, causal):
        pen = None
        if causal:
            pen = jnp.where(kt * KT + iota((KT, QT), 0) <= s0 + iota((KT, QT), 1), 0.0, NEG)
        attend(s_buf[slot], tiles(vst_ref, kt * (KT // QT), KT // QT), pen, m_s, l_s, acc_s)

    def sel_pair(j, c):
        sel_scores(2 * j + 1, 1)
        sel_attend(2 * j, 0, False)
        sel_scores(2 * j + 2, 0)
        sel_attend(2 * j + 1, 1, False)
        return c

    d0 = pl.multiple_of(s0, QT)
    a0 = pl.multiple_of(jnp.maximum(s0 - WINDOW, 0), QT)
    s_d = jnp.dot(kw_ref[pl.ds(d0, QT), :], qt, preferred_element_type=F32)
    s_a = jnp.dot(kw_ref[pl.ds(a0, WINDOW), :], qt, preferred_element_type=F32)
    pen_d = jnp.where(iota((QT, QT), 0) <= iota((QT, QT), 1), 0.0, NEG)
    kpos = a0 + iota((WINDOW, QT), 0)
    pen_a = jnp.where((kpos < s0) & (kpos > s0 + iota((WINDOW, QT), 1) - WINDOW), 0.0, NEG)

    sel_scores(0, 0)
    reset(m_w, l_w, acc_w)
    attend(s_d, vwt_ref[i], pen_d, m_w, l_w, acc_w)
    attend(s_a, tiles(vwt_ref, a0 // QT, WINDOW // QT), pen_a, m_w, l_w, acc_w)
    lax.fori_loop(0, diag // 2, sel_pair, 0)

    @pl.when(diag % 2 == 1)
    def _():
        sel_scores(diag, 1)
        sel_attend(diag - 1, 0, False)
        sel_attend(diag, 1, True)

    @pl.when(diag % 2 == 0)
    def _():
        sel_attend(diag, 0, True)

    gtt = gt_ref[...].T
    o_s = acc_s[...] / l_s[...]
    o_w = acc_w[...] / l_w[...]
    outs = []
    for h in heads:
        gate = lambda br: gtt[br * NS_HEADS + h:br * NS_HEADS + h + 1]
        outs.append(gate(0) * o_c[:, hcols(h)] + gate(1) * o_s[:, hcols(h)] + gate(2) * o_w[:, hcols(h)])
    pairs = [jnp.concatenate(outs[p:p + 2], axis=0).T for p in range(0, NS_HEADS, 2)]
    o_ref[...] = jnp.concatenate(pairs, axis=1).astype(BF16)


def _nsa_attn(q3, kcv, kcvt, kv3, vt, gt3, ovt):
    B, S, _ = q3.shape
    ncp = kcv.shape[2]
    R = NS_HEADS * QT
    ext = (jnp.arange(S, dtype=I32)[:, None] // SEL_BLOCK == jnp.arange(LANES, dtype=I32)[None, :]).astype(BF16)
    seq = lambda c: pl.BlockSpec((None, S, LANES), lambda b, i, c=c: (b, 0, c))
    seqt = lambda c: pl.BlockSpec((None, None, S // QT, LANES, QT), lambda b, i, c=c: (c, b, 0, 0, 0))
    return pl.pallas_call(
        functools.partial(_nsa_attn_kernel, ncp=ncp),
        grid=(B, S // QT),
        in_specs=[pl.BlockSpec((None, QT, NS_W), lambda b, i: (b, i, 0)),
                  pl.BlockSpec((None, None, ncp, LANES), lambda b, i: (b, 0, 0, 0)),
                  pl.BlockSpec((None, None, LANES, ncp), lambda b, i: (b, 1, 0, 0)),
                  seq(0), seqt(0), seq(1), seqt(1),
                  pl.BlockSpec((None, QT, LANES), lambda b, i: (b, i, 0)),
                  pl.BlockSpec(ovt.shape, lambda b, i: (0, 0)),
                  pl.BlockSpec(ext.shape, lambda b, i: (0, 0))],
        out_specs=pl.BlockSpec((None, QT, NS_W), lambda b, i: (b, i, 0)),
        out_shape=jax.ShapeDtypeStruct((B, S, NS_W), BF16),
        scratch_shapes=([pltpu.VMEM((1, R), F32), pltpu.VMEM((1, R), F32), pltpu.VMEM((NS_HEAD, R), F32)] * 2
                        + [pltpu.VMEM((2, KT, R), F32)]),
        compiler_params=_cparams(("parallel", "arbitrary")),
        name="nsa_attn",
    )(q3, kcv, kcvt, kv3, vt, kv3, vt, gt3, ovt, ext)


def _first_argmax(vals):
    m = vals[0]
    for v in vals[1:]:
        m = jnp.maximum(m, v)
    idx = jnp.full(m.shape, len(vals) - 1, I32)
    for j in range(len(vals) - 2, -1, -1):
        idx = jnp.where(vals[j] == m, j, idx)
    return m, idx


def _outproj_router_kernel(*refs, n_in):
    acts, ws = refs[:n_in], refs[n_in:2 * n_in]
    x_ref, lng_ref, lnb_ref, rwt_ref, rb_ref, tri_ref = refs[2 * n_in:2 * n_in + 6]
    y_ref, e_ref, wt_ref, pos_ref, cnt_ref, cnt = refs[2 * n_in + 6:]
    i = pl.program_id(0)

    @pl.when(i == 0)
    def _():
        cnt[...] = jnp.zeros_like(cnt)

    mix = _bdot(acts[0][...], ws[0][...])
    for a, w in zip(acts[1:], ws[1:]):
        mix = mix + _bdot(a[...], w[...])
    y = _layer_norm_rows(ALPHA * x_ref[...] + mix, lng_ref[...], lnb_ref[...])
    y_ref[...] = y

    split = lambda t: (t.astype(BF16), (t - t.astype(BF16).astype(F32)).astype(BF16))
    (w_hi, w_lo), (y_hi, y_lo) = split(rwt_ref[...]), split(y)
    logit = _dot_nt(w_hi, y_hi) + (_dot_nt(w_hi, y_lo) + _dot_nt(w_lo, y_hi))
    aff = jax.nn.sigmoid(logit)
    biased = aff + rb_ref[...]
    neg_inf = -jnp.inf
    g_score, g_i1, g_i2 = [], [], []
    for gi in range(N_GROUPS):
        vals = [biased[gi * EXP_PER_GROUP + j:gi * EXP_PER_GROUP + j + 1, :] for j in range(EXP_PER_GROUP)]
        m1, i1 = _first_argmax(vals)
        m2, i2 = _first_argmax([jnp.where(i1 == j, neg_inf, vals[j]) for j in range(EXP_PER_GROUP)])
        g_score.append(m1 + m2)
        g_i1.append(i1)
        g_i2.append(i2)
    _, grp = _first_argmax(g_score)
    loc1, loc2 = g_i1[-1], g_i2[-1]
    for gi in range(N_GROUPS - 2, -1, -1):
        loc1 = jnp.where(grp == gi, g_i1[gi], loc1)
        loc2 = jnp.where(grp == gi, g_i2[gi], loc2)
    e1 = grp * EXP_PER_GROUP + loc1
    e2 = grp * EXP_PER_GROUP + loc2
    eio = lax.broadcasted_iota(I32, aff.shape, 0)
    oh1 = eio == e1
    oh2 = eio == e2
    a1 = jnp.sum(jnp.where(oh1, aff, 0.0), axis=0, keepdims=True)
    a2 = jnp.sum(jnp.where(oh2, aff, 0.0), axis=0, keepdims=True)
    tot = a1 + a2
    e_ref[...] = jnp.concatenate([e1, e2], axis=0)
    wt_ref[...] = jnp.concatenate([a1 / tot, a2 / tot], axis=0)

    ohs = oh1.astype(F32) + oh2.astype(F32)
    before = jnp.dot(ohs.astype(BF16), tri_ref[...], preferred_element_type=F32) + cnt[...]
    p1 = jnp.sum(jnp.where(oh1, before, 0.0), axis=0, keepdims=True)
    p2 = jnp.sum(jnp.where(oh2, before, 0.0), axis=0, keepdims=True)
    pos_ref[...] = jnp.concatenate([p1, p2], axis=0).astype(I32)
    cnt[...] = cnt[...] + jnp.sum(ohs, axis=1, keepdims=True)
    cnt_ref[...] = jnp.broadcast_to(cnt[...], cnt_ref.shape)


def _outproj_router(acts, ws, xres, lng, lnb, rwt, rb, tm=1024):
    T = xres.shape[0]
    n_in = len(acts)
    tri = (lax.broadcasted_iota(I32, (tm, tm), 0) < lax.broadcasted_iota(I32, (tm, tm), 1)).astype(BF16)
    row = lambda i: (i, 0)
    const = lambda a: pl.BlockSpec(a.shape, lambda i: (0,) * a.ndim)
    lane_blk = pl.BlockSpec((TOP_K, tm), lambda i: (0, i))
    return pl.pallas_call(
        functools.partial(_outproj_router_kernel, n_in=n_in),
        grid=(T // tm,),
        in_specs=([pl.BlockSpec((tm, a.shape[1]), row) for a in acts] + [const(w) for w in ws]
                  + [pl.BlockSpec((tm, D_MODEL), row), const(lng), const(lnb), const(rwt), const(rb), const(tri)]),
        out_specs=[pl.BlockSpec((tm, D_MODEL), row), lane_blk, lane_blk, lane_blk,
                   pl.BlockSpec((N_EXPERTS, LANES), lambda i: (0, 0))],
        out_shape=[jax.ShapeDtypeStruct((T, D_MODEL), F32), jax.ShapeDtypeStruct((TOP_K, T), I32),
                   jax.ShapeDtypeStruct((TOP_K, T), F32), jax.ShapeDtypeStruct((TOP_K, T), I32),
                   jax.ShapeDtypeStruct((N_EXPERTS, LANES), F32)],
        scratch_shapes=[pltpu.VMEM((N_EXPERTS, 1), F32)],
        compiler_params=_cparams(("arbitrary",)),
        name="outproj_router",
    )(*acts, *ws, xres, lng, lnb, rwt, rb, tri)


def _dispatch_kernel(zrow_ref, dest_hbm, x_ref, xs_hbm, dsm, zbuf, sem_idx, sem, sem_z):
    i = pl.program_id(0)
    tm = x_ref.shape[0]

    @pl.when(i == 0)
    def _():
        zbuf[...] = jnp.zeros_like(zbuf)
        zero_copy = lambda j: pltpu.make_async_copy(
            zbuf, xs_hbm.at[pl.ds(pl.multiple_of(jnp.maximum(zrow_ref[j], 0), MOE_BLOCK), MOE_BLOCK)], sem_z)
        for j in range(zrow_ref.shape[0]):
            @pl.when(zrow_ref[j] >= 0)
            def _():
                zero_copy(j).start()
        for j in range(zrow_ref.shape[0]):
            @pl.when(zrow_ref[j] >= 0)
            def _():
                zero_copy(j).wait()

    idx_copy = pltpu.make_async_copy(dest_hbm.at[i], dsm, sem_idx)
    idx_copy.start()
    idx_copy.wait()

    def row_copy(r, k):
        return pltpu.make_async_copy(x_ref.at[pl.ds(r, 1)], xs_hbm.at[pl.ds(dsm[k * tm + r], 1)], sem)

    def start(r, c):
        for k in range(TOP_K):
            row_copy(r, k).start(priority=k % 2)
        return c

    lax.fori_loop(0, tm, start, 0, unroll=ROW_DMA_UNROLL)
    for k in range(TOP_K):
        pltpu.make_async_copy(x_ref, xs_hbm.at[pl.ds(0, tm)], sem).wait()


def _dispatch(zero_rows, dest_tiles, x, rows, tm):
    T = x.shape[0]
    return pl.pallas_call(
        _dispatch_kernel,
        grid_spec=pltpu.PrefetchScalarGridSpec(
            num_scalar_prefetch=1,
            grid=(T // tm,),
            in_specs=[pl.BlockSpec(memory_space=pl.ANY), pl.BlockSpec((tm, D_MODEL), lambda i, z: (i, 0))],
            out_specs=pl.BlockSpec(memory_space=pl.ANY),
            scratch_shapes=[pltpu.SMEM((TOP_K * tm,), I32), pltpu.VMEM((MOE_BLOCK, D_MODEL), F32),
                            pltpu.SemaphoreType.DMA(()), pltpu.SemaphoreType.DMA(()), pltpu.SemaphoreType.DMA(())]),
        out_shape=jax.ShapeDtypeStruct((rows, D_MODEL), F32),
        compiler_params=_cparams(("arbitrary",)),
        name="moe_dispatch",
    )(zero_rows, dest_tiles, x)


def _ffn_kernel(be_ref, nu_ref, xs_ref, wg_ref, wu_ref, wd_ref, y_ref, h_ref, wg_b, wu_b, wd_b):
    i = pl.program_id(0)
    half = D_EXPERT // 2
    used = i < nu_ref[0]

    @pl.when(used & ((i == 0) | (be_ref[i] != be_ref[jnp.maximum(i - 1, 0)])))
    def _():
        wg_b[...] = wg_ref[...].astype(BF16)
        wu_b[...] = wu_ref[...].astype(BF16)
        wd_b[...] = wd_ref[...].astype(BF16)

    @pl.when(used)
    def _():
        xb = xs_ref[...].astype(BF16)
        for c in range(2):
            cs = slice(c * half, (c + 1) * half)
            gt = jnp.dot(xb, wg_b[:, cs], preferred_element_type=F32)
            up = jnp.dot(xb, wu_b[:, cs], preferred_element_type=F32)
            h_ref[:, cs] = (jax.nn.silu(gt) * up).astype(BF16)
        y_ref[...] = jnp.dot(h_ref[...], wd_b[...], preferred_element_type=F32)

    @pl.when(i >= nu_ref[0])
    def _():
        y_ref[...] = jnp.zeros_like(y_ref)


def _ffn(blk_exp, n_used, xs, wg, wu, wd, layer):
    rows = xs.shape[0]
    wspec = lambda a: pl.BlockSpec((None, None) + a.shape[2:], lambda i, be, nu: (layer, be[i], 0, 0))
    wbuf = lambda a: pltpu.VMEM(a.shape[2:], BF16)
    return pl.pallas_call(
        _ffn_kernel,
        grid_spec=pltpu.PrefetchScalarGridSpec(
            num_scalar_prefetch=2,
            grid=(rows // MOE_BLOCK,),
            in_specs=[pl.BlockSpec((MOE_BLOCK, D_MODEL), lambda i, be, nu: (i, 0)), wspec(wg), wspec(wu), wspec(wd)],
            out_specs=pl.BlockSpec((MOE_BLOCK, D_MODEL), lambda i, be, nu: (i, 0)),
            scratch_shapes=[pltpu.VMEM((MOE_BLOCK, D_EXPERT), BF16), wbuf(wg), wbuf(wu), wbuf(wd)]),
        out_shape=jax.ShapeDtypeStruct((rows, D_MODEL), F32),
        compiler_params=_cparams(("arbitrary",)),
        name="moe_ffn",
    )(blk_exp, n_used, xs, wg, wu, wd)


def _combine_kernel(dest_hbm, y_hbm, x_ref, wt_ref, lng_ref, lnb_ref, o_ref, dsm0, dsm1, buf, sem_idx, sem):
    i = pl.program_id(0)
    tm = x_ref.shape[0]
    dsm = (dsm0, dsm1)

    def gather(step, s):
        idx_copy = pltpu.make_async_copy(dest_hbm.at[step], dsm[s], sem_idx)
        idx_copy.start()
        idx_copy.wait()

        def start(r, c):
            for k in range(TOP_K):
                pltpu.make_async_copy(y_hbm.at[pl.ds(dsm[s][k * tm + r], 1)], buf.at[s, k, pl.ds(r, 1)],
                                      sem.at[s]).start(priority=k % 2)
            return c

        lax.fori_loop(0, tm, start, 0, unroll=ROW_DMA_UNROLL)

    def reduce(s):
        for k in range(TOP_K):
            pltpu.make_async_copy(y_hbm.at[pl.ds(0, tm)], buf.at[s, k], sem.at[s]).wait()
        wt = wt_ref[...]
        z = ALPHA * x_ref[...] + wt[:, 0:1] * buf[s, 0] + wt[:, 1:2] * buf[s, 1]
        o_ref[...] = _layer_norm_rows(z, lng_ref[...], lnb_ref[...])

    @pl.when(i == 0)
    def _():
        gather(0, 0)

    for s in range(2):
        @pl.when(i % 2 == s)
        def _():
            @pl.when(i + 1 < pl.num_programs(0))
            def _():
                gather(i + 1, 1 - s)
            reduce(s)


def _combine(dest_tiles, y, x, wt, lng, lnb, tm):
    T = x.shape[0]
    row = lambda i: (i, 0)
    vec = pl.BlockSpec((1, D_MODEL), lambda i: (0, 0))
    return pl.pallas_call(
        _combine_kernel,
        grid=(T // tm,),
        in_specs=[pl.BlockSpec(memory_space=pl.ANY), pl.BlockSpec(memory_space=pl.ANY),
                  pl.BlockSpec((tm, D_MODEL), row), pl.BlockSpec((tm, TOP_K), row), vec, vec],
        out_specs=pl.BlockSpec((tm, D_MODEL), row),
        out_shape=jax.ShapeDtypeStruct((T, D_MODEL), F32),
        scratch_shapes=[pltpu.SMEM((TOP_K * tm,), I32), pltpu.SMEM((TOP_K * tm,), I32),
                        pltpu.VMEM((2, TOP_K, tm, D_MODEL), F32),
                        pltpu.SemaphoreType.DMA(()), pltpu.SemaphoreType.DMA((2,))],
        compiler_params=_cparams(("arbitrary",)),
        name="moe_combine",
    )(dest_tiles, y, x, wt, lng, lnb)


def _moe(x1, e, wt, pos, cnt, wg, wu, wd, layer, lng, lnb, tm_dispatch=2048, tm_combine=1024):
    T = x1.shape[0]
    tm_dispatch = tm_dispatch if T % tm_dispatch == 0 else tm_combine
    n_blocks = -(-(T * TOP_K) // MOE_BLOCK) + N_EXPERTS
    rows = n_blocks * MOE_BLOCK
    counts = cnt[:, 0].astype(I32)
    padded = (counts + MOE_BLOCK - 1) // MOE_BLOCK * MOE_BLOCK
    pad_end = jnp.cumsum(padded)
    pad_start = pad_end - padded
    dest = pos
    for j in range(N_EXPERTS):
        dest = dest + jnp.where(e == j, pad_start[j], 0)
    blk_start = jnp.arange(n_blocks, dtype=I32) * MOE_BLOCK
    blk_exp = jnp.minimum(jnp.sum((pad_end[None, :] <= blk_start[:, None]).astype(I32), axis=1), N_EXPERTS - 1)
    n_used = (pad_end[-1:] // MOE_BLOCK).astype(I32)
    tiles = lambda tm: dest.reshape(TOP_K, T // tm, tm).transpose(1, 0, 2).reshape(T // tm, TOP_K * tm)
    tail = jnp.where(padded > 0, pad_end - MOE_BLOCK, -1)
    trailing = jnp.arange(n_blocks - N_EXPERTS, n_blocks, dtype=I32)
    trailing = jnp.where(trailing >= n_used[0], trailing * MOE_BLOCK, -1)
    xs = _dispatch(jnp.concatenate([tail, trailing]).astype(I32), tiles(tm_dispatch), x1, rows, tm_dispatch)
    y = _ffn(blk_exp, n_used, xs, wg, wu, wd, layer)
    return _combine(tiles(tm_combine), y, x1, wt.T, lng, lnb, tm_combine)


def _proj_rt_kernel(x_ref, w_ref, c_ref, s_ref, q_ref, k_ref, v_ref, g_ref):
    xb = x_ref[...].astype(BF16)
    cs, sn = c_ref[...], s_ref[...]
    rope = lambda y: y * cs + pltpu.roll(y, RT_QK // 2, 1) * sn
    pair = 2 * RT_QK
    for c in range(RT_QKW // pair):
        yq = jnp.dot(xb, w_ref[:, c * pair:(c + 1) * pair], preferred_element_type=F32)
        yk = jnp.dot(xb, w_ref[:, RT_QKW + c * pair:RT_QKW + (c + 1) * pair], preferred_element_type=F32)
        for h in range(2):
            cols = slice(c * pair + h * RT_QK, c * pair + (h + 1) * RT_QK)
            q_ref[:, cols] = rope(yq[:, h * RT_QK:(h + 1) * RT_QK]).astype(BF16)
            k_ref[:, cols] = (rope(yk[:, h * RT_QK:(h + 1) * RT_QK]) * RT_QK ** -0.5).astype(BF16)
    step = 1024
    for c in range(RT_VW // step):
        cols = slice(c * step, (c + 1) * step)
        v_ref[:, cols] = jnp.dot(xb, w_ref[:, 2 * RT_QKW + c * step:2 * RT_QKW + (c + 1) * step],
                                 preferred_element_type=F32).astype(BF16)
        base = 2 * RT_QKW + RT_VW
        g_ref[:, cols] = jax.nn.silu(jnp.dot(xb, w_ref[:, base + c * step:base + (c + 1) * step],
                                             preferred_element_type=F32))


def _proj_rt(xt, w, cs, sn, S, tm=512):
    T = xt.shape[0]
    nseq = S // tm
    row = lambda i: (i, 0)
    tab = lambda i: (i % nseq, 0)
    return pl.pallas_call(
        _proj_rt_kernel,
        grid=(T // tm,),
        in_specs=[pl.BlockSpec((tm, D_MODEL), row), pl.BlockSpec(w.shape, lambda i: (0, 0)),
                  pl.BlockSpec((tm, RT_QK), tab), pl.BlockSpec((tm, RT_QK), tab)],
        out_specs=[pl.BlockSpec((tm, RT_QKW), row), pl.BlockSpec((tm, RT_QKW), row),
                   pl.BlockSpec((tm, RT_VW), row), pl.BlockSpec((tm, RT_VW), row)],
        out_shape=[jax.ShapeDtypeStruct((T, RT_QKW), BF16), jax.ShapeDtypeStruct((T, RT_QKW), BF16),
                   jax.ShapeDtypeStruct((T, RT_VW), BF16), jax.ShapeDtypeStruct((T, RT_VW), F32)],
        compiler_params=_cparams(("parallel",)),
        name="proj_rt",
    )(xt, w, cs, sn)


def _retention_kernel(q_ref, k_ref, v_ref, sg_ref, dec_ref, qd_ref, kd_ref, cd_ref, gng_ref, gnb_ref, o_ref, state):
    j = pl.program_id(1)

    @pl.when(j == 0)
    def _():
        state[...] = jnp.zeros_like(state)

    NB = q_ref.shape[0]
    H = range(NB * RT_HEADS)
    nh = lambda i: (i // RT_HEADS, i % RT_HEADS)
    qk_cols = lambda i: slice(nh(i)[1] * RT_QK, (nh(i)[1] + 1) * RT_QK)
    v_cols = lambda i: slice(nh(i)[1] * RT_V, (nh(i)[1] + 1) * RT_V)
    q = [q_ref[nh(i)[0], :, qk_cols(i)] for i in H]
    k = [k_ref[nh(i)[0], :, qk_cols(i)] for i in H]
    v = [v_ref[nh(i)[0], :, v_cols(i)] for i in H]
    r_old = [state[i] for i in H]
    att = [(_dot_nt(q[i], k[i]) * dec_ref[nh(i)[1]]).astype(BF16) for i in H]
    cross = [jnp.dot(q[i], r_old[i].astype(BF16), preferred_element_type=F32) * qd_ref[nh(i)[1]] for i in H]
    inner = [jnp.dot(att[i], v[i], preferred_element_type=F32) for i in H]
    kdec = [(k[i].astype(F32) * kd_ref[nh(i)[1]]).astype(BF16) for i in H]
    for i in H:
        upd = lax.dot_general(kdec[i], v[i], (((0,), (0,)), ((), ())), preferred_element_type=F32)
        state[i] = r_old[i] * cd_ref[nh(i)[1]][:, 0:1] + upd
    for i in H:
        o = inner[i] + cross[i]
        mu = jnp.mean(o, axis=1, keepdims=True)
        oc = o - mu
        var = jnp.mean(oc * oc, axis=1, keepdims=True)
        on = oc * lax.rsqrt(var + RT_GN_EPS) * gng_ref[:, v_cols(i)] + gnb_ref[:, v_cols(i)]
        o_ref[nh(i)[0], :, v_cols(i)] = (sg_ref[nh(i)[0], :, v_cols(i)] * on).astype(BF16)


def _retention(q3, k3, v3, sg3, dec, qd, kd, cd, gng, gnb, nb=2):
    B, S, _ = q3.shape
    C = RT_CHUNK
    nb = nb if B % nb == 0 else 1
    qk = pl.BlockSpec((nb, C, RT_QKW), lambda b, j: (b, j, 0))
    vv = pl.BlockSpec((nb, C, RT_VW), lambda b, j: (b, j, 0))
    const = lambda a: pl.BlockSpec(a.shape, lambda b, j: (0,) * a.ndim)
    return pl.pallas_call(
        _retention_kernel,
        grid=(B // nb, S // C),
        in_specs=[qk, qk, vv, vv, const(dec), const(qd), const(kd), const(cd), const(gng), const(gnb)],
        out_specs=vv,
        out_shape=jax.ShapeDtypeStruct((B, S, RT_VW), BF16),
        scratch_shapes=[pltpu.VMEM((nb * RT_HEADS, RT_QK, RT_V), F32)],
        compiler_params=_cparams(("parallel", "arbitrary")),
        name="retention",
    )(q3, k3, v3, sg3, dec, qd, kd, cd, gng, gnb)


def _nsa_rope_tables(S):
    half = ROPE_DIM // 2
    inv = ROPE_THETA ** (-jnp.arange(half, dtype=F32) / half)
    ang = jnp.arange(S, dtype=F32)[:, None] * inv[None, :]
    cos, sin = jnp.cos(ang), jnp.sin(ang)
    zeros = lambda n: jnp.zeros((S, n), F32)
    cn = jnp.concatenate([cos, cos, jnp.ones((S, NS_HEAD - ROPE_DIM), F32)], axis=1)
    s1 = jnp.concatenate([-sin, zeros(NS_HEAD - half)], axis=1)
    s2 = jnp.concatenate([zeros(half), sin, zeros(NS_HEAD - ROPE_DIM)], axis=1)
    two = lambda a: jnp.concatenate([a, a], axis=1)
    return two(cn), two(s1), two(s2)


def _rt_rope_tables(S):
    inv = RT_THETA ** (-jnp.linspace(0.0, 1.0, RT_QK // 2, dtype=F32))
    ang = jnp.arange(S, dtype=F32)[:, None] * inv[None, :]
    cos, sin = jnp.cos(ang), jnp.sin(ang)
    return jnp.concatenate([cos, cos], axis=1), jnp.concatenate([-sin, sin], axis=1)


def _rt_decay_tables():
    log_g = jnp.log(1.0 - 2.0 ** (-5.0 - jnp.arange(RT_HEADS, dtype=F32)))
    idx = jnp.arange(RT_CHUNK, dtype=F32)
    diff = idx[:, None] - idx[None, :]
    dec = jnp.where(diff >= 0, jnp.exp(jnp.maximum(diff, 0.0) * log_g[:, None, None]), 0.0)
    qd = jnp.exp((idx + 1.0) * log_g[:, None])[..., None]
    kd = jnp.exp((RT_CHUNK - 1.0 - idx) * log_g[:, None])[..., None]
    cd = jnp.broadcast_to(jnp.exp(RT_CHUNK * log_g)[:, None, None], (RT_HEADS, 1, LANES))
    return dec, qd, kd, cd


def _overlap_table(S, ncp):
    n_cmp = (S - CMP_LEN) // CMP_STRIDE + 1
    n_sel = S // SEL_BLOCK
    cs = jnp.arange(ncp) * CMP_STRIDE
    ss = jnp.arange(LANES) * SEL_BLOCK
    ov = jnp.clip(jnp.minimum(cs[:, None] + CMP_LEN, ss[None, :] + SEL_BLOCK)
                  - jnp.maximum(cs[:, None], ss[None, :]), 0, None).astype(F32) / CMP_LEN
    keep = (jnp.arange(ncp)[:, None] < n_cmp) & (jnp.arange(LANES)[None, :] < n_sel)
    return jnp.where(keep, ov, 0.0)


def kernel(x, ab_w_in, ab_w_out, rk_mu, rk_w0, rk_w1, rk_w2, rk_a0, rk_a1, rk_a2, rk_g1, rk_g2, rk_kk, rk_ka, rk_rk,
           rk_ln, ns_pe, ns_c_w1, ns_c_w2, rt_w_in, rt_w_out, rt_gn, router_w, router_b, moe_w_gate, moe_w_up,
           moe_w_down, ln):
    B, S, D = x.shape
    T = B * S
    assert D == D_MODEL and S % 256 == 0 and S // SEL_BLOCK <= LANES and S >= WINDOW
    xt = x.reshape(T, D)
    rwt = router_w.T
    rb = router_b.reshape(N_EXPERTS, 1)
    vec = lambda a: a.reshape(1, -1)

    w_in = ab_w_in[0]
    n_gate = 3 * NS_HEADS
    w_cat = jnp.concatenate([w_in[:, :-n_gate], jnp.pad(w_in[:, -n_gate:], ((0, 0), (0, LANES - n_gate)))],
                            axis=1).astype(BF16)
    cn, s1, s2 = _nsa_rope_tables(S)
    prk, q, kk2, gt, vt, cx = _proj_ab(xt, w_cat, cn, s1, s2, S)

    ones = (jnp.arange(RK_W)[:, None] // RK_HEAD == jnp.arange(RK_W)[None, :] // RK_HEAD).astype(BF16)
    b16 = lambda a: a.astype(BF16)
    r, lw, km, v, kk, bb, g, bon = _rwkv_prep(
        prk.reshape(B, S, 4 * RK_W), rk_mu[0], vec(rk_w0[0]), b16(rk_w1[0]), b16(rk_w2[0]), vec(rk_a0[0]),
        b16(rk_a1[0]), b16(rk_a2[0]), b16(rk_g1[0]), b16(rk_g2[0]), vec(rk_kk[0]), vec(rk_ka[0]), vec(rk_rk[0]), ones)
    o_a = _rwkv_chunk(r, lw, km, v, kk, bb, g, bon, rk_ln[0, 0:1], rk_ln[0, 1:2])

    ncp = S // CMP_STRIDE
    kcv, kcvt = _nsa_compress(cx.reshape(2, B, ncp, CMP_STRIDE * LANES), ns_c_w1[0], b16(ns_c_w2[0]),
                              ns_pe[0].reshape(2, 1, CMP_LEN * NS_HEAD))
    o_b = _nsa_attn(q.reshape(B, S, NS_W), kcv, kcvt, kk2.reshape(B, S, 2 * LANES),
                    vt.reshape(2, B, S // QT, LANES, QT), gt.reshape(B, S, LANES), _overlap_table(S, ncp).T)

    w_out = b16(ab_w_out[0])
    x1, e, wt, pos, cnt = _outproj_router([o_a.reshape(T, RK_W), o_b.reshape(T, NS_W)], [w_out[:RK_W], w_out[RK_W:]],
                                          xt, ln[0, 0, 0:1], ln[0, 0, 1:2], rwt, rb)
    x2 = _moe(x1, e, wt, pos, cnt, moe_w_gate, moe_w_up, moe_w_down, 0, ln[0, 1, 0:1], ln[0, 1, 1:2])

    cs, sn = _rt_rope_tables(S)
    qr, kr, vr, sg = _proj_rt(x2, b16(rt_w_in[0]), cs, sn, S)
    dec, qd, kd, cd = _rt_decay_tables()
    ret = _retention(qr.reshape(B, S, RT_QKW), kr.reshape(B, S, RT_QKW), vr.reshape(B, S, RT_VW),
                     sg.reshape(B, S, RT_VW), dec, qd, kd, cd, rt_gn[0, 0:1], rt_gn[0, 1:2])
    x3, e, wt, pos, cnt = _outproj_router([ret.reshape(T, RT_VW)], [b16(rt_w_out[0])], x2,
                                          ln[1, 0, 0:1], ln[1, 0, 1:2], rwt, rb)
    x4 = _moe(x3, e, wt, pos, cnt, moe_w_gate, moe_w_up, moe_w_down, 1, ln[1, 1, 0:1], ln[1, 1, 1:2])
    return x4.reshape(B, S, D)
```

```python
import functools
import math

import jax
import jax.numpy as jnp
from jax import lax
from jax.experimental import pallas as pl
from jax.experimental.pallas import tpu as pltpu

F32 = jnp.float32
BF16 = jnp.bfloat16
I32 = jnp.int32
HI = lax.Precision.HIGHEST

LANES = 128
SUBLANES = 8
BF16_ROWS = 16
VMEM_LIMIT = 56 * 1024 * 1024

D_MODEL = 1024
RK_HEADS, RK_HEAD = 8, 64
RK_W = RK_HEADS * RK_HEAD
RK_DECAY_SCALE = 0.606531
RK_LN_EPS = 64e-5
NS_HEADS, NS_KV, NS_HPG, NS_HEAD = 8, 2, 4, 64
NS_W = NS_HEADS * NS_HEAD
CMP_LEN, CMP_STRIDE, SEL_BLOCK, SEL_TOPK, WINDOW = 32, 16, 64, 16, 512
ROPE_THETA = 500000.0
ROPE_DIM = NS_HEAD // 4
Q_SCALE = NS_HEAD ** -0.5 * math.log2(math.e)
QT = 256
KT = 512
RT_HEADS, RT_QK, RT_V = 8, 128, 256
RT_QKW, RT_VW = RT_HEADS * RT_QK, RT_HEADS * RT_V
RT_CHUNK = 256
RT_THETA = 10000.0
RT_GN_EPS = 1e-5
N_EXPERTS, N_GROUPS, EXP_PER_GROUP, TOP_K = 16, 4, 4, 2
D_EXPERT = 1024
MOE_BLOCK = 512
ROW_DMA_UNROLL = 8
DEPTH = 2
ALPHA = (2.0 * DEPTH) ** 0.25
LN_EPS = 1e-5
NEG = -1e30


def _cparams(sem):
    return pltpu.CompilerParams(dimension_semantics=sem, vmem_limit_bytes=VMEM_LIMIT)


def _bdot(a, w):
    return jnp.dot(a.astype(BF16), w, preferred_element_type=F32)


def _dot_nt(a, b):
    return lax.dot_general(a, b, (((1,), (1,)), ((), ())), preferred_element_type=F32)


def _layer_norm_rows(z, g, b):
    mu = jnp.mean(z, axis=1, keepdims=True)
    zc = z - mu
    var = jnp.mean(zc * zc, axis=1, keepdims=True)
    return zc * lax.rsqrt(var + LN_EPS) * g + b


def _proj_ab_kernel(x_ref, w_ref, cn_ref, s1_ref, s2_ref, pick_ref, prk_ref, q_ref, kk_ref, gt_ref, vt_ref, cx_ref):
    xb = x_ref[...].astype(BF16)
    for c in range(4):
        prk_ref[:, c * RK_W:(c + 1) * RK_W] = jnp.dot(xb, w_ref[:, c * RK_W:(c + 1) * RK_W],
                                                      preferred_element_type=F32)
    cn, s1, s2 = cn_ref[...], s1_ref[...], s2_ref[...]

    def rope(y):
        return y * cn + pltpu.roll(y, LANES - ROPE_DIM // 2, 1) * s1 + pltpu.roll(y, ROPE_DIM // 2, 1) * s2

    base = 4 * RK_W
    yq = jnp.dot(xb, w_ref[:, base:base + NS_W], preferred_element_type=F32)
    for c in range(NS_W // LANES):
        q_ref[:, c * LANES:(c + 1) * LANES] = (rope(yq[:, c * LANES:(c + 1) * LANES]) * Q_SCALE).astype(BF16)
    base += NS_W
    ykv = jnp.dot(xb, w_ref[:, base:base + 6 * LANES], preferred_element_type=F32)
    for c in range(6):
        y = ykv[:, c * LANES:(c + 1) * LANES]
        if c % 2 == 0:
            y = rope(y)
        if c < 2:
            yb = y.astype(BF16)
            for l in range(CMP_STRIDE):
                cx_ref[c, :, l * LANES:(l + 1) * LANES] = jnp.dot(pick_ref[l], yb,
                                                                  preferred_element_type=F32).astype(BF16)
        elif c % 2 == 0:
            kk_ref[:, (c // 2 - 1) * LANES:(c // 2) * LANES] = y.astype(BF16)
        else:
            for t in range(y.shape[0] // QT):
                vt_ref[c // 2 - 1, t] = y[t * QT:(t + 1) * QT].T.astype(BF16)
    base += 6 * LANES
    gt_ref[...] = jax.nn.sigmoid(jnp.dot(xb, w_ref[:, base:base + LANES], preferred_element_type=F32))


def _proj_ab(xt, w, cn, s1, s2, S, tm=512):
    T = xt.shape[0]
    ncols = w.shape[1]
    nseq = S // tm
    row = lambda i: (i, 0)
    tab = lambda i: (i % nseq, 0)
    ng = tm // CMP_STRIDE
    pick = (jnp.arange(tm, dtype=I32)[None, None, :]
            == jnp.arange(ng, dtype=I32)[None, :, None] * CMP_STRIDE + jnp.arange(CMP_STRIDE, dtype=I32)[:, None, None])
    pick = pick.astype(BF16)
    return pl.pallas_call(
        _proj_ab_kernel,
        grid=(T // tm,),
        in_specs=[pl.BlockSpec((tm, D_MODEL), row),
                  pl.BlockSpec((D_MODEL, ncols), lambda i: (0, 0)),
                  pl.BlockSpec((tm, LANES), tab), pl.BlockSpec((tm, LANES), tab), pl.BlockSpec((tm, LANES), tab),
                  pl.BlockSpec(pick.shape, lambda i: (0, 0, 0))],
        out_specs=[pl.BlockSpec((tm, 4 * RK_W), row), pl.BlockSpec((tm, NS_W), row),
                   pl.BlockSpec((tm, 2 * LANES), row), pl.BlockSpec((tm, LANES), row),
                   pl.BlockSpec((2, tm // QT, LANES, QT), lambda i: (0, i, 0, 0)),
                   pl.BlockSpec((2, ng, CMP_STRIDE * LANES), lambda i: (0, i, 0))],
        out_shape=[jax.ShapeDtypeStruct((T, 4 * RK_W), F32), jax.ShapeDtypeStruct((T, NS_W), BF16),
                   jax.ShapeDtypeStruct((T, 2 * LANES), BF16), jax.ShapeDtypeStruct((T, LANES), F32),
                   jax.ShapeDtypeStruct((2, T // QT, LANES, QT), BF16),
                   jax.ShapeDtypeStruct((2, T // CMP_STRIDE, CMP_STRIDE * LANES), BF16)],
        compiler_params=_cparams(("parallel",)),
        name="proj_ab",
    )(xt, w, cn, s1, s2, pick)


def _rwkv_prep_kernel(p_ref, mu_ref, w0_ref, w1_ref, w2_ref, a0_ref, a1_ref, a2_ref, g1_ref, g2_ref,
                      kk_ref, ka_ref, rk_ref, ones_ref,
                      r_o, w_o, k_o, v_o, kk_o, b_o, g_o, bon_o, carry):
    j = pl.program_id(1)
    p = p_ref[...]
    tm = p.shape[0]

    @pl.when(j == 0)
    def _():
        carry[...] = jnp.zeros_like(carry)

    rowi = lax.broadcasted_iota(I32, p.shape, 0)
    prev = jnp.where(rowi == 0, carry[...], pltpu.roll(p, 1, 0))
    carry[...] = p[tm - 1:tm, :]
    dp = prev - p
    sl = lambda a, c: a[:, c * RK_W:(c + 1) * RK_W]
    mu = mu_ref[...]
    r = sl(p, 0) + sl(dp, 0) * mu[0:1]
    k = sl(p, 1) + sl(dp, 1) * mu[1:2]
    v = sl(p, 2) + sl(dp, 2) * mu[2:3]
    xw = sl(p, 3) + sl(dp, 3) * mu[3:4]
    xa = sl(p, 3) + sl(dp, 3) * mu[4:5]
    xg = sl(p, 3) + sl(dp, 3) * mu[5:6]
    lw = -RK_DECAY_SCALE * jax.nn.sigmoid(w0_ref[...] + _bdot(jnp.tanh(_bdot(xw, w1_ref[...])), w2_ref[...]))
    a = jax.nn.sigmoid(a0_ref[...] + _bdot(_bdot(xa, a1_ref[...]), a2_ref[...]))
    g = _bdot(jax.nn.sigmoid(_bdot(xg, g1_ref[...])), g2_ref[...])
    ones = ones_ref[...]

    def head_sum(t):
        hi = t.astype(BF16)
        lo = (t - hi.astype(F32)).astype(BF16)
        return (jnp.dot(hi, ones, preferred_element_type=F32) + jnp.dot(lo, ones, preferred_element_type=F32))

    kk = k * kk_ref[...]
    kk = kk / jnp.maximum(jnp.sqrt(head_sum(kk * kk)), 1e-12)
    km = k * (1.0 + (a - 1.0) * ka_ref[...])
    bon = head_sum(r * km * rk_ref[...]) * v
    w_o[...] = lw
    for ref, val in ((r_o, r), (k_o, km), (v_o, v), (kk_o, kk), (b_o, kk * a), (g_o, g), (bon_o, bon)):
        ref[...] = val.astype(ref.dtype)


def _rwkv_prep(prk3, mu, w0, w1, w2, a0, a1, a2, g1, g2, k_k, k_a, r_k, ones, tm=512):
    B, S, _ = prk3.shape
    full = lambda a: pl.BlockSpec(a.shape, lambda b, j: (0,) * a.ndim)
    params = [mu, w0, w1, w2, a0, a1, a2, g1, g2, k_k, k_a, r_k, ones]
    ospec = pl.BlockSpec((None, tm, RK_W), lambda b, j: (b, j, 0))
    return pl.pallas_call(
        _rwkv_prep_kernel,
        grid=(B, S // tm),
        in_specs=[pl.BlockSpec((None, tm, 4 * RK_W), lambda b, j: (b, j, 0))] + [full(a) for a in params],
        out_specs=[ospec] * 8,
        out_shape=[jax.ShapeDtypeStruct((B, S, RK_W), F32 if i == 1 else BF16) for i in range(8)],
        scratch_shapes=[pltpu.VMEM((1, 4 * RK_W), F32)],
        compiler_params=_cparams(("parallel", "arbitrary")),
        name="rwkv_prep",
    )(prk3, *params)


RK_CHUNK = 16


def _pdot(a, b, dims, precise):
    if precise:
        return lax.dot_general(a, b, (dims, ((), ())), precision=HI, preferred_element_type=F32)
    return lax.dot_general(a.astype(BF16), b.astype(BF16), (dims, ((), ())), preferred_element_type=F32)


def _rwkv_chunk_kernel(r_ref, lw_ref, k_ref, v_ref, kk_ref, b_ref, g_ref, bon_ref, lng_ref, lnb_ref, o_ref, ht,
                       *, precise):
    j = pl.program_id(1)

    @pl.when(j == 0)
    def _():
        ht[...] = jnp.zeros_like(ht)

    NB, TT = r_ref.shape[0], r_ref.shape[1]
    C, N = RK_CHUNK, RK_HEAD
    mm = lambda a, b: _pdot(a, b, ((1,), (0,)), precise)
    mm_nt = lambda a, b: _pdot(a, b, ((1,), (1,)), precise)
    mm_tn = lambda a, b: _pdot(a, b, ((0,), (0,)), precise)
    wide = lambda ref: jnp.concatenate([ref[n] for n in range(NB)], axis=1)

    lw = wide(lw_ref)
    rowc = lax.broadcasted_iota(I32, lw.shape, 0) & (C - 1)
    linc, lrev = lw, lw
    sh = 1
    while sh < C:
        linc = linc + jnp.where(rowc >= sh, pltpu.roll(linc, sh, 0), 0.0)
        lrev = lrev + jnp.where(rowc < C - sh, pltpu.roll(lrev, TT - sh, 0), 0.0)
        sh *= 2
    lrev = lrev - lw
    r, k, v, kk, b = wide(r_ref), wide(k_ref), wide(v_ref), wide(kk_ref), wide(b_ref)
    e_in, e_inv, e_rev = jnp.exp(linc), jnp.exp(-linc), jnp.exp(lrev)
    kkd = kk * jnp.exp(linc - lw)
    rd = r * e_in
    binv, kinv = b * e_inv, k * e_inv
    bd, kd = b * e_rev, k * e_rev
    gam = jnp.exp(linc + lrev)

    ti = lax.broadcasted_iota(I32, (TT, TT), 0)
    tj = lax.broadcasted_iota(I32, (TT, TT), 1)
    same = (ti // C) == (tj // C)
    strict = same & (tj < ti)
    incl = same & (tj <= ti)

    H = range(NB * RK_HEADS)
    lo = lambda a: a if precise else a.astype(BF16)
    hs = lambda a: [a[:, h * N:(h + 1) * N] for h in H]
    rows2 = lambda a, b: jnp.concatenate([a, b], axis=0)
    kkd_h, rd_h, v_h = hs(lo(kkd)), hs(lo(rd)), hs(lo(v))
    binv_h, kinv_h, bd_h, kd_h = hs(lo(binv)), hs(lo(kinv)), hs(lo(bd)), hs(lo(kd))
    gam_h = hs(gam)
    gm = [mm_nt(rows2(kkd_h[h], rd_h[h]), rows2(binv_h[h], kinv_h[h])) for h in H]
    a_b = [lo(jnp.where(strict, gm[h][:TT, :TT], 0.0)) for h in H]
    b_rb = [lo(jnp.where(incl, gm[h][TT:, :TT], 0.0)) for h in H]
    akb = [lo(rows2(jnp.where(strict, gm[h][:TT, TT:], 0.0), jnp.where(incl, gm[h][TT:, TT:], 0.0))) for h in H]
    av = [mm(akb[h], v_h[h]) for h in H]
    x = [jnp.concatenate([kkd_h[h].astype(F32), av[h][:TT]], axis=1) for h in H]
    cols2 = lambda a, b: jnp.concatenate([a, b], axis=1)
    ax = [mm(a_b[h], cols2(a_b[h], lo(x[h]))) for h in H]
    x = [x[h] - ax[h][:, TT:] for h in H]
    a2 = [lo(ax[h][:, :TT]) for h in H]
    ax = [mm(a2[h], cols2(a2[h], lo(x[h]))) for h in H]
    x = [x[h] + ax[h][:, TT:] for h in H]
    a4 = [lo(ax[h][:, :TT]) for h in H]
    ax = [mm(a4[h], cols2(a4[h], lo(x[h]))) for h in H]
    x = [x[h] + ax[h][:, TT:] for h in H]
    a8 = [lo(ax[h][:, :TT]) for h in H]
    x = [x[h] + mm(a8[h], lo(x[h])) for h in H]
    wt = [lo(x[h][:, :N]) for h in H]
    h_t = [ht[h] for h in H]
    us = [[] for _ in H]
    rhs = [[] for _ in H]
    for c in range(TT // C):
        rs = slice(c * C, (c + 1) * C)
        xh = [mm_nt(rows2(wt[h][rs], rd_h[h][rs]), lo(h_t[h])) for h in H]
        for h in H:
            u_c = -(xh[h][:C] + x[h][rs, N:])
            us[h].append(u_c)
            rhs[h].append(xh[h][C:])
        upd = [mm_tn(rows2(lo(us[h][c]), v_h[h][rs]), rows2(bd_h[h][rs], kd_h[h][rs])) for h in H]
        h_t = [h_t[h] * gam_h[h][c * C:c * C + 1] + upd[h] for h in H]
    outs = []
    for h in H:
        ht[h] = h_t[h]
        o = jnp.concatenate(rhs[h], axis=0) + mm(b_rb[h], lo(jnp.concatenate(us[h], axis=0))) + av[h][TT:]
        mu = jnp.mean(o, axis=1, keepdims=True)
        oc = o - mu
        var = jnp.mean(oc * oc, axis=1, keepdims=True)
        outs.append(oc * lax.rsqrt(var + RK_LN_EPS))
    for n in range(NB):
        on = jnp.concatenate(outs[n * RK_HEADS:(n + 1) * RK_HEADS], axis=1)
        o_ref[n] = (on * lng_ref[...] + lnb_ref[...] + bon_ref[n]) * g_ref[n]


def _rwkv_chunk(r, lw, k, v, kk, bb, g, bon, lng, lnb, tt=128, nb=4, precise=False):
    B, S, _ = r.shape
    nb = nb if B % nb == 0 else 1
    blk = pl.BlockSpec((nb, tt, RK_W), lambda b, j: (b, j, 0))
    vec = pl.BlockSpec((1, RK_W), lambda b, j: (0, 0))
    return pl.pallas_call(
        functools.partial(_rwkv_chunk_kernel, precise=precise),
        grid=(B // nb, S // tt),
        in_specs=[blk] * 8 + [vec, vec],
        out_specs=blk,
        out_shape=jax.ShapeDtypeStruct((B, S, RK_W), F32),
        scratch_shapes=[pltpu.VMEM((nb * RK_HEADS, RK_HEAD, RK_HEAD), F32)],
        compiler_params=_cparams(("parallel", "arbitrary")),
        name="rwkv_chunk",
    )(r, lw, k, v, kk, bb, g, bon, lng, lnb)


def _nsa_compress_kernel(x_ref, w1g_ref, w1f_ref, w2_ref, pe_ref, o_ref, ot_ref):
    bias = jnp.dot(jnp.broadcast_to(pe_ref[...], (SUBLANES, CMP_LEN * NS_HEAD)), w1f_ref[...], precision=HI,
                   preferred_element_type=F32)[0:1]
    x = x_ref[...]
    n = x.shape[0]
    outs = []
    for g in range(NS_KV):
        ya = jnp.dot(x, w1g_ref[g, 0], preferred_element_type=F32)
        yb = jnp.dot(x, w1g_ref[g, 1], preferred_element_type=F32)
        h = ya + pltpu.roll(yb, n - 1, 0) + bias
        outs.append(_bdot(jax.nn.gelu(h), w2_ref[...]))
    out = jnp.concatenate(outs, axis=1)
    o_ref[...] = out.astype(BF16)
    ot_ref[...] = out.T.astype(BF16)


def _nsa_compress(cx, w1, w2b, pe):
    _, B, ncp, width = cx.shape
    hid = w1.shape[-1]
    w1h = w1.reshape(2, 2, CMP_STRIDE, 1, NS_HEAD, hid)
    zero = jnp.zeros_like(w1h)
    w1g = jnp.stack([jnp.concatenate([w1h, zero], axis=3), jnp.concatenate([zero, w1h], axis=3)], axis=1)
    w1g = w1g.reshape(2, NS_KV, 2, width, hid).astype(BF16)
    return pl.pallas_call(
        _nsa_compress_kernel,
        grid=(B, 2),
        in_specs=[pl.BlockSpec((None, None, ncp, width), lambda b, c: (c, b, 0, 0)),
                  pl.BlockSpec((None, NS_KV, 2, width, hid), lambda b, c: (c, 0, 0, 0, 0)),
                  pl.BlockSpec((None, CMP_LEN * NS_HEAD, hid), lambda b, c: (c, 0, 0)),
                  pl.BlockSpec((None, hid, NS_HEAD), lambda b, c: (c, 0, 0)),
                  pl.BlockSpec((None, 1, CMP_LEN * NS_HEAD), lambda b, c: (c, 0, 0))],
        out_specs=[pl.BlockSpec((None, None, ncp, LANES), lambda b, c: (b, c, 0, 0)),
                   pl.BlockSpec((None, None, LANES, ncp), lambda b, c: (b, c, 0, 0))],
        out_shape=[jax.ShapeDtypeStruct((B, 2, ncp, LANES), BF16), jax.ShapeDtypeStruct((B, 2, LANES, ncp), BF16)],
        compiler_params=_cparams(("parallel", "parallel")),
        name="nsa_compress",
    )(cx, w1g, w1, w2b, pe)


def _nsa_attn_kernel(q_ref, kc_ref, vct_ref, ks_ref, vst_ref, kw_ref, vwt_ref, gt_ref, ovt_ref, ext_ref, o_ref,
                     m_s, l_s, acc_s, m_w, l_w, acc_w, s_buf, *, ncp):
    i = pl.program_id(1)
    s0 = i * QT
    heads = range(NS_HEADS)
    hcols = lambda h: slice(h * QT, (h + 1) * QT)
    gcols = lambda h: slice(h // NS_HPG * QT, (h // NS_HPG + 1) * QT)
    iota = lambda shape, d: lax.broadcasted_iota(I32, shape, d)
    qpos = lambda shape: s0 + (iota(shape, 1) & (QT - 1))

    def col_reduce(x, op, final):
        n = x.shape[0]
        while n > SUBLANES:
            n //= 2
            x = op(x[:n], x[n:])
        return final(x, axis=0, keepdims=True)

    q8 = q_ref[...].astype(F32)
    zeros = jnp.zeros((QT, NS_HEAD), F32)
    qt = []
    for h in heads:
        qh = q8[:, h * NS_HEAD:(h + 1) * NS_HEAD]
        qt.append(jnp.concatenate([qh, zeros] if h < NS_HPG else [zeros, qh], axis=1).T)
    qt = jnp.concatenate(qt, axis=1).astype(BF16)

    sc = jnp.dot(kc_ref[...], qt, preferred_element_type=F32)
    n_row = iota((ncp, QT), 0)
    cmask = (n_row * CMP_STRIDE + (CMP_LEN - 1) <= s0 + iota((ncp, QT), 1)) & (n_row < ncp - 1)
    cpen = jnp.where(cmask, 0.0, NEG)
    sc = jnp.concatenate([sc[:, hcols(h)] + cpen for h in heads], axis=1)
    mc = col_reduce(sc, jnp.maximum, jnp.max)
    pc = jnp.exp2(sc - mc)
    lc = col_reduce(pc, jnp.add, jnp.sum)
    pc = pc * jnp.where(mc > 0.5 * NEG, 1.0 / lc, 0.0)
    pcs = jnp.concatenate([functools.reduce(jnp.add, [pc[:, hcols(h)] for h in range(g * NS_HPG, (g + 1) * NS_HPG)])
                           for g in range(NS_KV)], axis=1)

    def group_dot(vt, p, extra=None):
        half = NS_HPG * QT
        outs = []
        for g in range(NS_KV):
            vg = vt[g * NS_HEAD:(g + 1) * NS_HEAD]
            if extra is not None:
                vg = jnp.concatenate([vg, extra], axis=0)
            outs.append(jnp.dot(vg, p[:, g * half:(g + 1) * half], preferred_element_type=F32))
        return jnp.concatenate(outs, axis=1)

    o_c = group_dot(vct_ref[...], pc.astype(BF16))

    imp = jnp.dot(ovt_ref[...], pcs, precision=HI, preferred_element_type=F32)
    blk = iota((LANES, NS_KV * QT), 0)
    cur = qpos((LANES, NS_KV * QT)) // SEL_BLOCK
    valid = blk <= cur
    forced = (blk == 0) | (blk == cur) | (blk == cur - 1)
    pri = jnp.where(valid & ~forced, imp, -jnp.inf)
    picked = forced
    blkf = blk.astype(F32)
    for _ in range(SEL_TOPK - 3):
        mx = col_reduce(pri, jnp.maximum, jnp.max)
        hit = blkf == col_reduce(jnp.where(pri == mx, blkf, float(LANES)), jnp.minimum, jnp.min)
        picked = picked | hit
        pri = jnp.where(hit, -jnp.inf, pri)
    selpen = jnp.where(picked & valid, 0.0, NEG).astype(BF16)
    wq = jnp.concatenate([qt, jnp.concatenate([selpen[:, gcols(h)] for h in heads], axis=1)], axis=0)

    def attend(s, vt, pen, m_ref, l_ref, acc_ref):
        if pen is not None:
            s = jnp.concatenate([s[:, hcols(h)] + pen for h in heads], axis=1)
        m_old = m_ref[...]
        m_new = jnp.maximum(m_old, col_reduce(s, jnp.maximum, jnp.max))
        alpha = jnp.exp2(m_old - m_new)
        p = jnp.exp2(s - m_new).astype(BF16)
        m_ref[...] = m_new
        pv = group_dot(vt, p, extra=jnp.ones((BF16_ROWS, vt.shape[1]), BF16))
        l_ref[...] = alpha * l_ref[...] + pv[NS_HEAD:NS_HEAD + 1]
        acc_ref[...] = alpha * acc_ref[...] + pv[:NS_HEAD]

    def reset(m_ref, l_ref, acc_ref):
        m_ref[...] = jnp.full(m_ref.shape, NEG, F32)
        l_ref[...] = jnp.zeros_like(l_ref)
        acc_ref[...] = jnp.zeros_like(acc_ref)

    tiles = lambda ref, t0, n: jnp.concatenate([ref[t0 + c] for c in range(n)], axis=1)

    reset(m_s, l_s, acc_s)
    diag = i // (KT // QT)

    def sel_scores(kt, slot):
        k0 = pl.multiple_of(kt * KT, KT)
        keys = jnp.concatenate([ks_ref[pl.ds(k0, KT), :], ext_ref[pl.ds(k0, KT), :]], axis=1)
        s_buf[slot] = jnp.dot(keys, wq, preferred_element_type=F32)

    def sel_attend(kt, slot, causal):
        pen = None
        if causal:
            pen = jnp.where(kt * KT + iota((KT, QT), 0) <= s0 + iota((KT, QT), 1), 0.0, NEG)
        attend(s_buf[slot], tiles(vst_ref, kt * (KT // QT), KT // QT), pen, m_s, l_s, acc_s)

    def sel_pair(j, c):
        sel_scores(2 * j + 1, 1)
        sel_attend(2 * j, 0, False)
        sel_scores(2 * j + 2, 0)
        sel_attend(2 * j + 1, 1, False)
        return c

    d0 = pl.multiple_of(s0, QT)
    a0 = pl.multiple_of(jnp.maximum(s0 - WINDOW, 0), QT)
    s_d = jnp.dot(kw_ref[pl.ds(d0, QT), :], qt, preferred_element_type=F32)
    s_a = jnp.dot(kw_ref[pl.ds(a0, WINDOW), :], qt, preferred_element_type=F32)
    pen_d = jnp.where(iota((QT, QT), 0) <= iota((QT, QT), 1), 0.0, NEG)
    kpos = a0 + iota((WINDOW, QT), 0)
    pen_a = jnp.where((kpos < s0) & (kpos > s0 + iota((WINDOW, QT), 1) - WINDOW), 0.0, NEG)

    sel_scores(0, 0)
    reset(m_w, l_w, acc_w)
    attend(s_d, vwt_ref[i], pen_d, m_w, l_w, acc_w)
    attend(s_a, tiles(vwt_ref, a0 // QT, WINDOW // QT), pen_a, m_w, l_w, acc_w)
    lax.fori_loop(0, diag // 2, sel_pair, 0)

    @pl.when(diag % 2 == 1)
    def _():
        sel_scores(diag, 1)
        sel_attend(diag - 1, 0, False)
        sel_attend(diag, 1, True)

    @pl.when(diag % 2 == 0)
    def _():
        sel_attend(diag, 0, True)

    gtt = gt_ref[...].T
    o_s = acc_s[...] / l_s[...]
    o_w = acc_w[...] / l_w[...]
    outs = []
    for h in heads:
        gate = lambda br: gtt[br * NS_HEADS + h:br * NS_HEADS + h + 1]
        outs.append(gate(0) * o_c[:, hcols(h)] + gate(1) * o_s[:, hcols(h)] + gate(2) * o_w[:, hcols(h)])
    pairs = [jnp.concatenate(outs[p:p + 2], axis=0).T for p in range(0, NS_HEADS, 2)]
    o_ref[...] = jnp.concatenate(pairs, axis=1).astype(BF16)


def _nsa_attn(q3, kcv, kcvt, kv3, vt, gt3, ovt):
    B, S, _ = q3.shape
    ncp = kcv.shape[2]
    R = NS_HEADS * QT
    ext = (jnp.arange(S, dtype=I32)[:, None] // SEL_BLOCK == jnp.arange(LANES, dtype=I32)[None, :]).astype(BF16)
    seq = lambda c: pl.BlockSpec((None, S, LANES), lambda b, i, c=c: (b, 0, c))
    seqt = lambda c: pl.BlockSpec((None, None, S // QT, LANES, QT), lambda b, i, c=c: (c, b, 0, 0, 0))
    return pl.pallas_call(
        functools.partial(_nsa_attn_kernel, ncp=ncp),
        grid=(B, S // QT),
        in_specs=[pl.BlockSpec((None, QT, NS_W), lambda b, i: (b, i, 0)),
                  pl.BlockSpec((None, None, ncp, LANES), lambda b, i: (b, 0, 0, 0)),
                  pl.BlockSpec((None, None, LANES, ncp), lambda b, i: (b, 1, 0, 0)),
                  seq(0), seqt(0), seq(1), seqt(1),
                  pl.BlockSpec((None, QT, LANES), lambda b, i: (b, i, 0)),
                  pl.BlockSpec(ovt.shape, lambda b, i: (0, 0)),
                  pl.BlockSpec(ext.shape, lambda b, i: (0, 0))],
        out_specs=pl.BlockSpec((None, QT, NS_W), lambda b, i: (b, i, 0)),
        out_shape=jax.ShapeDtypeStruct((B, S, NS_W), BF16),
        scratch_shapes=([pltpu.VMEM((1, R), F32), pltpu.VMEM((1, R), F32), pltpu.VMEM((NS_HEAD, R), F32)] * 2
                        + [pltpu.VMEM((2, KT, R), F32)]),
        compiler_params=_cparams(("parallel", "arbitrary")),
        name="nsa_attn",
    )(q3, kcv, kcvt, kv3, vt, kv3, vt, gt3, ovt, ext)


def _first_argmax(vals):
    m = vals[0]
    for v in vals[1:]:
        m = jnp.maximum(m, v)
    idx = jnp.full(m.shape, len(vals) - 1, I32)
    for j in range(len(vals) - 2, -1, -1):
        idx = jnp.where(vals[j] == m, j, idx)
    return m, idx


def _outproj_router_kernel(*refs, n_in):
    acts, ws = refs[:n_in], refs[n_in:2 * n_in]
    x_ref, lng_ref, lnb_ref, rwt_ref, rb_ref, tri_ref = refs[2 * n_in:2 * n_in + 6]
    y_ref, e_ref, wt_ref, pos_ref, cnt_ref, cnt = refs[2 * n_in + 6:]
    i = pl.program_id(0)

    @pl.when(i == 0)
    def _():
        cnt[...] = jnp.zeros_like(cnt)

    mix = _bdot(acts[0][...], ws[0][...])
    for a, w in zip(acts[1:], ws[1:]):
        mix = mix + _bdot(a[...], w[...])
    y = _layer_norm_rows(ALPHA * x_ref[...] + mix, lng_ref[...], lnb_ref[...])
    y_ref[...] = y

    split = lambda t: (t.astype(BF16), (t - t.astype(BF16).astype(F32)).astype(BF16))
    (w_hi, w_lo), (y_hi, y_lo) = split(rwt_ref[...]), split(y)
    logit = _dot_nt(w_hi, y_hi) + (_dot_nt(w_hi, y_lo) + _dot_nt(w_lo, y_hi))
    aff = jax.nn.sigmoid(logit)
    biased = aff + rb_ref[...]
    neg_inf = -jnp.inf
    g_score, g_i1, g_i2 = [], [], []
    for gi in range(N_GROUPS):
        vals = [biased[gi * EXP_PER_GROUP + j:gi * EXP_PER_GROUP + j + 1, :] for j in range(EXP_PER_GROUP)]
        m1, i1 = _first_argmax(vals)
        m2, i2 = _first_argmax([jnp.where(i1 == j, neg_inf, vals[j]) for j in range(EXP_PER_GROUP)])
        g_score.append(m1 + m2)
        g_i1.append(i1)
        g_i2.append(i2)
    _, grp = _first_argmax(g_score)
    loc1, loc2 = g_i1[-1], g_i2[-1]
    for gi in range(N_GROUPS - 2, -1, -1):
        loc1 = jnp.where(grp == gi, g_i1[gi], loc1)
        loc2 = jnp.where(grp == gi, g_i2[gi], loc2)
    e1 = grp * EXP_PER_GROUP + loc1
    e2 = grp * EXP_PER_GROUP + loc2
    eio = lax.broadcasted_iota(I32, aff.shape, 0)
    oh1 = eio == e1
    oh2 = eio == e2
    a1 = jnp.sum(jnp.where(oh1, aff, 0.0), axis=0, keepdims=True)
    a2 = jnp.sum(jnp.where(oh2, aff, 0.0), axis=0, keepdims=True)
    tot = a1 + a2
    e_ref[...] = jnp.concatenate([e1, e2], axis=0)
    wt_ref[...] = jnp.concatenate([a1 / tot, a2 / tot], axis=0)

    ohs = oh1.astype(F32) + oh2.astype(F32)
    before = jnp.dot(ohs.astype(BF16), tri_ref[...], preferred_element_type=F32) + cnt[...]
    p1 = jnp.sum(jnp.where(oh1, before, 0.0), axis=0, keepdims=True)
    p2 = jnp.sum(jnp.where(oh2, before, 0.0), axis=0, keepdims=True)
    pos_ref[...] = jnp.concatenate([p1, p2], axis=0).astype(I32)
    cnt[...] = cnt[...] + jnp.sum(ohs, axis=1, keepdims=True)
    cnt_ref[...] = jnp.broadcast_to(cnt[...], cnt_ref.shape)


def _outproj_router(acts, ws, xres, lng, lnb, rwt, rb, tm=1024):
    T = xres.shape[0]
    n_in = len(acts)
    tri = (lax.broadcasted_iota(I32, (tm, tm), 0) < lax.broadcasted_iota(I32, (tm, tm), 1)).astype(BF16)
    row = lambda i: (i, 0)
    const = lambda a: pl.BlockSpec(a.shape, lambda i: (0,) * a.ndim)
    lane_blk = pl.BlockSpec((TOP_K, tm), lambda i: (0, i))
    return pl.pallas_call(
        functools.partial(_outproj_router_kernel, n_in=n_in),
        grid=(T // tm,),
        in_specs=([pl.BlockSpec((tm, a.shape[1]), row) for a in acts] + [const(w) for w in ws]
                  + [pl.BlockSpec((tm, D_MODEL), row), const(lng), const(lnb), const(rwt), const(rb), const(tri)]),
        out_specs=[pl.BlockSpec((tm, D_MODEL), row), lane_blk, lane_blk, lane_blk,
                   pl.BlockSpec((N_EXPERTS, LANES), lambda i: (0, 0))],
        out_shape=[jax.ShapeDtypeStruct((T, D_MODEL), F32), jax.ShapeDtypeStruct((TOP_K, T), I32),
                   jax.ShapeDtypeStruct((TOP_K, T), F32), jax.ShapeDtypeStruct((TOP_K, T), I32),
                   jax.ShapeDtypeStruct((N_EXPERTS, LANES), F32)],
        scratch_shapes=[pltpu.VMEM((N_EXPERTS, 1), F32)],
        compiler_params=_cparams(("arbitrary",)),
        name="outproj_router",
    )(*acts, *ws, xres, lng, lnb, rwt, rb, tri)


def _dispatch_kernel(zrow_ref, dest_hbm, x_ref, xs_hbm, dsm, zbuf, sem_idx, sem, sem_z):
    i = pl.program_id(0)
    tm = x_ref.shape[0]

    @pl.when(i == 0)
    def _():
        zbuf[...] = jnp.zeros_like(zbuf)
        zero_copy = lambda j: pltpu.make_async_copy(
            zbuf, xs_hbm.at[pl.ds(pl.multiple_of(jnp.maximum(zrow_ref[j], 0), MOE_BLOCK), MOE_BLOCK)], sem_z)
        for j in range(zrow_ref.shape[0]):
            @pl.when(zrow_ref[j] >= 0)
            def _():
                zero_copy(j).start()
        for j in range(zrow_ref.shape[0]):
            @pl.when(zrow_ref[j] >= 0)
            def _():
                zero_copy(j).wait()

    idx_copy = pltpu.make_async_copy(dest_hbm.at[i], dsm, sem_idx)
    idx_copy.start()
    idx_copy.wait()

    def row_copy(r, k):
        return pltpu.make_async_copy(x_ref.at[pl.ds(r, 1)], xs_hbm.at[pl.ds(dsm[k * tm + r], 1)], sem)

    def start(r, c):
        for k in range(TOP_K):
            row_copy(r, k).start()
        return c

    lax.fori_loop(0, tm, start, 0, unroll=ROW_DMA_UNROLL)
    for k in range(TOP_K):
        pltpu.make_async_copy(x_ref, xs_hbm.at[pl.ds(0, tm)], sem).wait()


def _dispatch(zero_rows, dest_tiles, x, rows, tm):
    T = x.shape[0]
    return pl.pallas_call(
        _dispatch_kernel,
        grid_spec=pltpu.PrefetchScalarGridSpec(
            num_scalar_prefetch=1,
            grid=(T // tm,),
            in_specs=[pl.BlockSpec(memory_space=pl.ANY), pl.BlockSpec((tm, D_MODEL), lambda i, z: (i, 0))],
            out_specs=pl.BlockSpec(memory_space=pl.ANY),
            scratch_shapes=[pltpu.SMEM((TOP_K * tm,), I32), pltpu.VMEM((MOE_BLOCK, D_MODEL), F32),
                            pltpu.SemaphoreType.DMA(()), pltpu.SemaphoreType.DMA(()), pltpu.SemaphoreType.DMA(())]),
        out_shape=jax.ShapeDtypeStruct((rows, D_MODEL), F32),
        compiler_params=_cparams(("arbitrary",)),
        name="moe_dispatch",
    )(zero_rows, dest_tiles, x)


def _ffn_kernel(be_ref, nu_ref, xs_ref, wg_ref, wu_ref, wd_ref, y_ref, h_ref, wg_b, wu_b, wd_b):
    i = pl.program_id(0)
    half = D_EXPERT // 2
    used = i < nu_ref[0]

    @pl.when(used & ((i == 0) | (be_ref[i] != be_ref[jnp.maximum(i - 1, 0)])))
    def _():
        wg_b[...] = wg_ref[...].astype(BF16)
        wu_b[...] = wu_ref[...].astype(BF16)
        wd_b[...] = wd_ref[...].astype(BF16)

    @pl.when(used)
    def _():
        xb = xs_ref[...].astype(BF16)
        for c in range(2):
            cs = slice(c * half, (c + 1) * half)
            gt = jnp.dot(xb, wg_b[:, cs], preferred_element_type=F32)
            up = jnp.dot(xb, wu_b[:, cs], preferred_element_type=F32)
            h_ref[:, cs] = (jax.nn.silu(gt) * up).astype(BF16)
        y_ref[...] = jnp.dot(h_ref[...], wd_b[...], preferred_element_type=F32)

    @pl.when(i >= nu_ref[0])
    def _():
        y_ref[...] = jnp.zeros_like(y_ref)


def _ffn(blk_exp, n_used, xs, wg, wu, wd, layer):
    rows = xs.shape[0]
    wspec = lambda a: pl.BlockSpec((None, None) + a.shape[2:], lambda i, be, nu: (layer, be[i], 0, 0))
    wbuf = lambda a: pltpu.VMEM(a.shape[2:], BF16)
    return pl.pallas_call(
        _ffn_kernel,
        grid_spec=pltpu.PrefetchScalarGridSpec(
            num_scalar_prefetch=2,
            grid=(rows // MOE_BLOCK,),
            in_specs=[pl.BlockSpec((MOE_BLOCK, D_MODEL), lambda i, be, nu: (i, 0)), wspec(wg), wspec(wu), wspec(wd)],
            out_specs=pl.BlockSpec((MOE_BLOCK, D_MODEL), lambda i, be, nu: (i, 0)),
            scratch_shapes=[pltpu.VMEM((MOE_BLOCK, D_EXPERT), BF16), wbuf(wg), wbuf(wu), wbuf(wd)]),
        out_shape=jax.ShapeDtypeStruct((rows, D_MODEL), F32),
        compiler_params=_cparams(("arbitrary",)),
        name="moe_ffn",
    )(blk_exp, n_used, xs, wg, wu, wd)


def _combine_kernel(dest_hbm, y_hbm, x_ref, wt_ref, lng_ref, lnb_ref, o_ref, dsm0, dsm1, buf, sem_idx, sem):
    i = pl.program_id(0)
    tm = x_ref.shape[0]
    dsm = (dsm0, dsm1)

    def gather(step, s):
        idx_copy = pltpu.make_async_copy(dest_hbm.at[step], dsm[s], sem_idx)
        idx_copy.start()
        idx_copy.wait()

        def start(r, c):
            for k in range(TOP_K):
                pltpu.make_async_copy(y_hbm.at[pl.ds(dsm[s][k * tm + r], 1)], buf.at[s, k, pl.ds(r, 1)],
                                      sem.at[s]).start()
            return c

        lax.fori_loop(0, tm, start, 0, unroll=ROW_DMA_UNROLL)

    def reduce(s):
        for k in range(TOP_K):
            pltpu.make_async_copy(y_hbm.at[pl.ds(0, tm)], buf.at[s, k], sem.at[s]).wait()
        wt = wt_ref[...]
        z = ALPHA * x_ref[...] + wt[:, 0:1] * buf[s, 0] + wt[:, 1:2] * buf[s, 1]
        o_ref[...] = _layer_norm_rows(z, lng_ref[...], lnb_ref[...])

    @pl.when(i == 0)
    def _():
        gather(0, 0)

    for s in range(2):
        @pl.when(i % 2 == s)
        def _():
            @pl.when(i + 1 < pl.num_programs(0))
            def _():
                gather(i + 1, 1 - s)
            reduce(s)


def _combine(dest_tiles, y, x, wt, lng, lnb, tm):
    T = x.shape[0]
    row = lambda i: (i, 0)
    vec = pl.BlockSpec((1, D_MODEL), lambda i: (0, 0))
    return pl.pallas_call(
        _combine_kernel,
        grid=(T // tm,),
        in_specs=[pl.BlockSpec(memory_space=pl.ANY), pl.BlockSpec(memory_space=pl.ANY),
                  pl.BlockSpec((tm, D_MODEL), row), pl.BlockSpec((tm, TOP_K), row), vec, vec],
        out_specs=pl.BlockSpec((tm, D_MODEL), row),
        out_shape=jax.ShapeDtypeStruct((T, D_MODEL), F32),
        scratch_shapes=[pltpu.SMEM((TOP_K * tm,), I32), pltpu.SMEM((TOP_K * tm,), I32),
                        pltpu.VMEM((2, TOP_K, tm, D_MODEL), F32),
                        pltpu.SemaphoreType.DMA(()), pltpu.SemaphoreType.DMA((2,))],
        compiler_params=_cparams(("arbitrary",)),
        name="moe_combine",
    )(dest_tiles, y, x, wt, lng, lnb)


def _moe(x1, e, wt, pos, cnt, wg, wu, wd, layer, lng, lnb, tm_dispatch=2048, tm_combine=1024):
    T = x1.shape[0]
    tm_dispatch = tm_dispatch if T % tm_dispatch == 0 else tm_combine
    n_blocks = -(-(T * TOP_K) // MOE_BLOCK) + N_EXPERTS
    rows = n_blocks * MOE_BLOCK
    counts = cnt[:, 0].astype(I32)
    padded = (counts + MOE_BLOCK - 1) // MOE_BLOCK * MOE_BLOCK
    pad_end = jnp.cumsum(padded)
    pad_start = pad_end - padded
    dest = pos
    for j in range(N_EXPERTS):
        dest = dest + jnp.where(e == j, pad_start[j], 0)
    blk_start = jnp.arange(n_blocks, dtype=I32) * MOE_BLOCK
    blk_exp = jnp.minimum(jnp.sum((pad_end[None, :] <= blk_start[:, None]).astype(I32), axis=1), N_EXPERTS - 1)
    n_used = (pad_end[-1:] // MOE_BLOCK).astype(I32)
    tiles = lambda tm: dest.reshape(TOP_K, T // tm, tm).transpose(1, 0, 2).reshape(T // tm, TOP_K * tm)
    tail = jnp.where(padded > 0, pad_end - MOE_BLOCK, -1)
    trailing = jnp.arange(n_blocks - N_EXPERTS, n_blocks, dtype=I32)
    trailing = jnp.where(trailing >= n_used[0], trailing * MOE_BLOCK, -1)
    xs = _dispatch(jnp.concatenate([tail, trailing]).astype(I32), tiles(tm_dispatch), x1, rows, tm_dispatch)
    y = _ffn(blk_exp, n_used, xs, wg, wu, wd, layer)
    return _combine(tiles(tm_combine), y, x1, wt.T, lng, lnb, tm_combine)


def _proj_rt_kernel(x_ref, w_ref, c_ref, s_ref, q_ref, k_ref, v_ref, g_ref):
    xb = x_ref[...].astype(BF16)
    cs, sn = c_ref[...], s_ref[...]
    rope = lambda y: y * cs + pltpu.roll(y, RT_QK // 2, 1) * sn
    pair = 2 * RT_QK
    for c in range(RT_QKW // pair):
        yq = jnp.dot(xb, w_ref[:, c * pair:(c + 1) * pair], preferred_element_type=F32)
        yk = jnp.dot(xb, w_ref[:, RT_QKW + c * pair:RT_QKW + (c + 1) * pair], preferred_element_type=F32)
        for h in range(2):
            cols = slice(c * pair + h * RT_QK, c * pair + (h + 1) * RT_QK)
            q_ref[:, cols] = rope(yq[:, h * RT_QK:(h + 1) * RT_QK]).astype(BF16)
            k_ref[:, cols] = (rope(yk[:, h * RT_QK:(h + 1) * RT_QK]) * RT_QK ** -0.5).astype(BF16)
    step = 1024
    for c in range(RT_VW // step):
        cols = slice(c * step, (c + 1) * step)
        v_ref[:, cols] = jnp.dot(xb, w_ref[:, 2 * RT_QKW + c * step:2 * RT_QKW + (c + 1) * step],
                                 preferred_element_type=F32).astype(BF16)
        base = 2 * RT_QKW + RT_VW
        g_ref[:, cols] = jax.nn.silu(jnp.dot(xb, w_ref[:, base + c * step:base + (c + 1) * step],
                                             preferred_element_type=F32))


def _proj_rt(xt, w, cs, sn, S, tm=512):
    T = xt.shape[0]
    nseq = S // tm
    row = lambda i: (i, 0)
    tab = lambda i: (i % nseq, 0)
    return pl.pallas_call(
        _proj_rt_kernel,
        grid=(T // tm,),
        in_specs=[pl.BlockSpec((tm, D_MODEL), row), pl.BlockSpec(w.shape, lambda i: (0, 0)),
                  pl.BlockSpec((tm, RT_QK), tab), pl.BlockSpec((tm, RT_QK), tab)],
        out_specs=[pl.BlockSpec((tm, RT_QKW), row), pl.BlockSpec((tm, RT_QKW), row),
                   pl.BlockSpec((tm, RT_VW), row), pl.BlockSpec((tm, RT_VW), row)],
        out_shape=[jax.ShapeDtypeStruct((T, RT_QKW), BF16), jax.ShapeDtypeStruct((T, RT_QKW), BF16),
                   jax.ShapeDtypeStruct((T, RT_VW), BF16), jax.ShapeDtypeStruct((T, RT_VW), F32)],
        compiler_params=_cparams(("parallel",)),
        name="proj_rt",
    )(xt, w, cs, sn)


def _retention_kernel(q_ref, k_ref, v_ref, sg_ref, dec_ref, qd_ref, kd_ref, cd_ref, gng_ref, gnb_ref, o_ref, state):
    j = pl.program_id(1)

    @pl.when(j == 0)
    def _():
        state[...] = jnp.zeros_like(state)

    NB = q_ref.shape[0]
    H = range(NB * RT_HEADS)
    nh = lambda i: (i // RT_HEADS, i % RT_HEADS)
    qk_cols = lambda i: slice(nh(i)[1] * RT_QK, (nh(i)[1] + 1) * RT_QK)
    v_cols = lambda i: slice(nh(i)[1] * RT_V, (nh(i)[1] + 1) * RT_V)
    q = [q_ref[nh(i)[0], :, qk_cols(i)] for i in H]
    k = [k_ref[nh(i)[0], :, qk_cols(i)] for i in H]
    v = [v_ref[nh(i)[0], :, v_cols(i)] for i in H]
    r_old = [state[i] for i in H]
    att = [(_dot_nt(q[i], k[i]) * dec_ref[nh(i)[1]]).astype(BF16) for i in H]
    cross = [jnp.dot(q[i], r_old[i].astype(BF16), preferred_element_type=F32) * qd_ref[nh(i)[1]] for i in H]
    inner = [jnp.dot(att[i], v[i], preferred_element_type=F32) for i in H]
    kdec = [(k[i].astype(F32) * kd_ref[nh(i)[1]]).astype(BF16) for i in H]
    for i in H:
        upd = lax.dot_general(kdec[i], v[i], (((0,), (0,)), ((), ())), preferred_element_type=F32)
        state[i] = r_old[i] * cd_ref[nh(i)[1]][:, 0:1] + upd
    for i in H:
        o = inner[i] + cross[i]
        mu = jnp.mean(o, axis=1, keepdims=True)
        oc = o - mu
        var = jnp.mean(oc * oc, axis=1, keepdims=True)
        on = oc * lax.rsqrt(var + RT_GN_EPS) * gng_ref[:, v_cols(i)] + gnb_ref[:, v_cols(i)]
        o_ref[nh(i)[0], :, v_cols(i)] = (sg_ref[nh(i)[0], :, v_cols(i)] * on).astype(BF16)


def _retention(q3, k3, v3, sg3, dec, qd, kd, cd, gng, gnb, nb=2):
    B, S, _ = q3.shape
    C = RT_CHUNK
    nb = nb if B % nb == 0 else 1
    qk = pl.BlockSpec((nb, C, RT_QKW), lambda b, j: (b, j, 0))
    vv = pl.BlockSpec((nb, C, RT_VW), lambda b, j: (b, j, 0))
    const = lambda a: pl.BlockSpec(a.shape, lambda b, j: (0,) * a.ndim)
    return pl.pallas_call(
        _retention_kernel,
        grid=(B // nb, S // C),
        in_specs=[qk, qk, vv, vv, const(dec), const(qd), const(kd), const(cd), const(gng), const(gnb)],
        out_specs=vv,
        out_shape=jax.ShapeDtypeStruct((B, S, RT_VW), BF16),
        scratch_shapes=[pltpu.VMEM((nb * RT_HEADS, RT_QK, RT_V), F32)],
        compiler_params=_cparams(("parallel", "arbitrary")),
        name="retention",
    )(q3, k3, v3, sg3, dec, qd, kd, cd, gng, gnb)


def _nsa_rope_tables(S):
    half = ROPE_DIM // 2
    inv = ROPE_THETA ** (-jnp.arange(half, dtype=F32) / half)
    ang = jnp.arange(S, dtype=F32)[:, None] * inv[None, :]
    cos, sin = jnp.cos(ang), jnp.sin(ang)
    zeros = lambda n: jnp.zeros((S, n), F32)
    cn = jnp.concatenate([cos, cos, jnp.ones((S, NS_HEAD - ROPE_DIM), F32)], axis=1)
    s1 = jnp.concatenate([-sin, zeros(NS_HEAD - half)], axis=1)
    s2 = jnp.concatenate([zeros(half), sin, zeros(NS_HEAD - ROPE_DIM)], axis=1)
    two = lambda a: jnp.concatenate([a, a], axis=1)
    return two(cn), two(s1), two(s2)


def _rt_rope_tables(S):
    inv = RT_THETA ** (-jnp.linspace(0.0, 1.0, RT_QK // 2, dtype=F32))
    ang = jnp.arange(S, dtype=F32)[:, None] * inv[None, :]
    cos, sin = jnp.cos(ang), jnp.sin(ang)
    return jnp.concatenate([cos, cos], axis=1), jnp.concatenate([-sin, sin], axis=1)


def _rt_decay_tables():
    log_g = jnp.log(1.0 - 2.0 ** (-5.0 - jnp.arange(RT_HEADS, dtype=F32)))
    idx = jnp.arange(RT_CHUNK, dtype=F32)
    diff = idx[:, None] - idx[None, :]
    dec = jnp.where(diff >= 0, jnp.exp(jnp.maximum(diff, 0.0) * log_g[:, None, None]), 0.0)
    qd = jnp.exp((idx + 1.0) * log_g[:, None])[..., None]
    kd = jnp.exp((RT_CHUNK - 1.0 - idx) * log_g[:, None])[..., None]
    cd = jnp.broadcast_to(jnp.exp(RT_CHUNK * log_g)[:, None, None], (RT_HEADS, 1, LANES))
    return dec, qd, kd, cd


def _overlap_table(S, ncp):
    n_cmp = (S - CMP_LEN) // CMP_STRIDE + 1
    n_sel = S // SEL_BLOCK
    cs = jnp.arange(ncp) * CMP_STRIDE
    ss = jnp.arange(LANES) * SEL_BLOCK
    ov = jnp.clip(jnp.minimum(cs[:, None] + CMP_LEN, ss[None, :] + SEL_BLOCK)
                  - jnp.maximum(cs[:, None], ss[None, :]), 0, None).astype(F32) / CMP_LEN
    keep = (jnp.arange(ncp)[:, None] < n_cmp) & (jnp.arange(LANES)[None, :] < n_sel)
    return jnp.where(keep, ov, 0.0)


def kernel(x, ab_w_in, ab_w_out, rk_mu, rk_w0, rk_w1, rk_w2, rk_a0, rk_a1, rk_a2, rk_g1, rk_g2, rk_kk, rk_ka, rk_rk,
           rk_ln, ns_pe, ns_c_w1, ns_c_w2, rt_w_in, rt_w_out, rt_gn, router_w, router_b, moe_w_gate, moe_w_up,
           moe_w_down, ln):
    B, S, D = x.shape
    T = B * S
    assert D == D_MODEL and S % 256 == 0 and S // SEL_BLOCK <= LANES and S >= WINDOW
    xt = x.reshape(T, D)
    rwt = router_w.T
    rb = router_b.reshape(N_EXPERTS, 1)
    vec = lambda a: a.reshape(1, -1)

    w_in = ab_w_in[0]
    n_gate = 3 * NS_HEADS
    w_cat = jnp.concatenate([w_in[:, :-n_gate], jnp.pad(w_in[:, -n_gate:], ((0, 0), (0, LANES - n_gate)))],
                            axis=1).astype(BF16)
    cn, s1, s2 = _nsa_rope_tables(S)
    prk, q, kk2, gt, vt, cx = _proj_ab(xt, w_cat, cn, s1, s2, S)

    ones = (jnp.arange(RK_W)[:, None] // RK_HEAD == jnp.arange(RK_W)[None, :] // RK_HEAD).astype(BF16)
    b16 = lambda a: a.astype(BF16)
    r, lw, km, v, kk, bb, g, bon = _rwkv_prep(
        prk.reshape(B, S, 4 * RK_W), rk_mu[0], vec(rk_w0[0]), b16(rk_w1[0]), b16(rk_w2[0]), vec(rk_a0[0]),
        b16(rk_a1[0]), b16(rk_a2[0]), b16(rk_g1[0]), b16(rk_g2[0]), vec(rk_kk[0]), vec(rk_ka[0]), vec(rk_rk[0]), ones)
    o_a = _rwkv_chunk(r, lw, km, v, kk, bb, g, bon, rk_ln[0, 0:1], rk_ln[0, 1:2])

    ncp = S // CMP_STRIDE
    kcv, kcvt = _nsa_compress(cx.reshape(2, B, ncp, CMP_STRIDE * LANES), ns_c_w1[0], b16(ns_c_w2[0]),
                              ns_pe[0].reshape(2, 1, CMP_LEN * NS_HEAD))
    o_b = _nsa_attn(q.reshape(B, S, NS_W), kcv, kcvt, kk2.reshape(B, S, 2 * LANES),
                    vt.reshape(2, B, S // QT, LANES, QT), gt.reshape(B, S, LANES), _overlap_table(S, ncp).T)

    w_out = b16(ab_w_out[0])
    x1, e, wt, pos, cnt = _outproj_router([o_a.reshape(T, RK_W), o_b.reshape(T, NS_W)], [w_out[:RK_W], w_out[RK_W:]],
                                          xt, ln[0, 0, 0:1], ln[0, 0, 1:2], rwt, rb)
    x2 = _moe(x1, e, wt, pos, cnt, moe_w_gate, moe_w_up, moe_w_down, 0, ln[0, 1, 0:1], ln[0, 1, 1:2])

    cs, sn = _rt_rope_tables(S)
    qr, kr, vr, sg = _proj_rt(x2, b16(rt_w_in[0]), cs, sn, S)
    dec, qd, kd, cd = _rt_decay_tables()
    ret = _retention(qr.reshape(B, S, RT_QKW), kr.reshape(B, S, RT_QKW), vr.reshape(B, S, RT_VW),
                     sg.reshape(B, S, RT_VW), dec, qd, kd, cd, rt_gn[0, 0:1], rt_gn[0, 1:2])
    x3, e, wt, pos, cnt = _outproj_router([ret.reshape(T, RT_VW)], [b16(rt_w_out[0])], x2,
                                          ln[1, 0, 0:1], ln[1, 0, 1:2], rwt, rb)
    x4 = _moe(x3, e, wt, pos, cnt, moe_w_gate, moe_w_up, moe_w_down, 1, ln[1, 1, 0:1], ln[1, 1, 1:2])
    return x4.reshape(B, S, D)
```
